```python
import jax, jax.numpy as jnp
from jax import lax
import numpy as np

D_MODEL = 1024
BATCH = 4
SEQ = 4096
DEPTH = 4

GRID_W = 64
CTX_LEN = 256
EPS = 1e-6
N_MOD = 6
N_BRANCH = 3
GLA_HEADS = 4
GLA_DK = 64
GLA_DV = 128
GLA_RANK = 16
GLA_TAU = 16.0
GLA_CHUNK = 64
GLA_SCALE = GLA_DK ** -0.5
FNET_GROUPS = 4
FNET_GROUP_W = 128
SWA_HEADS = 8
SWA_KV_HEADS = 2
SWA_HEAD_DIM = 64
SWA_GROUP = SWA_HEADS // SWA_KV_HEADS
WINDOW = 128
ROPE_BASE = 10000.0
AXIS_DIM = SWA_HEAD_DIM // 2
MOE_GROUPS = 4
MOE_EXPERTS_PER_GROUP = 4
MOE_EXPERTS = MOE_GROUPS * MOE_EXPERTS_PER_GROUP
MOE_TOPK = 2
D_EXPERT = 512

GLA_QK_W = GLA_HEADS * GLA_DK
GLA_V_W = GLA_HEADS * GLA_DV
FNET_W = FNET_GROUPS * FNET_GROUP_W
SWA_Q_W = SWA_HEADS * SWA_HEAD_DIM
SWA_KV_W = SWA_KV_HEADS * SWA_HEAD_DIM
IN_SIZES = [GLA_QK_W, GLA_V_W, SWA_KV_W, SWA_KV_W, GLA_QK_W, GLA_V_W, FNET_W, SWA_Q_W, N_BRANCH * D_MODEL]
IN_COLS = sum(IN_SIZES)

kernel_name = "hybrid_gla_fourier_swa_hmoe_dit"


def rms_norm(x, g):
    xf = x.astype(jnp.float32)
    y = xf * lax.rsqrt(jnp.mean(xf * xf, axis=-1, keepdims=True) + EPS)
    return (y * g.astype(jnp.float32)).astype(x.dtype)


def modulate(x, shift, scale):
    return x * (1 + scale[..., None, :]) + shift[..., None, :]


def split_by(t, sizes):
    return jnp.split(t, [int(i) for i in np.cumsum(sizes)[:-1]], axis=-1)


def heads(t, n):
    return t.reshape(t.shape[0], t.shape[1], n, -1)


def flip_seq(t):
    return jnp.flip(t, axis=1)


def gla_log_decay(h, w_down, w_up, b):
    z = (h @ w_down) @ w_up + b
    la = jax.nn.log_sigmoid(z.astype(jnp.float32)) / GLA_TAU
    return la.reshape(h.shape[0], h.shape[1], GLA_HEADS, GLA_DK)


def gla_chunked(q, k, v, log_a, s0):
    b_, l_, h_, _ = q.shape
    n = l_ // GLA_CHUNK

    def to_chunks(t):
        t = t.astype(jnp.float32)
        return jnp.moveaxis(t.reshape(b_, n, GLA_CHUNK, *t.shape[2:]), 1, 0)

    lower = jnp.tril(jnp.ones((GLA_CHUNK, GLA_CHUNK), dtype=bool))[None, :, :, None, None]

    def step(state, inp):
        qc, kc, vc, ac = inp
        cum = jnp.cumsum(ac, axis=1)
        inter = jnp.einsum('bchk,bhkv->bchv', qc * jnp.exp(cum), state)
        diff = cum[:, :, None] - cum[:, None, :]
        decay = jnp.exp(jnp.where(lower, diff, -jnp.inf))
        scores = jnp.einsum('bthk,bshk,btshk->btsh', qc, kc, decay)
        intra = jnp.einsum('btsh,bshv->bthv', scores, vc)
        last = cum[:, -1]
        new_state = jnp.exp(last)[..., None] * state + jnp.einsum(
            'bshk,bshv->bhkv', kc * jnp.exp(last[:, None] - cum), vc)
        return new_state, inter + intra

    state, out = lax.scan(step, s0, (to_chunks(q), to_chunks(k), to_chunks(v), to_chunks(log_a)))
    out = jnp.moveaxis(out, 0, 1).reshape(b_, l_, h_, v.shape[-1])
    return out, state


def gla_final_state(k, v, log_a):
    cum = jnp.cumsum(log_a.astype(jnp.float32), axis=1)
    return jnp.einsum('bshk,bshv->bhkv', k.astype(jnp.float32) * jnp.exp(cum[:, -1:] - cum),
                      v.astype(jnp.float32))


def gla_bidirectional(q, k, v, la_f, la_b, s0_f, s0_b):
    o_f, s_f = gla_chunked(q, k, v, la_f, s0_f)
    o_b, s_b = gla_chunked(flip_seq(q), flip_seq(k), flip_seq(v), flip_seq(la_b), s0_b)
    return o_f + flip_seq(o_b), s_f, s_b


def gla_output(o, r, g_gla):
    o = rms_norm(o, g_gla).reshape(o.shape[0], o.shape[1], GLA_V_W)
    return (o * jax.nn.silu(r.astype(jnp.float32))).astype(r.dtype)


def fourier_mix(u):
    b_, l_, _ = u.shape
    ug = u.astype(jnp.float32).reshape(b_, l_, FNET_GROUPS, FNET_GROUP_W)
    f = jnp.fft.fftn(ug, axes=(1, 3), norm='ortho').real
    return f.reshape(b_, l_, FNET_W).astype(u.dtype)


def axial_angles(n_tokens):
    rows = n_tokens // GRID_W
    row = jnp.repeat(jnp.arange(rows, dtype=jnp.float32), GRID_W)
    col = jnp.tile(jnp.arange(GRID_W, dtype=jnp.float32), rows)
    inv_freq = ROPE_BASE ** (-jnp.arange(0, AXIS_DIM, 2, dtype=jnp.float32) / AXIS_DIM)
    return row[:, None] * inv_freq, col[:, None] * inv_freq


def rope_1d(x, ang):
    x1, x2 = jnp.split(x.astype(jnp.float32), 2, axis=-1)
    cos = jnp.cos(ang)[None, :, None, :]
    sin = jnp.sin(ang)[None, :, None, :]
    return jnp.concatenate([x1 * cos - x2 * sin, x1 * sin + x2 * cos], axis=-1)


def rope_2d(x, ang_row, ang_col):
    xr, xc = jnp.split(x, 2, axis=-1)
    return jnp.concatenate([rope_1d(xr, ang_row), rope_1d(xc, ang_col)], axis=-1).astype(x.dtype)


def windowed_gqa_with_context(q, k, v, kc, vc, sink):
    b_, l_, _, d = q.shape
    nb = l_ // WINDOW
    scale = d ** -0.5
    qb = q.reshape(b_, nb, WINDOW, SWA_KV_HEADS, SWA_GROUP, d)
    pad = ((0, 0), (WINDOW, WINDOW), (0, 0), (0, 0))
    kb = jnp.pad(k, pad).reshape(b_, nb + 2, WINDOW, SWA_KV_HEADS, d)
    vb = jnp.pad(v, pad).reshape(b_, nb + 2, WINDOW, SWA_KV_HEADS, d)
    kw = jnp.concatenate([kb[:, :-2], kb[:, 1:-1], kb[:, 2:]], axis=2)
    vw = jnp.concatenate([vb[:, :-2], vb[:, 1:-1], vb[:, 2:]], axis=2)
    a = jnp.arange(WINDOW)[:, None]
    j = jnp.arange(3 * WINDOW)[None, :]
    band = (j - a >= 0) & (j - a <= 2 * WINDOW)
    key_pos = (jnp.arange(nb)[:, None, None] - 1) * WINDOW + j[None]
    valid = band[None] & (key_pos >= 0) & (key_pos < l_)
    s_win = jnp.einsum('bnqhgd,bnkhd->bnhgqk', qb, kw).astype(jnp.float32) * scale
    s_win = jnp.where(valid[None, :, None, None], s_win, -jnp.inf)
    s_ctx = jnp.einsum('bnqhgd,bkhd->bnhgqk', qb, kc).astype(jnp.float32) * scale
    s_sink = jnp.broadcast_to(
        sink.astype(jnp.float32).reshape(SWA_KV_HEADS, SWA_GROUP)[None, None, :, :, None, None],
        s_win.shape[:-1] + (1,))
    probs = jax.nn.softmax(jnp.concatenate([s_win, s_ctx, s_sink], axis=-1), axis=-1)
    n_win = 3 * WINDOW
    p_win = probs[..., :n_win].astype(v.dtype)
    p_ctx = probs[..., n_win:n_win + kc.shape[1]].astype(v.dtype)
    out = (jnp.einsum('bnhgqk,bnkhd->bnqhgd', p_win, vw)
           + jnp.einsum('bnhgqk,bkhd->bnqhgd', p_ctx, vc))
    return out.reshape(b_, l_, SWA_Q_W)


def context_gqa(qc, kc, vc, sink):
    b_, n_, _, d = qc.shape
    q = qc.reshape(b_, n_, SWA_KV_HEADS, SWA_GROUP, d)
    s = jnp.einsum('bqhgd,bkhd->bhgqk', q, kc).astype(jnp.float32) * d ** -0.5
    s_sink = jnp.broadcast_to(
        sink.astype(jnp.float32).reshape(SWA_KV_HEADS, SWA_GROUP)[None, :, :, None, None],
        s.shape[:-1] + (1,))
    p = jax.nn.softmax(jnp.concatenate([s, s_sink], axis=-1), axis=-1)[..., :-1].astype(vc.dtype)
    return jnp.einsum('bhgqk,bkhd->bqhgd', p, vc).reshape(b_, n_, SWA_Q_W)


def merge_branches(y_a, y_b, y_c, gate_logits, w_pa, w_pb, w_pc, w_out):
    g_a, g_b, g_c = jnp.split(jax.nn.sigmoid(gate_logits), N_BRANCH, axis=-1)
    return (g_a * (y_a @ w_pa) + g_b * (y_b @ w_pb) + g_c * (y_c @ w_pc)) @ w_out


def hybrid_mixer(h, hc, w_in, w_decay_down, w_decay_up, b_decay, g_gla, sink, w_pa, w_pb, w_pc, w_out, last):
    b_, l_, _ = h.shape
    k_a, v_a, k_c, v_c, q_a, r_a, u_b, q_c, gate_logits = split_by(h @ w_in, IN_SIZES)
    n_parts = 4 if last else len(IN_SIZES)
    ctx_sizes = IN_SIZES[:n_parts]
    cparts = split_by(hc @ w_in[:, :sum(ctx_sizes)], ctx_sizes)
    ck_a, cv_a = heads(cparts[0], GLA_HEADS), heads(cparts[1], GLA_HEADS)
    ck_c, cv_c = heads(cparts[2], SWA_KV_HEADS), heads(cparts[3], SWA_KV_HEADS)

    la_f = gla_log_decay(h, w_decay_down[0], w_decay_up[0], b_decay[0])
    la_b = gla_log_decay(h, w_decay_down[1], w_decay_up[1], b_decay[1])
    cla_f = gla_log_decay(hc, w_decay_down[0], w_decay_up[0], b_decay[0])
    cla_b = gla_log_decay(hc, w_decay_down[1], w_decay_up[1], b_decay[1])
    if last:
        s_f = gla_final_state(ck_a, cv_a, cla_f)
        s_b = gla_final_state(flip_seq(ck_a), flip_seq(cv_a), flip_seq(cla_b))
    else:
        s0 = jnp.zeros((b_, GLA_HEADS, GLA_DK, GLA_DV), jnp.float32)
        co_a, s_f, s_b = gla_bidirectional(heads(cparts[4], GLA_HEADS) * GLA_SCALE, ck_a, cv_a,
                                           cla_f, cla_b, s0, s0)
    o_a, _, _ = gla_bidirectional(heads(q_a, GLA_HEADS) * GLA_SCALE, heads(k_a, GLA_HEADS),
                                  heads(v_a, GLA_HEADS), la_f, la_b, s_f, s_b)
    y_a = gla_output(o_a, r_a, g_gla)

    y_b = fourier_mix(u_b)

    ang_row, ang_col = axial_angles(l_)
    q_rot = rope_2d(heads(q_c, SWA_HEADS), ang_row, ang_col)
    k_rot = rope_2d(heads(k_c, SWA_KV_HEADS), ang_row, ang_col)
    y_c = windowed_gqa_with_context(q_rot, k_rot, heads(v_c, SWA_KV_HEADS), ck_c, cv_c, sink)

    y = merge_branches(y_a, y_b, y_c, gate_logits, w_pa, w_pb, w_pc, w_out)
    if last:
        return y, None
    yc_a = gla_output(co_a, cparts[5], g_gla)
    yc_b = fourier_mix(cparts[6])
    yc_c = context_gqa(heads(cparts[7], SWA_HEADS), ck_c, cv_c, sink)
    yc = merge_branches(yc_a, yc_b, yc_c, cparts[8], w_pa, w_pb, w_pc, w_out)
    return y, yc


def hierarchical_moe(h, w_router_group, w_router_expert, w1, w3, w2):
    g_logits = (h @ w_router_group).astype(jnp.float32)
    g_sel = jnp.argmax(g_logits, axis=-1)
    g_gate = jnp.take_along_axis(jax.nn.softmax(g_logits, axis=-1), g_sel[:, None], axis=-1)
    e_logits = (h @ w_router_expert).astype(jnp.float32).reshape(-1, MOE_GROUPS, MOE_EXPERTS_PER_GROUP)
    e_in_group = jnp.take_along_axis(e_logits, g_sel[:, None, None], axis=1)[:, 0]
    top_vals, top_idx = lax.top_k(e_in_group, MOE_TOPK)
    top_w = jax.nn.softmax(top_vals, axis=-1) * g_gate
    expert_id = g_sel[:, None] * MOE_EXPERTS_PER_GROUP + top_idx
    gates = jnp.sum(jax.nn.one_hot(expert_id, MOE_EXPERTS, dtype=jnp.float32) * top_w[..., None], axis=1)
    gates = gates.astype(h.dtype)
    out = jnp.zeros(h.shape, h.dtype)
    for e in range(MOE_EXPERTS):
        hid = jax.nn.silu(h @ w1[e]) * (h @ w3[e])
        out = out + (hid @ w2[e]) * gates[:, e:e + 1]
    return out


def trunk_layer(x, xc, c_act, cc_act, w_ada, b_ada, g_mix, g_ffn, w_in, w_decay_down, w_decay_up, b_decay,
                g_gla, sink, w_pa, w_pb, w_pc, w_out, w_router_group, w_router_expert, w1, w3, w2, last):
    shift1, scale1, gate1, shift2, scale2, gate2 = jnp.split(c_act @ w_ada + b_ada, N_MOD, axis=-1)
    n_ctx_mod = 2 if last else N_MOD
    mods_c = jnp.split(cc_act @ w_ada[:, :n_ctx_mod * D_MODEL] + b_ada[:n_ctx_mod * D_MODEL], n_ctx_mod, axis=-1)
    h = modulate(rms_norm(x, g_mix), shift1, scale1)
    hc = modulate(rms_norm(xc, g_mix), mods_c[0], mods_c[1])
    y, yc = hybrid_mixer(h, hc, w_in, w_decay_down, w_decay_up, b_decay, g_gla, sink,
                         w_pa, w_pb, w_pc, w_out, last)
    x = x + gate1[:, None, :] * y
    h2 = modulate(rms_norm(x, g_ffn), shift2, scale2)
    x = x + gate2[:, None, :] * hierarchical_moe(h2.reshape(-1, D_MODEL), w_router_group, w_router_expert,
                                                 w1, w3, w2).reshape(x.shape)
    if last:
        return x, None
    xc = xc + mods_c[2] * yc
    hc2 = modulate(rms_norm(xc, g_ffn), mods_c[3], mods_c[4])
    xc = xc + mods_c[5] * hierarchical_moe(hc2.reshape(-1, D_MODEL), w_router_group, w_router_expert,
                                           w1, w3, w2).reshape(xc.shape)
    return x, xc


def setup_inputs(seed: int = 0) -> dict:
    key = jax.random.key(seed)
    ks = jax.random.split(key, 24)
    D = D_MODEL

    def nrm(k, shape, s):
        return jax.random.normal(k, shape, jnp.float32) * s

    return {
        'x': nrm(ks[0], (BATCH, SEQ, D), 1.0),
        'c': nrm(ks[1], (BATCH, D), 1.0),
        'ctx': nrm(ks[2], (BATCH, CTX_LEN, D), 1.0),
        'c_ctx': nrm(ks[3], (D,), 1.0),
        'w_ada': nrm(ks[4], (DEPTH, D, N_MOD * D), 0.5 * D ** -0.5),
        'b_ada': nrm(ks[5], (DEPTH, N_MOD * D), 0.02),
        'g_mix': 1.0 + nrm(ks[6], (DEPTH, D), 0.05),
        'g_ffn': 1.0 + nrm(ks[7], (DEPTH, D), 0.05),
        'w_in': nrm(ks[8], (DEPTH, D, IN_COLS), D ** -0.5),
        'w_decay_down': nrm(ks[9], (DEPTH, 2, D, GLA_RANK), D ** -0.5),
        'w_decay_up': nrm(ks[10], (DEPTH, 2, GLA_RANK, GLA_QK_W), GLA_RANK ** -0.5),
        'b_decay': nrm(ks[11], (DEPTH, 2, GLA_QK_W), 0.5),
        'g_gla': 1.0 + nrm(ks[12], (DEPTH, GLA_DV), 0.05),
        'sink': nrm(ks[13], (DEPTH, SWA_HEADS), 0.5),
        'w_pa': nrm(ks[14], (DEPTH, GLA_V_W, D), GLA_V_W ** -0.5),
        'w_pb': nrm(ks[15], (DEPTH, FNET_W, D), FNET_W ** -0.5),
        'w_pc': nrm(ks[16], (DEPTH, SWA_Q_W, D), SWA_Q_W ** -0.5),
        'w_out': nrm(ks[17], (DEPTH, D, D), D ** -0.5),
        'w_router_group': nrm(ks[18], (DEPTH, D, MOE_GROUPS), D ** -0.5),
        'w_router_expert': nrm(ks[19], (DEPTH, D, MOE_EXPERTS), D ** -0.5),
        'w1': nrm(ks[20], (DEPTH, MOE_EXPERTS, D, D_EXPERT), D ** -0.5),
        'w3': nrm(ks[21], (DEPTH, MOE_EXPERTS, D, D_EXPERT), D ** -0.5),
        'w2': nrm(ks[22], (DEPTH, MOE_EXPERTS, D_EXPERT, D), D_EXPERT ** -0.5),
        'g_final': 1.0 + nrm(ks[23], (D,), 0.05),
    }


def reference(x, c, ctx, c_ctx, w_ada, b_ada, g_mix, g_ffn, w_in, w_decay_down, w_decay_up, b_decay, g_gla,
              sink, w_pa, w_pb, w_pc, w_out, w_router_group, w_router_expert, w1, w3, w2, g_final):
    c_act = jax.nn.silu(c)
    cc_act = jax.nn.silu(c_ctx)
    xc = ctx
    for l in range(DEPTH):
        x, xc = trunk_layer(x, xc, c_act, cc_act, w_ada[l], b_ada[l], g_mix[l], g_ffn[l], w_in[l],
                            w_decay_down[l], w_decay_up[l], b_decay[l], g_gla[l], sink[l], w_pa[l], w_pb[l],
                            w_pc[l], w_out[l], w_router_group[l], w_router_expert[l], w1[l], w3[l], w2[l],
                            l == DEPTH - 1)
    return rms_norm(x, g_final)
```

```python
import functools
import math

import jax
import jax.numpy as jnp
from jax import lax
from jax.experimental import pallas as pl
from jax.experimental.pallas import tpu as pltpu

F32 = jnp.float32
BF16 = jnp.bfloat16
HIGHEST = lax.Precision.HIGHEST

D_MODEL = 1024
GRID_W = 64
EPS = 1e-6
N_MOD = 6
GLA_HEADS = 4
GLA_DK = 64
GLA_DV = 128
GLA_TAU = 16.0
GLA_CHUNK = 64
GLA_SCALE = GLA_DK ** -0.5
FNET_GROUPS = 4
FNET_GROUP_W = 128
SWA_HEADS = 8
SWA_KV_HEADS = 2
SWA_HEAD_DIM = 64
WINDOW = 128
ROPE_BASE = 10000.0
AXIS_DIM = SWA_HEAD_DIM // 2
MOE_GROUPS = 4
MOE_EXPERTS_PER_GROUP = 4
MOE_EXPERTS = MOE_GROUPS * MOE_EXPERTS_PER_GROUP
D_EXPERT = 512

GLA_QK_W = GLA_HEADS * GLA_DK
GLA_V_W = GLA_HEADS * GLA_DV
FNET_W = FNET_GROUPS * FNET_GROUP_W
SWA_Q_W = SWA_HEADS * SWA_HEAD_DIM
SWA_KV_W = SWA_KV_HEADS * SWA_HEAD_DIM
IN_SIZES = (GLA_QK_W, GLA_V_W, SWA_KV_W, SWA_KV_W, GLA_QK_W, GLA_V_W, FNET_W, SWA_Q_W, 3 * D_MODEL)
IN_OFFS = tuple(int(sum(IN_SIZES[:i])) for i in range(len(IN_SIZES) + 1))
IN_COLS = IN_OFFS[-1]

LANES = 128
TOKEN_TILE = 256
MOE_TILE = 1024
DECAY_PAD = LANES
ROUTER_PAD = LANES
VMEM_LIMIT = 56 * 1024 * 1024


def _params(sem, vmem=VMEM_LIMIT):
    return pltpu.CompilerParams(dimension_semantics=sem, vmem_limit_bytes=vmem)


def _sigmoid(x):
    return 1.0 / (1.0 + jnp.exp(-x))


def _dot(a, b):
    return jnp.dot(a, b, preferred_element_type=F32)


def _dot_nt(a, b):
    return lax.dot_general(a, b, (((1,), (1,)), ((), ())), preferred_element_type=F32)


def _dot_tn(a, b):
    return lax.dot_general(a, b, (((0,), (0,)), ((), ())), preferred_element_type=F32)


def _ada_kernel(c_ref, w_ref, b_ref, o_ref):
    c = c_ref[...]
    a = c * _sigmoid(c)
    o_ref[...] = jnp.dot(a, w_ref[...], preferred_element_type=F32, precision=HIGHEST) + b_ref[...]


def _adaln(c_all, w_ada, b_ada):
    depth, d, n = w_ada.shape
    rows = c_all.shape[0]
    tn = 1536
    return pl.pallas_call(
        _ada_kernel,
        out_shape=jax.ShapeDtypeStruct((depth, rows, n), F32),
        grid=(depth, n // tn),
        in_specs=[pl.BlockSpec((rows, d), lambda l, j: (0, 0)),
                  pl.BlockSpec((None, d, tn), lambda l, j: (l, 0, j)),
                  pl.BlockSpec((None, 1, tn), lambda l, j: (l, 0, j))],
        out_specs=pl.BlockSpec((None, rows, tn), lambda l, j: (l, 0, j)),
        compiler_params=_params(("parallel", "parallel")),
        name="adaln",
    )(c_all, w_ada, b_ada.reshape(depth, 1, n))


class _Layout:
    def __init__(self, batch, seq, ctx, tile):
        assert seq % tile == 0 and ctx % tile == 0
        self.batch, self.seq, self.ctx, self.tile = batch, seq, ctx, tile
        self.lat_tiles = seq // tile
        self.ctx_tiles = ctx // tile
        self.n_lat = batch * self.lat_tiles
        self.n_tiles = self.n_lat + batch * self.ctx_tiles
        self.n_tok = self.n_tiles * tile

    def batch_of(self, t):
        return jnp.where(t < self.n_lat, t // self.lat_tiles, (t - self.n_lat) // self.ctx_tiles)

    def mod_row(self, t):
        return jnp.where(t < self.n_lat, t // self.lat_tiles, self.batch)

    def seq_tile(self, t):
        return jnp.where(t < self.n_lat, t % self.lat_tiles,
                         self.lat_tiles + (t - self.n_lat) % self.ctx_tiles)


def _rope(x, cos, sin_signed):
    n = x.shape[-1]
    lane = lax.broadcasted_iota(jnp.int32, x.shape, 1)
    half = AXIS_DIM // 2
    partner = jnp.where((lane & half) == 0, pltpu.roll(x, n - half, 1), pltpu.roll(x, half, 1))
    return x * cos + partner * sin_signed


def _inproj_kernel(x_ref, g_ref, mod_ref, w_ref, wu_ref, bdec_ref, cs_ref, cos_ref, sin_ref,
                   ka_ref, va_ref, kc_ref, vc_ref, qa_ref, ra_ref, ua_ref, us_ref, qc_ref, gl_ref,
                   laf_ref, lab_ref):
    d = D_MODEL
    x = x_ref[...]
    shift = mod_ref[:, 0:d]
    scale = mod_ref[:, d:2 * d]
    h = x * lax.rsqrt(jnp.mean(x * x, axis=-1, keepdims=True) + EPS) * g_ref[...]
    hb = (h * (1.0 + scale) + shift).astype(BF16)

    def proj(i):
        return _dot(hb, w_ref[:, IN_OFFS[i]:IN_OFFS[i + 1]])

    ka_ref[...] = proj(0).astype(BF16)
    va_ref[...] = proj(1).astype(BF16)
    cos = cos_ref[...]
    sin = sin_ref[...]
    kc_ref[...] = _rope(proj(2), cos, sin).astype(BF16)
    vc_ref[...] = proj(3).astype(BF16)
    qa_ref[...] = proj(4).astype(BF16)
    ra_ref[...] = proj(5).astype(BF16)
    u = proj(6).astype(BF16)
    for g in range(FNET_GROUPS):
        sl = slice(g * FNET_GROUP_W, (g + 1) * FNET_GROUP_W)
        ab = _dot(u[:, sl], cs_ref[...])
        ua_ref[:, sl] = ab[:, :FNET_GROUP_W].astype(BF16)
        us_ref[:, sl] = ab[:, FNET_GROUP_W:].astype(BF16)
    reps = SWA_Q_W // LANES
    qc_ref[...] = _rope(proj(7), jnp.concatenate([cos] * reps, axis=1),
                        jnp.concatenate([sin] * reps, axis=1)).astype(BF16)
    gl_ref[...] = proj(8).astype(BF16)
    low = _dot(hb, w_ref[:, IN_COLS:IN_COLS + DECAY_PAD]).astype(BF16)
    z = _dot(low, wu_ref[...]) + bdec_ref[...]
    la = (jnp.minimum(z, 0.0) - jnp.log(1.0 + jnp.exp(-jnp.abs(z)))) * (1.0 / GLA_TAU)
    laf_ref[...] = la[:, :GLA_QK_W]
    lab_ref[...] = la[:, GLA_QK_W:]


def _inproj(lay, x, g_mix, mods, w_ext, w_up, b_dec, cs, cos_t, sin_t):
    tm, d = lay.tile, D_MODEL
    n = lay.n_tok
    bf = lambda w: jax.ShapeDtypeStruct((n, w), BF16)
    tok = lambda w: pl.BlockSpec((tm, w), lambda t: (t, 0))
    const = lambda a: pl.BlockSpec(a.shape, lambda t: (0,) * a.ndim)
    seq_rows = lay.seq + lay.ctx
    fnet_shape = jax.ShapeDtypeStruct((seq_rows, lay.batch * FNET_W), BF16)
    fnet_spec = pl.BlockSpec((tm, FNET_W), lambda t: (lay.seq_tile(t), lay.batch_of(t)))
    pos_spec = pl.BlockSpec((tm, LANES), lambda t: (lay.seq_tile(t), 0))
    return pl.pallas_call(
        _inproj_kernel,
        out_shape=(bf(GLA_QK_W), bf(GLA_V_W), bf(SWA_KV_W), bf(SWA_KV_W), bf(GLA_QK_W), bf(GLA_V_W),
                   fnet_shape, fnet_shape, bf(SWA_Q_W), bf(3 * d),
                   jax.ShapeDtypeStruct((n, GLA_QK_W), F32), jax.ShapeDtypeStruct((n, GLA_QK_W), F32)),
        grid=(lay.n_tiles,),
        in_specs=[tok(d), const(g_mix),
                  pl.BlockSpec((None, 1, N_MOD * d), lambda t: (lay.mod_row(t), 0, 0)),
                  const(w_ext), const(w_up), const(b_dec), const(cs), pos_spec, pos_spec],
        out_specs=(tok(GLA_QK_W), tok(GLA_V_W), tok(SWA_KV_W), tok(SWA_KV_W), tok(GLA_QK_W), tok(GLA_V_W),
                   fnet_spec, fnet_spec, tok(SWA_Q_W), tok(3 * d), tok(GLA_QK_W), tok(GLA_QK_W)),
        compiler_params=_params(("parallel",)),
        name="inproj",
    )(x, g_mix, mods, w_ext, w_up, b_dec, cs, cos_t, sin_t)


def _split3(x):
    a1 = x.astype(BF16)
    r1 = x - a1.astype(F32)
    a2 = r1.astype(BF16)
    a3 = (r1 - a2.astype(F32)).astype(BF16)
    return a1, a2, a3


def _gla_direction(q_ref, k_ref, v_ref, la_ref, o_ref, s_ref, rev):
    t_rows = q_ref.shape[0]
    c = GLA_CHUNK
    n_sub = t_rows // c
    hk = GLA_QK_W
    la3 = _split3(la_ref[...])
    row = lax.broadcasted_iota(jnp.int32, (t_rows, t_rows), 0)
    col = lax.broadcasted_iota(jnp.int32, (t_rows, t_rows), 1)
    tri = ((row // c) == (col // c)) & ((col >= row) if rev else (col <= row))
    tri = jnp.where(tri, 1.0, 0.0).astype(BF16)
    lc = _dot(tri, la3[0]) + _dot(tri, la3[1]) + _dot(tri, la3[2])
    q = q_ref[...].astype(F32)
    k = k_ref[...].astype(F32)
    qd = (q * jnp.exp(lc) * GLA_SCALE).astype(BF16)
    kd = (k * jnp.exp(-lc)).astype(BF16)
    hrow = lax.broadcasted_iota(jnp.int32, (GLA_HEADS * c, hk), 0) // c
    hcol = lax.broadcasted_iota(jnp.int32, (GLA_HEADS * c, hk), 1) // GLA_DK
    head_mask = hrow == hcol
    trow = lax.broadcasted_iota(jnp.int32, (GLA_HEADS * c, c), 0) % c
    scol = lax.broadcasted_iota(jnp.int32, (GLA_HEADS * c, c), 1)
    causal = (scol >= trow) if rev else (scol <= trow)
    ones = jnp.ones((c, GLA_DV), BF16)
    order = range(n_sub - 1, -1, -1) if rev else range(n_sub)
    for i in order:
        rows = slice(i * c, (i + 1) * c)
        lc_i = lc[rows]
        last = lc_i[0:1] if rev else lc_i[c - 1:c]
        kh = (k[rows] * jnp.exp(last - lc_i)).astype(BF16)
        v_i = v_ref[rows, :]
        q_stack = jnp.where(head_mask, jnp.concatenate([qd[rows]] * GLA_HEADS, axis=0), 0.0).astype(BF16)
        state = s_ref[...]
        inter = _dot(q_stack, state.astype(BF16))
        scores = jnp.where(causal, _dot_nt(q_stack, kd[rows]), 0.0).astype(BF16)
        kv = _dot_tn(kh, v_i)
        dsum = (_dot_tn(la3[0][rows], ones) + _dot_tn(la3[1][rows], ones) + _dot_tn(la3[2][rows], ones))
        decay = jnp.exp(dsum)
        outs = []
        for h in range(GLA_HEADS):
            hr = slice(h * c, (h + 1) * c)
            hv = slice(h * GLA_DV, (h + 1) * GLA_DV)
            outs.append(inter[hr] + _dot(scores[hr], v_i[:, hv]))
            kr = slice(h * GLA_DK, (h + 1) * GLA_DK)
            s_ref[kr, :] = decay[kr] * state[kr] + kv[kr, hv]
        o_ref[rows, :] = jnp.concatenate(outs, axis=1).astype(BF16)


def _gla_kernel(qf_ref, kf_ref, vf_ref, laf_ref, qb_ref, kb_ref, vb_ref, lab_ref,
                of_ref, ob_ref, sf_ref, sb_ref):
    @pl.when(pl.program_id(1) == 0)
    def _():
        sf_ref[...] = jnp.zeros_like(sf_ref)
        sb_ref[...] = jnp.zeros_like(sb_ref)

    _gla_direction(qf_ref, kf_ref, vf_ref, laf_ref, of_ref, sf_ref, False)
    _gla_direction(qb_ref, kb_ref, vb_ref, lab_ref, ob_ref, sb_ref, True)


def _gla(lay, q_a, k_a, v_a, la_f, la_b):
    tm = lay.tile
    nc, nl = lay.ctx_tiles, lay.lat_tiles

    def fwd(b, j):
        return jnp.where(j < nc, lay.n_lat + b * nc + j, b * nl + (j - nc))

    def bwd(b, j):
        return jnp.where(j < nc, lay.n_lat + b * nc + (nc - 1 - j), b * nl + (nl - 1 - (j - nc)))

    spec = lambda w, f: pl.BlockSpec((tm, w), lambda b, j: (f(b, j), 0))
    out = jax.ShapeDtypeStruct((lay.n_tok, GLA_V_W), BF16)
    return pl.pallas_call(
        _gla_kernel,
        out_shape=(out, out),
        grid=(lay.batch, nc + nl),
        in_specs=[spec(GLA_QK_W, fwd), spec(GLA_QK_W, fwd), spec(GLA_V_W, fwd), spec(GLA_QK_W, fwd),
                  spec(GLA_QK_W, bwd), spec(GLA_QK_W, bwd), spec(GLA_V_W, bwd), spec(GLA_QK_W, bwd)],
        out_specs=(spec(GLA_V_W, fwd), spec(GLA_V_W, bwd)),
        scratch_shapes=[pltpu.VMEM((GLA_QK_W, GLA_DV), F32), pltpu.VMEM((GLA_QK_W, GLA_DV), F32)],
        compiler_params=_params(("parallel", "arbitrary")),
        name="gla",
    )(q_a, k_a, v_a, la_f, q_a, k_a, v_a, la_b)


def _dft_kernel(c_ref, s_ref, a_ref, b_ref, o_ref, acc_ref, *, scale):
    k = pl.program_id(1)

    @pl.when(k == 0)
    def _():
        acc_ref[...] = jnp.zeros_like(acc_ref)

    acc_ref[...] += _dot(c_ref[...], a_ref[...]) + _dot(s_ref[...], b_ref[...])

    @pl.when(k == pl.num_programs(1) - 1)
    def _():
        o_ref[...] = (acc_ref[...] * scale).astype(BF16)


def _dft_mats(n):
    i = jnp.arange(n, dtype=jnp.int32)
    ang = ((i[:, None] * i[None, :]) % n).astype(F32) * (2.0 * math.pi / n)
    return jnp.cos(ang).astype(BF16), (-jnp.sin(ang)).astype(BF16)


def _fourier(ua, us, cmat, smat, row0, length, tm, tk):
    width = ua.shape[1]
    off = row0 // tk
    scale = 1.0 / math.sqrt(length * FNET_GROUP_W)
    return pl.pallas_call(
        functools.partial(_dft_kernel, scale=scale),
        out_shape=jax.ShapeDtypeStruct((length, width), BF16),
        grid=(length // tm, length // tk),
        in_specs=[pl.BlockSpec((tm, tk), lambda i, k: (i, k)),
                  pl.BlockSpec((tm, tk), lambda i, k: (i, k)),
                  pl.BlockSpec((tk, width), lambda i, k: (off + k, 0)),
                  pl.BlockSpec((tk, width), lambda i, k: (off + k, 0))],
        out_specs=pl.BlockSpec((tm, width), lambda i, k: (i, 0)),
        scratch_shapes=[pltpu.VMEM((tm, width), F32)],
        compiler_params=_params(("parallel", "arbitrary")),
        name="fourier",
    )(cmat, smat, ua, us)


def _attend(q_ref, k_all, v_all, valid, sink_ref, o_ref):
    hd = SWA_HEAD_DIM
    swap = lambda a: jnp.concatenate([a[:, hd:], a[:, :hd]], axis=1)
    k_sw, v_sw = swap(k_all), swap(v_all)
    lane = lax.broadcasted_iota(jnp.int32, (1, LANES), 1)
    half = (lane < hd, lane >= hd)
    scale = hd ** -0.5
    heads_per_kv = SWA_HEADS // SWA_KV_HEADS
    zero = jnp.zeros((), BF16)
    for p in range(SWA_HEADS // 2):
        qp = q_ref[:, p * LANES:(p + 1) * LANES]
        acc = None
        for par in range(2):
            head = 2 * p + par
            g = head // heads_per_kv
            k_use, v_use = (k_all, v_all) if g == par else (k_sw, v_sw)
            qh = jnp.where(half[par], qp, zero)
            s = _dot_nt(qh, k_use) * scale
            if valid is not None:
                s = jnp.where(valid, s, -jnp.inf)
            sink = sink_ref[head:head + 1, 0:1]
            m = jnp.maximum(jnp.max(s, axis=-1, keepdims=True), sink)
            e = jnp.exp(s - m)
            denom = jnp.sum(e, axis=-1, keepdims=True) + jnp.exp(sink - m)
            prob = (e / denom).astype(BF16)
            part = _dot(prob, jnp.where(half[par], v_use, zero))
            acc = part if acc is None else acc + part
        o_ref[:, p * LANES:(p + 1) * LANES] = acc.astype(BF16)


def _swa_kernel(q_ref, kp_ref, kc_ref, kn_ref, kx_ref, vp_ref, vc_ref, vn_ref, vx_ref, sink_ref, o_ref,
                *, seq):
    n = pl.program_id(1)
    w = WINDOW
    k_all = jnp.concatenate([kp_ref[...], kc_ref[...], kn_ref[...], kx_ref[...]], axis=0)
    v_all = jnp.concatenate([vp_ref[...], vc_ref[...], vn_ref[...], vx_ref[...]], axis=0)
    nk = k_all.shape[0]
    a = lax.broadcasted_iota(jnp.int32, (w, nk), 0)
    j = lax.broadcasted_iota(jnp.int32, (w, nk), 1)
    key_pos = (n - 1) * w + j
    valid = (j >= 3 * w) | ((j >= a) & (j - a <= 2 * w) & (key_pos >= 0) & (key_pos < seq))
    _attend(q_ref, k_all, v_all, valid, sink_ref, o_ref)


def _swa(lay, q_c, k_c, v_c, sink_b):
    w = WINDOW
    nq = lay.seq // w
    ctx_blk = (lay.batch * lay.seq) // lay.ctx

    def kv(f):
        return pl.BlockSpec((w, SWA_KV_W), lambda b, n: (b * nq + f(n), 0))

    prev = lambda n: jnp.maximum(n - 1, 0)
    cur = lambda n: n
    nxt = lambda n: jnp.minimum(n + 1, nq - 1)
    ctx_spec = pl.BlockSpec((lay.ctx, SWA_KV_W), lambda b, n: (ctx_blk + b, 0))
    return pl.pallas_call(
        functools.partial(_swa_kernel, seq=lay.seq),
        out_shape=jax.ShapeDtypeStruct((lay.batch * lay.seq, SWA_Q_W), BF16),
        grid=(lay.batch, nq),
        in_specs=[pl.BlockSpec((w, SWA_Q_W), lambda b, n: (b * nq + n, 0)),
                  kv(prev), kv(cur), kv(nxt), ctx_spec, kv(prev), kv(cur), kv(nxt), ctx_spec,
                  pl.BlockSpec(sink_b.shape, lambda b, n: (0, 0))],
        out_specs=pl.BlockSpec((w, SWA_Q_W), lambda b, n: (b * nq + n, 0)),
        compiler_params=_params(("parallel", "parallel")),
        name="swa",
    )(q_c, k_c, k_c, k_c, k_c, v_c, v_c, v_c, v_c, sink_b)


def _ctx_attn_kernel(q_ref, kx_ref, vx_ref, sink_ref, o_ref):
    _attend(q_ref, kx_ref[...], vx_ref[...], None, sink_ref, o_ref)


def _ctx_attn(lay, q_c, k_c, v_c, sink_b):
    ctx_blk = (lay.batch * lay.seq) // lay.ctx
    spec = lambda wd: pl.BlockSpec((lay.ctx, wd), lambda b: (ctx_blk + b, 0))
    return pl.pallas_call(
        _ctx_attn_kernel,
        out_shape=jax.ShapeDtypeStruct((lay.batch * lay.ctx, SWA_Q_W), BF16),
        grid=(lay.batch,),
        in_specs=[spec(SWA_Q_W), spec(SWA_KV_W), spec(SWA_KV_W),
                  pl.BlockSpec(sink_b.shape, lambda b: (0, 0))],
        out_specs=pl.BlockSpec((lay.ctx, SWA_Q_W), lambda b: (b, 0)),
        compiler_params=_params(("parallel",)),
        name="ctx_attn",
    )(q_c, k_c, v_c, sink_b)


def _merge_kernel(x_ref, of_ref, ob_ref, ra_ref, ybl_ref, ybc_ref, ycl_ref, ycc_ref, gl_ref, mod_ref,
                  ggla_ref, gffn_ref, wpa_ref, wpb_ref, wpc_ref, wout_ref, wr_ref,
                  x1_ref, h2_ref, gates_ref, *, n_lat):
    d = D_MODEL
    is_ctx = pl.program_id(0) >= n_lat
    o = of_ref[...].astype(F32) + ob_ref[...].astype(F32)
    r = ra_ref[...].astype(F32)
    parts = []
    for h in range(GLA_HEADS):
        sl = slice(h * GLA_DV, (h + 1) * GLA_DV)
        oh = o[:, sl]
        parts.append(oh * lax.rsqrt(jnp.mean(oh * oh, axis=-1, keepdims=True) + EPS) * ggla_ref[...])
    y_a = (jnp.concatenate(parts, axis=1) * (r * _sigmoid(r))).astype(BF16)
    y_b = jnp.where(is_ctx, ybc_ref[...], ybl_ref[...])
    y_c = jnp.where(is_ctx, ycc_ref[...], ycl_ref[...])
    gl = gl_ref[...].astype(F32)
    mix = (_sigmoid(gl[:, 0:d]) * _dot(y_a, wpa_ref[...])
           + _sigmoid(gl[:, d:2 * d]) * _dot(y_b, wpb_ref[...])
           + _sigmoid(gl[:, 2 * d:3 * d]) * _dot(y_c, wpc_ref[...]))
    y = _dot(mix.astype(BF16), wout_ref[...])
    x1 = x_ref[...] + mod_ref[:, 2 * d:3 * d] * y
    x1_ref[...] = x1
    h2 = x1 * lax.rsqrt(jnp.mean(x1 * x1, axis=-1, keepdims=True) + EPS) * gffn_ref[...]
    h2 = h2 * (1.0 + mod_ref[:, 4 * d:5 * d]) + mod_ref[:, 3 * d:4 * d]
    h2_ref[...] = h2.astype(BF16)
    logits = jnp.dot(h2, wr_ref[...], preferred_element_type=F32, precision=HIGHEST)
    lane = lax.broadcasted_iota(jnp.int32, logits.shape, 1)
    neg = -jnp.inf
    big = ROUTER_PAD
    is_group = (lane >= MOE_EXPERTS) & (lane < MOE_EXPERTS + MOE_GROUPS)
    gl_m = jnp.where(is_group, logits, neg)
    g_max = jnp.max(gl_m, axis=-1, keepdims=True)
    g_sel = jnp.min(jnp.where(gl_m == g_max, lane, big), axis=-1, keepdims=True) - MOE_EXPERTS
    g_gate = 1.0 / jnp.sum(jnp.where(is_group, jnp.exp(logits - g_max), 0.0), axis=-1, keepdims=True)
    lo = g_sel * MOE_EXPERTS_PER_GROUP
    in_group = (lane >= lo) & (lane < lo + MOE_EXPERTS_PER_GROUP)
    e1 = jnp.where(in_group, logits, neg)
    v1 = jnp.max(e1, axis=-1, keepdims=True)
    i1 = jnp.min(jnp.where(e1 == v1, lane, big), axis=-1, keepdims=True)
    e2 = jnp.where(lane == i1, neg, e1)
    v2 = jnp.max(e2, axis=-1, keepdims=True)
    i2 = jnp.min(jnp.where(e2 == v2, lane, big), axis=-1, keepdims=True)
    t = jnp.exp(v2 - v1)
    w1 = g_gate / (1.0 + t)
    w2 = g_gate * t / (1.0 + t)
    gates_ref[...] = jnp.where(lane == i1, w1, 0.0) + jnp.where(lane == i2, w2, 0.0)


def _merge(lay, x, o_f, o_b, r_a, yb_lat, yb_ctx, yc_lat, yc_ctx, gate_logits, mods, g_gla, g_ffn,
           w_pa, w_pb, w_pc, w_out, w_router):
    tm, d = lay.tile, D_MODEL
    n = lay.n_tok
    tok = lambda w: pl.BlockSpec((tm, w), lambda t: (t, 0))
    const = lambda a: pl.BlockSpec(a.shape, lambda t: (0,) * a.ndim)
    lt, ct = lay.lat_tiles, lay.ctx_tiles
    yb_lat_spec = pl.BlockSpec((tm, FNET_W), lambda t: (jnp.minimum(t, lay.n_lat - 1) % lt,
                                                         jnp.minimum(t, lay.n_lat - 1) // lt))
    yb_ctx_spec = pl.BlockSpec((tm, FNET_W), lambda t: (jnp.maximum(t - lay.n_lat, 0) % ct,
                                                         jnp.maximum(t - lay.n_lat, 0) // ct))
    yc_lat_spec = pl.BlockSpec((tm, SWA_Q_W), lambda t: (jnp.minimum(t, lay.n_lat - 1), 0))
    yc_ctx_spec = pl.BlockSpec((tm, SWA_Q_W), lambda t: (jnp.maximum(t - lay.n_lat, 0), 0))
    return pl.pallas_call(
        functools.partial(_merge_kernel, n_lat=lay.n_lat),
        out_shape=(jax.ShapeDtypeStruct((n, d), F32), jax.ShapeDtypeStruct((n, d), BF16),
                   jax.ShapeDtypeStruct((n, ROUTER_PAD), F32)),
        grid=(lay.n_tiles,),
        in_specs=[tok(d), tok(GLA_V_W), tok(GLA_V_W), tok(GLA_V_W), yb_lat_spec, yb_ctx_spec,
                  yc_lat_spec, yc_ctx_spec, tok(3 * d),
                  pl.BlockSpec((None, 1, N_MOD * d), lambda t: (lay.mod_row(t), 0, 0)),
                  const(g_gla), const(g_ffn), const(w_pa), const(w_pb), const(w_pc), const(w_out),
                  const(w_router)],
        out_specs=(tok(d), tok(d), tok(ROUTER_PAD)),
        compiler_params=_params(("parallel",)),
        name="merge",
    )(x, o_f, o_b, r_a, yb_lat, yb_ctx, yc_lat, yc_ctx, gate_logits, mods, g_gla, g_ffn,
      w_pa, w_pb, w_pc, w_out, w_router)


def _moe_kernel(h_ref, x1_ref, gates_ref, mod_ref, w1_ref, w3_ref, w2_ref, o_ref, acc_ref):
    d = D_MODEL
    e = pl.program_id(1)

    @pl.when(e == 0)
    def _():
        acc_ref[...] = jnp.zeros_like(acc_ref)

    h = h_ref[...]
    a = _dot(h, w1_ref[...])
    hid = (a * _sigmoid(a)) * _dot(h, w3_ref[...])
    y = _dot(hid.astype(BF16), w2_ref[...])
    gates = gates_ref[...]
    lane = lax.broadcasted_iota(jnp.int32, gates.shape, 1)
    g = jnp.sum(jnp.where(lane == e, gates, 0.0), axis=-1, keepdims=True)
    acc_ref[...] += y * g

    @pl.when(e == pl.num_programs(1) - 1)
    def _():
        o_ref[...] = x1_ref[...] + mod_ref[:, 5 * d:6 * d] * acc_ref[...]


def _moe(lay, h2, x1, gates, mods, w1, w3, w2):
    tm, d = MOE_TILE, D_MODEL
    assert lay.seq % tm == 0 and (lay.batch * lay.ctx) % tm == 0
    lat_tiles = lay.seq // tm
    n_lat = lay.batch * lat_tiles
    n_tiles = lay.n_tok // tm
    row = lambda t: jnp.where(t < n_lat, t // lat_tiles, lay.batch)
    tok = lambda w: pl.BlockSpec((tm, w), lambda t, e: (t, 0))
    return pl.pallas_call(
        _moe_kernel,
        out_shape=jax.ShapeDtypeStruct((lay.n_tok, d), F32),
        grid=(n_tiles, MOE_EXPERTS),
        in_specs=[tok(d), tok(d), tok(ROUTER_PAD),
                  pl.BlockSpec((None, 1, N_MOD * d), lambda t, e: (row(t), 0, 0)),
                  pl.BlockSpec((None, d, D_EXPERT), lambda t, e: (e, 0, 0)),
                  pl.BlockSpec((None, d, D_EXPERT), lambda t, e: (e, 0, 0)),
                  pl.BlockSpec((None, D_EXPERT, d), lambda t, e: (e, 0, 0))],
        out_specs=tok(d),
        scratch_shapes=[pltpu.VMEM((tm, d), F32)],
        compiler_params=_params(("parallel", "arbitrary")),
        name="moe",
    )(h2, x1, gates, mods, w1, w3, w2)


def _final_kernel(x_ref, g_ref, o_ref):
    x = x_ref[...]
    o_ref[...] = x * lax.rsqrt(jnp.mean(x * x, axis=-1, keepdims=True) + EPS) * g_ref[...]


def _final_norm(x, g, rows, tile):
    d = x.shape[1]
    return pl.pallas_call(
        _final_kernel,
        out_shape=jax.ShapeDtypeStruct((rows, d), F32),
        grid=(rows // tile,),
        in_specs=[pl.BlockSpec((tile, d), lambda t: (t, 0)), pl.BlockSpec((1, d), lambda t: (0, 0))],
        out_specs=pl.BlockSpec((tile, d), lambda t: (t, 0)),
        compiler_params=_params(("parallel",)),
        name="final_norm",
    )(x, g)


def _rope_tables(seq, ctx):
    pos = jnp.arange(seq, dtype=jnp.int32)
    inv_freq = ROPE_BASE ** (-jnp.arange(0, AXIS_DIM, 2, dtype=F32) / AXIS_DIM)
    ang_row = (pos // GRID_W).astype(F32)[:, None] * inv_freq
    ang_col = (pos % GRID_W).astype(F32)[:, None] * inv_freq
    cos_h = jnp.concatenate([jnp.cos(ang_row)] * 2 + [jnp.cos(ang_col)] * 2, axis=1)
    sin_h = jnp.concatenate([-jnp.sin(ang_row), jnp.sin(ang_row), -jnp.sin(ang_col), jnp.sin(ang_col)], axis=1)
    reps = LANES // SWA_HEAD_DIM
    cos_t = jnp.concatenate([jnp.tile(cos_h, (1, reps)), jnp.ones((ctx, LANES), F32)], axis=0)
    sin_t = jnp.concatenate([jnp.tile(sin_h, (1, reps)), jnp.zeros((ctx, LANES), F32)], axis=0)
    return cos_t, sin_t


def _channel_dft():
    i = jnp.arange(FNET_GROUP_W, dtype=jnp.int32)
    ang = ((i[:, None] * i[None, :]) % FNET_GROUP_W).astype(F32) * (2.0 * math.pi / FNET_GROUP_W)
    return jnp.concatenate([jnp.cos(ang), jnp.sin(ang)], axis=1).astype(BF16)


def kernel(x, c, ctx, c_ctx, w_ada, b_ada, g_mix, g_ffn, w_in, w_decay_down, w_decay_up, b_decay, g_gla, sink,
           w_pa, w_pb, w_pc, w_out, w_router_group, w_router_expert, w1, w3, w2, g_final):
    batch, seq, d = x.shape
    n_ctx = ctx.shape[1]
    depth = w_ada.shape[0]
    assert d == D_MODEL and seq % GRID_W == 0 and seq % n_ctx == 0
    lay = _Layout(batch, seq, n_ctx, TOKEN_TILE)

    rows = -(-(batch + 1) // 8) * 8
    c_all = jnp.zeros((rows, d), F32).at[:batch].set(c).at[batch].set(c_ctx)
    mods_all = _adaln(c_all, w_ada, b_ada).reshape(depth, rows, 1, N_MOD * d)

    cos_t, sin_t = _rope_tables(seq, n_ctx)
    cs = _channel_dft()
    c_lat, s_lat = _dft_mats(seq)
    c_ctx_m, s_ctx_m = _dft_mats(n_ctx)

    xs = jnp.concatenate([x.reshape(batch * seq, d), ctx.reshape(batch * n_ctx, d)], axis=0)
    rank = w_decay_down.shape[-1]
    for l in range(depth):
        last = l == depth - 1
        mods = mods_all[l]
        down = jnp.concatenate([w_decay_down[l, 0], w_decay_down[l, 1]], axis=1)
        down = jnp.pad(down, ((0, 0), (0, DECAY_PAD - 2 * rank)))
        w_ext = jnp.concatenate([w_in[l], down], axis=1).astype(BF16)
        w_up = jnp.zeros((DECAY_PAD, 2 * GLA_QK_W), F32)
        w_up = w_up.at[:rank, :GLA_QK_W].set(w_decay_up[l, 0]).at[rank:2 * rank, GLA_QK_W:].set(w_decay_up[l, 1])
        b_dec = b_decay[l].reshape(1, 2 * GLA_QK_W)
        (k_a, v_a, k_c, v_c, q_a, r_a, u_cos, u_sin, q_c, gate_logits, la_f, la_b) = _inproj(
            lay, xs, g_mix[l].reshape(1, d), mods, w_ext, w_up.astype(BF16), b_dec, cs, cos_t, sin_t)

        o_f, o_b = _gla(lay, q_a, k_a, v_a, la_f, la_b)
        yb_lat = _fourier(u_cos, u_sin, c_lat, s_lat, 0, seq, min(seq, 1024), min(seq, 512))
        yc_lat = _swa(lay, q_c, k_c, v_c, jnp.broadcast_to(sink[l][:, None], (SWA_HEADS, LANES)))
        if last:
            yb_ctx = jnp.zeros((n_ctx, batch * FNET_W), BF16)
            yc_ctx = jnp.zeros((batch * n_ctx, SWA_Q_W), BF16)
        else:
            yb_ctx = _fourier(u_cos, u_sin, c_ctx_m, s_ctx_m, seq, n_ctx, n_ctx, n_ctx)
            yc_ctx = _ctx_attn(lay, q_c, k_c, v_c, jnp.broadcast_to(sink[l][:, None], (SWA_HEADS, LANES)))

        w_router = jnp.zeros((d, ROUTER_PAD), F32)
        w_router = w_router.at[:, :MOE_EXPERTS].set(w_router_expert[l])
        w_router = w_router.at[:, MOE_EXPERTS:MOE_EXPERTS + MOE_GROUPS].set(w_router_group[l])
        x1, h2, gates = _merge(lay, xs, o_f, o_b, r_a, yb_lat, yb_ctx, yc_lat, yc_ctx, gate_logits, mods,
                               g_gla[l].reshape(1, GLA_DV), g_ffn[l].reshape(1, d),
                               w_pa[l].astype(BF16), w_pb[l].astype(BF16), w_pc[l].astype(BF16),
                               w_out[l].astype(BF16), w_router)
        xs = _moe(lay, h2, x1, gates, mods, w1[l].astype(BF16), w3[l].astype(BF16), w2[l].astype(BF16))

    out = _final_norm(xs, g_final.reshape(1, d), batch * seq, TOKEN_TILE)
    return out.reshape(batch, seq, d)
```

```python
import functools
import math

import jax
import jax.numpy as jnp
from jax import lax
from jax.experimental import pallas as pl
from jax.experimental.pallas import tpu as pltpu

F32 = jnp.float32
BF16 = jnp.bfloat16
HIGHEST = lax.Precision.HIGHEST

D_MODEL = 1024
GRID_W = 64
EPS = 1e-6
N_MOD = 6
GLA_HEADS = 4
GLA_DK = 64
GLA_DV = 128
GLA_TAU = 16.0
GLA_CHUNK = 64
GLA_SCALE = GLA_DK ** -0.5
FNET_GROUPS = 4
FNET_GROUP_W = 128
SWA_HEADS = 8
SWA_KV_HEADS = 2
SWA_HEAD_DIM = 64
WINDOW = 128
ROPE_BASE = 10000.0
AXIS_DIM = SWA_HEAD_DIM // 2
MOE_GROUPS = 4
MOE_EXPERTS_PER_GROUP = 4
MOE_EXPERTS = MOE_GROUPS * MOE_EXPERTS_PER_GROUP
MOE_TOPK = 2
D_EXPERT = 512

GLA_QK_W = GLA_HEADS * GLA_DK
GLA_V_W = GLA_HEADS * GLA_DV
FNET_W = FNET_GROUPS * FNET_GROUP_W
SWA_Q_W = SWA_HEADS * SWA_HEAD_DIM
SWA_KV_W = SWA_KV_HEADS * SWA_HEAD_DIM
IN_SIZES = (GLA_QK_W, GLA_V_W, SWA_KV_W, SWA_KV_W, GLA_QK_W, GLA_V_W, FNET_W, SWA_Q_W, 3 * D_MODEL)
IN_OFFS = tuple(int(sum(IN_SIZES[:i])) for i in range(len(IN_SIZES) + 1))
IN_COLS = IN_OFFS[-1]

LANES = 128
TOKEN_TILE = 256
DECAY_PAD = LANES
ROUTER_PAD = LANES
ROUTE_ID = 0
ROUTE_W = 2
DISPATCH_TILE = 512
ROW_CHUNK = 16
FFN_BLOCK = 256
SLOT_ROWS = -(-(MOE_TOPK * DISPATCH_TILE + MOE_EXPERTS * (ROW_CHUNK - 1)) // LANES) * LANES
VMEM_LIMIT = 56 * 1024 * 1024


def _params(sem, vmem=VMEM_LIMIT):
    return pltpu.CompilerParams(dimension_semantics=sem, vmem_limit_bytes=vmem)


def _sigmoid(x):
    return 1.0 / (1.0 + jnp.exp(-x))


def _dot(a, b):
    return jnp.dot(a, b, preferred_element_type=F32)


def _dot_nt(a, b):
    return lax.dot_general(a, b, (((1,), (1,)), ((), ())), preferred_element_type=F32)


def _dot_tn(a, b):
    return lax.dot_general(a, b, (((0,), (0,)), ((), ())), preferred_element_type=F32)


def _ada_kernel(c_ref, w_ref, b_ref, o_ref):
    c = c_ref[...]
    a = c * _sigmoid(c)
    o_ref[...] = jnp.dot(a, w_ref[...], preferred_element_type=F32, precision=HIGHEST) + b_ref[...]


def _adaln(c_all, w_ada, b_ada):
    depth, d, n = w_ada.shape
    rows = c_all.shape[0]
    tn = 1536
    return pl.pallas_call(
        _ada_kernel,
        out_shape=jax.ShapeDtypeStruct((depth, rows, n), F32),
        grid=(depth, n // tn),
        in_specs=[pl.BlockSpec((rows, d), lambda l, j: (0, 0)),
                  pl.BlockSpec((None, d, tn), lambda l, j: (l, 0, j)),
                  pl.BlockSpec((None, 1, tn), lambda l, j: (l, 0, j))],
        out_specs=pl.BlockSpec((None, rows, tn), lambda l, j: (l, 0, j)),
        compiler_params=_params(("parallel", "parallel")),
        name="adaln",
    )(c_all, w_ada, b_ada.reshape(depth, 1, n))


class _Layout:
    def __init__(self, batch, seq, ctx, tile):
        assert seq % tile == 0 and ctx % tile == 0
        self.batch, self.seq, self.ctx, self.tile = batch, seq, ctx, tile
        self.lat_tiles = seq // tile
        self.ctx_tiles = ctx // tile
        self.n_lat = batch * self.lat_tiles
        self.n_tiles = self.n_lat + batch * self.ctx_tiles
        self.n_tok = self.n_tiles * tile

    def batch_of(self, t):
        return jnp.where(t < self.n_lat, t // self.lat_tiles, (t - self.n_lat) // self.ctx_tiles)

    def mod_row(self, t):
        return jnp.where(t < self.n_lat, t // self.lat_tiles, self.batch)

    def seq_tile(self, t):
        return jnp.where(t < self.n_lat, t % self.lat_tiles,
                         self.lat_tiles + (t - self.n_lat) % self.ctx_tiles)


def _rope(x, cos, sin_signed):
    n = x.shape[-1]
    lane = lax.broadcasted_iota(jnp.int32, x.shape, 1)
    half = AXIS_DIM // 2
    partner = jnp.where((lane & half) == 0, pltpu.roll(x, n - half, 1), pltpu.roll(x, half, 1))
    return x * cos + partner * sin_signed


def _inproj_kernel(x_ref, g_ref, mod_ref, w_ref, wu_ref, bdec_ref, cs_ref, cos_ref, sin_ref,
                   ka_ref, va_ref, kc_ref, vc_ref, qa_ref, ra_ref, ua_ref, us_ref, qc_ref, gl_ref,
                   laf_ref, lab_ref):
    d = D_MODEL
    x = x_ref[...]
    shift = mod_ref[:, 0:d]
    scale = mod_ref[:, d:2 * d]
    h = x * lax.rsqrt(jnp.mean(x * x, axis=-1, keepdims=True) + EPS) * g_ref[...]
    hb = (h * (1.0 + scale) + shift).astype(BF16)

    def proj(i):
        return _dot(hb, w_ref[:, IN_OFFS[i]:IN_OFFS[i + 1]])

    ka_ref[...] = proj(0).astype(BF16)
    va_ref[...] = proj(1).astype(BF16)
    cos = cos_ref[...]
    sin = sin_ref[...]
    kc_ref[...] = _rope(proj(2), cos, sin).astype(BF16)
    vc_ref[...] = proj(3).astype(BF16)
    qa_ref[...] = proj(4).astype(BF16)
    ra_ref[...] = proj(5).astype(BF16)
    u = proj(6).astype(BF16)
    for g in range(FNET_GROUPS):
        sl = slice(g * FNET_GROUP_W, (g + 1) * FNET_GROUP_W)
        ab = _dot(u[:, sl], cs_ref[...])
        ua_ref[:, sl] = ab[:, :FNET_GROUP_W].astype(BF16)
        us_ref[:, sl] = ab[:, FNET_GROUP_W:].astype(BF16)
    reps = SWA_Q_W // LANES
    qc_ref[...] = (_rope(proj(7), jnp.concatenate([cos] * reps, axis=1), jnp.concatenate([sin] * reps, axis=1))
                   * SWA_HEAD_DIM ** -0.5).astype(BF16)
    gl_ref[...] = proj(8).astype(BF16)
    low = _dot(hb, w_ref[:, IN_COLS:IN_COLS + DECAY_PAD]).astype(BF16)
    z = _dot(low, wu_ref[...]) + bdec_ref[...]
    la = (jnp.minimum(z, 0.0) - jnp.log(1.0 + jnp.exp(-jnp.abs(z)))) * (1.0 / GLA_TAU)
    laf_ref[...] = la[:, :GLA_QK_W]
    lab_ref[...] = la[:, GLA_QK_W:]


def _inproj(lay, x, g_mix, mods, w_ext, w_up, b_dec, cs, cos_t, sin_t):
    tm, d = lay.tile, D_MODEL
    n = lay.n_tok
    bf = lambda w: jax.ShapeDtypeStruct((n, w), BF16)
    tok = lambda w: pl.BlockSpec((tm, w), lambda t: (t, 0))
    const = lambda a: pl.BlockSpec(a.shape, lambda t: (0,) * a.ndim)
    seq_rows = lay.seq + lay.ctx
    fnet_shape = jax.ShapeDtypeStruct((seq_rows, lay.batch * FNET_W), BF16)
    fnet_spec = pl.BlockSpec((tm, FNET_W), lambda t: (lay.seq_tile(t), lay.batch_of(t)))
    pos_spec = pl.BlockSpec((tm, LANES), lambda t: (lay.seq_tile(t), 0))
    return pl.pallas_call(
        _inproj_kernel,
        out_shape=(bf(GLA_QK_W), bf(GLA_V_W), bf(SWA_KV_W), bf(SWA_KV_W), bf(GLA_QK_W), bf(GLA_V_W),
                   fnet_shape, fnet_shape, bf(SWA_Q_W), bf(3 * d),
                   jax.ShapeDtypeStruct((n, GLA_QK_W), F32), jax.ShapeDtypeStruct((n, GLA_QK_W), F32)),
        grid=(lay.n_tiles,),
        in_specs=[tok(d), const(g_mix),
                  pl.BlockSpec((None, 1, N_MOD * d), lambda t: (lay.mod_row(t), 0, 0)),
                  const(w_ext), const(w_up), const(b_dec), const(cs), pos_spec, pos_spec],
        out_specs=(tok(GLA_QK_W), tok(GLA_V_W), tok(SWA_KV_W), tok(SWA_KV_W), tok(GLA_QK_W), tok(GLA_V_W),
                   fnet_spec, fnet_spec, tok(SWA_Q_W), tok(3 * d), tok(GLA_QK_W), tok(GLA_QK_W)),
        compiler_params=_params(("parallel",)),
        name="inproj",
    )(x, g_mix, mods, w_ext, w_up, b_dec, cs, cos_t, sin_t)


def _split3(x):
    a1 = x.astype(BF16)
    r1 = x - a1.astype(F32)
    a2 = r1.astype(BF16)
    a3 = (r1 - a2.astype(F32)).astype(BF16)
    return a1, a2, a3


def _gla_direction(q_ref, k_ref, v_ref, la_ref, o_ref, s_ref, rev):
    t_rows = q_ref.shape[0]
    c = GLA_CHUNK
    n_sub = t_rows // c
    hk = GLA_QK_W
    la3 = _split3(la_ref[...])
    row = lax.broadcasted_iota(jnp.int32, (t_rows, t_rows), 0)
    col = lax.broadcasted_iota(jnp.int32, (t_rows, t_rows), 1)
    tri = ((row // c) == (col // c)) & ((col >= row) if rev else (col <= row))
    tri = jnp.where(tri, 1.0, 0.0).astype(BF16)
    lc = _dot(tri, la3[0]) + _dot(tri, la3[1]) + _dot(tri, la3[2])
    q = q_ref[...].astype(F32)
    k = k_ref[...].astype(F32)
    qd = (q * jnp.exp(lc) * GLA_SCALE).astype(BF16)
    kd = (k * jnp.exp(-lc)).astype(BF16)
    hrow = lax.broadcasted_iota(jnp.int32, (GLA_HEADS * c, hk), 0) // c
    hcol = lax.broadcasted_iota(jnp.int32, (GLA_HEADS * c, hk), 1) // GLA_DK
    head_mask = hrow == hcol
    trow = lax.broadcasted_iota(jnp.int32, (GLA_HEADS * c, c), 0) % c
    scol = lax.broadcasted_iota(jnp.int32, (GLA_HEADS * c, c), 1)
    causal = (scol >= trow) if rev else (scol <= trow)
    ones = jnp.ones((c, GLA_DV), BF16)
    order = range(n_sub - 1, -1, -1) if rev else range(n_sub)
    for i in order:
        rows = slice(i * c, (i + 1) * c)
        lc_i = lc[rows]
        last = lc_i[0:1] if rev else lc_i[c - 1:c]
        kh = (k[rows] * jnp.exp(last - lc_i)).astype(BF16)
        v_i = v_ref[rows, :]
        q_stack = jnp.where(head_mask, jnp.concatenate([qd[rows]] * GLA_HEADS, axis=0), 0.0).astype(BF16)
        state = s_ref[...]
        inter = _dot(q_stack, state.astype(BF16))
        scores = jnp.where(causal, _dot_nt(q_stack, kd[rows]), 0.0).astype(BF16)
        kv = _dot_tn(kh, v_i)
        dsum = (_dot_tn(la3[0][rows], ones) + _dot_tn(la3[1][rows], ones) + _dot_tn(la3[2][rows], ones))
        decay = jnp.exp(dsum)
        outs = []
        for h in range(GLA_HEADS):
            hr = slice(h * c, (h + 1) * c)
            hv = slice(h * GLA_DV, (h + 1) * GLA_DV)
            outs.append(inter[hr] + _dot(scores[hr], v_i[:, hv]))
            kr = slice(h * GLA_DK, (h + 1) * GLA_DK)
            s_ref[kr, :] = decay[kr] * state[kr] + kv[kr, hv]
        o_ref[rows, :] = jnp.concatenate(outs, axis=1).astype(BF16)


def _gla_kernel(qf_ref, kf_ref, vf_ref, laf_ref, qb_ref, kb_ref, vb_ref, lab_ref,
                of_ref, ob_ref, sf_ref, sb_ref):
    @pl.when(pl.program_id(1) == 0)
    def _():
        sf_ref[...] = jnp.zeros_like(sf_ref)
        sb_ref[...] = jnp.zeros_like(sb_ref)

    _gla_direction(qf_ref, kf_ref, vf_ref, laf_ref, of_ref, sf_ref, False)
    _gla_direction(qb_ref, kb_ref, vb_ref, lab_ref, ob_ref, sb_ref, True)


def _gla(lay, q_a, k_a, v_a, la_f, la_b):
    tm = lay.tile
    nc, nl = lay.ctx_tiles, lay.lat_tiles

    def fwd(b, j):
        return jnp.where(j < nc, lay.n_lat + b * nc + j, b * nl + (j - nc))

    def bwd(b, j):
        return jnp.where(j < nc, lay.n_lat + b * nc + (nc - 1 - j), b * nl + (nl - 1 - (j - nc)))

    spec = lambda w, f: pl.BlockSpec((tm, w), lambda b, j: (f(b, j), 0))
    out = jax.ShapeDtypeStruct((lay.n_tok, GLA_V_W), BF16)
    return pl.pallas_call(
        _gla_kernel,
        out_shape=(out, out),
        grid=(lay.batch, nc + nl),
        in_specs=[spec(GLA_QK_W, fwd), spec(GLA_QK_W, fwd), spec(GLA_V_W, fwd), spec(GLA_QK_W, fwd),
                  spec(GLA_QK_W, bwd), spec(GLA_QK_W, bwd), spec(GLA_V_W, bwd), spec(GLA_QK_W, bwd)],
        out_specs=(spec(GLA_V_W, fwd), spec(GLA_V_W, bwd)),
        scratch_shapes=[pltpu.VMEM((GLA_QK_W, GLA_DV), F32), pltpu.VMEM((GLA_QK_W, GLA_DV), F32)],
        compiler_params=_params(("parallel", "arbitrary")),
        name="gla",
    )(q_a, k_a, v_a, la_f, q_a, k_a, v_a, la_b)


def _dft_kernel(c_ref, s_ref, a_ref, b_ref, o_ref, acc_ref, *, scale):
    k = pl.program_id(1)

    @pl.when(k == 0)
    def _():
        acc_ref[...] = jnp.zeros_like(acc_ref)

    acc_ref[...] += _dot(c_ref[...], a_ref[...]) + _dot(s_ref[...], b_ref[...])

    @pl.when(k == pl.num_programs(1) - 1)
    def _():
        o_ref[...] = (acc_ref[...] * scale).astype(BF16)


def _dft_mats(n):
    i = jnp.arange(n, dtype=jnp.int32)
    ang = ((i[:, None] * i[None, :]) % n).astype(F32) * (2.0 * math.pi / n)
    return jnp.cos(ang).astype(BF16), (-jnp.sin(ang)).astype(BF16)


def _fourier(ua, us, cmat, smat, row0, length, tm, tk):
    width = ua.shape[1]
    off = row0 // tk
    scale = 1.0 / math.sqrt(length * FNET_GROUP_W)
    return pl.pallas_call(
        functools.partial(_dft_kernel, scale=scale),
        out_shape=jax.ShapeDtypeStruct((length, width), BF16),
        grid=(length // tm, length // tk),
        in_specs=[pl.BlockSpec((tm, tk), lambda i, k: (i, k)),
                  pl.BlockSpec((tm, tk), lambda i, k: (i, k)),
                  pl.BlockSpec((tk, width), lambda i, k: (off + k, 0)),
                  pl.BlockSpec((tk, width), lambda i, k: (off + k, 0))],
        out_specs=pl.BlockSpec((tm, width), lambda i, k: (i, 0)),
        scratch_shapes=[pltpu.VMEM((tm, width), F32)],
        compiler_params=_params(("parallel", "arbitrary")),
        name="fourier",
    )(cmat, smat, ua, us)


def _attend(q_ref, k_all, v_all, valid, sink_ref, o_ref, s_ref):
    hd = SWA_HEAD_DIM
    swap = lambda a: jnp.concatenate([a[:, hd:], a[:, :hd]], axis=1)
    k_sw, v_sw = swap(k_all), swap(v_all)
    lane = lax.broadcasted_iota(jnp.int32, (1, LANES), 1)
    half = (lane < hd, lane >= hd)
    heads_per_kv = SWA_HEADS // SWA_KV_HEADS
    zero = jnp.zeros((), BF16)
    k_for = lambda head: k_all if head // heads_per_kv == head % 2 else k_sw
    v_for = lambda head: v_all if head // heads_per_kv == head % 2 else v_sw
    row_max = []
    for head in range(SWA_HEADS):
        p, par = divmod(head, 2)
        qh = jnp.where(half[par], q_ref[:, p * LANES:(p + 1) * LANES], zero)
        s = _dot_nt(qh, k_for(head))
        if valid is not None:
            kw = valid.shape[1]
            s = jnp.concatenate([jnp.where(valid, s[:, :kw], -jnp.inf), s[:, kw:]], axis=1)
        s_ref[head] = s
        row_max.append(jnp.maximum(jnp.max(s, axis=-1, keepdims=True), sink_ref[head:head + 1, 0:1]))
    for p in range(SWA_HEADS // 2):
        acc = None
        for par in range(2):
            head = 2 * p + par
            m = row_max[head]
            e = jnp.exp(s_ref[head] - m)
            denom = jnp.sum(e, axis=-1, keepdims=True) + jnp.exp(sink_ref[head:head + 1, 0:1] - m)
            part = _dot(e.astype(BF16), jnp.where(half[par], v_for(head), zero)) * (1.0 / denom)
            acc = part if acc is None else acc + part
        o_ref[:, p * LANES:(p + 1) * LANES] = acc.astype(BF16)


def _swa_kernel(q_ref, kp_ref, kc_ref, kn_ref, kx_ref, vp_ref, vc_ref, vn_ref, vx_ref, sink_ref, o_ref,
                s_ref, *, seq):
    n = pl.program_id(1)
    w = WINDOW
    k_all = jnp.concatenate([kp_ref[...], kc_ref[...], kn_ref[...], kx_ref[...]], axis=0)
    v_all = jnp.concatenate([vp_ref[...], vc_ref[...], vn_ref[...], vx_ref[...]], axis=0)
    a = lax.broadcasted_iota(jnp.int32, (w, 3 * w), 0)
    j = lax.broadcasted_iota(jnp.int32, (w, 3 * w), 1)
    key_pos = (n - 1) * w + j
    valid = (j >= a) & (j - a <= 2 * w) & (key_pos >= 0) & (key_pos < seq)
    _attend(q_ref, k_all, v_all, valid, sink_ref, o_ref, s_ref)


def _swa(lay, q_c, k_c, v_c, sink_b):
    w = WINDOW
    nq = lay.seq // w
    ctx_blk = (lay.batch * lay.seq) // lay.ctx

    def kv(f):
        return pl.BlockSpec((w, SWA_KV_W), lambda b, n: (b * nq + f(n), 0))

    prev = lambda n: jnp.maximum(n - 1, 0)
    cur = lambda n: n
    nxt = lambda n: jnp.minimum(n + 1, nq - 1)
    ctx_spec = pl.BlockSpec((lay.ctx, SWA_KV_W), lambda b, n: (ctx_blk + b, 0))
    return pl.pallas_call(
        functools.partial(_swa_kernel, seq=lay.seq),
        out_shape=jax.ShapeDtypeStruct((lay.batch * lay.seq, SWA_Q_W), BF16),
        grid=(lay.batch, nq),
        in_specs=[pl.BlockSpec((w, SWA_Q_W), lambda b, n: (b * nq + n, 0)),
                  kv(prev), kv(cur), kv(nxt), ctx_spec, kv(prev), kv(cur), kv(nxt), ctx_spec,
                  pl.BlockSpec(sink_b.shape, lambda b, n: (0, 0))],
        out_specs=pl.BlockSpec((w, SWA_Q_W), lambda b, n: (b * nq + n, 0)),
        scratch_shapes=[pltpu.VMEM((SWA_HEADS, w, 3 * w + lay.ctx), F32)],
        compiler_params=_params(("parallel", "parallel")),
        name="swa",
    )(q_c, k_c, k_c, k_c, k_c, v_c, v_c, v_c, v_c, sink_b)


def _ctx_attn_kernel(q_ref, kx_ref, vx_ref, sink_ref, o_ref, s_ref):
    _attend(q_ref, kx_ref[...], vx_ref[...], None, sink_ref, o_ref, s_ref)


def _ctx_attn(lay, q_c, k_c, v_c, sink_b):
    ctx_blk = (lay.batch * lay.seq) // lay.ctx
    spec = lambda wd: pl.BlockSpec((lay.ctx, wd), lambda b: (ctx_blk + b, 0))
    return pl.pallas_call(
        _ctx_attn_kernel,
        out_shape=jax.ShapeDtypeStruct((lay.batch * lay.ctx, SWA_Q_W), BF16),
        grid=(lay.batch,),
        in_specs=[spec(SWA_Q_W), spec(SWA_KV_W), spec(SWA_KV_W),
                  pl.BlockSpec(sink_b.shape, lambda b: (0, 0))],
        out_specs=pl.BlockSpec((lay.ctx, SWA_Q_W), lambda b: (b, 0)),
        scratch_shapes=[pltpu.VMEM((SWA_HEADS, lay.ctx, lay.ctx), F32)],
        compiler_params=_params(("parallel",)),
        name="ctx_attn",
    )(q_c, k_c, v_c, sink_b)


def _merge_kernel(x_ref, of_ref, ob_ref, ra_ref, ybl_ref, ybc_ref, ycl_ref, ycc_ref, gl_ref, mod_ref,
                  ggla_ref, gffn_ref, wpa_ref, wpb_ref, wpc_ref, wout_ref, wr_ref,
                  x1_ref, h2_ref, gates_ref, *, n_lat):
    d = D_MODEL
    is_ctx = pl.program_id(0) >= n_lat
    o = of_ref[...].astype(F32) + ob_ref[...].astype(F32)
    r = ra_ref[...].astype(F32)
    parts = []
    for h in range(GLA_HEADS):
        sl = slice(h * GLA_DV, (h + 1) * GLA_DV)
        oh = o[:, sl]
        parts.append(oh * lax.rsqrt(jnp.mean(oh * oh, axis=-1, keepdims=True) + EPS) * ggla_ref[...])
    y_a = (jnp.concatenate(parts, axis=1) * (r * _sigmoid(r))).astype(BF16)
    y_b = jnp.where(is_ctx, ybc_ref[...], ybl_ref[...])
    y_c = jnp.where(is_ctx, ycc_ref[...], ycl_ref[...])
    gl = gl_ref[...].astype(F32)
    mix = (_sigmoid(gl[:, 0:d]) * _dot(y_a, wpa_ref[...])
           + _sigmoid(gl[:, d:2 * d]) * _dot(y_b, wpb_ref[...])
           + _sigmoid(gl[:, 2 * d:3 * d]) * _dot(y_c, wpc_ref[...]))
    y = _dot(mix.astype(BF16), wout_ref[...])
    x1 = x_ref[...] + mod_ref[:, 2 * d:3 * d] * y
    x1_ref[...] = x1
    h2 = x1 * lax.rsqrt(jnp.mean(x1 * x1, axis=-1, keepdims=True) + EPS) * gffn_ref[...]
    h2 = h2 * (1.0 + mod_ref[:, 4 * d:5 * d]) + mod_ref[:, 3 * d:4 * d]
    h2_ref[...] = h2.astype(BF16)
    logits = jnp.dot(h2, wr_ref[...], preferred_element_type=F32, precision=HIGHEST)
    lane = lax.broadcasted_iota(jnp.int32, logits.shape, 1)
    neg = -jnp.inf
    big = ROUTER_PAD
    is_group = (lane >= MOE_EXPERTS) & (lane < MOE_EXPERTS + MOE_GROUPS)
    gl_m = jnp.where(is_group, logits, neg)
    g_max = jnp.max(gl_m, axis=-1, keepdims=True)
    g_sel = jnp.min(jnp.where(gl_m == g_max, lane, big), axis=-1, keepdims=True) - MOE_EXPERTS
    g_gate = 1.0 / jnp.sum(jnp.where(is_group, jnp.exp(logits - g_max), 0.0), axis=-1, keepdims=True)
    lo = g_sel * MOE_EXPERTS_PER_GROUP
    in_group = (lane >= lo) & (lane < lo + MOE_EXPERTS_PER_GROUP)
    e1 = jnp.where(in_group, logits, neg)
    v1 = jnp.max(e1, axis=-1, keepdims=True)
    i1 = jnp.min(jnp.where(e1 == v1, lane, big), axis=-1, keepdims=True)
    e2 = jnp.where(lane == i1, neg, e1)
    v2 = jnp.max(e2, axis=-1, keepdims=True)
    i2 = jnp.min(jnp.where(e2 == v2, lane, big), axis=-1, keepdims=True)
    t = jnp.exp(v2 - v1)
    w1 = g_gate / (1.0 + t)
    w2 = g_gate * t / (1.0 + t)
    route = jnp.where(lane == ROUTE_ID, i1.astype(F32), 0.0) + jnp.where(lane == ROUTE_ID + 1, i2.astype(F32), 0.0)
    route = route + jnp.where(lane == ROUTE_W, w1, 0.0) + jnp.where(lane == ROUTE_W + 1, w2, 0.0)
    gates_ref[...] = route


def _merge(lay, x, o_f, o_b, r_a, yb_lat, yb_ctx, yc_lat, yc_ctx, gate_logits, mods, g_gla, g_ffn,
           w_pa, w_pb, w_pc, w_out, w_router):
    tm, d = lay.tile, D_MODEL
    n = lay.n_tok
    tok = lambda w: pl.BlockSpec((tm, w), lambda t: (t, 0))
    const = lambda a: pl.BlockSpec(a.shape, lambda t: (0,) * a.ndim)
    lt, ct = lay.lat_tiles, lay.ctx_tiles
    yb_lat_spec = pl.BlockSpec((tm, FNET_W), lambda t: (jnp.minimum(t, lay.n_lat - 1) % lt,
                                                         jnp.minimum(t, lay.n_lat - 1) // lt))
    yb_ctx_spec = pl.BlockSpec((tm, FNET_W), lambda t: (jnp.maximum(t - lay.n_lat, 0) % ct,
                                                         jnp.maximum(t - lay.n_lat, 0) // ct))
    yc_lat_spec = pl.BlockSpec((tm, SWA_Q_W), lambda t: (jnp.minimum(t, lay.n_lat - 1), 0))
    yc_ctx_spec = pl.BlockSpec((tm, SWA_Q_W), lambda t: (jnp.maximum(t - lay.n_lat, 0), 0))
    return pl.pallas_call(
        functools.partial(_merge_kernel, n_lat=lay.n_lat),
        out_shape=(jax.ShapeDtypeStruct((n, d), F32), jax.ShapeDtypeStruct((n, d), BF16),
                   jax.ShapeDtypeStruct((n, ROUTER_PAD), F32)),
        grid=(lay.n_tiles,),
        in_specs=[tok(d), tok(GLA_V_W), tok(GLA_V_W), tok(GLA_V_W), yb_lat_spec, yb_ctx_spec,
                  yc_lat_spec, yc_ctx_spec, tok(3 * d),
                  pl.BlockSpec((None, 1, N_MOD * d), lambda t: (lay.mod_row(t), 0, 0)),
                  const(g_gla), const(g_ffn), const(w_pa), const(w_pb), const(w_pc), const(w_out),
                  const(w_router)],
        out_specs=(tok(d), tok(d), tok(ROUTER_PAD)),
        compiler_params=_params(("parallel",)),
        name="merge",
    )(x, o_f, o_b, r_a, yb_lat, yb_ctx, yc_lat, yc_ctx, gate_logits, mods, g_gla, g_ffn,
      w_pa, w_pb, w_pc, w_out, w_router)


def _route_ids(route, axis):
    take = (lambda i: route[:, i:i + 1]) if axis == 1 else (lambda i: route[i:i + 1, :])
    return take(ROUTE_ID).astype(jnp.int32), take(ROUTE_ID + 1).astype(jnp.int32)


def _count_kernel(route_ref, o_ref):
    route = route_ref[...]
    lane = lax.broadcasted_iota(jnp.int32, route.shape, 1)
    e1, e2 = _route_ids(route, 1)
    hit = jnp.where((lane == e1) | (lane == e2), 1.0, 0.0)
    o_ref[...] = jnp.sum(hit, axis=0, keepdims=True).astype(jnp.int32)


def _expert_counts(route, tile):
    n_tiles = route.shape[0] // tile
    return pl.pallas_call(
        _count_kernel,
        out_shape=jax.ShapeDtypeStruct((n_tiles, 1, ROUTER_PAD), jnp.int32),
        grid=(n_tiles,),
        in_specs=[pl.BlockSpec((tile, ROUTER_PAD), lambda t: (t, 0))],
        out_specs=pl.BlockSpec((None, 1, ROUTER_PAD), lambda t: (t, 0, 0)),
        compiler_params=_params(("parallel",)),
        name="moe_count",
    )(route)


def _moe_plan(counts, n_blocks):
    cnt = counts[:, 0, :MOE_EXPERTS]
    pc = (cnt + ROW_CHUNK - 1) // ROW_CHUNK * ROW_CHUNK
    lstart = jnp.cumsum(pc, axis=1) - pc
    tot = jnp.sum(pc, axis=0)
    tot_pad = (tot + FFN_BLOCK - 1) // FFN_BLOCK * FFN_BLOCK
    eend = jnp.cumsum(tot_pad)
    estart = eend - tot_pad
    base = estart[None, :] + jnp.cumsum(pc, axis=0) - pc
    n_used = eend[-1] // FFN_BLOCK
    blk = jnp.minimum(jnp.arange(n_blocks, dtype=jnp.int32), n_used - 1)
    bexp = jnp.sum((blk[:, None] * FFN_BLOCK >= eend[None, :]).astype(jnp.int32), axis=1)
    flat = lambda a: a.reshape(-1).astype(jnp.int32)
    return dict(base=flat(base), lstart=flat(lstart), nch=flat(pc // ROW_CHUNK),
                gap0=flat(estart + tot), gapn=flat((tot_pad - tot) // ROW_CHUNK),
                bexp=flat(bexp), nused=flat(n_used))


def _segment_copies(t, base_ref, lstart_ref, nch_ref, make, start):
    def per_expert(e, carry):
        idx = t * MOE_EXPERTS + e
        loc = lstart_ref[idx]
        glob = base_ref[idx]

        def chunk(i, c):
            cp = make(pl.multiple_of(loc + i * ROW_CHUNK, ROW_CHUNK), pl.multiple_of(glob + i * ROW_CHUNK, ROW_CHUNK))
            cp.start() if start else cp.wait()
            return c

        return lax.fori_loop(0, nch_ref[idx], chunk, carry)

    lax.fori_loop(0, MOE_EXPERTS, per_expert, 0)


def _dispatch_kernel(base_ref, lstart_ref, nch_ref, gap0_ref, gapn_ref, nused_ref, h_ref, route_ref, xs_ref,
                     buf_ref, sem):
    t = pl.program_id(0)
    tile = h_ref.shape[0]
    slots = buf_ref.shape[0]
    rt = route_ref[...].T
    e1, e2 = _route_ids(rt, 0)
    sub = lax.broadcasted_iota(jnp.int32, rt.shape, 0)
    oh1, oh2 = sub == e1, sub == e2
    hit = jnp.where(oh1 | oh2, 1.0, 0.0).astype(BF16)
    before = (lax.broadcasted_iota(jnp.int32, (tile, tile), 0)
              < lax.broadcasted_iota(jnp.int32, (tile, tile), 1))
    rank = _dot(hit, jnp.where(before, 1.0, 0.0).astype(BF16))
    sub1 = lax.broadcasted_iota(jnp.int32, (rt.shape[0], 1), 0)
    seg = jnp.zeros((rt.shape[0], 1), F32)
    for e in range(MOE_EXPERTS):
        seg = jnp.where(sub1 == e, lstart_ref[t * MOE_EXPERTS + e].astype(F32), seg)
    slot_of = rank + seg
    pos1 = jnp.sum(jnp.where(oh1, slot_of, 0.0), axis=0, keepdims=True).astype(jnp.int32)
    pos2 = jnp.sum(jnp.where(oh2, slot_of, 0.0), axis=0, keepdims=True).astype(jnp.int32)
    slot = lax.broadcasted_iota(jnp.int32, (slots, tile), 0)
    perm = jnp.where((slot == pos1) | (slot == pos2), 1.0, 0.0).astype(BF16)
    buf_ref[...] = _dot(perm, h_ref[...]).astype(BF16)

    def make(loc, glob):
        return pltpu.make_async_copy(buf_ref.at[pl.ds(loc, ROW_CHUNK)], xs_ref.at[pl.ds(glob, ROW_CHUNK)], sem)

    _segment_copies(t, base_ref, lstart_ref, nch_ref, make, True)
    _segment_copies(t, base_ref, lstart_ref, nch_ref, make, False)

    @pl.when(t == pl.num_programs(0) - 1)
    def _():
        buf_ref[0:FFN_BLOCK, :] = jnp.zeros((FFN_BLOCK, buf_ref.shape[1]), BF16)
        n_blocks = xs_ref.shape[0] // FFN_BLOCK

        def fill(start):
            def per_expert(e, carry):
                def chunk(i, c):
                    row = pl.multiple_of(gap0_ref[e] + i * ROW_CHUNK, ROW_CHUNK)
                    cp = pltpu.make_async_copy(buf_ref.at[0:ROW_CHUNK], xs_ref.at[pl.ds(row, ROW_CHUNK)], sem)
                    cp.start() if start else cp.wait()
                    return c
                return lax.fori_loop(0, gapn_ref[e], chunk, carry)

            def tail(b, c):
                row = pl.multiple_of(b * FFN_BLOCK, FFN_BLOCK)
                cp = pltpu.make_async_copy(buf_ref.at[0:FFN_BLOCK], xs_ref.at[pl.ds(row, FFN_BLOCK)], sem)
                cp.start() if start else cp.wait()
                return c

            lax.fori_loop(0, MOE_EXPERTS, per_expert, 0)
            lax.fori_loop(nused_ref[0], n_blocks, tail, 0)

        fill(True)
        fill(False)


def _dispatch(plan, h2, route, n_rows):
    tile, d = DISPATCH_TILE, D_MODEL
    n_tiles = h2.shape[0] // tile
    return pl.pallas_call(
        _dispatch_kernel,
        out_shape=jax.ShapeDtypeStruct((n_rows, d), BF16),
        grid_spec=pltpu.PrefetchScalarGridSpec(
            num_scalar_prefetch=6,
            grid=(n_tiles,),
            in_specs=[pl.BlockSpec((tile, d), lambda t, *_: (t, 0)),
                      pl.BlockSpec((tile, ROUTER_PAD), lambda t, *_: (t, 0))],
            out_specs=pl.BlockSpec(memory_space=pl.ANY),
            scratch_shapes=[pltpu.VMEM((SLOT_ROWS, d), BF16), pltpu.SemaphoreType.DMA],
        ),
        compiler_params=_params(("arbitrary",)),
        name="moe_dispatch",
    )(plan["base"], plan["lstart"], plan["nch"], plan["gap0"], plan["gapn"], plan["nused"], h2, route)


def _ffn_kernel(bexp_ref, nused_ref, x_ref, w1_ref, w3_ref, w2_ref, y_ref):
    used = pl.program_id(0) < nused_ref[0]

    @pl.when(used)
    def _():
        x = x_ref[...]
        a = _dot(x, w1_ref[...])
        hid = (a * _sigmoid(a)) * _dot(x, w3_ref[...])
        y_ref[...] = _dot(hid.astype(BF16), w2_ref[...]).astype(BF16)

    @pl.when(jnp.logical_not(used))
    def _():
        y_ref[...] = jnp.zeros_like(y_ref)


def _expert_ffn(plan, xs, w1, w3, w2):
    d = D_MODEL
    n_blocks = xs.shape[0] // FFN_BLOCK
    row = lambda b, bexp, nused: (jnp.minimum(b, nused[0] - 1), 0)
    wsel = lambda b, bexp, nused: (bexp[b], 0, 0)
    return pl.pallas_call(
        _ffn_kernel,
        out_shape=jax.ShapeDtypeStruct(xs.shape, BF16),
        grid_spec=pltpu.PrefetchScalarGridSpec(
            num_scalar_prefetch=2,
            grid=(n_blocks,),
            in_specs=[pl.BlockSpec((FFN_BLOCK, d), row),
                      pl.BlockSpec((None, d, D_EXPERT), wsel),
                      pl.BlockSpec((None, d, D_EXPERT), wsel),
                      pl.BlockSpec((None, D_EXPERT, d), wsel)],
            out_specs=pl.BlockSpec((FFN_BLOCK, d), lambda b, bexp, nused: (b, 0)),
        ),
        compiler_params=_params(("arbitrary",)),
        name="moe_ffn",
    )(plan["bexp"], plan["nused"], xs, w1, w3, w2)


def _combine_kernel(base_ref, lstart_ref, nch_ref, route_ref, x1_ref, mod_ref, ys_ref, o_ref, buf_ref, sem):
    d = D_MODEL
    t = pl.program_id(0)
    tile = route_ref.shape[0]
    slots = buf_ref.shape[0]

    @pl.when(t == 0)
    def _():
        buf_ref[...] = jnp.zeros_like(buf_ref)

    def make(loc, glob):
        return pltpu.make_async_copy(ys_ref.at[pl.ds(glob, ROW_CHUNK)], buf_ref.at[pl.ds(loc, ROW_CHUNK)], sem)

    _segment_copies(t, base_ref, lstart_ref, nch_ref, make, True)
    route = route_ref[...]
    e1, e2 = _route_ids(route, 1)
    lane = lax.broadcasted_iota(jnp.int32, route.shape, 1)
    oh1, oh2 = lane == e1, lane == e2
    hit = jnp.where(oh1 | oh2, 1.0, 0.0).astype(BF16)
    before = (lax.broadcasted_iota(jnp.int32, (tile, tile), 1)
              < lax.broadcasted_iota(jnp.int32, (tile, tile), 0))
    rank = _dot(jnp.where(before, 1.0, 0.0).astype(BF16), hit)
    lane1 = lax.broadcasted_iota(jnp.int32, (1, route.shape[1]), 1)
    seg = jnp.zeros((1, route.shape[1]), F32)
    for e in range(MOE_EXPERTS):
        seg = jnp.where(lane1 == e, lstart_ref[t * MOE_EXPERTS + e].astype(F32), seg)
    slot_of = rank + seg
    pos1 = jnp.sum(jnp.where(oh1, slot_of, 0.0), axis=1, keepdims=True).astype(jnp.int32)
    pos2 = jnp.sum(jnp.where(oh2, slot_of, 0.0), axis=1, keepdims=True).astype(jnp.int32)
    slot = lax.broadcasted_iota(jnp.int32, (tile, slots), 1)
    w1 = route[:, ROUTE_W:ROUTE_W + 1]
    w2 = route[:, ROUTE_W + 1:ROUTE_W + 2]
    comb = (jnp.where(slot == pos1, w1, 0.0) + jnp.where(slot == pos2, w2, 0.0)).astype(BF16)
    _segment_copies(t, base_ref, lstart_ref, nch_ref, make, False)
    moe = _dot(comb, buf_ref[...])
    o_ref[...] = x1_ref[...] + mod_ref[:, 5 * d:6 * d] * moe


def _combine(lay, plan, route, x1, mods, ys):
    tile, d = DISPATCH_TILE, D_MODEL
    assert lay.seq % tile == 0 and (lay.batch * lay.ctx) % tile == 0
    lat_tiles = lay.seq // tile
    n_lat = lay.batch * lat_tiles
    n_tiles = lay.n_tok // tile
    row = lambda t: jnp.where(t < n_lat, t // lat_tiles, lay.batch)
    return pl.pallas_call(
        _combine_kernel,
        out_shape=jax.ShapeDtypeStruct((lay.n_tok, d), F32),
        grid_spec=pltpu.PrefetchScalarGridSpec(
            num_scalar_prefetch=3,
            grid=(n_tiles,),
            in_specs=[pl.BlockSpec((tile, ROUTER_PAD), lambda t, *_: (t, 0)),
                      pl.BlockSpec((tile, d), lambda t, *_: (t, 0)),
                      pl.BlockSpec((None, 1, N_MOD * d), lambda t, *_: (row(t), 0, 0)),
                      pl.BlockSpec(memory_space=pl.ANY)],
            out_specs=pl.BlockSpec((tile, d), lambda t, *_: (t, 0)),
            scratch_shapes=[pltpu.VMEM((SLOT_ROWS, d), BF16), pltpu.SemaphoreType.DMA],
        ),
        compiler_params=_params(("arbitrary",)),
        name="moe_combine",
    )(plan["base"], plan["lstart"], plan["nch"], route, x1, mods, ys)


def _moe(lay, h2, x1, route, mods, w1, w3, w2):
    n_tiles = lay.n_tok // DISPATCH_TILE
    max_rows = (MOE_TOPK * lay.n_tok + n_tiles * MOE_EXPERTS * (ROW_CHUNK - 1)
                + MOE_EXPERTS * (FFN_BLOCK - 1))
    n_blocks = -(-max_rows // FFN_BLOCK)
    plan = _moe_plan(_expert_counts(route, DISPATCH_TILE), n_blocks)
    xs = _dispatch(plan, h2, route, n_blocks * FFN_BLOCK)
    ys = _expert_ffn(plan, xs, w1, w3, w2)
    return _combine(lay, plan, route, x1, mods, ys)


def _final_kernel(x_ref, g_ref, o_ref):
    x = x_ref[...]
    o_ref[...] = x * lax.rsqrt(jnp.mean(x * x, axis=-1, keepdims=True) + EPS) * g_ref[...]


def _final_norm(x, g, rows, tile):
    d = x.shape[1]
    return pl.pallas_call(
        _final_kernel,
        out_shape=jax.ShapeDtypeStruct((rows, d), F32),
        grid=(rows // tile,),
        in_specs=[pl.BlockSpec((tile, d), lambda t: (t, 0)), pl.BlockSpec((1, d), lambda t: (0, 0))],
        out_specs=pl.BlockSpec((tile, d), lambda t: (t, 0)),
        compiler_params=_params(("parallel",)),
        name="final_norm",
    )(x, g)


def _rope_tables(seq, ctx):
    pos = jnp.arange(seq, dtype=jnp.int32)
    inv_freq = ROPE_BASE ** (-jnp.arange(0, AXIS_DIM, 2, dtype=F32) / AXIS_DIM)
    ang_row = (pos // GRID_W).astype(F32)[:, None] * inv_freq
    ang_col = (pos % GRID_W).astype(F32)[:, None] * inv_freq
    cos_h = jnp.concatenate([jnp.cos(ang_row)] * 2 + [jnp.cos(ang_col)] * 2, axis=1)
    sin_h = jnp.concatenate([-jnp.sin(ang_row), jnp.sin(ang_row), -jnp.sin(ang_col), jnp.sin(ang_col)], axis=1)
    reps = LANES // SWA_HEAD_DIM
    cos_t = jnp.concatenate([jnp.tile(cos_h, (1, reps)), jnp.ones((ctx, LANES), F32)], axis=0)
    sin_t = jnp.concatenate([jnp.tile(sin_h, (1, reps)), jnp.zeros((ctx, LANES), F32)], axis=0)
    return cos_t, sin_t


def _channel_dft():
    i = jnp.arange(FNET_GROUP_W, dtype=jnp.int32)
    ang = ((i[:, None] * i[None, :]) % FNET_GROUP_W).astype(F32) * (2.0 * math.pi / FNET_GROUP_W)
    return jnp.concatenate([jnp.cos(ang), jnp.sin(ang)], axis=1).astype(BF16)


def kernel(x, c, ctx, c_ctx, w_ada, b_ada, g_mix, g_ffn, w_in, w_decay_down, w_decay_up, b_decay, g_gla, sink,
           w_pa, w_pb, w_pc, w_out, w_router_group, w_router_expert, w1, w3, w2, g_final):
    batch, seq, d = x.shape
    n_ctx = ctx.shape[1]
    depth = w_ada.shape[0]
    assert d == D_MODEL and seq % GRID_W == 0 and seq % n_ctx == 0
    lay = _Layout(batch, seq, n_ctx, TOKEN_TILE)

    rows = -(-(batch + 1) // 8) * 8
    c_all = jnp.zeros((rows, d), F32).at[:batch].set(c).at[batch].set(c_ctx)
    mods_all = _adaln(c_all, w_ada, b_ada).reshape(depth, rows, 1, N_MOD * d)

    cos_t, sin_t = _rope_tables(seq, n_ctx)
    cs = _channel_dft()
    c_lat, s_lat = _dft_mats(seq)
    c_ctx_m, s_ctx_m = _dft_mats(n_ctx)

    xs = jnp.concatenate([x.reshape(batch * seq, d), ctx.reshape(batch * n_ctx, d)], axis=0)
    rank = w_decay_down.shape[-1]
    for l in range(depth):
        last = l == depth - 1
        mods = mods_all[l]
        down = jnp.concatenate([w_decay_down[l, 0], w_decay_down[l, 1]], axis=1)
        down = jnp.pad(down, ((0, 0), (0, DECAY_PAD - 2 * rank)))
        w_ext = jnp.concatenate([w_in[l], down], axis=1).astype(BF16)
        w_up = jnp.zeros((DECAY_PAD, 2 * GLA_QK_W), F32)
        w_up = w_up.at[:rank, :GLA_QK_W].set(w_decay_up[l, 0]).at[rank:2 * rank, GLA_QK_W:].set(w_decay_up[l, 1])
        b_dec = b_decay[l].reshape(1, 2 * GLA_QK_W)
        (k_a, v_a, k_c, v_c, q_a, r_a, u_cos, u_sin, q_c, gate_logits, la_f, la_b) = _inproj(
            lay, xs, g_mix[l].reshape(1, d), mods, w_ext, w_up.astype(BF16), b_dec, cs, cos_t, sin_t)

        o_f, o_b = _gla(lay, q_a, k_a, v_a, la_f, la_b)
        yb_lat = _fourier(u_cos, u_sin, c_lat, s_lat, 0, seq, min(seq, 1024), min(seq, 512))
        yc_lat = _swa(lay, q_c, k_c, v_c, jnp.broadcast_to(sink[l][:, None], (SWA_HEADS, LANES)))
        if last:
            yb_ctx = jnp.zeros((n_ctx, batch * FNET_W), BF16)
            yc_ctx = jnp.zeros((batch * n_ctx, SWA_Q_W), BF16)
        else:
            yb_ctx = _fourier(u_cos, u_sin, c_ctx_m, s_ctx_m, seq, n_ctx, n_ctx, n_ctx)
            yc_ctx = _ctx_attn(lay, q_c, k_c, v_c, jnp.broadcast_to(sink[l][:, None], (SWA_HEADS, LANES)))

        w_router = jnp.zeros((d, ROUTER_PAD), F32)
        w_router = w_router.at[:, :MOE_EXPERTS].set(w_router_expert[l])
        w_router = w_router.at[:, MOE_EXPERTS:MOE_EXPERTS + MOE_GROUPS].set(w_router_group[l])
        x1, h2, gates = _merge(lay, xs, o_f, o_b, r_a, yb_lat, yb_ctx, yc_lat, yc_ctx, gate_logits, mods,
                               g_gla[l].reshape(1, GLA_DV), g_ffn[l].reshape(1, d),
                               w_pa[l].astype(BF16), w_pb[l].astype(BF16), w_pc[l].astype(BF16),
                               w_out[l].astype(BF16), w_router)
        xs = _moe(lay, h2, x1, gates, mods, w1[l].astype(BF16), w3[l].astype(BF16), w2[l].astype(BF16))

    out = _final_norm(xs, g_final.reshape(1, d), batch * seq, TOKEN_TILE)
    return out.reshape(batch, seq, d)
```

```python
import functools
import math

import jax
import jax.numpy as jnp
from jax import lax
from jax.experimental import pallas as pl
from jax.experimental.pallas import tpu as pltpu

F32 = jnp.float32
BF16 = jnp.bfloat16
HIGHEST = lax.Precision.HIGHEST

D_MODEL = 1024
GRID_W = 64
EPS = 1e-6
N_MOD = 6
GLA_HEADS = 4
GLA_DK = 64
GLA_DV = 128
GLA_TAU = 16.0
GLA_CHUNK = 64
GLA_SCALE = GLA_DK ** -0.5
FNET_GROUPS = 4
FNET_GROUP_W = 128
SWA_HEADS = 8
SWA_KV_HEADS = 2
SWA_HEAD_DIM = 64
WINDOW = 128
ROPE_BASE = 10000.0
AXIS_DIM = SWA_HEAD_DIM // 2
MOE_GROUPS = 4
MOE_EXPERTS_PER_GROUP = 4
MOE_EXPERTS = MOE_GROUPS * MOE_EXPERTS_PER_GROUP
MOE_TOPK = 2
D_EXPERT = 512

GLA_QK_W = GLA_HEADS * GLA_DK
GLA_V_W = GLA_HEADS * GLA_DV
FNET_W = FNET_GROUPS * FNET_GROUP_W
SWA_Q_W = SWA_HEADS * SWA_HEAD_DIM
SWA_KV_W = SWA_KV_HEADS * SWA_HEAD_DIM
IN_SIZES = (GLA_QK_W, GLA_V_W, SWA_KV_W, SWA_KV_W, GLA_QK_W, GLA_V_W, FNET_W, SWA_Q_W, 3 * D_MODEL)
IN_OFFS = tuple(int(sum(IN_SIZES[:i])) for i in range(len(IN_SIZES) + 1))
IN_COLS = IN_OFFS[-1]

LANES = 128
TOKEN_TILE = 256
DECAY_PAD = LANES
ROUTER_PAD = LANES
ROUTE_ID = 0
ROUTE_W = 2
DISPATCH_TILE = 512
ROW_CHUNK = 16
FFN_BLOCK = 256
SLOT_ROWS = -(-(MOE_TOPK * DISPATCH_TILE + MOE_EXPERTS * (ROW_CHUNK - 1)) // LANES) * LANES
VMEM_LIMIT = 56 * 1024 * 1024


def _params(sem, vmem=VMEM_LIMIT):
    return pltpu.CompilerParams(dimension_semantics=sem, vmem_limit_bytes=vmem)


def _sigmoid(x):
    return 0.5 * jnp.tanh(0.5 * x) + 0.5


def _dot(a, b):
    return jnp.dot(a, b, preferred_element_type=F32)


def _dot_nt(a, b):
    return lax.dot_general(a, b, (((1,), (1,)), ((), ())), preferred_element_type=F32)


def _ada_kernel(c_ref, w_ref, b_ref, o_ref):
    c = c_ref[...]
    a = c * _sigmoid(c)
    o_ref[...] = jnp.dot(a, w_ref[...], preferred_element_type=F32, precision=HIGHEST) + b_ref[...]


def _adaln(c_all, w_ada, b_ada):
    depth, d, n = w_ada.shape
    rows = c_all.shape[0]
    tn = 1536
    return pl.pallas_call(
        _ada_kernel,
        out_shape=jax.ShapeDtypeStruct((depth, rows, n), F32),
        grid=(depth, n // tn),
        in_specs=[pl.BlockSpec((rows, d), lambda l, j: (0, 0)),
                  pl.BlockSpec((None, d, tn), lambda l, j: (l, 0, j)),
                  pl.BlockSpec((None, 1, tn), lambda l, j: (l, 0, j))],
        out_specs=pl.BlockSpec((None, rows, tn), lambda l, j: (l, 0, j)),
        compiler_params=_params(("parallel", "parallel")),
        name="adaln",
    )(c_all, w_ada, b_ada.reshape(depth, 1, n))


class _Layout:
    def __init__(self, batch, seq, ctx, tile):
        assert seq % tile == 0 and ctx % tile == 0
        self.batch, self.seq, self.ctx, self.tile = batch, seq, ctx, tile
        self.lat_tiles = seq // tile
        self.ctx_tiles = ctx // tile
        self.n_lat = batch * self.lat_tiles
        self.n_tiles = self.n_lat + batch * self.ctx_tiles
        self.n_tok = self.n_tiles * tile

    def batch_of(self, t):
        return jnp.where(t < self.n_lat, t // self.lat_tiles, (t - self.n_lat) // self.ctx_tiles)

    def mod_row(self, t):
        return jnp.where(t < self.n_lat, t // self.lat_tiles, self.batch)

    def seq_tile(self, t):
        return jnp.where(t < self.n_lat, t % self.lat_tiles,
                         self.lat_tiles + (t - self.n_lat) % self.ctx_tiles)


def _rope(x, cos, sin_signed):
    n = x.shape[-1]
    lane = lax.broadcasted_iota(jnp.int32, x.shape, 1)
    half = AXIS_DIM // 2
    partner = jnp.where((lane & half) == 0, pltpu.roll(x, n - half, 1), pltpu.roll(x, half, 1))
    return x * cos + partner * sin_signed


def _inproj_kernel(x_ref, g_ref, mod_ref, w_ref, wd_ref, wu_ref, bdec_ref, cs_ref, cos_ref, sin_ref,
                   ka_ref, va_ref, kc_ref, vc_ref, qa_ref, ra_ref, ua_ref, us_ref, qc_ref, gl_ref,
                   laf_ref, lab_ref):
    d = D_MODEL
    x = x_ref[...]
    shift = mod_ref[:, 0:d]
    scale = mod_ref[:, d:2 * d]
    h = x * lax.rsqrt(jnp.mean(x * x, axis=-1, keepdims=True) + EPS) * g_ref[...]
    hb = (h * (1.0 + scale) + shift).astype(BF16)

    def proj(i):
        return _dot(hb, w_ref[:, IN_OFFS[i]:IN_OFFS[i + 1]])

    ka_ref[...] = proj(0).astype(BF16)
    va_ref[...] = proj(1).astype(BF16)
    cos = cos_ref[...]
    sin = sin_ref[...]
    kc_ref[...] = _rope(proj(2), cos, sin).astype(BF16)
    vc_ref[...] = proj(3).astype(BF16)
    qa_ref[...] = proj(4).astype(BF16)
    ra_ref[...] = proj(5).astype(BF16)
    u = proj(6).astype(BF16)
    for g in range(FNET_GROUPS):
        sl = slice(g * FNET_GROUP_W, (g + 1) * FNET_GROUP_W)
        ab = _dot(u[:, sl], cs_ref[...])
        ua_ref[:, sl] = ab[:, :FNET_GROUP_W].astype(BF16)
        us_ref[:, sl] = ab[:, FNET_GROUP_W:].astype(BF16)
    reps = SWA_Q_W // LANES
    qc_ref[...] = (_rope(proj(7), jnp.concatenate([cos] * reps, axis=1), jnp.concatenate([sin] * reps, axis=1))
                   * SWA_HEAD_DIM ** -0.5).astype(BF16)
    gl_ref[...] = proj(8).astype(BF16)
    low = _dot(hb, wd_ref[...]).astype(BF16)
    z = _dot(low, wu_ref[...]) + bdec_ref[...]
    la = (jnp.minimum(z, 0.0) - jnp.log(1.0 + jnp.exp(-jnp.abs(z)))) * (1.0 / GLA_TAU)
    laf_ref[...] = la[:, :GLA_QK_W]
    lab_ref[...] = la[:, GLA_QK_W:]


def _inproj(lay, x, g_mix, mods, layer, w_in, w_down, w_up, b_dec, cs, cos_t, sin_t):
    tm, d = lay.tile, D_MODEL
    n = lay.n_tok
    bf = lambda w: jax.ShapeDtypeStruct((n, w), BF16)
    tok = lambda w: pl.BlockSpec((tm, w), lambda t: (t, 0))
    const = lambda a: pl.BlockSpec(a.shape, lambda t: (0,) * a.ndim)
    seq_rows = lay.seq + lay.ctx
    fnet_shape = jax.ShapeDtypeStruct((seq_rows, lay.batch * FNET_W), BF16)
    fnet_spec = pl.BlockSpec((tm, FNET_W), lambda t: (lay.seq_tile(t), lay.batch_of(t)))
    pos_spec = pl.BlockSpec((tm, LANES), lambda t: (lay.seq_tile(t), 0))
    return pl.pallas_call(
        _inproj_kernel,
        out_shape=(bf(GLA_QK_W), bf(GLA_V_W), bf(SWA_KV_W), bf(SWA_KV_W), bf(GLA_QK_W), bf(GLA_V_W),
                   fnet_shape, fnet_shape, bf(SWA_Q_W), bf(3 * d),
                   jax.ShapeDtypeStruct((n, GLA_QK_W), F32), jax.ShapeDtypeStruct((n, GLA_QK_W), F32)),
        grid=(lay.n_tiles,),
        in_specs=[tok(d), const(g_mix),
                  pl.BlockSpec((None, 1, N_MOD * d), lambda t: (lay.mod_row(t), 0, 0)),
                  pl.BlockSpec((None,) + w_in.shape[1:], lambda t: (layer, 0, 0)),
                  const(w_down), const(w_up), const(b_dec), const(cs), pos_spec, pos_spec],
        out_specs=(tok(GLA_QK_W), tok(GLA_V_W), tok(SWA_KV_W), tok(SWA_KV_W), tok(GLA_QK_W), tok(GLA_V_W),
                   fnet_spec, fnet_spec, tok(SWA_Q_W), tok(3 * d), tok(GLA_QK_W), tok(GLA_QK_W)),
        compiler_params=_params(("parallel",)),
        name="inproj",
    )(x, g_mix, mods, w_in, w_down, w_up, b_dec, cs, cos_t, sin_t)


def _split3(x):
    a1 = x.astype(BF16)
    r1 = x - a1.astype(F32)
    a2 = r1.astype(BF16)
    a3 = (r1 - a2.astype(F32)).astype(BF16)
    return a1, a2, a3


def _gla_direction(q_ref, k_ref, v_ref, la_ref, o_ref, s_ref, rev):
    t_rows = q_ref.shape[0]
    c = GLA_CHUNK
    n_sub = t_rows // c
    hk = GLA_QK_W
    la3 = _split3(la_ref[...])
    row = lax.broadcasted_iota(jnp.int32, (t_rows, t_rows), 0)
    col = lax.broadcasted_iota(jnp.int32, (t_rows, t_rows), 1)
    tri = ((row // c) == (col // c)) & ((col >= row) if rev else (col <= row))
    tri = jnp.where(tri, 1.0, 0.0).astype(BF16)
    lc = _dot(tri, la3[0]) + _dot(tri, la3[1]) + _dot(tri, la3[2])
    q = q_ref[...].astype(F32)
    k = k_ref[...].astype(F32)
    qd = (q * jnp.exp(lc) * GLA_SCALE).astype(BF16)
    kd = (k * jnp.exp(-lc)).astype(BF16)
    hrow = lax.broadcasted_iota(jnp.int32, (GLA_HEADS * c, hk), 0) // c
    hcol = lax.broadcasted_iota(jnp.int32, (GLA_HEADS * c, hk), 1) // GLA_DK
    head_mask = hrow == hcol
    trow = lax.broadcasted_iota(jnp.int32, (GLA_HEADS * c, c), 0) % c
    scol = lax.broadcasted_iota(jnp.int32, (GLA_HEADS * c, c), 1)
    causal = (scol >= trow) if rev else (scol <= trow)
    edge = 0 if rev else c - 1
    lasts = [lc[i * c + edge:i * c + edge + 1] for i in range(n_sub)]
    to_end = jnp.concatenate([jnp.broadcast_to(l, (c, hk)) for l in lasts], axis=0) - lc
    kh_t = (k * jnp.exp(to_end)).T
    pad = jnp.zeros((LANES - n_sub, hk), F32)
    decay_t = jnp.exp(jnp.concatenate(lasts + [pad], axis=0).T)
    chunk_of = lax.broadcasted_iota(jnp.int32, (1, t_rows), 1) // c
    kv = []
    for h in range(GLA_HEADS):
        kh_h = kh_t[h * GLA_DK:(h + 1) * GLA_DK]
        stack = jnp.concatenate([jnp.where(chunk_of == i, kh_h, 0.0) for i in range(n_sub)], axis=0)
        kv.append(_dot(stack.astype(BF16), v_ref[:, h * GLA_DV:(h + 1) * GLA_DV]))
    order = range(n_sub - 1, -1, -1) if rev else range(n_sub)
    for i in order:
        rows = slice(i * c, (i + 1) * c)
        v_i = v_ref[rows, :]
        q_stack = jnp.where(head_mask, jnp.concatenate([qd[rows]] * GLA_HEADS, axis=0), 0.0).astype(BF16)
        state = s_ref[...]
        inter = _dot(q_stack, state.astype(BF16))
        scores = jnp.where(causal, _dot_nt(q_stack, kd[rows]), 0.0).astype(BF16)
        decay = decay_t[:, i:i + 1]
        outs = []
        for h in range(GLA_HEADS):
            hr = slice(h * c, (h + 1) * c)
            hv = slice(h * GLA_DV, (h + 1) * GLA_DV)
            outs.append(inter[hr] + _dot(scores[hr], v_i[:, hv]))
            kr = slice(h * GLA_DK, (h + 1) * GLA_DK)
            s_ref[kr, :] = decay[kr] * state[kr] + kv[h][i * GLA_DK:(i + 1) * GLA_DK]
        o_ref[rows, :] = jnp.concatenate(outs, axis=1).astype(BF16)


def _gla_kernel(qf_ref, kf_ref, vf_ref, laf_ref, qb_ref, kb_ref, vb_ref, lab_ref,
                of_ref, ob_ref, sf_ref, sb_ref):
    @pl.when(pl.program_id(1) == 0)
    def _():
        sf_ref[...] = jnp.zeros_like(sf_ref)
        sb_ref[...] = jnp.zeros_like(sb_ref)

    _gla_direction(qf_ref, kf_ref, vf_ref, laf_ref, of_ref, sf_ref, False)
    _gla_direction(qb_ref, kb_ref, vb_ref, lab_ref, ob_ref, sb_ref, True)


def _gla(lay, q_a, k_a, v_a, la_f, la_b):
    tm = lay.tile
    nc, nl = lay.ctx_tiles, lay.lat_tiles

    def fwd(b, j):
        return jnp.where(j < nc, lay.n_lat + b * nc + j, b * nl + (j - nc))

    def bwd(b, j):
        return jnp.where(j < nc, lay.n_lat + b * nc + (nc - 1 - j), b * nl + (nl - 1 - (j - nc)))

    spec = lambda w, f: pl.BlockSpec((tm, w), lambda b, j: (f(b, j), 0))
    out = jax.ShapeDtypeStruct((lay.n_tok, GLA_V_W), BF16)
    return pl.pallas_call(
        _gla_kernel,
        out_shape=(out, out),
        grid=(lay.batch, nc + nl),
        in_specs=[spec(GLA_QK_W, fwd), spec(GLA_QK_W, fwd), spec(GLA_V_W, fwd), spec(GLA_QK_W, fwd),
                  spec(GLA_QK_W, bwd), spec(GLA_QK_W, bwd), spec(GLA_V_W, bwd), spec(GLA_QK_W, bwd)],
        out_specs=(spec(GLA_V_W, fwd), spec(GLA_V_W, bwd)),
        scratch_shapes=[pltpu.VMEM((GLA_QK_W, GLA_DV), F32), pltpu.VMEM((GLA_QK_W, GLA_DV), F32)],
        compiler_params=_params(("parallel", "arbitrary")),
        name="gla",
    )(q_a, k_a, v_a, la_f, q_a, k_a, v_a, la_b)


def _dft_kernel(c_ref, s_ref, a_ref, b_ref, o_ref, acc_ref, *, scale):
    k = pl.program_id(1)

    @pl.when(k == 0)
    def _():
        acc_ref[...] = jnp.zeros_like(acc_ref)

    acc_ref[...] += _dot(c_ref[...], a_ref[...]) + _dot(s_ref[...], b_ref[...])

    @pl.when(k == pl.num_programs(1) - 1)
    def _():
        o_ref[...] = (acc_ref[...] * scale).astype(BF16)


def _dft_mats(n):
    f = GRID_W
    assert n % f == 0
    k = jnp.arange(n, dtype=jnp.int32)[None, :]

    def table(rows, period):
        ang = ((jnp.arange(rows, dtype=jnp.int32)[:, None] * k) % period).astype(F32) * (2.0 * math.pi / period)
        return jnp.cos(ang), jnp.sin(ang)

    ca, sa = table(n // f, n // f)
    cb, sb = table(f, n)
    cos = ca[:, None, :] * cb[None, :, :] - sa[:, None, :] * sb[None, :, :]
    sin = sa[:, None, :] * cb[None, :, :] + ca[:, None, :] * sb[None, :, :]
    return cos.reshape(n, n).astype(BF16), (-sin).reshape(n, n).astype(BF16)


def _fourier(ua, us, cmat, smat, row0, length, tm, tk):
    width = ua.shape[1]
    off = row0 // tk
    scale = 1.0 / math.sqrt(length * FNET_GROUP_W)
    return pl.pallas_call(
        functools.partial(_dft_kernel, scale=scale),
        out_shape=jax.ShapeDtypeStruct((length, width), BF16),
        grid=(length // tm, length // tk),
        in_specs=[pl.BlockSpec((tm, tk), lambda i, k: (i, k)),
                  pl.BlockSpec((tm, tk), lambda i, k: (i, k)),
                  pl.BlockSpec((tk, width), lambda i, k: (off + k, 0)),
                  pl.BlockSpec((tk, width), lambda i, k: (off + k, 0))],
        out_specs=pl.BlockSpec((tm, width), lambda i, k: (i, 0)),
        scratch_shapes=[pltpu.VMEM((tm, width), F32)],
        compiler_params=_params(("parallel", "arbitrary")),
        name="fourier",
    )(cmat, smat, ua, us)


def _attend(q_ref, k_all, v_all, valid, sink_ref, o_ref, s_ref):
    hd = SWA_HEAD_DIM
    swap = lambda a: jnp.concatenate([a[:, hd:], a[:, :hd]], axis=1)
    k_sw, v_sw = swap(k_all), swap(v_all)
    lane = lax.broadcasted_iota(jnp.int32, (1, LANES), 1)
    half = (lane < hd, lane >= hd)
    heads_per_kv = SWA_HEADS // SWA_KV_HEADS
    zero = jnp.zeros((), BF16)
    k_for = lambda head: k_all if head // heads_per_kv == head % 2 else k_sw
    v_for = lambda head: v_all if head // heads_per_kv == head % 2 else v_sw
    row_max = []
    for head in range(SWA_HEADS):
        p, par = divmod(head, 2)
        qh = jnp.where(half[par], q_ref[:, p * LANES:(p + 1) * LANES], zero)
        s = _dot_nt(qh, k_for(head))
        if valid is not None:
            kw = valid.shape[1]
            s = jnp.concatenate([jnp.where(valid, s[:, :kw], -jnp.inf), s[:, kw:]], axis=1)
        s_ref[head] = s
        row_max.append(jnp.maximum(jnp.max(s, axis=-1, keepdims=True), sink_ref[head:head + 1, 0:1]))
    for p in range(SWA_HEADS // 2):
        acc = None
        for par in range(2):
            head = 2 * p + par
            m = row_max[head]
            e = jnp.exp(s_ref[head] - m)
            denom = jnp.sum(e, axis=-1, keepdims=True) + jnp.exp(sink_ref[head:head + 1, 0:1] - m)
            part = _dot(e.astype(BF16), jnp.where(half[par], v_for(head), zero)) * (1.0 / denom)
            acc = part if acc is None else acc + part
        o_ref[:, p * LANES:(p + 1) * LANES] = acc.astype(BF16)


def _swa_kernel(q_ref, kp_ref, kc_ref, kn_ref, kx_ref, vp_ref, vc_ref, vn_ref, vx_ref, sink_ref, o_ref,
                s_ref, *, seq):
    n = pl.program_id(1)
    w = WINDOW
    k_blocks = [kp_ref[...], kc_ref[0:w, :], kc_ref[w:2 * w, :], kn_ref[...]]
    v_blocks = [vp_ref[...], vc_ref[0:w, :], vc_ref[w:2 * w, :], vn_ref[...]]
    a = lax.broadcasted_iota(jnp.int32, (w, 3 * w), 0)
    j = lax.broadcasted_iota(jnp.int32, (w, 3 * w), 1)
    band = (j >= a) & (j - a <= 2 * w)
    for half in range(2):
        key_pos = (2 * n + half - 1) * w + j
        valid = band & (key_pos >= 0) & (key_pos < seq)
        k_all = jnp.concatenate(k_blocks[half:half + 3] + [kx_ref[...]], axis=0)
        v_all = jnp.concatenate(v_blocks[half:half + 3] + [vx_ref[...]], axis=0)
        rows = pl.ds(half * w, w)
        _attend(q_ref.at[rows], k_all, v_all, valid, sink_ref, o_ref.at[rows], s_ref.at[half])


def _swa(lay, q_c, k_c, v_c, sink_b):
    w = WINDOW
    assert lay.seq % (2 * w) == 0
    nq = lay.seq // w
    steps = nq // 2
    ctx_blk = (lay.batch * lay.seq) // lay.ctx

    def edge(f):
        return pl.BlockSpec((w, SWA_KV_W), lambda b, n: (b * nq + f(n), 0))

    prev = edge(lambda n: jnp.maximum(2 * n - 1, 0))
    nxt = edge(lambda n: jnp.minimum(2 * n + 2, nq - 1))
    cur = pl.BlockSpec((2 * w, SWA_KV_W), lambda b, n: (b * steps + n, 0))
    ctx_spec = pl.BlockSpec((lay.ctx, SWA_KV_W), lambda b, n: (ctx_blk + b, 0))
    q_spec = pl.BlockSpec((2 * w, SWA_Q_W), lambda b, n: (b * steps + n, 0))
    return pl.pallas_call(
        functools.partial(_swa_kernel, seq=lay.seq),
        out_shape=jax.ShapeDtypeStruct((lay.batch * lay.seq, SWA_Q_W), BF16),
        grid=(lay.batch, steps),
        in_specs=[q_spec, prev, cur, nxt, ctx_spec, prev, cur, nxt, ctx_spec,
                  pl.BlockSpec(sink_b.shape, lambda b, n: (0, 0))],
        out_specs=q_spec,
        scratch_shapes=[pltpu.VMEM((2, SWA_HEADS, w, 3 * w + lay.ctx), F32)],
        compiler_params=_params(("parallel", "parallel")),
        name="swa",
    )(q_c, k_c, k_c, k_c, k_c, v_c, v_c, v_c, v_c, sink_b)


def _ctx_attn_kernel(q_ref, kx_ref, vx_ref, sink_ref, o_ref, s_ref):
    _attend(q_ref, kx_ref[...], vx_ref[...], None, sink_ref, o_ref, s_ref)


def _ctx_attn(lay, q_c, k_c, v_c, sink_b):
    ctx_blk = (lay.batch * lay.seq) // lay.ctx
    spec = lambda wd: pl.BlockSpec((lay.ctx, wd), lambda b: (ctx_blk + b, 0))
    return pl.pallas_call(
        _ctx_attn_kernel,
        out_shape=jax.ShapeDtypeStruct((lay.batch * lay.ctx, SWA_Q_W), BF16),
        grid=(lay.batch,),
        in_specs=[spec(SWA_Q_W), spec(SWA_KV_W), spec(SWA_KV_W),
                  pl.BlockSpec(sink_b.shape, lambda b: (0, 0))],
        out_specs=pl.BlockSpec((lay.ctx, SWA_Q_W), lambda b: (b, 0)),
        scratch_shapes=[pltpu.VMEM((SWA_HEADS, lay.ctx, lay.ctx), F32)],
        compiler_params=_params(("parallel",)),
        name="ctx_attn",
    )(q_c, k_c, v_c, sink_b)


def _merge_kernel(x_ref, of_ref, ob_ref, ra_ref, ybl_ref, ybc_ref, ycl_ref, ycc_ref, gl_ref, mod_ref,
                  ggla_ref, gffn_ref, wpa_ref, wpb_ref, wpc_ref, wout_ref, wr_ref,
                  x1_ref, h2_ref, gates_ref, *, n_lat):
    d = D_MODEL
    is_ctx = pl.program_id(0) >= n_lat
    o = of_ref[...].astype(F32) + ob_ref[...].astype(F32)
    r = ra_ref[...].astype(F32)
    parts = []
    for h in range(GLA_HEADS):
        sl = slice(h * GLA_DV, (h + 1) * GLA_DV)
        oh = o[:, sl]
        parts.append(oh * lax.rsqrt(jnp.mean(oh * oh, axis=-1, keepdims=True) + EPS) * ggla_ref[...])
    y_a = (jnp.concatenate(parts, axis=1) * (r * _sigmoid(r))).astype(BF16)
    y_b = jnp.where(is_ctx, ybc_ref[...], ybl_ref[...])
    y_c = jnp.where(is_ctx, ycc_ref[...], ycl_ref[...])
    gl = gl_ref[...].astype(F32)
    mix = (_sigmoid(gl[:, 0:d]) * _dot(y_a, wpa_ref[...])
           + _sigmoid(gl[:, d:2 * d]) * _dot(y_b, wpb_ref[...])
           + _sigmoid(gl[:, 2 * d:3 * d]) * _dot(y_c, wpc_ref[...]))
    y = _dot(mix.astype(BF16), wout_ref[...])
    x1 = x_ref[...] + mod_ref[:, 2 * d:3 * d] * y
    x1_ref[...] = x1
    h2 = x1 * lax.rsqrt(jnp.mean(x1 * x1, axis=-1, keepdims=True) + EPS) * gffn_ref[...]
    h2 = h2 * (1.0 + mod_ref[:, 4 * d:5 * d]) + mod_ref[:, 3 * d:4 * d]
    h2_hi = h2.astype(BF16)
    h2_ref[...] = h2_hi
    h2_lo = (h2 - h2_hi.astype(F32)).astype(BF16)
    both = _dot(h2_hi, wr_ref[...])
    logits = both[:, :ROUTER_PAD] + both[:, ROUTER_PAD:] + _dot(h2_lo, wr_ref[:, :ROUTER_PAD])
    lane = lax.broadcasted_iota(jnp.int32, logits.shape, 1)
    neg = -jnp.inf
    big = ROUTER_PAD
    is_group = (lane >= MOE_EXPERTS) & (lane < MOE_EXPERTS + MOE_GROUPS)
    gl_m = jnp.where(is_group, logits, neg)
    g_max = jnp.max(gl_m, axis=-1, keepdims=True)
    g_sel = jnp.min(jnp.where(gl_m == g_max, lane, big), axis=-1, keepdims=True) - MOE_EXPERTS
    g_gate = 1.0 / jnp.sum(jnp.where(is_group, jnp.exp(logits - g_max), 0.0), axis=-1, keepdims=True)
    lo = g_sel * MOE_EXPERTS_PER_GROUP
    in_group = (lane >= lo) & (lane < lo + MOE_EXPERTS_PER_GROUP)
    e1 = jnp.where(in_group, logits, neg)
    v1 = jnp.max(e1, axis=-1, keepdims=True)
    i1 = jnp.min(jnp.where(e1 == v1, lane, big), axis=-1, keepdims=True)
    e2 = jnp.where(lane == i1, neg, e1)
    v2 = jnp.max(e2, axis=-1, keepdims=True)
    i2 = jnp.min(jnp.where(e2 == v2, lane, big), axis=-1, keepdims=True)
    t = jnp.exp(v2 - v1)
    w1 = g_gate / (1.0 + t)
    w2 = g_gate * t / (1.0 + t)
    route = jnp.where(lane == ROUTE_ID, i1.astype(F32), 0.0) + jnp.where(lane == ROUTE_ID + 1, i2.astype(F32), 0.0)
    route = route + jnp.where(lane == ROUTE_W, w1, 0.0) + jnp.where(lane == ROUTE_W + 1, w2, 0.0)
    gates_ref[...] = route


def _merge(lay, x, o_f, o_b, r_a, yb_lat, yb_ctx, yc_lat, yc_ctx, gate_logits, mods, g_gla, g_ffn,
           w_pa, w_pb, w_pc, w_out, w_router):
    tm, d = lay.tile, D_MODEL
    n = lay.n_tok
    tok = lambda w: pl.BlockSpec((tm, w), lambda t: (t, 0))
    const = lambda a: pl.BlockSpec(a.shape, lambda t: (0,) * a.ndim)
    lt, ct = lay.lat_tiles, lay.ctx_tiles
    yb_lat_spec = pl.BlockSpec((tm, FNET_W), lambda t: (jnp.minimum(t, lay.n_lat - 1) % lt,
                                                         jnp.minimum(t, lay.n_lat - 1) // lt))
    yb_ctx_spec = pl.BlockSpec((tm, FNET_W), lambda t: (jnp.maximum(t - lay.n_lat, 0) % ct,
                                                         jnp.maximum(t - lay.n_lat, 0) // ct))
    yc_lat_spec = pl.BlockSpec((tm, SWA_Q_W), lambda t: (jnp.minimum(t, lay.n_lat - 1), 0))
    yc_ctx_spec = pl.BlockSpec((tm, SWA_Q_W), lambda t: (jnp.maximum(t - lay.n_lat, 0), 0))
    return pl.pallas_call(
        functools.partial(_merge_kernel, n_lat=lay.n_lat),
        out_shape=(jax.ShapeDtypeStruct((n, d), F32), jax.ShapeDtypeStruct((n, d), BF16),
                   jax.ShapeDtypeStruct((n, ROUTER_PAD), F32)),
        grid=(lay.n_tiles,),
        in_specs=[tok(d), tok(GLA_V_W), tok(GLA_V_W), tok(GLA_V_W), yb_lat_spec, yb_ctx_spec,
                  yc_lat_spec, yc_ctx_spec, tok(3 * d),
                  pl.BlockSpec((None, 1, N_MOD * d), lambda t: (lay.mod_row(t), 0, 0)),
                  const(g_gla), const(g_ffn), const(w_pa), const(w_pb), const(w_pc), const(w_out),
                  const(w_router)],
        out_specs=(tok(d), tok(d), tok(ROUTER_PAD)),
        compiler_params=_params(("parallel",)),
        name="merge",
    )(x, o_f, o_b, r_a, yb_lat, yb_ctx, yc_lat, yc_ctx, gate_logits, mods, g_gla, g_ffn,
      w_pa, w_pb, w_pc, w_out, w_router)


def _route_ids(route, axis):
    take = (lambda i: route[:, i:i + 1]) if axis == 1 else (lambda i: route[i:i + 1, :])
    return take(ROUTE_ID).astype(jnp.int32), take(ROUTE_ID + 1).astype(jnp.int32)


def _count_kernel(route_ref, o_ref):
    route = route_ref[...]
    lane = lax.broadcasted_iota(jnp.int32, route.shape, 1)
    e1, e2 = _route_ids(route, 1)
    hit = jnp.where((lane == e1) | (lane == e2), 1.0, 0.0)
    o_ref[...] = jnp.sum(hit, axis=0, keepdims=True).astype(jnp.int32)


def _expert_counts(route, tile):
    n_tiles = route.shape[0] // tile
    return pl.pallas_call(
        _count_kernel,
        out_shape=jax.ShapeDtypeStruct((n_tiles, 1, ROUTER_PAD), jnp.int32),
        grid=(n_tiles,),
        in_specs=[pl.BlockSpec((tile, ROUTER_PAD), lambda t: (t, 0))],
        out_specs=pl.BlockSpec((None, 1, ROUTER_PAD), lambda t: (t, 0, 0)),
        compiler_params=_params(("parallel",)),
        name="moe_count",
    )(route)


def _moe_plan(counts, n_blocks):
    cnt = counts[:, 0, :MOE_EXPERTS]
    pc = (cnt + ROW_CHUNK - 1) // ROW_CHUNK * ROW_CHUNK
    lstart = jnp.cumsum(pc, axis=1) - pc
    tot = jnp.sum(pc, axis=0)
    tot_pad = (tot + FFN_BLOCK - 1) // FFN_BLOCK * FFN_BLOCK
    eend = jnp.cumsum(tot_pad)
    estart = eend - tot_pad
    base = estart[None, :] + jnp.cumsum(pc, axis=0) - pc
    n_used = eend[-1] // FFN_BLOCK
    blk = jnp.minimum(jnp.arange(n_blocks, dtype=jnp.int32), n_used - 1)
    bexp = jnp.sum((blk[:, None] * FFN_BLOCK >= eend[None, :]).astype(jnp.int32), axis=1)
    flat = lambda a: a.reshape(-1).astype(jnp.int32)
    return dict(base=flat(base), lstart=flat(lstart), nch=flat(pc // ROW_CHUNK),
                gap0=flat(estart + tot), gapn=flat((tot_pad - tot) // ROW_CHUNK),
                bexp=flat(bexp), nused=flat(n_used))


def _segment_copies(t, base_ref, lstart_ref, nch_ref, make, start):
    def per_expert(e, carry):
        idx = t * MOE_EXPERTS + e
        loc = lstart_ref[idx]
        glob = base_ref[idx]

        def chunk(i, c):
            cp = make(pl.multiple_of(loc + i * ROW_CHUNK, ROW_CHUNK), pl.multiple_of(glob + i * ROW_CHUNK, ROW_CHUNK))
            cp.start() if start else cp.wait()
            return c

        return lax.fori_loop(0, nch_ref[idx], chunk, carry)

    lax.fori_loop(0, MOE_EXPERTS, per_expert, 0)


def _dispatch_kernel(base_ref, lstart_ref, nch_ref, gap0_ref, gapn_ref, nused_ref, h_ref, route_ref, xs_ref,
                     buf_ref, sem):
    t = pl.program_id(0)
    tile = h_ref.shape[0]
    slots = buf_ref.shape[0]
    rt = route_ref[...].T
    e1, e2 = _route_ids(rt, 0)
    sub = lax.broadcasted_iota(jnp.int32, rt.shape, 0)
    oh1, oh2 = sub == e1, sub == e2
    hit = jnp.where(oh1 | oh2, 1.0, 0.0).astype(BF16)
    before = (lax.broadcasted_iota(jnp.int32, (tile, tile), 0)
              < lax.broadcasted_iota(jnp.int32, (tile, tile), 1))
    rank = _dot(hit, jnp.where(before, 1.0, 0.0).astype(BF16))
    sub1 = lax.broadcasted_iota(jnp.int32, (rt.shape[0], 1), 0)
    seg = jnp.zeros((rt.shape[0], 1), F32)
    for e in range(MOE_EXPERTS):
        seg = jnp.where(sub1 == e, lstart_ref[t * MOE_EXPERTS + e].astype(F32), seg)
    slot_of = rank + seg
    pos1 = jnp.sum(jnp.where(oh1, slot_of, 0.0), axis=0, keepdims=True).astype(jnp.int32)
    pos2 = jnp.sum(jnp.where(oh2, slot_of, 0.0), axis=0, keepdims=True).astype(jnp.int32)
    slot = lax.broadcasted_iota(jnp.int32, (slots, tile), 0)
    perm = jnp.where((slot == pos1) | (slot == pos2), 1.0, 0.0).astype(BF16)
    buf_ref[...] = _dot(perm, h_ref[...]).astype(BF16)

    def make(loc, glob):
        return pltpu.make_async_copy(buf_ref.at[pl.ds(loc, ROW_CHUNK)], xs_ref.at[pl.ds(glob, ROW_CHUNK)], sem)

    _segment_copies(t, base_ref, lstart_ref, nch_ref, make, True)
    _segment_copies(t, base_ref, lstart_ref, nch_ref, make, False)

    @pl.when(t == pl.num_programs(0) - 1)
    def _():
        buf_ref[0:FFN_BLOCK, :] = jnp.zeros((FFN_BLOCK, buf_ref.shape[1]), BF16)
        n_blocks = xs_ref.shape[0] // FFN_BLOCK

        def fill(start):
            def per_expert(e, carry):
                def chunk(i, c):
                    row = pl.multiple_of(gap0_ref[e] + i * ROW_CHUNK, ROW_CHUNK)
                    cp = pltpu.make_async_copy(buf_ref.at[0:ROW_CHUNK], xs_ref.at[pl.ds(row, ROW_CHUNK)], sem)
                    cp.start() if start else cp.wait()
                    return c
                return lax.fori_loop(0, gapn_ref[e], chunk, carry)

            def tail(b, c):
                row = pl.multiple_of(b * FFN_BLOCK, FFN_BLOCK)
                cp = pltpu.make_async_copy(buf_ref.at[0:FFN_BLOCK], xs_ref.at[pl.ds(row, FFN_BLOCK)], sem)
                cp.start() if start else cp.wait()
                return c

            lax.fori_loop(0, MOE_EXPERTS, per_expert, 0)
            lax.fori_loop(nused_ref[0], n_blocks, tail, 0)

        fill(True)
        fill(False)


def _dispatch(plan, h2, route, n_rows):
    tile, d = DISPATCH_TILE, D_MODEL
    n_tiles = h2.shape[0] // tile
    return pl.pallas_call(
        _dispatch_kernel,
        out_shape=jax.ShapeDtypeStruct((n_rows, d), BF16),
        grid_spec=pltpu.PrefetchScalarGridSpec(
            num_scalar_prefetch=6,
            grid=(n_tiles,),
            in_specs=[pl.BlockSpec((tile, d), lambda t, *_: (t, 0)),
                      pl.BlockSpec((tile, ROUTER_PAD), lambda t, *_: (t, 0))],
            out_specs=pl.BlockSpec(memory_space=pl.ANY),
            scratch_shapes=[pltpu.VMEM((SLOT_ROWS, d), BF16), pltpu.SemaphoreType.DMA],
        ),
        compiler_params=_params(("arbitrary",)),
        name="moe_dispatch",
    )(plan["base"], plan["lstart"], plan["nch"], plan["gap0"], plan["gapn"], plan["nused"], h2, route)


def _ffn_kernel(bexp_ref, nused_ref, x_ref, w1_ref, w3_ref, w2_ref, y_ref, w13_s, w2_s):
    b = pl.program_id(0)
    used = b < nused_ref[0]
    de = D_EXPERT

    @pl.when(used & ((b == 0) | (bexp_ref[b] != bexp_ref[jnp.maximum(b - 1, 0)])))
    def _():
        w13_s[:, :de] = w1_ref[...].astype(BF16)
        w13_s[:, de:] = w3_ref[...].astype(BF16)
        w2_s[...] = w2_ref[...].astype(BF16)

    @pl.when(used)
    def _():
        ab = _dot(x_ref[...], w13_s[...])
        a = ab[:, :de]
        hid = (a * _sigmoid(a)) * ab[:, de:]
        y_ref[...] = _dot(hid.astype(BF16), w2_s[...]).astype(BF16)

    @pl.when(jnp.logical_not(used))
    def _():
        y_ref[...] = jnp.zeros_like(y_ref)


def _expert_ffn(plan, xs, layer, w1, w3, w2):
    d = D_MODEL
    n_blocks = xs.shape[0] // FFN_BLOCK
    row = lambda b, bexp, nused: (jnp.minimum(b, nused[0] - 1), 0)
    wsel = lambda b, bexp, nused: (layer, bexp[b], 0, 0)
    return pl.pallas_call(
        _ffn_kernel,
        out_shape=jax.ShapeDtypeStruct(xs.shape, BF16),
        grid_spec=pltpu.PrefetchScalarGridSpec(
            num_scalar_prefetch=2,
            grid=(n_blocks,),
            in_specs=[pl.BlockSpec((FFN_BLOCK, d), row),
                      pl.BlockSpec((None, None, d, D_EXPERT), wsel),
                      pl.BlockSpec((None, None, d, D_EXPERT), wsel),
                      pl.BlockSpec((None, None, D_EXPERT, d), wsel)],
            out_specs=pl.BlockSpec((FFN_BLOCK, d), lambda b, bexp, nused: (b, 0)),
            scratch_shapes=[pltpu.VMEM((d, 2 * D_EXPERT), BF16), pltpu.VMEM((D_EXPERT, d), BF16)],
        ),
        compiler_params=_params(("arbitrary",)),
        name="moe_ffn",
    )(plan["bexp"], plan["nused"], xs, w1, w3, w2)


def _combine_kernel(base_ref, lstart_ref, nch_ref, route_ref, x1_ref, mod_ref, ys_ref, o_ref, buf_ref, sem):
    d = D_MODEL
    t = pl.program_id(0)
    tile = route_ref.shape[0]
    slots = buf_ref.shape[0]

    @pl.when(t == 0)
    def _():
        buf_ref[...] = jnp.zeros_like(buf_ref)

    def make(loc, glob):
        return pltpu.make_async_copy(ys_ref.at[pl.ds(glob, ROW_CHUNK)], buf_ref.at[pl.ds(loc, ROW_CHUNK)], sem)

    _segment_copies(t, base_ref, lstart_ref, nch_ref, make, True)
    route = route_ref[...]
    e1, e2 = _route_ids(route, 1)
    lane = lax.broadcasted_iota(jnp.int32, route.shape, 1)
    oh1, oh2 = lane == e1, lane == e2
    hit = jnp.where(oh1 | oh2, 1.0, 0.0).astype(BF16)
    before = (lax.broadcasted_iota(jnp.int32, (tile, tile), 1)
              < lax.broadcasted_iota(jnp.int32, (tile, tile), 0))
    rank = _dot(jnp.where(before, 1.0, 0.0).astype(BF16), hit)
    lane1 = lax.broadcasted_iota(jnp.int32, (1, route.shape[1]), 1)
    seg = jnp.zeros((1, route.shape[1]), F32)
    for e in range(MOE_EXPERTS):
        seg = jnp.where(lane1 == e, lstart_ref[t * MOE_EXPERTS + e].astype(F32), seg)
    slot_of = rank + seg
    pos1 = jnp.sum(jnp.where(oh1, slot_of, 0.0), axis=1, keepdims=True).astype(jnp.int32)
    pos2 = jnp.sum(jnp.where(oh2, slot_of, 0.0), axis=1, keepdims=True).astype(jnp.int32)
    slot = lax.broadcasted_iota(jnp.int32, (tile, slots), 1)
    w1 = route[:, ROUTE_W:ROUTE_W + 1]
    w2 = route[:, ROUTE_W + 1:ROUTE_W + 2]
    comb = (jnp.where(slot == pos1, w1, 0.0) + jnp.where(slot == pos2, w2, 0.0)).astype(BF16)
    _segment_copies(t, base_ref, lstart_ref, nch_ref, make, False)
    moe = _dot(comb, buf_ref[...])
    o_ref[...] = x1_ref[...] + mod_ref[:, 5 * d:6 * d] * moe


def _combine(lay, plan, route, x1, mods, ys):
    tile, d = DISPATCH_TILE, D_MODEL
    assert lay.seq % tile == 0 and (lay.batch * lay.ctx) % tile == 0
    lat_tiles = lay.seq // tile
    n_lat = lay.batch * lat_tiles
    n_tiles = lay.n_tok // tile
    row = lambda t: jnp.where(t < n_lat, t // lat_tiles, lay.batch)
    return pl.pallas_call(
        _combine_kernel,
        out_shape=jax.ShapeDtypeStruct((lay.n_tok, d), F32),
        grid_spec=pltpu.PrefetchScalarGridSpec(
            num_scalar_prefetch=3,
            grid=(n_tiles,),
            in_specs=[pl.BlockSpec((tile, ROUTER_PAD), lambda t, *_: (t, 0)),
                      pl.BlockSpec((tile, d), lambda t, *_: (t, 0)),
                      pl.BlockSpec((None, 1, N_MOD * d), lambda t, *_: (row(t), 0, 0)),
                      pl.BlockSpec(memory_space=pl.ANY)],
            out_specs=pl.BlockSpec((tile, d), lambda t, *_: (t, 0)),
            scratch_shapes=[pltpu.VMEM((SLOT_ROWS, d), BF16), pltpu.SemaphoreType.DMA],
        ),
        compiler_params=_params(("arbitrary",)),
        name="moe_combine",
    )(plan["base"], plan["lstart"], plan["nch"], route, x1, mods, ys)


def _moe(lay, h2, x1, route, mods, layer, w1, w3, w2):
    n_tiles = lay.n_tok // DISPATCH_TILE
    max_rows = (MOE_TOPK * lay.n_tok + n_tiles * MOE_EXPERTS * (ROW_CHUNK - 1)
                + MOE_EXPERTS * (FFN_BLOCK - 1))
    n_blocks = -(-max_rows // FFN_BLOCK)
    plan = _moe_plan(_expert_counts(route, DISPATCH_TILE), n_blocks)
    xs = _dispatch(plan, h2, route, n_blocks * FFN_BLOCK)
    ys = _expert_ffn(plan, xs, layer, w1, w3, w2)
    return _combine(lay, plan, route, x1, mods, ys)


def _final_kernel(x_ref, g_ref, o_ref):
    x = x_ref[...]
    o_ref[...] = x * lax.rsqrt(jnp.mean(x * x, axis=-1, keepdims=True) + EPS) * g_ref[...]


def _final_norm(x, g, rows, tile):
    d = x.shape[1]
    return pl.pallas_call(
        _final_kernel,
        out_shape=jax.ShapeDtypeStruct((rows, d), F32),
        grid=(rows // tile,),
        in_specs=[pl.BlockSpec((tile, d), lambda t: (t, 0)), pl.BlockSpec((1, d), lambda t: (0, 0))],
        out_specs=pl.BlockSpec((tile, d), lambda t: (t, 0)),
        compiler_params=_params(("parallel",)),
        name="final_norm",
    )(x, g)


def _rope_tables(seq, ctx):
    pos = jnp.arange(seq, dtype=jnp.int32)
    inv_freq = ROPE_BASE ** (-jnp.arange(0, AXIS_DIM, 2, dtype=F32) / AXIS_DIM)
    ang_row = (pos // GRID_W).astype(F32)[:, None] * inv_freq
    ang_col = (pos % GRID_W).astype(F32)[:, None] * inv_freq
    cos_h = jnp.concatenate([jnp.cos(ang_row)] * 2 + [jnp.cos(ang_col)] * 2, axis=1)
    sin_h = jnp.concatenate([-jnp.sin(ang_row), jnp.sin(ang_row), -jnp.sin(ang_col), jnp.sin(ang_col)], axis=1)
    reps = LANES // SWA_HEAD_DIM
    cos_t = jnp.concatenate([jnp.tile(cos_h, (1, reps)), jnp.ones((ctx, LANES), F32)], axis=0)
    sin_t = jnp.concatenate([jnp.tile(sin_h, (1, reps)), jnp.zeros((ctx, LANES), F32)], axis=0)
    return cos_t, sin_t


def _channel_dft():
    i = jnp.arange(FNET_GROUP_W, dtype=jnp.int32)
    ang = ((i[:, None] * i[None, :]) % FNET_GROUP_W).astype(F32) * (2.0 * math.pi / FNET_GROUP_W)
    return jnp.concatenate([jnp.cos(ang), jnp.sin(ang)], axis=1).astype(BF16)


def kernel(x, c, ctx, c_ctx, w_ada, b_ada, g_mix, g_ffn, w_in, w_decay_down, w_decay_up, b_decay, g_gla, sink,
           w_pa, w_pb, w_pc, w_out, w_router_group, w_router_expert, w1, w3, w2, g_final):
    batch, seq, d = x.shape
    n_ctx = ctx.shape[1]
    depth = w_ada.shape[0]
    assert d == D_MODEL and seq % GRID_W == 0 and seq % n_ctx == 0
    lay = _Layout(batch, seq, n_ctx, TOKEN_TILE)

    rows = -(-(batch + 1) // 8) * 8
    c_all = jnp.zeros((rows, d), F32).at[:batch].set(c).at[batch].set(c_ctx)
    mods_all = _adaln(c_all, w_ada, b_ada).reshape(depth, rows, 1, N_MOD * d)

    cos_t, sin_t = _rope_tables(seq, n_ctx)
    cs = _channel_dft()
    c_lat, s_lat = _dft_mats(seq)
    c_ctx_m, s_ctx_m = _dft_mats(n_ctx)

    xs = jnp.concatenate([x.reshape(batch * seq, d), ctx.reshape(batch * n_ctx, d)], axis=0)
    rank = w_decay_down.shape[-1]
    w_in_bf = w_in.astype(BF16)
    for l in range(depth):
        last = l == depth - 1
        mods = mods_all[l]
        down = jnp.concatenate([w_decay_down[l, 0], w_decay_down[l, 1]], axis=1)
        down = jnp.pad(down, ((0, 0), (0, DECAY_PAD - 2 * rank))).astype(BF16)
        w_up = jnp.zeros((DECAY_PAD, 2 * GLA_QK_W), F32)
        w_up = w_up.at[:rank, :GLA_QK_W].set(w_decay_up[l, 0]).at[rank:2 * rank, GLA_QK_W:].set(w_decay_up[l, 1])
        b_dec = b_decay[l].reshape(1, 2 * GLA_QK_W)
        (k_a, v_a, k_c, v_c, q_a, r_a, u_cos, u_sin, q_c, gate_logits, la_f, la_b) = _inproj(
            lay, xs, g_mix[l].reshape(1, d), mods, l, w_in_bf, down, w_up.astype(BF16), b_dec, cs, cos_t, sin_t)

        o_f, o_b = _gla(lay, q_a, k_a, v_a, la_f, la_b)
        yb_lat = _fourier(u_cos, u_sin, c_lat, s_lat, 0, seq, min(seq, 1024), min(seq, 512))
        yc_lat = _swa(lay, q_c, k_c, v_c, jnp.broadcast_to(sink[l][:, None], (SWA_HEADS, LANES)))
        if last:
            yb_ctx = jnp.zeros((n_ctx, batch * FNET_W), BF16)
            yc_ctx = jnp.zeros((batch * n_ctx, SWA_Q_W), BF16)
        else:
            yb_ctx = _fourier(u_cos, u_sin, c_ctx_m, s_ctx_m, seq, n_ctx, n_ctx, n_ctx)
            yc_ctx = _ctx_attn(lay, q_c, k_c, v_c, jnp.broadcast_to(sink[l][:, None], (SWA_HEADS, LANES)))

        w_router = jnp.zeros((d, ROUTER_PAD), F32)
        w_router = w_router.at[:, :MOE_EXPERTS].set(w_router_expert[l])
        w_router = w_router.at[:, MOE_EXPERTS:MOE_EXPERTS + MOE_GROUPS].set(w_router_group[l])
        w_router_hi = w_router.astype(BF16)
        w_router = jnp.concatenate([w_router_hi, (w_router - w_router_hi.astype(F32)).astype(BF16)], axis=1)
        x1, h2, gates = _merge(lay, xs, o_f, o_b, r_a, yb_lat, yb_ctx, yc_lat, yc_ctx, gate_logits, mods,
                               g_gla[l].reshape(1, GLA_DV), g_ffn[l].reshape(1, d),
                               w_pa[l].astype(BF16), w_pb[l].astype(BF16), w_pc[l].astype(BF16),
                               w_out[l].astype(BF16), w_router)
        xs = _moe(lay, h2, x1, gates, mods, l, w1, w3, w2)

    out = _final_norm(xs, g_final.reshape(1, d), batch * seq, TOKEN_TILE)
    return out.reshape(batch, seq, d)
```

```python
import functools
import math

import jax
import jax.numpy as jnp
from jax import lax
from jax.experimental import pallas as pl
from jax.experimental.pallas import tpu as pltpu

F32 = jnp.float32
BF16 = jnp.bfloat16
HIGHEST = lax.Precision.HIGHEST

D_MODEL = 1024
GRID_W = 64
EPS = 1e-6
N_MOD = 6
GLA_HEADS = 4
GLA_DK = 64
GLA_DV = 128
GLA_TAU = 16.0
GLA_CHUNK = 64
GLA_SCALE = GLA_DK ** -0.5
FNET_GROUPS = 4
FNET_GROUP_W = 128
SWA_HEADS = 8
SWA_KV_HEADS = 2
SWA_HEAD_DIM = 64
WINDOW = 128
ROPE_BASE = 10000.0
AXIS_DIM = SWA_HEAD_DIM // 2
MOE_GROUPS = 4
MOE_EXPERTS_PER_GROUP = 4
MOE_EXPERTS = MOE_GROUPS * MOE_EXPERTS_PER_GROUP
MOE_TOPK = 2
D_EXPERT = 512

GLA_QK_W = GLA_HEADS * GLA_DK
GLA_V_W = GLA_HEADS * GLA_DV
FNET_W = FNET_GROUPS * FNET_GROUP_W
SWA_Q_W = SWA_HEADS * SWA_HEAD_DIM
SWA_KV_W = SWA_KV_HEADS * SWA_HEAD_DIM
IN_SIZES = (GLA_QK_W, GLA_V_W, SWA_KV_W, SWA_KV_W, GLA_QK_W, GLA_V_W, FNET_W, SWA_Q_W, 3 * D_MODEL)
IN_OFFS = tuple(int(sum(IN_SIZES[:i])) for i in range(len(IN_SIZES) + 1))
IN_COLS = IN_OFFS[-1]

LANES = 128
TOKEN_TILE = 256
DECAY_PAD = LANES
ROUTER_PAD = LANES
ROUTE_ID = 0
ROUTE_W = 2
DISPATCH_TILE = 512
ROW_CHUNK = 16
FFN_BLOCK = 512
SLOT_ROWS = -(-(MOE_TOPK * DISPATCH_TILE + MOE_EXPERTS * (ROW_CHUNK - 1)) // LANES) * LANES
VMEM_LIMIT = 56 * 1024 * 1024


def _params(sem, vmem=VMEM_LIMIT):
    return pltpu.CompilerParams(dimension_semantics=sem, vmem_limit_bytes=vmem)


def _sigmoid(x):
    return 0.5 * jnp.tanh(0.5 * x) + 0.5


def _dot(a, b):
    return jnp.dot(a, b, preferred_element_type=F32)


def _dot_nt(a, b):
    return lax.dot_general(a, b, (((1,), (1,)), ((), ())), preferred_element_type=F32)


def _ada_kernel(c_ref, w_ref, b_ref, o_ref):
    c = c_ref[...]
    a = c * _sigmoid(c)
    o_ref[...] = jnp.dot(a, w_ref[...], preferred_element_type=F32, precision=HIGHEST) + b_ref[...]


def _adaln(c_all, w_ada, b_ada):
    depth, d, n = w_ada.shape
    rows = c_all.shape[0]
    tn = 1536
    return pl.pallas_call(
        _ada_kernel,
        out_shape=jax.ShapeDtypeStruct((depth, rows, n), F32),
        grid=(depth, n // tn),
        in_specs=[pl.BlockSpec((rows, d), lambda l, j: (0, 0)),
                  pl.BlockSpec((None, d, tn), lambda l, j: (l, 0, j)),
                  pl.BlockSpec((None, 1, tn), lambda l, j: (l, 0, j))],
        out_specs=pl.BlockSpec((None, rows, tn), lambda l, j: (l, 0, j)),
        compiler_params=_params(("parallel", "parallel")),
        name="adaln",
    )(c_all, w_ada, b_ada.reshape(depth, 1, n))


class _Layout:
    def __init__(self, batch, seq, ctx, tile):
        assert seq % tile == 0 and ctx % tile == 0
        self.batch, self.seq, self.ctx, self.tile = batch, seq, ctx, tile
        self.lat_tiles = seq // tile
        self.ctx_tiles = ctx // tile
        self.n_lat = batch * self.lat_tiles
        self.n_tiles = self.n_lat + batch * self.ctx_tiles
        self.n_tok = self.n_tiles * tile

    def batch_of(self, t):
        return jnp.where(t < self.n_lat, t // self.lat_tiles, (t - self.n_lat) // self.ctx_tiles)

    def mod_row(self, t):
        return jnp.where(t < self.n_lat, t // self.lat_tiles, self.batch)

    def seq_tile(self, t):
        return jnp.where(t < self.n_lat, t % self.lat_tiles,
                         self.lat_tiles + (t - self.n_lat) % self.ctx_tiles)


def _rope(x, cos, sin_signed):
    n = x.shape[-1]
    lane = lax.broadcasted_iota(jnp.int32, x.shape, 1)
    half = AXIS_DIM // 2
    partner = jnp.where((lane & half) == 0, pltpu.roll(x, n - half, 1), pltpu.roll(x, half, 1))
    return x * cos + partner * sin_signed


def _inproj_kernel(x_ref, g_ref, mod_ref, w_ref, wd_ref, wu_ref, bdec_ref, cs_ref, cos_ref, sin_ref,
                   ka_ref, va_ref, kc_ref, vc_ref, qa_ref, ra_ref, ua_ref, us_ref, qc_ref, gl_ref,
                   laf_ref, lab_ref):
    d = D_MODEL
    x = x_ref[...]
    shift = mod_ref[:, 0:d]
    scale = mod_ref[:, d:2 * d]
    h = x * lax.rsqrt(jnp.mean(x * x, axis=-1, keepdims=True) + EPS) * g_ref[...]
    hb = (h * (1.0 + scale) + shift).astype(BF16)

    def proj(i):
        return _dot(hb, w_ref[:, IN_OFFS[i]:IN_OFFS[i + 1]])

    ka_ref[...] = proj(0).astype(BF16)
    va_ref[...] = proj(1).astype(BF16)
    cos = cos_ref[...]
    sin = sin_ref[...]
    kc_ref[...] = _rope(proj(2), cos, sin).astype(BF16)
    vc_ref[...] = proj(3).astype(BF16)
    qa_ref[...] = proj(4).astype(BF16)
    ra_ref[...] = proj(5).astype(BF16)
    u = proj(6).astype(BF16)
    for g in range(FNET_GROUPS):
        sl = slice(g * FNET_GROUP_W, (g + 1) * FNET_GROUP_W)
        ab = _dot(u[:, sl], cs_ref[...])
        ua_ref[:, sl] = ab[:, :FNET_GROUP_W].astype(BF16)
        us_ref[:, sl] = ab[:, FNET_GROUP_W:].astype(BF16)
    reps = SWA_Q_W // LANES
    qc_ref[...] = (_rope(proj(7), jnp.concatenate([cos] * reps, axis=1), jnp.concatenate([sin] * reps, axis=1))
                   * SWA_HEAD_DIM ** -0.5).astype(BF16)
    gl_ref[...] = proj(8).astype(BF16)
    low = _dot(hb, wd_ref[...]).astype(BF16)
    z = _dot(low, wu_ref[...]) + bdec_ref[...]
    la = (jnp.minimum(z, 0.0) - jnp.log(1.0 + jnp.exp(-jnp.abs(z)))) * (1.0 / GLA_TAU)
    laf_ref[...] = la[:, :GLA_QK_W]
    lab_ref[...] = la[:, GLA_QK_W:]


def _inproj(lay, x, g_mix, mods, layer, w_in, w_down, w_up, b_dec, cs, cos_t, sin_t):
    tm, d = lay.tile, D_MODEL
    n = lay.n_tok
    bf = lambda w: jax.ShapeDtypeStruct((n, w), BF16)
    tok = lambda w: pl.BlockSpec((tm, w), lambda t: (t, 0))
    const = lambda a: pl.BlockSpec(a.shape, lambda t: (0,) * a.ndim)
    seq_rows = lay.seq + lay.ctx
    fnet_shape = jax.ShapeDtypeStruct((seq_rows, lay.batch * FNET_W), BF16)
    fnet_spec = pl.BlockSpec((tm, FNET_W), lambda t: (lay.seq_tile(t), lay.batch_of(t)))
    pos_spec = pl.BlockSpec((tm, LANES), lambda t: (lay.seq_tile(t), 0))
    return pl.pallas_call(
        _inproj_kernel,
        out_shape=(bf(GLA_QK_W), bf(GLA_V_W), bf(SWA_KV_W), bf(SWA_KV_W), bf(GLA_QK_W), bf(GLA_V_W),
                   fnet_shape, fnet_shape, bf(SWA_Q_W), bf(3 * d),
                   jax.ShapeDtypeStruct((n, GLA_QK_W), F32), jax.ShapeDtypeStruct((n, GLA_QK_W), F32)),
        grid=(lay.n_tiles,),
        in_specs=[tok(d), const(g_mix),
                  pl.BlockSpec((None, 1, N_MOD * d), lambda t: (lay.mod_row(t), 0, 0)),
                  pl.BlockSpec((None,) + w_in.shape[1:], lambda t: (layer, 0, 0)),
                  const(w_down), const(w_up), const(b_dec), const(cs), pos_spec, pos_spec],
        out_specs=(tok(GLA_QK_W), tok(GLA_V_W), tok(SWA_KV_W), tok(SWA_KV_W), tok(GLA_QK_W), tok(GLA_V_W),
                   fnet_spec, fnet_spec, tok(SWA_Q_W), tok(3 * d), tok(GLA_QK_W), tok(GLA_QK_W)),
        compiler_params=_params(("parallel",)),
        name="inproj",
    )(x, g_mix, mods, w_in, w_down, w_up, b_dec, cs, cos_t, sin_t)


def _split3(x):
    a1 = x.astype(BF16)
    r1 = x - a1.astype(F32)
    a2 = r1.astype(BF16)
    a3 = (r1 - a2.astype(F32)).astype(BF16)
    return a1, a2, a3


def _gla_direction(q_ref, k_ref, v_ref, la_ref, o_ref, s_ref, rev):
    t_rows = q_ref.shape[0]
    c = GLA_CHUNK
    n_sub = t_rows // c
    hk = GLA_QK_W
    la3 = _split3(la_ref[...])
    row = lax.broadcasted_iota(jnp.int32, (t_rows, t_rows), 0)
    col = lax.broadcasted_iota(jnp.int32, (t_rows, t_rows), 1)
    tri = ((row // c) == (col // c)) & ((col >= row) if rev else (col <= row))
    tri = jnp.where(tri, 1.0, 0.0).astype(BF16)
    lc = _dot(tri, la3[0]) + _dot(tri, la3[1]) + _dot(tri, la3[2])
    q = q_ref[...].astype(F32)
    k = k_ref[...].astype(F32)
    qd = (q * jnp.exp(lc) * GLA_SCALE).astype(BF16)
    kd = (k * jnp.exp(-lc)).astype(BF16)
    hrow = lax.broadcasted_iota(jnp.int32, (GLA_HEADS * c, hk), 0) // c
    hcol = lax.broadcasted_iota(jnp.int32, (GLA_HEADS * c, hk), 1) // GLA_DK
    head_mask = hrow == hcol
    trow = lax.broadcasted_iota(jnp.int32, (GLA_HEADS * c, c), 0) % c
    scol = lax.broadcasted_iota(jnp.int32, (GLA_HEADS * c, c), 1)
    causal = (scol >= trow) if rev else (scol <= trow)
    edge = 0 if rev else c - 1
    lasts = [lc[i * c + edge:i * c + edge + 1] for i in range(n_sub)]
    to_end = jnp.concatenate([jnp.broadcast_to(l, (c, hk)) for l in lasts], axis=0) - lc
    kh_t = (k * jnp.exp(to_end)).T
    pad = jnp.zeros((LANES - n_sub, hk), F32)
    decay_t = jnp.exp(jnp.concatenate(lasts + [pad], axis=0).T)
    chunk_of = lax.broadcasted_iota(jnp.int32, (1, t_rows), 1) // c
    kv = []
    for h in range(GLA_HEADS):
        kh_h = kh_t[h * GLA_DK:(h + 1) * GLA_DK]
        stack = jnp.concatenate([jnp.where(chunk_of == i, kh_h, 0.0) for i in range(n_sub)], axis=0)
        kv.append(_dot(stack.astype(BF16), v_ref[:, h * GLA_DV:(h + 1) * GLA_DV]))
    order = range(n_sub - 1, -1, -1) if rev else range(n_sub)
    for i in order:
        rows = slice(i * c, (i + 1) * c)
        v_i = v_ref[rows, :]
        q_stack = jnp.where(head_mask, jnp.concatenate([qd[rows]] * GLA_HEADS, axis=0), 0.0).astype(BF16)
        state = s_ref[...]
        inter = _dot(q_stack, state.astype(BF16))
        scores = jnp.where(causal, _dot_nt(q_stack, kd[rows]), 0.0).astype(BF16)
        decay = decay_t[:, i:i + 1]
        outs = []
        for h in range(GLA_HEADS):
            hr = slice(h * c, (h + 1) * c)
            hv = slice(h * GLA_DV, (h + 1) * GLA_DV)
            outs.append(inter[hr] + _dot(scores[hr], v_i[:, hv]))
            kr = slice(h * GLA_DK, (h + 1) * GLA_DK)
            s_ref[kr, :] = decay[kr] * state[kr] + kv[h][i * GLA_DK:(i + 1) * GLA_DK]
        o_ref[rows, :] = jnp.concatenate(outs, axis=1).astype(BF16)


def _gla_kernel(qf_ref, kf_ref, vf_ref, laf_ref, qb_ref, kb_ref, vb_ref, lab_ref,
                of_ref, ob_ref, sf_ref, sb_ref):
    @pl.when(pl.program_id(1) == 0)
    def _():
        sf_ref[...] = jnp.zeros_like(sf_ref)
        sb_ref[...] = jnp.zeros_like(sb_ref)

    _gla_direction(qf_ref, kf_ref, vf_ref, laf_ref, of_ref, sf_ref, False)
    _gla_direction(qb_ref, kb_ref, vb_ref, lab_ref, ob_ref, sb_ref, True)


def _gla(lay, q_a, k_a, v_a, la_f, la_b):
    tm = lay.tile
    nc, nl = lay.ctx_tiles, lay.lat_tiles

    def fwd(b, j):
        return jnp.where(j < nc, lay.n_lat + b * nc + j, b * nl + (j - nc))

    def bwd(b, j):
        return jnp.where(j < nc, lay.n_lat + b * nc + (nc - 1 - j), b * nl + (nl - 1 - (j - nc)))

    spec = lambda w, f: pl.BlockSpec((tm, w), lambda b, j: (f(b, j), 0))
    out = jax.ShapeDtypeStruct((lay.n_tok, GLA_V_W), BF16)
    return pl.pallas_call(
        _gla_kernel,
        out_shape=(out, out),
        grid=(lay.batch, nc + nl),
        in_specs=[spec(GLA_QK_W, fwd), spec(GLA_QK_W, fwd), spec(GLA_V_W, fwd), spec(GLA_QK_W, fwd),
                  spec(GLA_QK_W, bwd), spec(GLA_QK_W, bwd), spec(GLA_V_W, bwd), spec(GLA_QK_W, bwd)],
        out_specs=(spec(GLA_V_W, fwd), spec(GLA_V_W, bwd)),
        scratch_shapes=[pltpu.VMEM((GLA_QK_W, GLA_DV), F32), pltpu.VMEM((GLA_QK_W, GLA_DV), F32)],
        compiler_params=_params(("parallel", "arbitrary")),
        name="gla",
    )(q_a, k_a, v_a, la_f, q_a, k_a, v_a, la_b)


def _dft_kernel(c_ref, s_ref, a_ref, b_ref, o_ref, acc_ref, *, scale):
    k = pl.program_id(1)

    @pl.when(k == 0)
    def _():
        acc_ref[...] = jnp.zeros_like(acc_ref)

    acc_ref[...] += _dot(c_ref[...], a_ref[...]) + _dot(s_ref[...], b_ref[...])

    @pl.when(k == pl.num_programs(1) - 1)
    def _():
        o_ref[...] = (acc_ref[...] * scale).astype(BF16)


def _dft_mats(n):
    f = GRID_W
    assert n % f == 0
    k = jnp.arange(n, dtype=jnp.int32)[None, :]

    def table(rows, period):
        ang = ((jnp.arange(rows, dtype=jnp.int32)[:, None] * k) % period).astype(F32) * (2.0 * math.pi / period)
        return jnp.cos(ang), jnp.sin(ang)

    ca, sa = table(n // f, n // f)
    cb, sb = table(f, n)
    cos = ca[:, None, :] * cb[None, :, :] - sa[:, None, :] * sb[None, :, :]
    sin = sa[:, None, :] * cb[None, :, :] + ca[:, None, :] * sb[None, :, :]
    return cos.reshape(n, n).astype(BF16), (-sin).reshape(n, n).astype(BF16)


def _fourier(ua, us, cmat, smat, row0, length, tm, tk):
    width = ua.shape[1]
    off = row0 // tk
    scale = 1.0 / math.sqrt(length * FNET_GROUP_W)
    return pl.pallas_call(
        functools.partial(_dft_kernel, scale=scale),
        out_shape=jax.ShapeDtypeStruct((length, width), BF16),
        grid=(length // tm, length // tk),
        in_specs=[pl.BlockSpec((tm, tk), lambda i, k: (i, k)),
                  pl.BlockSpec((tm, tk), lambda i, k: (i, k)),
                  pl.BlockSpec((tk, width), lambda i, k: (off + k, 0)),
                  pl.BlockSpec((tk, width), lambda i, k: (off + k, 0))],
        out_specs=pl.BlockSpec((tm, width), lambda i, k: (i, 0)),
        scratch_shapes=[pltpu.VMEM((tm, width), F32)],
        compiler_params=_params(("parallel", "arbitrary")),
        name="fourier",
    )(cmat, smat, ua, us)


def _attend(q_ref, k_all, v_all, valid, sink_ref, o_ref):
    hd = SWA_HEAD_DIM
    heads_per_kv = SWA_HEADS // SWA_KV_HEADS
    k_sw = jnp.concatenate([k_all[:, hd:], k_all[:, :hd]], axis=1)
    v_t = v_all.astype(F32).T
    v_t_sw = jnp.concatenate([v_t[hd:], v_t[:hd]], axis=0)
    lane = lax.broadcasted_iota(jnp.int32, (1, LANES), 1)
    lane_half = (lane < hd, lane >= hd)
    row = lax.broadcasted_iota(jnp.int32, (LANES, 1), 0)
    row_half = (row < hd, row >= hd)
    zero = jnp.zeros((), BF16)
    for p in range(SWA_HEADS // 2):
        acc = None
        for par in range(2):
            head = 2 * p + par
            aligned = head // heads_per_kv == par
            qh = jnp.where(lane_half[par], q_ref[:, p * LANES:(p + 1) * LANES], zero)
            s = _dot_nt(k_all if aligned else k_sw, qh)
            if valid is not None:
                kw = valid.shape[0]
                s = jnp.concatenate([jnp.where(valid, s[:kw], -jnp.inf), s[kw:]], axis=0)
            sink = sink_ref[head:head + 1, 0:1]
            m = jnp.maximum(jnp.max(s, axis=0, keepdims=True), sink)
            e = jnp.exp(s - m)
            denom = jnp.sum(e, axis=0, keepdims=True) + jnp.exp(sink - m)
            v_use = jnp.where(row_half[par], v_t if aligned else v_t_sw, 0.0).astype(BF16)
            part = _dot(v_use, e.astype(BF16)) * (1.0 / denom)
            acc = part if acc is None else acc + part
        o_ref[:, p * LANES:(p + 1) * LANES] = acc.T.astype(BF16)


def _swa_kernel(q_ref, kp_ref, kc_ref, kn_ref, kx_ref, vp_ref, vc_ref, vn_ref, vx_ref, sink_ref, o_ref,
                *, seq):
    n = pl.program_id(1)
    w = WINDOW
    k_blocks = [kp_ref[...], kc_ref[0:w, :], kc_ref[w:2 * w, :], kn_ref[...]]
    v_blocks = [vp_ref[...], vc_ref[0:w, :], vc_ref[w:2 * w, :], vn_ref[...]]
    j = lax.broadcasted_iota(jnp.int32, (3 * w, w), 0)
    a = lax.broadcasted_iota(jnp.int32, (3 * w, w), 1)
    band = (j >= a) & (j - a <= 2 * w)
    for half in range(2):
        key_pos = (2 * n + half - 1) * w + j
        valid = band & (key_pos >= 0) & (key_pos < seq)
        k_all = jnp.concatenate(k_blocks[half:half + 3] + [kx_ref[...]], axis=0)
        v_all = jnp.concatenate(v_blocks[half:half + 3] + [vx_ref[...]], axis=0)
        rows = pl.ds(half * w, w)
        _attend(q_ref.at[rows], k_all, v_all, valid, sink_ref, o_ref.at[rows])


def _swa(lay, q_c, k_c, v_c, sink_b):
    w = WINDOW
    assert lay.seq % (2 * w) == 0
    nq = lay.seq // w
    steps = nq // 2
    ctx_blk = (lay.batch * lay.seq) // lay.ctx

    def edge(f):
        return pl.BlockSpec((w, SWA_KV_W), lambda b, n: (b * nq + f(n), 0))

    prev = edge(lambda n: jnp.maximum(2 * n - 1, 0))
    nxt = edge(lambda n: jnp.minimum(2 * n + 2, nq - 1))
    cur = pl.BlockSpec((2 * w, SWA_KV_W), lambda b, n: (b * steps + n, 0))
    ctx_spec = pl.BlockSpec((lay.ctx, SWA_KV_W), lambda b, n: (ctx_blk + b, 0))
    q_spec = pl.BlockSpec((2 * w, SWA_Q_W), lambda b, n: (b * steps + n, 0))
    return pl.pallas_call(
        functools.partial(_swa_kernel, seq=lay.seq),
        out_shape=jax.ShapeDtypeStruct((lay.batch * lay.seq, SWA_Q_W), BF16),
        grid=(lay.batch, steps),
        in_specs=[q_spec, prev, cur, nxt, ctx_spec, prev, cur, nxt, ctx_spec,
                  pl.BlockSpec(sink_b.shape, lambda b, n: (0, 0))],
        out_specs=q_spec,
        compiler_params=_params(("parallel", "parallel")),
        name="swa",
    )(q_c, k_c, k_c, k_c, k_c, v_c, v_c, v_c, v_c, sink_b)


def _ctx_attn_kernel(q_ref, kx_ref, vx_ref, sink_ref, o_ref):
    _attend(q_ref, kx_ref[...], vx_ref[...], None, sink_ref, o_ref)


def _ctx_attn(lay, q_c, k_c, v_c, sink_b):
    ctx_blk = (lay.batch * lay.seq) // lay.ctx
    spec = lambda wd: pl.BlockSpec((lay.ctx, wd), lambda b: (ctx_blk + b, 0))
    return pl.pallas_call(
        _ctx_attn_kernel,
        out_shape=jax.ShapeDtypeStruct((lay.batch * lay.ctx, SWA_Q_W), BF16),
        grid=(lay.batch,),
        in_specs=[spec(SWA_Q_W), spec(SWA_KV_W), spec(SWA_KV_W),
                  pl.BlockSpec(sink_b.shape, lambda b: (0, 0))],
        out_specs=pl.BlockSpec((lay.ctx, SWA_Q_W), lambda b: (b, 0)),
        compiler_params=_params(("parallel",)),
        name="ctx_attn",
    )(q_c, k_c, v_c, sink_b)


def _merge_kernel(x_ref, of_ref, ob_ref, ra_ref, ybl_ref, ybc_ref, ycl_ref, ycc_ref, gl_ref, mod_ref,
                  ggla_ref, gffn_ref, wpa_ref, wpb_ref, wpc_ref, wout_ref, wr_ref,
                  x1_ref, h2_ref, gates_ref, count_ref, *, n_lat):
    d = D_MODEL
    is_ctx = pl.program_id(0) >= n_lat
    o = of_ref[...].astype(F32) + ob_ref[...].astype(F32)
    r = ra_ref[...].astype(F32)
    parts = []
    for h in range(GLA_HEADS):
        sl = slice(h * GLA_DV, (h + 1) * GLA_DV)
        oh = o[:, sl]
        parts.append(oh * lax.rsqrt(jnp.mean(oh * oh, axis=-1, keepdims=True) + EPS) * ggla_ref[...])
    y_a = (jnp.concatenate(parts, axis=1) * (r * _sigmoid(r))).astype(BF16)
    y_b = jnp.where(is_ctx, ybc_ref[...], ybl_ref[...])
    y_c = jnp.where(is_ctx, ycc_ref[...], ycl_ref[...])
    gl = gl_ref[...].astype(F32)
    mix = (_sigmoid(gl[:, 0:d]) * _dot(y_a, wpa_ref[...])
           + _sigmoid(gl[:, d:2 * d]) * _dot(y_b, wpb_ref[...])
           + _sigmoid(gl[:, 2 * d:3 * d]) * _dot(y_c, wpc_ref[...]))
    y = _dot(mix.astype(BF16), wout_ref[...])
    x1 = x_ref[...] + mod_ref[:, 2 * d:3 * d] * y
    x1_ref[...] = x1
    h2 = x1 * lax.rsqrt(jnp.mean(x1 * x1, axis=-1, keepdims=True) + EPS) * gffn_ref[...]
    h2 = h2 * (1.0 + mod_ref[:, 4 * d:5 * d]) + mod_ref[:, 3 * d:4 * d]
    h2_hi = h2.astype(BF16)
    h2_ref[...] = h2_hi
    h2_lo = (h2 - h2_hi.astype(F32)).astype(BF16)
    both = _dot(h2_hi, wr_ref[...])
    logits = both[:, :ROUTER_PAD] + both[:, ROUTER_PAD:] + _dot(h2_lo, wr_ref[:, :ROUTER_PAD])
    lane = lax.broadcasted_iota(jnp.int32, logits.shape, 1)
    neg = -jnp.inf
    big = ROUTER_PAD
    is_group = (lane >= MOE_EXPERTS) & (lane < MOE_EXPERTS + MOE_GROUPS)
    gl_m = jnp.where(is_group, logits, neg)
    g_max = jnp.max(gl_m, axis=-1, keepdims=True)
    g_sel = jnp.min(jnp.where(gl_m == g_max, lane, big), axis=-1, keepdims=True) - MOE_EXPERTS
    g_gate = 1.0 / jnp.sum(jnp.where(is_group, jnp.exp(logits - g_max), 0.0), axis=-1, keepdims=True)
    lo = g_sel * MOE_EXPERTS_PER_GROUP
    in_group = (lane >= lo) & (lane < lo + MOE_EXPERTS_PER_GROUP)
    e1 = jnp.where(in_group, logits, neg)
    v1 = jnp.max(e1, axis=-1, keepdims=True)
    i1 = jnp.min(jnp.where(e1 == v1, lane, big), axis=-1, keepdims=True)
    e2 = jnp.where(lane == i1, neg, e1)
    v2 = jnp.max(e2, axis=-1, keepdims=True)
    i2 = jnp.min(jnp.where(e2 == v2, lane, big), axis=-1, keepdims=True)
    t = jnp.exp(v2 - v1)
    w1 = g_gate / (1.0 + t)
    w2 = g_gate * t / (1.0 + t)
    route = jnp.where(lane == ROUTE_ID, i1.astype(F32), 0.0) + jnp.where(lane == ROUTE_ID + 1, i2.astype(F32), 0.0)
    route = route + jnp.where(lane == ROUTE_W, w1, 0.0) + jnp.where(lane == ROUTE_W + 1, w2, 0.0)
    gates_ref[...] = route
    hit = jnp.where((lane == i1) | (lane == i2), 1.0, 0.0)
    count_ref[...] = jnp.sum(hit, axis=0, keepdims=True).astype(jnp.int32)


def _merge(lay, x, o_f, o_b, r_a, yb_lat, yb_ctx, yc_lat, yc_ctx, gate_logits, mods, g_gla, g_ffn,
           w_pa, w_pb, w_pc, w_out, w_router):
    tm, d = lay.tile, D_MODEL
    n = lay.n_tok
    tok = lambda w: pl.BlockSpec((tm, w), lambda t: (t, 0))
    const = lambda a: pl.BlockSpec(a.shape, lambda t: (0,) * a.ndim)
    lt, ct = lay.lat_tiles, lay.ctx_tiles
    yb_lat_spec = pl.BlockSpec((tm, FNET_W), lambda t: (jnp.minimum(t, lay.n_lat - 1) % lt,
                                                         jnp.minimum(t, lay.n_lat - 1) // lt))
    yb_ctx_spec = pl.BlockSpec((tm, FNET_W), lambda t: (jnp.maximum(t - lay.n_lat, 0) % ct,
                                                         jnp.maximum(t - lay.n_lat, 0) // ct))
    yc_lat_spec = pl.BlockSpec((tm, SWA_Q_W), lambda t: (jnp.minimum(t, lay.n_lat - 1), 0))
    yc_ctx_spec = pl.BlockSpec((tm, SWA_Q_W), lambda t: (jnp.maximum(t - lay.n_lat, 0), 0))
    return pl.pallas_call(
        functools.partial(_merge_kernel, n_lat=lay.n_lat),
        out_shape=(jax.ShapeDtypeStruct((n, d), F32), jax.ShapeDtypeStruct((n, d), BF16),
                   jax.ShapeDtypeStruct((n, ROUTER_PAD), F32),
                   jax.ShapeDtypeStruct((lay.n_tiles, 1, ROUTER_PAD), jnp.int32)),
        grid=(lay.n_tiles,),
        in_specs=[tok(d), tok(GLA_V_W), tok(GLA_V_W), tok(GLA_V_W), yb_lat_spec, yb_ctx_spec,
                  yc_lat_spec, yc_ctx_spec, tok(3 * d),
                  pl.BlockSpec((None, 1, N_MOD * d), lambda t: (lay.mod_row(t), 0, 0)),
                  const(g_gla), const(g_ffn), const(w_pa), const(w_pb), const(w_pc), const(w_out),
                  const(w_router)],
        out_specs=(tok(d), tok(d), tok(ROUTER_PAD),
                   pl.BlockSpec((None, 1, ROUTER_PAD), lambda t: (t, 0, 0))),
        compiler_params=_params(("parallel",)),
        name="merge",
    )(x, o_f, o_b, r_a, yb_lat, yb_ctx, yc_lat, yc_ctx, gate_logits, mods, g_gla, g_ffn,
      w_pa, w_pb, w_pc, w_out, w_router)


def _route_ids(route, axis):
    take = (lambda i: route[:, i:i + 1]) if axis == 1 else (lambda i: route[i:i + 1, :])
    return take(ROUTE_ID).astype(jnp.int32), take(ROUTE_ID + 1).astype(jnp.int32)


def _moe_plan(counts, n_blocks):
    cnt = counts
    pc = (cnt + ROW_CHUNK - 1) // ROW_CHUNK * ROW_CHUNK
    lstart = jnp.cumsum(pc, axis=1) - pc
    tot = jnp.sum(pc, axis=0)
    tot_pad = (tot + FFN_BLOCK - 1) // FFN_BLOCK * FFN_BLOCK
    eend = jnp.cumsum(tot_pad)
    estart = eend - tot_pad
    base = estart[None, :] + jnp.cumsum(pc, axis=0) - pc
    n_used = eend[-1] // FFN_BLOCK
    blk = jnp.minimum(jnp.arange(n_blocks, dtype=jnp.int32), n_used - 1)
    bexp = jnp.sum((blk[:, None] * FFN_BLOCK >= eend[None, :]).astype(jnp.int32), axis=1)
    flat = lambda a: a.reshape(-1).astype(jnp.int32)
    return dict(base=flat(base), lstart=flat(lstart), nch=flat(pc // ROW_CHUNK),
                gap0=flat(estart + tot), gapn=flat((tot_pad - tot) // ROW_CHUNK),
                bexp=flat(bexp), nused=flat(n_used))


def _segment_copies(t, base_ref, lstart_ref, nch_ref, make, start):
    def per_expert(e, carry):
        idx = t * MOE_EXPERTS + e
        loc = lstart_ref[idx]
        glob = base_ref[idx]

        def chunk(i, c):
            cp = make(pl.multiple_of(loc + i * ROW_CHUNK, ROW_CHUNK), pl.multiple_of(glob + i * ROW_CHUNK, ROW_CHUNK))
            cp.start() if start else cp.wait()
            return c

        return lax.fori_loop(0, nch_ref[idx], chunk, carry)

    lax.fori_loop(0, MOE_EXPERTS, per_expert, 0)


def _dispatch_kernel(base_ref, lstart_ref, nch_ref, gap0_ref, gapn_ref, nused_ref, h_ref, route_ref, xs_ref,
                     buf_ref, sem):
    t = pl.program_id(0)
    tile = h_ref.shape[0]
    slots = buf_ref.shape[1]
    rt = route_ref[...].T
    e1, e2 = _route_ids(rt, 0)
    sub = lax.broadcasted_iota(jnp.int32, rt.shape, 0)
    oh1, oh2 = sub == e1, sub == e2
    hit = jnp.where(oh1 | oh2, 1.0, 0.0).astype(BF16)
    before = (lax.broadcasted_iota(jnp.int32, (tile, tile), 0)
              < lax.broadcasted_iota(jnp.int32, (tile, tile), 1))
    rank = _dot(hit, jnp.where(before, 1.0, 0.0).astype(BF16))
    sub1 = lax.broadcasted_iota(jnp.int32, (rt.shape[0], 1), 0)
    seg = jnp.zeros((rt.shape[0], 1), F32)
    for e in range(MOE_EXPERTS):
        seg = jnp.where(sub1 == e, lstart_ref[t * MOE_EXPERTS + e].astype(F32), seg)
    slot_of = rank + seg
    pos1 = jnp.sum(jnp.where(oh1, slot_of, 0.0), axis=0, keepdims=True).astype(jnp.int32)
    pos2 = jnp.sum(jnp.where(oh2, slot_of, 0.0), axis=0, keepdims=True).astype(jnp.int32)
    slot = lax.broadcasted_iota(jnp.int32, (slots, tile), 0)
    perm = jnp.where((slot == pos1) | (slot == pos2), 1.0, 0.0).astype(BF16)
    cur = t % 2
    buf_ref[cur] = _dot(perm, h_ref[...]).astype(BF16)

    def maker(b):
        def make(loc, glob):
            return pltpu.make_async_copy(buf_ref.at[b, pl.ds(loc, ROW_CHUNK)], xs_ref.at[pl.ds(glob, ROW_CHUNK)],
                                         sem.at[b])
        return make

    _segment_copies(t, base_ref, lstart_ref, nch_ref, maker(cur), True)

    @pl.when(t > 0)
    def _():
        _segment_copies(t - 1, base_ref, lstart_ref, nch_ref, maker(1 - cur), False)

    @pl.when(t == pl.num_programs(0) - 1)
    def _():
        _segment_copies(t, base_ref, lstart_ref, nch_ref, maker(cur), False)
        buf_ref[cur, 0:FFN_BLOCK, :] = jnp.zeros((FFN_BLOCK, buf_ref.shape[2]), BF16)
        n_blocks = xs_ref.shape[0] // FFN_BLOCK

        def fill(start):
            def per_expert(e, carry):
                def chunk(i, c):
                    row = pl.multiple_of(gap0_ref[e] + i * ROW_CHUNK, ROW_CHUNK)
                    cp = pltpu.make_async_copy(buf_ref.at[cur, 0:ROW_CHUNK], xs_ref.at[pl.ds(row, ROW_CHUNK)],
                                               sem.at[cur])
                    cp.start() if start else cp.wait()
                    return c
                return lax.fori_loop(0, gapn_ref[e], chunk, carry)

            def tail(b, c):
                row = pl.multiple_of(b * FFN_BLOCK, FFN_BLOCK)
                cp = pltpu.make_async_copy(buf_ref.at[cur, 0:FFN_BLOCK], xs_ref.at[pl.ds(row, FFN_BLOCK)],
                                           sem.at[cur])
                cp.start() if start else cp.wait()
                return c

            lax.fori_loop(0, MOE_EXPERTS, per_expert, 0)
            lax.fori_loop(nused_ref[0], n_blocks, tail, 0)

        fill(True)
        fill(False)


def _dispatch(plan, h2, route, n_rows):
    tile, d = DISPATCH_TILE, D_MODEL
    n_tiles = h2.shape[0] // tile
    return pl.pallas_call(
        _dispatch_kernel,
        out_shape=jax.ShapeDtypeStruct((n_rows, d), BF16),
        grid_spec=pltpu.PrefetchScalarGridSpec(
            num_scalar_prefetch=6,
            grid=(n_tiles,),
            in_specs=[pl.BlockSpec((tile, d), lambda t, *_: (t, 0)),
                      pl.BlockSpec((tile, ROUTER_PAD), lambda t, *_: (t, 0))],
            out_specs=pl.BlockSpec(memory_space=pl.ANY),
            scratch_shapes=[pltpu.VMEM((2, SLOT_ROWS, d), BF16), pltpu.SemaphoreType.DMA((2,))],
        ),
        compiler_params=_params(("arbitrary",)),
        name="moe_dispatch",
    )(plan["base"], plan["lstart"], plan["nch"], plan["gap0"], plan["gapn"], plan["nused"], h2, route)


def _ffn_kernel(bexp_ref, nused_ref, x_ref, w1_ref, w3_ref, w2_ref, y_ref, w13_s, w2_s):
    b = pl.program_id(0)
    used = b < nused_ref[0]
    de = D_EXPERT

    @pl.when(used & ((b == 0) | (bexp_ref[b] != bexp_ref[jnp.maximum(b - 1, 0)])))
    def _():
        w13_s[:, :de] = w1_ref[...].astype(BF16)
        w13_s[:, de:] = w3_ref[...].astype(BF16)
        w2_s[...] = w2_ref[...].astype(BF16)

    @pl.when(used)
    def _():
        ab = _dot(x_ref[...], w13_s[...])
        a = ab[:, :de]
        hid = (a * _sigmoid(a)) * ab[:, de:]
        y_ref[...] = _dot(hid.astype(BF16), w2_s[...]).astype(BF16)

    @pl.when(jnp.logical_not(used))
    def _():
        y_ref[...] = jnp.zeros_like(y_ref)


def _expert_ffn(plan, xs, layer, w1, w3, w2):
    d = D_MODEL
    n_blocks = xs.shape[0] // FFN_BLOCK
    row = lambda b, bexp, nused: (jnp.minimum(b, nused[0] - 1), 0)
    wsel = lambda b, bexp, nused: (layer, bexp[b], 0, 0)
    return pl.pallas_call(
        _ffn_kernel,
        out_shape=jax.ShapeDtypeStruct(xs.shape, BF16),
        grid_spec=pltpu.PrefetchScalarGridSpec(
            num_scalar_prefetch=2,
            grid=(n_blocks,),
            in_specs=[pl.BlockSpec((FFN_BLOCK, d), row),
                      pl.BlockSpec((None, None, d, D_EXPERT), wsel),
                      pl.BlockSpec((None, None, d, D_EXPERT), wsel),
                      pl.BlockSpec((None, None, D_EXPERT, d), wsel)],
            out_specs=pl.BlockSpec((FFN_BLOCK, d), lambda b, bexp, nused: (b, 0)),
            scratch_shapes=[pltpu.VMEM((d, 2 * D_EXPERT), BF16), pltpu.VMEM((D_EXPERT, d), BF16)],
        ),
        compiler_params=_params(("arbitrary",)),
        name="moe_ffn",
    )(plan["bexp"], plan["nused"], xs, w1, w3, w2)


def _combine_kernel(base_ref, lstart_ref, nch_ref, route_ref, x1_ref, mod_ref, ys_ref, *rest, n_out_tiles):
    g_ref = rest[0] if len(rest) == 4 else None
    o_ref, buf_ref, sem = rest[-3:]
    d = D_MODEL
    t = pl.program_id(0)
    tile = route_ref.shape[0]
    slots = buf_ref.shape[1]
    cur = t % 2

    def maker(b):
        def make(loc, glob):
            return pltpu.make_async_copy(ys_ref.at[pl.ds(glob, ROW_CHUNK)], buf_ref.at[b, pl.ds(loc, ROW_CHUNK)],
                                         sem.at[b])
        return make

    @pl.when(t == 0)
    def _():
        buf_ref[...] = jnp.zeros_like(buf_ref)
        _segment_copies(t, base_ref, lstart_ref, nch_ref, maker(cur), True)

    @pl.when(t + 1 < pl.num_programs(0))
    def _():
        _segment_copies(t + 1, base_ref, lstart_ref, nch_ref, maker(1 - cur), True)

    route = route_ref[...]
    e1, e2 = _route_ids(route, 1)
    lane = lax.broadcasted_iota(jnp.int32, route.shape, 1)
    oh1, oh2 = lane == e1, lane == e2
    hit = jnp.where(oh1 | oh2, 1.0, 0.0).astype(BF16)
    before = (lax.broadcasted_iota(jnp.int32, (tile, tile), 1)
              < lax.broadcasted_iota(jnp.int32, (tile, tile), 0))
    rank = _dot(jnp.where(before, 1.0, 0.0).astype(BF16), hit)
    lane1 = lax.broadcasted_iota(jnp.int32, (1, route.shape[1]), 1)
    seg = jnp.zeros((1, route.shape[1]), F32)
    for e in range(MOE_EXPERTS):
        seg = jnp.where(lane1 == e, lstart_ref[t * MOE_EXPERTS + e].astype(F32), seg)
    slot_of = rank + seg
    pos1 = jnp.sum(jnp.where(oh1, slot_of, 0.0), axis=1, keepdims=True).astype(jnp.int32)
    pos2 = jnp.sum(jnp.where(oh2, slot_of, 0.0), axis=1, keepdims=True).astype(jnp.int32)
    slot = lax.broadcasted_iota(jnp.int32, (tile, slots), 1)
    w1 = route[:, ROUTE_W:ROUTE_W + 1]
    w2 = route[:, ROUTE_W + 1:ROUTE_W + 2]
    comb = (jnp.where(slot == pos1, w1, 0.0) + jnp.where(slot == pos2, w2, 0.0)).astype(BF16)
    _segment_copies(t, base_ref, lstart_ref, nch_ref, maker(cur), False)
    moe = _dot(comb, buf_ref[cur])
    x2 = x1_ref[...] + mod_ref[:, 5 * d:6 * d] * moe
    if g_ref is None:
        o_ref[...] = x2
    else:
        @pl.when(t < n_out_tiles)
        def _():
            o_ref[...] = x2 * lax.rsqrt(jnp.mean(x2 * x2, axis=-1, keepdims=True) + EPS) * g_ref[...]


def _combine(lay, plan, route, x1, mods, ys, g_final):
    tile, d = DISPATCH_TILE, D_MODEL
    assert lay.seq % tile == 0 and (lay.batch * lay.ctx) % tile == 0
    lat_tiles = lay.seq // tile
    n_lat = lay.batch * lat_tiles
    n_tiles = lay.n_tok // tile
    row = lambda t: jnp.where(t < n_lat, t // lat_tiles, lay.batch)
    final = g_final is not None
    n_out_tiles = n_lat if final else n_tiles
    extra_specs = [pl.BlockSpec((1, d), lambda t, *_: (0, 0))] if final else []
    extra_args = [g_final] if final else []
    return pl.pallas_call(
        functools.partial(_combine_kernel, n_out_tiles=n_out_tiles),
        out_shape=jax.ShapeDtypeStruct((n_out_tiles * tile, d), F32),
        grid_spec=pltpu.PrefetchScalarGridSpec(
            num_scalar_prefetch=3,
            grid=(n_tiles,),
            in_specs=[pl.BlockSpec((tile, ROUTER_PAD), lambda t, *_: (t, 0)),
                      pl.BlockSpec((tile, d), lambda t, *_: (t, 0)),
                      pl.BlockSpec((None, 1, N_MOD * d), lambda t, *_: (row(t), 0, 0)),
                      pl.BlockSpec(memory_space=pl.ANY)] + extra_specs,
            out_specs=pl.BlockSpec((tile, d), lambda t, *_: (jnp.minimum(t, n_out_tiles - 1), 0)),
            scratch_shapes=[pltpu.VMEM((2, SLOT_ROWS, d), BF16), pltpu.SemaphoreType.DMA((2,))],
        ),
        compiler_params=_params(("arbitrary",)),
        name="moe_combine",
    )(plan["base"], plan["lstart"], plan["nch"], route, x1, mods, ys, *extra_args)


def _moe(lay, h2, x1, route, tile_counts, mods, layer, w1, w3, w2, g_final=None):
    n_tiles = lay.n_tok // DISPATCH_TILE
    max_rows = (MOE_TOPK * lay.n_tok + n_tiles * MOE_EXPERTS * (ROW_CHUNK - 1)
                + MOE_EXPERTS * (FFN_BLOCK - 1))
    n_blocks = -(-max_rows // FFN_BLOCK)
    counts = jnp.sum(tile_counts[:, 0, :MOE_EXPERTS].reshape(n_tiles, DISPATCH_TILE // lay.tile, MOE_EXPERTS), axis=1)
    plan = _moe_plan(counts, n_blocks)
    xs = _dispatch(plan, h2, route, n_blocks * FFN_BLOCK)
    ys = _expert_ffn(plan, xs, layer, w1, w3, w2)
    return _combine(lay, plan, route, x1, mods, ys, g_final)


def _rope_tables(seq, ctx):
    pos = jnp.arange(seq, dtype=jnp.int32)
    inv_freq = ROPE_BASE ** (-jnp.arange(0, AXIS_DIM, 2, dtype=F32) / AXIS_DIM)
    ang_row = (pos // GRID_W).astype(F32)[:, None] * inv_freq
    ang_col = (pos % GRID_W).astype(F32)[:, None] * inv_freq
    cos_h = jnp.concatenate([jnp.cos(ang_row)] * 2 + [jnp.cos(ang_col)] * 2, axis=1)
    sin_h = jnp.concatenate([-jnp.sin(ang_row), jnp.sin(ang_row), -jnp.sin(ang_col), jnp.sin(ang_col)], axis=1)
    reps = LANES // SWA_HEAD_DIM
    cos_t = jnp.concatenate([jnp.tile(cos_h, (1, reps)), jnp.ones((ctx, LANES), F32)], axis=0)
    sin_t = jnp.concatenate([jnp.tile(sin_h, (1, reps)), jnp.zeros((ctx, LANES), F32)], axis=0)
    return cos_t, sin_t


def _channel_dft():
    i = jnp.arange(FNET_GROUP_W, dtype=jnp.int32)
    ang = ((i[:, None] * i[None, :]) % FNET_GROUP_W).astype(F32) * (2.0 * math.pi / FNET_GROUP_W)
    return jnp.concatenate([jnp.cos(ang), jnp.sin(ang)], axis=1).astype(BF16)


def kernel(x, c, ctx, c_ctx, w_ada, b_ada, g_mix, g_ffn, w_in, w_decay_down, w_decay_up, b_decay, g_gla, sink,
           w_pa, w_pb, w_pc, w_out, w_router_group, w_router_expert, w1, w3, w2, g_final):
    batch, seq, d = x.shape
    n_ctx = ctx.shape[1]
    depth = w_ada.shape[0]
    assert d == D_MODEL and seq % GRID_W == 0 and seq % n_ctx == 0
    lay = _Layout(batch, seq, n_ctx, TOKEN_TILE)

    rows = -(-(batch + 1) // 8) * 8
    c_all = jnp.zeros((rows, d), F32).at[:batch].set(c).at[batch].set(c_ctx)
    mods_all = _adaln(c_all, w_ada, b_ada).reshape(depth, rows, 1, N_MOD * d)

    cos_t, sin_t = _rope_tables(seq, n_ctx)
    cs = _channel_dft()
    c_lat, s_lat = _dft_mats(seq)
    c_ctx_m, s_ctx_m = _dft_mats(n_ctx)

    xs = jnp.concatenate([x.reshape(batch * seq, d), ctx.reshape(batch * n_ctx, d)], axis=0)
    rank = w_decay_down.shape[-1]
    w_in_bf = w_in.astype(BF16)
    for l in range(depth):
        last = l == depth - 1
        mods = mods_all[l]
        down = jnp.concatenate([w_decay_down[l, 0], w_decay_down[l, 1]], axis=1)
        down = jnp.pad(down, ((0, 0), (0, DECAY_PAD - 2 * rank))).astype(BF16)
        w_up = jnp.zeros((DECAY_PAD, 2 * GLA_QK_W), F32)
        w_up = w_up.at[:rank, :GLA_QK_W].set(w_decay_up[l, 0]).at[rank:2 * rank, GLA_QK_W:].set(w_decay_up[l, 1])
        b_dec = b_decay[l].reshape(1, 2 * GLA_QK_W)
        (k_a, v_a, k_c, v_c, q_a, r_a, u_cos, u_sin, q_c, gate_logits, la_f, la_b) = _inproj(
            lay, xs, g_mix[l].reshape(1, d), mods, l, w_in_bf, down, w_up.astype(BF16), b_dec, cs, cos_t, sin_t)

        o_f, o_b = _gla(lay, q_a, k_a, v_a, la_f, la_b)
        yb_lat = _fourier(u_cos, u_sin, c_lat, s_lat, 0, seq, min(seq, 1024), min(seq, 512))
        yc_lat = _swa(lay, q_c, k_c, v_c, jnp.broadcast_to(sink[l][:, None], (SWA_HEADS, LANES)))
        if last:
            yb_ctx = jnp.zeros((n_ctx, batch * FNET_W), BF16)
            yc_ctx = jnp.zeros((batch * n_ctx, SWA_Q_W), BF16)
        else:
            yb_ctx = _fourier(u_cos, u_sin, c_ctx_m, s_ctx_m, seq, n_ctx, n_ctx, n_ctx)
            yc_ctx = _ctx_attn(lay, q_c, k_c, v_c, jnp.broadcast_to(sink[l][:, None], (SWA_HEADS, LANES)))

        w_router = jnp.zeros((d, ROUTER_PAD), F32)
        w_router = w_router.at[:, :MOE_EXPERTS].set(w_router_expert[l])
        w_router = w_router.at[:, MOE_EXPERTS:MOE_EXPERTS + MOE_GROUPS].set(w_router_group[l])
        w_router_hi = w_router.astype(BF16)
        w_router = jnp.concatenate([w_router_hi, (w_router - w_router_hi.astype(F32)).astype(BF16)], axis=1)
        x1, h2, gates, tile_counts = _merge(lay, xs, o_f, o_b, r_a, yb_lat, yb_ctx, yc_lat, yc_ctx, gate_logits, mods,
                               g_gla[l].reshape(1, GLA_DV), g_ffn[l].reshape(1, d),
                               w_pa[l].astype(BF16), w_pb[l].astype(BF16), w_pc[l].astype(BF16),
                               w_out[l].astype(BF16), w_router)
        xs = _moe(lay, h2, x1, gates, tile_counts, mods, l, w1, w3, w2,
                  g_final.reshape(1, d) if last else None)

    return xs.reshape(batch, seq, d)
```

```python
import functools
import math

import jax
import jax.numpy as jnp
from jax import lax
from jax.experimental import pallas as pl
from jax.experimental.pallas import tpu as pltpu

F32 = jnp.float32
BF16 = jnp.bfloat16
HIGHEST = lax.Precision.HIGHEST

D_MODEL = 1024
GRID_W = 64
EPS = 1e-6
N_MOD = 6
GLA_HEADS = 4
GLA_DK = 64
GLA_DV = 128
GLA_TAU = 16.0
GLA_CHUNK = 64
GLA_SCALE = GLA_DK ** -0.5
FNET_GROUPS = 4
FNET_GROUP_W = 128
SWA_HEADS = 8
SWA_KV_HEADS = 2
SWA_HEAD_DIM = 64
WINDOW = 128
ROPE_BASE = 10000.0
AXIS_DIM = SWA_HEAD_DIM // 2
MOE_GROUPS = 4
MOE_EXPERTS_PER_GROUP = 4
MOE_EXPERTS = MOE_GROUPS * MOE_EXPERTS_PER_GROUP
MOE_TOPK = 2
D_EXPERT = 512

GLA_QK_W = GLA_HEADS * GLA_DK
GLA_V_W = GLA_HEADS * GLA_DV
FNET_W = FNET_GROUPS * FNET_GROUP_W
SWA_Q_W = SWA_HEADS * SWA_HEAD_DIM
SWA_KV_W = SWA_KV_HEADS * SWA_HEAD_DIM
IN_SIZES = (GLA_QK_W, GLA_V_W, SWA_KV_W, SWA_KV_W, GLA_QK_W, GLA_V_W, FNET_W, SWA_Q_W, 3 * D_MODEL)
IN_OFFS = tuple(int(sum(IN_SIZES[:i])) for i in range(len(IN_SIZES) + 1))
IN_COLS = IN_OFFS[-1]

LANES = 128
TOKEN_TILE = 256
DECAY_PAD = LANES
ROUTER_PAD = LANES
ROUTE_ID = 0
ROUTE_W = 2
DISPATCH_TILE = 512
ROW_CHUNK = 16
BIG_CHUNK = 64
FOURIER_TILE = 512
FOURIER_COL_SPLIT = 2
FFN_BLOCK = 512
SLOT_ROWS = -(-(MOE_TOPK * DISPATCH_TILE + MOE_EXPERTS * (ROW_CHUNK - 1)) // LANES) * LANES
VMEM_LIMIT = 56 * 1024 * 1024


def _params(sem, vmem=VMEM_LIMIT):
    return pltpu.CompilerParams(dimension_semantics=sem, vmem_limit_bytes=vmem)


def _sigmoid(x):
    return 0.5 * jnp.tanh(0.5 * x) + 0.5


def _dot(a, b):
    return jnp.dot(a, b, preferred_element_type=F32)


def _dot_nt(a, b):
    return lax.dot_general(a, b, (((1,), (1,)), ((), ())), preferred_element_type=F32)


def _ada_kernel(c_ref, w_ref, b_ref, o_ref):
    c = c_ref[...]
    a = c * _sigmoid(c)
    o_ref[...] = jnp.dot(a, w_ref[...], preferred_element_type=F32, precision=HIGHEST) + b_ref[...]


def _adaln(c_all, w_ada, b_ada):
    depth, d, n = w_ada.shape
    rows = c_all.shape[0]
    tn = 1536
    return pl.pallas_call(
        _ada_kernel,
        out_shape=jax.ShapeDtypeStruct((depth, rows, n), F32),
        grid=(depth, n // tn),
        in_specs=[pl.BlockSpec((rows, d), lambda l, j: (0, 0)),
                  pl.BlockSpec((None, d, tn), lambda l, j: (l, 0, j)),
                  pl.BlockSpec((None, 1, tn), lambda l, j: (l, 0, j))],
        out_specs=pl.BlockSpec((None, rows, tn), lambda l, j: (l, 0, j)),
        compiler_params=_params(("parallel", "parallel")),
        name="adaln",
    )(c_all, w_ada, b_ada.reshape(depth, 1, n))


class _Layout:
    def __init__(self, batch, seq, ctx, tile):
        assert seq % tile == 0 and ctx % tile == 0
        self.batch, self.seq, self.ctx, self.tile = batch, seq, ctx, tile
        self.lat_tiles = seq // tile
        self.ctx_tiles = ctx // tile
        self.n_lat = batch * self.lat_tiles
        self.n_tiles = self.n_lat + batch * self.ctx_tiles
        self.n_tok = self.n_tiles * tile

    def batch_of(self, t):
        return jnp.where(t < self.n_lat, t // self.lat_tiles, (t - self.n_lat) // self.ctx_tiles)

    def mod_row(self, t):
        return jnp.where(t < self.n_lat, t // self.lat_tiles, self.batch)

    def seq_tile(self, t):
        return jnp.where(t < self.n_lat, t % self.lat_tiles,
                         self.lat_tiles + (t - self.n_lat) % self.ctx_tiles)


def _rope(x, cos, sin_signed):
    n = x.shape[-1]
    lane = lax.broadcasted_iota(jnp.int32, x.shape, 1)
    half = AXIS_DIM // 2
    partner = jnp.where((lane & half) == 0, pltpu.roll(x, n - half, 1), pltpu.roll(x, half, 1))
    return x * cos + partner * sin_signed


def _inproj_kernel(x_ref, g_ref, mod_ref, w_ref, wd_ref, wu_ref, bdec_ref, cs_ref, cos_ref, sin_ref,
                   ka_ref, va_ref, kc_ref, vc_ref, qa_ref, ra_ref, ua_ref, us_ref, qc_ref, gl_ref,
                   laf_ref, lab_ref):
    d = D_MODEL
    x = x_ref[...]
    shift = mod_ref[:, 0:d]
    scale = mod_ref[:, d:2 * d]
    h = x * lax.rsqrt(jnp.mean(x * x, axis=-1, keepdims=True) + EPS) * g_ref[...]
    hb = (h * (1.0 + scale) + shift).astype(BF16)

    def proj(i):
        return _dot(hb, w_ref[:, IN_OFFS[i]:IN_OFFS[i + 1]])

    ka_ref[...] = proj(0).astype(BF16)
    va_ref[...] = proj(1).astype(BF16)
    cos = cos_ref[...]
    sin = sin_ref[...]
    kc_ref[...] = _rope(proj(2), cos, sin).astype(BF16)
    vc_ref[...] = proj(3).astype(BF16)
    qa_ref[...] = proj(4).astype(BF16)
    ra_ref[...] = proj(5).astype(BF16)
    u = proj(6).astype(BF16)
    for g in range(FNET_GROUPS):
        sl = slice(g * FNET_GROUP_W, (g + 1) * FNET_GROUP_W)
        ab = _dot(u[:, sl], cs_ref[...])
        ua_ref[:, sl] = ab[:, :FNET_GROUP_W].astype(BF16)
        us_ref[:, sl] = ab[:, FNET_GROUP_W:].astype(BF16)
    reps = SWA_Q_W // LANES
    qc_ref[...] = (_rope(proj(7), jnp.concatenate([cos] * reps, axis=1), jnp.concatenate([sin] * reps, axis=1))
                   * SWA_HEAD_DIM ** -0.5).astype(BF16)
    gl_ref[...] = proj(8).astype(BF16)
    low = _dot(hb, wd_ref[...]).astype(BF16)
    z = _dot(low, wu_ref[...]) + bdec_ref[...]
    la = (jnp.minimum(z, 0.0) - jnp.log(1.0 + jnp.exp(-jnp.abs(z)))) * (1.0 / GLA_TAU)
    laf_ref[...] = la[:, :GLA_QK_W]
    lab_ref[...] = la[:, GLA_QK_W:]


def _inproj(lay, x, g_mix, mods, layer, w_in, w_down, w_up, b_dec, cs, cos_t, sin_t):
    tm, d = lay.tile, D_MODEL
    n = lay.n_tok
    bf = lambda w: jax.ShapeDtypeStruct((n, w), BF16)
    tok = lambda w: pl.BlockSpec((tm, w), lambda t: (t, 0))
    const = lambda a: pl.BlockSpec(a.shape, lambda t: (0,) * a.ndim)
    seq_rows = lay.seq + lay.ctx
    fnet_shape = jax.ShapeDtypeStruct((seq_rows, lay.batch * FNET_W), BF16)
    fnet_spec = pl.BlockSpec((tm, FNET_W), lambda t: (lay.seq_tile(t), lay.batch_of(t)))
    pos_spec = pl.BlockSpec((tm, LANES), lambda t: (lay.seq_tile(t), 0))
    return pl.pallas_call(
        _inproj_kernel,
        out_shape=(bf(GLA_QK_W), bf(GLA_V_W), bf(SWA_KV_W), bf(SWA_KV_W), bf(GLA_QK_W), bf(GLA_V_W),
                   fnet_shape, fnet_shape, bf(SWA_Q_W), bf(3 * d),
                   jax.ShapeDtypeStruct((n, GLA_QK_W), F32), jax.ShapeDtypeStruct((n, GLA_QK_W), F32)),
        grid=(lay.n_tiles,),
        in_specs=[tok(d), const(g_mix),
                  pl.BlockSpec((None, 1, N_MOD * d), lambda t: (lay.mod_row(t), 0, 0)),
                  pl.BlockSpec((None,) + w_in.shape[1:], lambda t: (layer, 0, 0)),
                  const(w_down), const(w_up), const(b_dec), const(cs), pos_spec, pos_spec],
        out_specs=(tok(GLA_QK_W), tok(GLA_V_W), tok(SWA_KV_W), tok(SWA_KV_W), tok(GLA_QK_W), tok(GLA_V_W),
                   fnet_spec, fnet_spec, tok(SWA_Q_W), tok(3 * d), tok(GLA_QK_W), tok(GLA_QK_W)),
        compiler_params=_params(("parallel",)),
        name="inproj",
    )(x, g_mix, mods, w_in, w_down, w_up, b_dec, cs, cos_t, sin_t)


def _split3(x):
    a1 = x.astype(BF16)
    r1 = x - a1.astype(F32)
    a2 = r1.astype(BF16)
    a3 = (r1 - a2.astype(F32)).astype(BF16)
    return a1, a2, a3


def _gla_direction(q_ref, k_ref, v_ref, la_ref, o_ref, s_ref, rev):
    t_rows = q_ref.shape[0]
    c = GLA_CHUNK
    n_sub = t_rows // c
    hk = GLA_QK_W
    la3 = _split3(la_ref[...])
    row = lax.broadcasted_iota(jnp.int32, (t_rows, t_rows), 0)
    col = lax.broadcasted_iota(jnp.int32, (t_rows, t_rows), 1)
    tri = ((row // c) == (col // c)) & ((col >= row) if rev else (col <= row))
    tri = jnp.where(tri, 1.0, 0.0).astype(BF16)
    lc = _dot(tri, la3[0]) + _dot(tri, la3[1]) + _dot(tri, la3[2])
    q = q_ref[...].astype(F32)
    k = k_ref[...].astype(F32)
    qd = (q * jnp.exp(lc) * GLA_SCALE).astype(BF16)
    kd = (k * jnp.exp(-lc)).astype(BF16)
    hrow = lax.broadcasted_iota(jnp.int32, (GLA_HEADS * c, hk), 0) // c
    hcol = lax.broadcasted_iota(jnp.int32, (GLA_HEADS * c, hk), 1) // GLA_DK
    head_mask = hrow == hcol
    trow = lax.broadcasted_iota(jnp.int32, (GLA_HEADS * c, c), 0) % c
    scol = lax.broadcasted_iota(jnp.int32, (GLA_HEADS * c, c), 1)
    causal = (scol >= trow) if rev else (scol <= trow)
    edge = 0 if rev else c - 1
    lasts = [lc[i * c + edge:i * c + edge + 1] for i in range(n_sub)]
    to_end = jnp.concatenate([jnp.broadcast_to(l, (c, hk)) for l in lasts], axis=0) - lc
    kh_t = (k * jnp.exp(to_end)).T
    pad = jnp.zeros((LANES - n_sub, hk), F32)
    decay_t = jnp.exp(jnp.concatenate(lasts + [pad], axis=0).T)
    chunk_of = lax.broadcasted_iota(jnp.int32, (1, t_rows), 1) // c
    kv = []
    for h in range(GLA_HEADS):
        kh_h = kh_t[h * GLA_DK:(h + 1) * GLA_DK]
        stack = jnp.concatenate([jnp.where(chunk_of == i, kh_h, 0.0) for i in range(n_sub)], axis=0)
        kv.append(_dot(stack.astype(BF16), v_ref[:, h * GLA_DV:(h + 1) * GLA_DV]))
    order = range(n_sub - 1, -1, -1) if rev else range(n_sub)
    state = s_ref[...]
    state_at = {}
    for i in order:
        state_at[i] = state
        inc = jnp.concatenate([kv[h][i * GLA_DK:(i + 1) * GLA_DK] for h in range(GLA_HEADS)], axis=0)
        state = decay_t[:, i:i + 1] * state + inc
    s_ref[...] = state
    for i in order:
        rows = slice(i * c, (i + 1) * c)
        v_i = v_ref[rows, :]
        q_stack = jnp.where(head_mask, jnp.concatenate([qd[rows]] * GLA_HEADS, axis=0), 0.0).astype(BF16)
        inter = _dot(q_stack, state_at[i].astype(BF16))
        scores = jnp.where(causal, _dot_nt(q_stack, kd[rows]), 0.0).astype(BF16)
        outs = []
        for h in range(GLA_HEADS):
            hr = slice(h * c, (h + 1) * c)
            hv = slice(h * GLA_DV, (h + 1) * GLA_DV)
            outs.append(inter[hr] + _dot(scores[hr], v_i[:, hv]))
        o_ref[rows, :] = jnp.concatenate(outs, axis=1).astype(BF16)


def _gla_kernel(qf_ref, kf_ref, vf_ref, laf_ref, qb_ref, kb_ref, vb_ref, lab_ref,
                of_ref, ob_ref, sf_ref, sb_ref):
    @pl.when(pl.program_id(1) == 0)
    def _():
        sf_ref[...] = jnp.zeros_like(sf_ref)
        sb_ref[...] = jnp.zeros_like(sb_ref)

    _gla_direction(qf_ref, kf_ref, vf_ref, laf_ref, of_ref, sf_ref, False)
    _gla_direction(qb_ref, kb_ref, vb_ref, lab_ref, ob_ref, sb_ref, True)


def _gla(lay, q_a, k_a, v_a, la_f, la_b):
    tm = lay.tile
    nc, nl = lay.ctx_tiles, lay.lat_tiles

    def fwd(b, j):
        return jnp.where(j < nc, lay.n_lat + b * nc + j, b * nl + (j - nc))

    def bwd(b, j):
        return jnp.where(j < nc, lay.n_lat + b * nc + (nc - 1 - j), b * nl + (nl - 1 - (j - nc)))

    spec = lambda w, f: pl.BlockSpec((tm, w), lambda b, j: (f(b, j), 0))
    out = jax.ShapeDtypeStruct((lay.n_tok, GLA_V_W), BF16)
    return pl.pallas_call(
        _gla_kernel,
        out_shape=(out, out),
        grid=(lay.batch, nc + nl),
        in_specs=[spec(GLA_QK_W, fwd), spec(GLA_QK_W, fwd), spec(GLA_V_W, fwd), spec(GLA_QK_W, fwd),
                  spec(GLA_QK_W, bwd), spec(GLA_QK_W, bwd), spec(GLA_V_W, bwd), spec(GLA_QK_W, bwd)],
        out_specs=(spec(GLA_V_W, fwd), spec(GLA_V_W, bwd)),
        scratch_shapes=[pltpu.VMEM((GLA_QK_W, GLA_DV), F32), pltpu.VMEM((GLA_QK_W, GLA_DV), F32)],
        compiler_params=_params(("parallel", "arbitrary")),
        name="gla",
    )(q_a, k_a, v_a, la_f, q_a, k_a, v_a, la_b)


def _dft_kernel(c_ref, s_ref, a_ref, b_ref, o_ref, acc_ref, *, scale):
    k = pl.program_id(1)

    @pl.when(k == 0)
    def _():
        acc_ref[...] = jnp.zeros_like(acc_ref)

    acc_ref[...] += _dot(c_ref[...], a_ref[...]) + _dot(s_ref[...], b_ref[...])

    @pl.when(k == pl.num_programs(1) - 1)
    def _():
        o_ref[...] = (acc_ref[...] * scale).astype(BF16)


def _dft_mats(n):
    f = GRID_W
    assert n % f == 0
    k = jnp.arange(n, dtype=jnp.int32)[None, :]

    def table(rows, period):
        ang = ((jnp.arange(rows, dtype=jnp.int32)[:, None] * k) % period).astype(F32) * (2.0 * math.pi / period)
        return jnp.cos(ang), jnp.sin(ang)

    ca, sa = table(n // f, n // f)
    cb, sb = table(f, n)
    cos = ca[:, None, :] * cb[None, :, :] - sa[:, None, :] * sb[None, :, :]
    sin = sa[:, None, :] * cb[None, :, :] + ca[:, None, :] * sb[None, :, :]
    return cos.reshape(n, n).astype(BF16), (-sin).reshape(n, n).astype(BF16)


def _dft_half_kernel(c_ref, s_ref, cx_ref, sx_ref, a_ref, b_ref, j_ref, o_ref, *, scale):
    a = a_ref[...]
    b = b_ref[...]
    p = _dot(c_ref[...], a)
    q = _dot(s_ref[...], b)
    o_ref[0] = ((p + q) * scale).astype(BF16)
    mirror = ((p - q) * scale).astype(BF16)
    flipped = _dot(j_ref[...], mirror)
    first = (_dot(cx_ref[...], a) - _dot(sx_ref[...], b))[0:1, :] * scale
    row = lax.broadcasted_iota(jnp.int32, (flipped.shape[0], 1), 0)
    o_ref[1] = jnp.where(row == 0, first, flipped).astype(BF16)


def _dft_half_mats(n, tm):
    f = GRID_W
    half = n // 2
    assert half % f == 0 and half % tm == 0
    k = jnp.arange(n, dtype=jnp.int32)[None, :]

    def table(rows, period):
        ang = ((rows[:, None] * k) % period).astype(F32) * (2.0 * math.pi / period)
        return jnp.cos(ang), jnp.sin(ang)

    ca, sa = table(jnp.arange(half // f, dtype=jnp.int32), n // f)
    cb, sb = table(jnp.arange(f, dtype=jnp.int32), n)
    cos = ca[:, None, :] * cb[None, :, :] - sa[:, None, :] * sb[None, :, :]
    sin = sa[:, None, :] * cb[None, :, :] + ca[:, None, :] * sb[None, :, :]
    n_it = half // tm
    cx, sx = table(jnp.arange(1, n_it + 1, dtype=jnp.int32) * tm, n)
    spread = lambda m: jnp.zeros((n_it, 8, n), F32).at[:, 0, :].set(m).reshape(n_it * 8, n).astype(BF16)
    return (cos.reshape(half, n).astype(BF16), (-sin).reshape(half, n).astype(BF16), spread(cx), spread(-sx))


def _fourier_half(ua, us, mats, length, tm):
    cmat, smat, cx, sx = mats
    width = ua.shape[1]
    half = length // 2
    scale = 1.0 / math.sqrt(length * FNET_GROUP_W)
    r = lax.broadcasted_iota(jnp.int32, (tm, tm), 0)
    c = lax.broadcasted_iota(jnp.int32, (tm, tm), 1)
    jmat = jnp.where(c == tm - r, 1.0, 0.0).astype(BF16)
    once = pl.Buffered(1)
    cols = width // FOURIER_COL_SPLIT
    return pl.pallas_call(
        functools.partial(_dft_half_kernel, scale=scale),
        out_shape=jax.ShapeDtypeStruct((2, half, width), BF16),
        grid=(FOURIER_COL_SPLIT, half // tm),
        in_specs=[pl.BlockSpec((tm, length), lambda j, i: (i, 0)),
                  pl.BlockSpec((tm, length), lambda j, i: (i, 0)),
                  pl.BlockSpec((8, length), lambda j, i: (i, 0)),
                  pl.BlockSpec((8, length), lambda j, i: (i, 0)),
                  pl.BlockSpec((length, cols), lambda j, i: (0, j), pipeline_mode=once),
                  pl.BlockSpec((length, cols), lambda j, i: (0, j), pipeline_mode=once),
                  pl.BlockSpec((tm, tm), lambda j, i: (0, 0), pipeline_mode=once)],
        out_specs=pl.BlockSpec((2, tm, cols), lambda j, i: (0, i, j)),
        compiler_params=_params(("parallel", "parallel")),
        name="fourier_half",
    )(cmat, smat, cx, sx, ua, us, jmat)


def _fourier(ua, us, cmat, smat, row0, length, tm, tk):
    width = ua.shape[1]
    off = row0 // tk
    scale = 1.0 / math.sqrt(length * FNET_GROUP_W)
    return pl.pallas_call(
        functools.partial(_dft_kernel, scale=scale),
        out_shape=jax.ShapeDtypeStruct((length, width), BF16),
        grid=(length // tm, length // tk),
        in_specs=[pl.BlockSpec((tm, tk), lambda i, k: (i, k)),
                  pl.BlockSpec((tm, tk), lambda i, k: (i, k)),
                  pl.BlockSpec((tk, width), lambda i, k: (off + k, 0)),
                  pl.BlockSpec((tk, width), lambda i, k: (off + k, 0))],
        out_specs=pl.BlockSpec((tm, width), lambda i, k: (i, 0)),
        scratch_shapes=[pltpu.VMEM((tm, width), F32)],
        compiler_params=_params(("parallel", "arbitrary")),
        name="fourier",
    )(cmat, smat, ua, us)


def _attend(q_ref, k_all, v_all, valid, sink_ref, o_ref):
    hd = SWA_HEAD_DIM
    heads_per_kv = SWA_HEADS // SWA_KV_HEADS
    k_sw = jnp.concatenate([k_all[:, hd:], k_all[:, :hd]], axis=1)
    v_t = v_all.astype(F32).T
    v_t_sw = jnp.concatenate([v_t[hd:], v_t[:hd]], axis=0)
    lane = lax.broadcasted_iota(jnp.int32, (1, LANES), 1)
    lane_half = (lane < hd, lane >= hd)
    row = lax.broadcasted_iota(jnp.int32, (LANES, 1), 0)
    row_half = (row < hd, row >= hd)
    zero = jnp.zeros((), BF16)
    for p in range(SWA_HEADS // 2):
        acc = None
        for par in range(2):
            head = 2 * p + par
            aligned = head // heads_per_kv == par
            qh = jnp.where(lane_half[par], q_ref[:, p * LANES:(p + 1) * LANES], zero)
            s = _dot_nt(k_all if aligned else k_sw, qh)
            if valid is not None:
                kw = valid.shape[0]
                s = jnp.concatenate([jnp.where(valid, s[:kw], -jnp.inf), s[kw:]], axis=0)
            sink = sink_ref[head:head + 1, 0:1]
            m = jnp.maximum(jnp.max(s, axis=0, keepdims=True), sink)
            e = jnp.exp(s - m)
            denom = jnp.sum(e, axis=0, keepdims=True) + jnp.exp(sink - m)
            v_use = jnp.where(row_half[par], v_t if aligned else v_t_sw, 0.0).astype(BF16)
            part = _dot(v_use, e.astype(BF16)) * (1.0 / denom)
            acc = part if acc is None else acc + part
        o_ref[:, p * LANES:(p + 1) * LANES] = acc.T.astype(BF16)


def _swa_kernel(q_ref, kp_ref, kc_ref, kn_ref, kx_ref, vp_ref, vc_ref, vn_ref, vx_ref, sink_ref, o_ref,
                *, seq):
    n = pl.program_id(1)
    w = WINDOW
    k_blocks = [kp_ref[...], kc_ref[0:w, :], kc_ref[w:2 * w, :], kn_ref[...]]
    v_blocks = [vp_ref[...], vc_ref[0:w, :], vc_ref[w:2 * w, :], vn_ref[...]]
    j = lax.broadcasted_iota(jnp.int32, (3 * w, w), 0)
    a = lax.broadcasted_iota(jnp.int32, (3 * w, w), 1)
    band = (j >= a) & (j - a <= 2 * w)
    for half in range(2):
        key_pos = (2 * n + half - 1) * w + j
        valid = band & (key_pos >= 0) & (key_pos < seq)
        k_all = jnp.concatenate(k_blocks[half:half + 3] + [kx_ref[...]], axis=0)
        v_all = jnp.concatenate(v_blocks[half:half + 3] + [vx_ref[...]], axis=0)
        rows = pl.ds(half * w, w)
        _attend(q_ref.at[rows], k_all, v_all, valid, sink_ref, o_ref.at[rows])


def _swa(lay, q_c, k_c, v_c, sink_b):
    w = WINDOW
    assert lay.seq % (2 * w) == 0
    nq = lay.seq // w
    steps = nq // 2
    ctx_blk = (lay.batch * lay.seq) // lay.ctx

    def edge(f):
        return pl.BlockSpec((w, SWA_KV_W), lambda b, n: (b * nq + f(n), 0))

    prev = edge(lambda n: jnp.maximum(2 * n - 1, 0))
    nxt = edge(lambda n: jnp.minimum(2 * n + 2, nq - 1))
    cur = pl.BlockSpec((2 * w, SWA_KV_W), lambda b, n: (b * steps + n, 0))
    ctx_spec = pl.BlockSpec((lay.ctx, SWA_KV_W), lambda b, n: (ctx_blk + b, 0))
    q_spec = pl.BlockSpec((2 * w, SWA_Q_W), lambda b, n: (b * steps + n, 0))
    return pl.pallas_call(
        functools.partial(_swa_kernel, seq=lay.seq),
        out_shape=jax.ShapeDtypeStruct((lay.batch * lay.seq, SWA_Q_W), BF16),
        grid=(lay.batch, steps),
        in_specs=[q_spec, prev, cur, nxt, ctx_spec, prev, cur, nxt, ctx_spec,
                  pl.BlockSpec(sink_b.shape, lambda b, n: (0, 0))],
        out_specs=q_spec,
        compiler_params=_params(("parallel", "parallel")),
        name="swa",
    )(q_c, k_c, k_c, k_c, k_c, v_c, v_c, v_c, v_c, sink_b)


def _ctx_attn_kernel(q_ref, kx_ref, vx_ref, sink_ref, o_ref):
    _attend(q_ref, kx_ref[...], vx_ref[...], None, sink_ref, o_ref)


def _ctx_attn(lay, q_c, k_c, v_c, sink_b):
    ctx_blk = (lay.batch * lay.seq) // lay.ctx
    spec = lambda wd: pl.BlockSpec((lay.ctx, wd), lambda b: (ctx_blk + b, 0))
    return pl.pallas_call(
        _ctx_attn_kernel,
        out_shape=jax.ShapeDtypeStruct((lay.batch * lay.ctx, SWA_Q_W), BF16),
        grid=(lay.batch,),
        in_specs=[spec(SWA_Q_W), spec(SWA_KV_W), spec(SWA_KV_W),
                  pl.BlockSpec(sink_b.shape, lambda b: (0, 0))],
        out_specs=pl.BlockSpec((lay.ctx, SWA_Q_W), lambda b: (b, 0)),
        compiler_params=_params(("parallel",)),
        name="ctx_attn",
    )(q_c, k_c, v_c, sink_b)


def _merge_kernel(x_ref, of_ref, ob_ref, ra_ref, ybl_ref, ybc_ref, ycl_ref, ycc_ref, gl_ref, mod_ref,
                  ggla_ref, gffn_ref, wpa_ref, wpb_ref, wpc_ref, wout_ref, wr_ref,
                  x1_ref, h2_ref, gates_ref, count_ref, *, n_lat):
    d = D_MODEL
    is_ctx = pl.program_id(0) >= n_lat
    o = of_ref[...].astype(F32) + ob_ref[...].astype(F32)
    r = ra_ref[...].astype(F32)
    parts = []
    for h in range(GLA_HEADS):
        sl = slice(h * GLA_DV, (h + 1) * GLA_DV)
        oh = o[:, sl]
        parts.append(oh * lax.rsqrt(jnp.mean(oh * oh, axis=-1, keepdims=True) + EPS) * ggla_ref[...])
    y_a = (jnp.concatenate(parts, axis=1) * (r * _sigmoid(r))).astype(BF16)
    y_b = jnp.where(is_ctx, ybc_ref[...], ybl_ref[...])
    y_c = jnp.where(is_ctx, ycc_ref[...], ycl_ref[...])
    gl = gl_ref[...].astype(F32)
    mix = (_sigmoid(gl[:, 0:d]) * _dot(y_a, wpa_ref[...])
           + _sigmoid(gl[:, d:2 * d]) * _dot(y_b, wpb_ref[...])
           + _sigmoid(gl[:, 2 * d:3 * d]) * _dot(y_c, wpc_ref[...]))
    y = _dot(mix.astype(BF16), wout_ref[...])
    x1 = x_ref[...] + mod_ref[:, 2 * d:3 * d] * y
    x1_ref[...] = x1
    h2 = x1 * lax.rsqrt(jnp.mean(x1 * x1, axis=-1, keepdims=True) + EPS) * gffn_ref[...]
    h2 = h2 * (1.0 + mod_ref[:, 4 * d:5 * d]) + mod_ref[:, 3 * d:4 * d]
    h2_hi = h2.astype(BF16)
    h2_ref[...] = h2_hi
    h2_lo = (h2 - h2_hi.astype(F32)).astype(BF16)
    both = _dot(h2_hi, wr_ref[...])
    logits = both[:, :ROUTER_PAD] + both[:, ROUTER_PAD:] + _dot(h2_lo, wr_ref[:, :ROUTER_PAD])
    lane = lax.broadcasted_iota(jnp.int32, logits.shape, 1)
    lane_f = lane.astype(F32)
    neg = -jnp.inf
    big = float(ROUTER_PAD)
    is_group = (lane >= MOE_EXPERTS) & (lane < MOE_EXPERTS + MOE_GROUPS)
    gl_m = jnp.where(is_group, logits, neg)
    g_max = jnp.max(gl_m, axis=-1, keepdims=True)
    g_sel = jnp.min(jnp.where(gl_m == g_max, lane_f, big), axis=-1, keepdims=True) - MOE_EXPERTS
    g_gate = 1.0 / jnp.sum(jnp.where(is_group, jnp.exp(logits - g_max), 0.0), axis=-1, keepdims=True)
    lo = g_sel * MOE_EXPERTS_PER_GROUP
    in_group = (lane_f >= lo) & (lane_f < lo + MOE_EXPERTS_PER_GROUP)
    e1 = jnp.where(in_group, logits, neg)
    v1 = jnp.max(e1, axis=-1, keepdims=True)
    i1 = jnp.min(jnp.where(e1 == v1, lane_f, big), axis=-1, keepdims=True)
    e2 = jnp.where(lane_f == i1, neg, e1)
    v2 = jnp.max(e2, axis=-1, keepdims=True)
    i2 = jnp.min(jnp.where(e2 == v2, lane_f, big), axis=-1, keepdims=True)
    t = jnp.exp(v2 - v1)
    w1 = g_gate / (1.0 + t)
    w2 = g_gate * t / (1.0 + t)
    route = jnp.where(lane == ROUTE_ID, i1, 0.0) + jnp.where(lane == ROUTE_ID + 1, i2, 0.0)
    route = route + jnp.where(lane == ROUTE_W, w1, 0.0) + jnp.where(lane == ROUTE_W + 1, w2, 0.0)
    gates_ref[...] = route
    hit = jnp.where((lane_f == i1) | (lane_f == i2), 1.0, 0.0)
    count_ref[...] = jnp.sum(hit, axis=0, keepdims=True).astype(jnp.int32)


def _merge(lay, x, o_f, o_b, r_a, yb_lat, fourier_tile, yb_ctx, yc_lat, yc_ctx, gate_logits, mods, g_gla, g_ffn,
           w_pa, w_pb, w_pc, w_out, w_router):
    tm, d = lay.tile, D_MODEL
    n = lay.n_tok
    tok = lambda w: pl.BlockSpec((tm, w), lambda t: (t, 0))
    const = lambda a: pl.BlockSpec(a.shape, lambda t: (0,) * a.ndim)
    lt, ct = lay.lat_tiles, lay.ctx_tiles
    sub = fourier_tile // tm

    def yb_lat_index(t):
        tl = jnp.minimum(t, lay.n_lat - 1)
        s = tl % lt
        u = lt - 1 - s
        upper = s >= lt // 2
        blk = jnp.where(upper, (u // sub) * sub + sub - 1 - u % sub, s)
        return (upper.astype(jnp.int32), blk, tl // lt)

    yb_lat_spec = pl.BlockSpec((None, tm, FNET_W), yb_lat_index)
    yb_ctx_spec = pl.BlockSpec((tm, FNET_W), lambda t: (jnp.maximum(t - lay.n_lat, 0) % ct,
                                                         jnp.maximum(t - lay.n_lat, 0) // ct))
    yc_lat_spec = pl.BlockSpec((tm, SWA_Q_W), lambda t: (jnp.minimum(t, lay.n_lat - 1), 0))
    yc_ctx_spec = pl.BlockSpec((tm, SWA_Q_W), lambda t: (jnp.maximum(t - lay.n_lat, 0), 0))
    return pl.pallas_call(
        functools.partial(_merge_kernel, n_lat=lay.n_lat),
        out_shape=(jax.ShapeDtypeStruct((n, d), F32), jax.ShapeDtypeStruct((n, d), BF16),
                   jax.ShapeDtypeStruct((n, ROUTER_PAD), F32),
                   jax.ShapeDtypeStruct((lay.n_tiles, 1, ROUTER_PAD), jnp.int32)),
        grid=(lay.n_tiles,),
        in_specs=[tok(d), tok(GLA_V_W), tok(GLA_V_W), tok(GLA_V_W), yb_lat_spec, yb_ctx_spec,
                  yc_lat_spec, yc_ctx_spec, tok(3 * d),
                  pl.BlockSpec((None, 1, N_MOD * d), lambda t: (lay.mod_row(t), 0, 0)),
                  const(g_gla), const(g_ffn), const(w_pa), const(w_pb), const(w_pc), const(w_out),
                  const(w_router)],
        out_specs=(tok(d), tok(d), tok(ROUTER_PAD),
                   pl.BlockSpec((None, 1, ROUTER_PAD), lambda t: (t, 0, 0))),
        compiler_params=_params(("parallel",)),
        name="merge",
    )(x, o_f, o_b, r_a, yb_lat, yb_ctx, yc_lat, yc_ctx, gate_logits, mods, g_gla, g_ffn,
      w_pa, w_pb, w_pc, w_out, w_router)


def _route_ids(route, axis):
    take = (lambda i: route[:, i:i + 1]) if axis == 1 else (lambda i: route[i:i + 1, :])
    return take(ROUTE_ID).astype(jnp.int32), take(ROUTE_ID + 1).astype(jnp.int32)


def _moe_plan(counts, n_blocks):
    cnt = counts
    pc = (cnt + ROW_CHUNK - 1) // ROW_CHUNK * ROW_CHUNK
    lstart = jnp.cumsum(pc, axis=1) - pc
    tot = jnp.sum(pc, axis=0)
    tot_pad = (tot + FFN_BLOCK - 1) // FFN_BLOCK * FFN_BLOCK
    eend = jnp.cumsum(tot_pad)
    estart = eend - tot_pad
    base = estart[None, :] + jnp.cumsum(pc, axis=0) - pc
    n_used = eend[-1] // FFN_BLOCK
    blk = jnp.minimum(jnp.arange(n_blocks, dtype=jnp.int32), n_used - 1)
    bexp = jnp.sum((blk[:, None] * FFN_BLOCK >= eend[None, :]).astype(jnp.int32), axis=1)
    flat = lambda a: a.reshape(-1).astype(jnp.int32)
    return dict(base=flat(base), lstart=flat(lstart), nch=flat(pc // ROW_CHUNK),
                gap0=flat(estart + tot), gapn=flat((tot_pad - tot) // ROW_CHUNK),
                bexp=flat(bexp), nused=flat(n_used))


def _segment_copies(t, base_ref, lstart_ref, nch_ref, make, start):
    per_big = BIG_CHUNK // ROW_CHUNK

    def per_expert(e, carry):
        idx = t * MOE_EXPERTS + e
        loc = lstart_ref[idx]
        glob = base_ref[idx]
        n_big = nch_ref[idx] // per_big
        n_small = nch_ref[idx] - n_big * per_big

        def piece(rows, first):
            def body(i, c):
                off = first + i * rows
                cp = make(pl.multiple_of(loc + off, ROW_CHUNK), pl.multiple_of(glob + off, ROW_CHUNK), rows)
                cp.start() if start else cp.wait()
                return c
            return body

        carry = lax.fori_loop(0, n_big, piece(BIG_CHUNK, 0), carry)
        return lax.fori_loop(0, n_small, piece(ROW_CHUNK, n_big * BIG_CHUNK), carry)

    lax.fori_loop(0, MOE_EXPERTS, per_expert, 0)


def _dispatch_kernel(base_ref, lstart_ref, nch_ref, gap0_ref, gapn_ref, nused_ref, h_ref, route_ref, xs_ref,
                     buf_ref, sem):
    t = pl.program_id(0)
    tile = h_ref.shape[0]
    slots = buf_ref.shape[1]
    rt = route_ref[...].T
    e1, e2 = _route_ids(rt, 0)
    sub = lax.broadcasted_iota(jnp.int32, rt.shape, 0)
    oh1, oh2 = sub == e1, sub == e2
    hit = jnp.where(oh1 | oh2, 1.0, 0.0).astype(BF16)
    before = (lax.broadcasted_iota(jnp.int32, (tile, tile), 0)
              < lax.broadcasted_iota(jnp.int32, (tile, tile), 1))
    rank = _dot(hit, jnp.where(before, 1.0, 0.0).astype(BF16))
    sub1 = lax.broadcasted_iota(jnp.int32, (rt.shape[0], 1), 0)
    seg = jnp.zeros((rt.shape[0], 1), F32)
    for e in range(MOE_EXPERTS):
        seg = jnp.where(sub1 == e, lstart_ref[t * MOE_EXPERTS + e].astype(F32), seg)
    slot_of = rank + seg
    pos1 = jnp.sum(jnp.where(oh1, slot_of, 0.0), axis=0, keepdims=True).astype(jnp.int32)
    pos2 = jnp.sum(jnp.where(oh2, slot_of, 0.0), axis=0, keepdims=True).astype(jnp.int32)
    slot = lax.broadcasted_iota(jnp.int32, (slots, tile), 0)
    perm = jnp.where((slot == pos1) | (slot == pos2), 1.0, 0.0).astype(BF16)
    cur = t % 2
    buf_ref[cur] = _dot(perm, h_ref[...]).astype(BF16)

    def maker(b):
        def make(loc, glob, rows):
            return pltpu.make_async_copy(buf_ref.at[b, pl.ds(loc, rows)], xs_ref.at[pl.ds(glob, rows)], sem.at[b])
        return make

    _segment_copies(t, base_ref, lstart_ref, nch_ref, maker(cur), True)

    @pl.when(t > 0)
    def _():
        _segment_copies(t - 1, base_ref, lstart_ref, nch_ref, maker(1 - cur), False)

    @pl.when(t == pl.num_programs(0) - 1)
    def _():
        _segment_copies(t, base_ref, lstart_ref, nch_ref, maker(cur), False)
        buf_ref[cur, 0:FFN_BLOCK, :] = jnp.zeros((FFN_BLOCK, buf_ref.shape[2]), BF16)
        n_blocks = xs_ref.shape[0] // FFN_BLOCK

        def fill(start):
            def per_expert(e, carry):
                def chunk(i, c):
                    row = pl.multiple_of(gap0_ref[e] + i * ROW_CHUNK, ROW_CHUNK)
                    cp = pltpu.make_async_copy(buf_ref.at[cur, 0:ROW_CHUNK], xs_ref.at[pl.ds(row, ROW_CHUNK)],
                                               sem.at[cur])
                    cp.start() if start else cp.wait()
                    return c
                return lax.fori_loop(0, gapn_ref[e], chunk, carry)

            def tail(b, c):
                row = pl.multiple_of(b * FFN_BLOCK, FFN_BLOCK)
                cp = pltpu.make_async_copy(buf_ref.at[cur, 0:FFN_BLOCK], xs_ref.at[pl.ds(row, FFN_BLOCK)],
                                           sem.at[cur])
                cp.start() if start else cp.wait()
                return c

            lax.fori_loop(0, MOE_EXPERTS, per_expert, 0)
            lax.fori_loop(nused_ref[0], n_blocks, tail, 0)

        fill(True)
        fill(False)


def _dispatch(plan, h2, route, n_rows):
    tile, d = DISPATCH_TILE, D_MODEL
    n_tiles = h2.shape[0] // tile
    return pl.pallas_call(
        _dispatch_kernel,
        out_shape=jax.ShapeDtypeStruct((n_rows, d), BF16),
        grid_spec=pltpu.PrefetchScalarGridSpec(
            num_scalar_prefetch=6,
            grid=(n_tiles,),
            in_specs=[pl.BlockSpec((tile, d), lambda t, *_: (t, 0)),
                      pl.BlockSpec((tile, ROUTER_PAD), lambda t, *_: (t, 0))],
            out_specs=pl.BlockSpec(memory_space=pl.ANY),
            scratch_shapes=[pltpu.VMEM((2, SLOT_ROWS, d), BF16), pltpu.SemaphoreType.DMA((2,))],
        ),
        compiler_params=_params(("arbitrary",)),
        name="moe_dispatch",
    )(plan["base"], plan["lstart"], plan["nch"], plan["gap0"], plan["gapn"], plan["nused"], h2, route)


def _ffn_kernel(bexp_ref, nused_ref, x_ref, w1_ref, w3_ref, w2_ref, y_ref, w13_s, w2_s):
    b = pl.program_id(0)
    used = b < nused_ref[0]
    de = D_EXPERT

    @pl.when(used & ((b == 0) | (bexp_ref[b] != bexp_ref[jnp.maximum(b - 1, 0)])))
    def _():
        w13_s[:, :de] = w1_ref[...].astype(BF16)
        w13_s[:, de:] = w3_ref[...].astype(BF16)
        w2_s[...] = w2_ref[...].astype(BF16)

    @pl.when(used)
    def _():
        ab = _dot(x_ref[...], w13_s[...])
        a = ab[:, :de]
        hid = (a * _sigmoid(a)) * ab[:, de:]
        y_ref[...] = _dot(hid.astype(BF16), w2_s[...]).astype(BF16)

    @pl.when(jnp.logical_not(used))
    def _():
        y_ref[...] = jnp.zeros_like(y_ref)


def _expert_ffn(plan, xs, layer, w1, w3, w2):
    d = D_MODEL
    n_blocks = xs.shape[0] // FFN_BLOCK
    row = lambda b, bexp, nused: (jnp.minimum(b, nused[0] - 1), 0)
    wsel = lambda b, bexp, nused: (layer, bexp[b], 0, 0)
    return pl.pallas_call(
        _ffn_kernel,
        out_shape=jax.ShapeDtypeStruct(xs.shape, BF16),
        grid_spec=pltpu.PrefetchScalarGridSpec(
            num_scalar_prefetch=2,
            grid=(n_blocks,),
            in_specs=[pl.BlockSpec((FFN_BLOCK, d), row),
                      pl.BlockSpec((None, None, d, D_EXPERT), wsel),
                      pl.BlockSpec((None, None, d, D_EXPERT), wsel),
                      pl.BlockSpec((None, None, D_EXPERT, d), wsel)],
            out_specs=pl.BlockSpec((FFN_BLOCK, d), lambda b, bexp, nused: (b, 0)),
            scratch_shapes=[pltpu.VMEM((d, 2 * D_EXPERT), BF16), pltpu.VMEM((D_EXPERT, d), BF16)],
        ),
        compiler_params=_params(("arbitrary",)),
        name="moe_ffn",
    )(plan["bexp"], plan["nused"], xs, w1, w3, w2)


def _combine_kernel(base_ref, lstart_ref, nch_ref, route_ref, x1_ref, mod_ref, ys_ref, *rest, n_out_tiles):
    g_ref = rest[0] if len(rest) == 4 else None
    o_ref, buf_ref, sem = rest[-3:]
    d = D_MODEL
    t = pl.program_id(0)
    tile = route_ref.shape[0]
    slots = buf_ref.shape[1]
    cur = t % 2

    def maker(b):
        def make(loc, glob, rows):
            return pltpu.make_async_copy(ys_ref.at[pl.ds(glob, rows)], buf_ref.at[b, pl.ds(loc, rows)], sem.at[b])
        return make

    @pl.when(t == 0)
    def _():
        buf_ref[...] = jnp.zeros_like(buf_ref)
        _segment_copies(t, base_ref, lstart_ref, nch_ref, maker(cur), True)

    @pl.when(t + 1 < pl.num_programs(0))
    def _():
        _segment_copies(t + 1, base_ref, lstart_ref, nch_ref, maker(1 - cur), True)

    route = route_ref[...]
    e1, e2 = _route_ids(route, 1)
    lane = lax.broadcasted_iota(jnp.int32, route.shape, 1)
    oh1, oh2 = lane == e1, lane == e2
    hit = jnp.where(oh1 | oh2, 1.0, 0.0).astype(BF16)
    before = (lax.broadcasted_iota(jnp.int32, (tile, tile), 1)
              < lax.broadcasted_iota(jnp.int32, (tile, tile), 0))
    rank = _dot(jnp.where(before, 1.0, 0.0).astype(BF16), hit)
    lane1 = lax.broadcasted_iota(jnp.int32, (1, route.shape[1]), 1)
    seg = jnp.zeros((1, route.shape[1]), F32)
    for e in range(MOE_EXPERTS):
        seg = jnp.where(lane1 == e, lstart_ref[t * MOE_EXPERTS + e].astype(F32), seg)
    slot_of = rank + seg
    pos1 = jnp.sum(jnp.where(oh1, slot_of, 0.0), axis=1, keepdims=True).astype(jnp.int32)
    pos2 = jnp.sum(jnp.where(oh2, slot_of, 0.0), axis=1, keepdims=True).astype(jnp.int32)
    slot = lax.broadcasted_iota(jnp.int32, (tile, slots), 1)
    w1 = route[:, ROUTE_W:ROUTE_W + 1]
    w2 = route[:, ROUTE_W + 1:ROUTE_W + 2]
    comb = (jnp.where(slot == pos1, w1, 0.0) + jnp.where(slot == pos2, w2, 0.0)).astype(BF16)
    _segment_copies(t, base_ref, lstart_ref, nch_ref, maker(cur), False)
    moe = _dot(comb, buf_ref[cur])
    x2 = x1_ref[...] + mod_ref[:, 5 * d:6 * d] * moe
    if g_ref is None:
        o_ref[...] = x2
    else:
        @pl.when(t < n_out_tiles)
        def _():
            o_ref[...] = x2 * lax.rsqrt(jnp.mean(x2 * x2, axis=-1, keepdims=True) + EPS) * g_ref[...]


def _combine(lay, plan, route, x1, mods, ys, g_final):
    tile, d = DISPATCH_TILE, D_MODEL
    assert lay.seq % tile == 0 and (lay.batch * lay.ctx) % tile == 0
    lat_tiles = lay.seq // tile
    n_lat = lay.batch * lat_tiles
    n_tiles = lay.n_tok // tile
    row = lambda t: jnp.where(t < n_lat, t // lat_tiles, lay.batch)
    final = g_final is not None
    n_out_tiles = n_lat if final else n_tiles
    extra_specs = [pl.BlockSpec((1, d), lambda t, *_: (0, 0))] if final else []
    extra_args = [g_final] if final else []
    return pl.pallas_call(
        functools.partial(_combine_kernel, n_out_tiles=n_out_tiles),
        out_shape=jax.ShapeDtypeStruct((n_out_tiles * tile, d), F32),
        grid_spec=pltpu.PrefetchScalarGridSpec(
            num_scalar_prefetch=3,
            grid=(n_tiles,),
            in_specs=[pl.BlockSpec((tile, ROUTER_PAD), lambda t, *_: (t, 0)),
                      pl.BlockSpec((tile, d), lambda t, *_: (t, 0)),
                      pl.BlockSpec((None, 1, N_MOD * d), lambda t, *_: (row(t), 0, 0)),
                      pl.BlockSpec(memory_space=pl.ANY)] + extra_specs,
            out_specs=pl.BlockSpec((tile, d), lambda t, *_: (jnp.minimum(t, n_out_tiles - 1), 0)),
            scratch_shapes=[pltpu.VMEM((2, SLOT_ROWS, d), BF16), pltpu.SemaphoreType.DMA((2,))],
        ),
        compiler_params=_params(("arbitrary",)),
        name="moe_combine",
    )(plan["base"], plan["lstart"], plan["nch"], route, x1, mods, ys, *extra_args)


def _moe(lay, h2, x1, route, tile_counts, mods, layer, w1, w3, w2, g_final=None):
    n_tiles = lay.n_tok // DISPATCH_TILE
    max_rows = (MOE_TOPK * lay.n_tok + n_tiles * MOE_EXPERTS * (ROW_CHUNK - 1)
                + MOE_EXPERTS * (FFN_BLOCK - 1))
    n_blocks = -(-max_rows // FFN_BLOCK)
    counts = jnp.sum(tile_counts[:, 0, :MOE_EXPERTS].reshape(n_tiles, DISPATCH_TILE // lay.tile, MOE_EXPERTS), axis=1)
    plan = _moe_plan(counts, n_blocks)
    xs = _dispatch(plan, h2, route, n_blocks * FFN_BLOCK)
    ys = _expert_ffn(plan, xs, layer, w1, w3, w2)
    return _combine(lay, plan, route, x1, mods, ys, g_final)


def _rope_tables(seq, ctx):
    pos = jnp.arange(seq, dtype=jnp.int32)
    inv_freq = ROPE_BASE ** (-jnp.arange(0, AXIS_DIM, 2, dtype=F32) / AXIS_DIM)
    ang_row = (pos // GRID_W).astype(F32)[:, None] * inv_freq
    ang_col = (pos % GRID_W).astype(F32)[:, None] * inv_freq
    cos_h = jnp.concatenate([jnp.cos(ang_row)] * 2 + [jnp.cos(ang_col)] * 2, axis=1)
    sin_h = jnp.concatenate([-jnp.sin(ang_row), jnp.sin(ang_row), -jnp.sin(ang_col), jnp.sin(ang_col)], axis=1)
    reps = LANES // SWA_HEAD_DIM
    cos_t = jnp.concatenate([jnp.tile(cos_h, (1, reps)), jnp.ones((ctx, LANES), F32)], axis=0)
    sin_t = jnp.concatenate([jnp.tile(sin_h, (1, reps)), jnp.zeros((ctx, LANES), F32)], axis=0)
    return cos_t, sin_t


def _channel_dft():
    i = jnp.arange(FNET_GROUP_W, dtype=jnp.int32)
    ang = ((i[:, None] * i[None, :]) % FNET_GROUP_W).astype(F32) * (2.0 * math.pi / FNET_GROUP_W)
    return jnp.concatenate([jnp.cos(ang), jnp.sin(ang)], axis=1).astype(BF16)


def kernel(x, c, ctx, c_ctx, w_ada, b_ada, g_mix, g_ffn, w_in, w_decay_down, w_decay_up, b_decay, g_gla, sink,
           w_pa, w_pb, w_pc, w_out, w_router_group, w_router_expert, w1, w3, w2, g_final):
    batch, seq, d = x.shape
    n_ctx = ctx.shape[1]
    depth = w_ada.shape[0]
    assert d == D_MODEL and seq % GRID_W == 0 and seq % n_ctx == 0
    lay = _Layout(batch, seq, n_ctx, TOKEN_TILE)

    rows = -(-(batch + 1) // 8) * 8
    c_all = jnp.zeros((rows, d), F32).at[:batch].set(c).at[batch].set(c_ctx)
    mods_all = _adaln(c_all, w_ada, b_ada).reshape(depth, rows, 1, N_MOD * d)

    cos_t, sin_t = _rope_tables(seq, n_ctx)
    cs = _channel_dft()
    fourier_tile = min(seq // 2, FOURIER_TILE)
    assert fourier_tile % TOKEN_TILE == 0 and seq % (2 * fourier_tile) == 0
    lat_mats = _dft_half_mats(seq, fourier_tile)
    c_ctx_m, s_ctx_m = _dft_mats(n_ctx)

    xs = jnp.concatenate([x.reshape(batch * seq, d), ctx.reshape(batch * n_ctx, d)], axis=0)
    rank = w_decay_down.shape[-1]
    w_in_bf = w_in.astype(BF16)
    for l in range(depth):
        last = l == depth - 1
        mods = mods_all[l]
        down = jnp.concatenate([w_decay_down[l, 0], w_decay_down[l, 1]], axis=1)
        down = jnp.pad(down, ((0, 0), (0, DECAY_PAD - 2 * rank))).astype(BF16)
        w_up = jnp.zeros((DECAY_PAD, 2 * GLA_QK_W), F32)
        w_up = w_up.at[:rank, :GLA_QK_W].set(w_decay_up[l, 0]).at[rank:2 * rank, GLA_QK_W:].set(w_decay_up[l, 1])
        b_dec = b_decay[l].reshape(1, 2 * GLA_QK_W)
        (k_a, v_a, k_c, v_c, q_a, r_a, u_cos, u_sin, q_c, gate_logits, la_f, la_b) = _inproj(
            lay, xs, g_mix[l].reshape(1, d), mods, l, w_in_bf, down, w_up.astype(BF16), b_dec, cs, cos_t, sin_t)

        o_f, o_b = _gla(lay, q_a, k_a, v_a, la_f, la_b)
        yb_lat = _fourier_half(u_cos, u_sin, lat_mats, seq, fourier_tile)
        yc_lat = _swa(lay, q_c, k_c, v_c, jnp.broadcast_to(sink[l][:, None], (SWA_HEADS, LANES)))
        if last:
            yb_ctx = jnp.zeros((n_ctx, batch * FNET_W), BF16)
            yc_ctx = jnp.zeros((batch * n_ctx, SWA_Q_W), BF16)
        else:
            yb_ctx = _fourier(u_cos, u_sin, c_ctx_m, s_ctx_m, seq, n_ctx, n_ctx, n_ctx)
            yc_ctx = _ctx_attn(lay, q_c, k_c, v_c, jnp.broadcast_to(sink[l][:, None], (SWA_HEADS, LANES)))

        w_router = jnp.zeros((d, ROUTER_PAD), F32)
        w_router = w_router.at[:, :MOE_EXPERTS].set(w_router_expert[l])
        w_router = w_router.at[:, MOE_EXPERTS:MOE_EXPERTS + MOE_GROUPS].set(w_router_group[l])
        w_router_hi = w_router.astype(BF16)
        w_router = jnp.concatenate([w_router_hi, (w_router - w_router_hi.astype(F32)).astype(BF16)], axis=1)
        x1, h2, gates, tile_counts = _merge(lay, xs, o_f, o_b, r_a, yb_lat, fourier_tile, yb_ctx, yc_lat, yc_ctx,
                               gate_logits, mods,
                               g_gla[l].reshape(1, GLA_DV), g_ffn[l].reshape(1, d),
                               w_pa[l].astype(BF16), w_pb[l].astype(BF16), w_pc[l].astype(BF16),
                               w_out[l].astype(BF16), w_router)
        xs = _moe(lay, h2, x1, gates, tile_counts, mods, l, w1, w3, w2,
                  g_final.reshape(1, d) if last else None)

    return xs.reshape(batch, seq, d)
```

```python
import functools
import math

import jax
import jax.numpy as jnp
from jax import lax
from jax.experimental import pallas as pl
from jax.experimental.pallas import tpu as pltpu

F32 = jnp.float32
BF16 = jnp.bfloat16
HIGHEST = lax.Precision.HIGHEST

D_MODEL = 1024
GRID_W = 64
EPS = 1e-6
N_MOD = 6
GLA_HEADS = 4
GLA_DK = 64
GLA_DV = 128
GLA_TAU = 16.0
GLA_CHUNK = 64
GLA_SCALE = GLA_DK ** -0.5
FNET_GROUPS = 4
FNET_GROUP_W = 128
SWA_HEADS = 8
SWA_KV_HEADS = 2
SWA_HEAD_DIM = 64
WINDOW = 128
ROPE_BASE = 10000.0
AXIS_DIM = SWA_HEAD_DIM // 2
MOE_GROUPS = 4
MOE_EXPERTS_PER_GROUP = 4
MOE_EXPERTS = MOE_GROUPS * MOE_EXPERTS_PER_GROUP
MOE_TOPK = 2
D_EXPERT = 512

GLA_QK_W = GLA_HEADS * GLA_DK
GLA_V_W = GLA_HEADS * GLA_DV
FNET_W = FNET_GROUPS * FNET_GROUP_W
SWA_Q_W = SWA_HEADS * SWA_HEAD_DIM
SWA_KV_W = SWA_KV_HEADS * SWA_HEAD_DIM
IN_SIZES = (GLA_QK_W, GLA_V_W, SWA_KV_W, SWA_KV_W, GLA_QK_W, GLA_V_W, FNET_W, SWA_Q_W, 3 * D_MODEL)
IN_OFFS = tuple(int(sum(IN_SIZES[:i])) for i in range(len(IN_SIZES) + 1))
IN_COLS = IN_OFFS[-1]

LANES = 128
TOKEN_TILE = 256
DECAY_PAD = LANES
ROUTER_PAD = LANES
ROUTE_ID = 0
ROUTE_W = 2
DISPATCH_TILE = 512
ROW_CHUNK = 16
BIG_CHUNK = 64
FOURIER_TILE = 512
FOURIER_COL_SPLIT = 2
FFN_BLOCK = 512
SLOT_ROWS = -(-(MOE_TOPK * DISPATCH_TILE + MOE_EXPERTS * (ROW_CHUNK - 1)) // LANES) * LANES
VMEM_LIMIT = 56 * 1024 * 1024


def _params(sem, vmem=VMEM_LIMIT):
    return pltpu.CompilerParams(dimension_semantics=sem, vmem_limit_bytes=vmem)


def _sigmoid(x):
    return 0.5 * jnp.tanh(0.5 * x) + 0.5


def _dot(a, b):
    return jnp.dot(a, b, preferred_element_type=F32)


def _dot_nt(a, b):
    return lax.dot_general(a, b, (((1,), (1,)), ((), ())), preferred_element_type=F32)


def _ada_kernel(c_ref, w_ref, b_ref, o_ref):
    c = c_ref[...]
    a = c * _sigmoid(c)
    o_ref[...] = jnp.dot(a, w_ref[...], preferred_element_type=F32, precision=HIGHEST) + b_ref[...]


def _adaln(c_all, w_ada, b_ada):
    depth, d, n = w_ada.shape
    rows = c_all.shape[0]
    tn = 1536
    return pl.pallas_call(
        _ada_kernel,
        out_shape=jax.ShapeDtypeStruct((depth, rows, n), F32),
        grid=(depth, n // tn),
        in_specs=[pl.BlockSpec((rows, d), lambda l, j: (0, 0)),
                  pl.BlockSpec((None, d, tn), lambda l, j: (l, 0, j)),
                  pl.BlockSpec((None, 1, tn), lambda l, j: (l, 0, j))],
        out_specs=pl.BlockSpec((None, rows, tn), lambda l, j: (l, 0, j)),
        compiler_params=_params(("parallel", "parallel")),
        name="adaln",
    )(c_all, w_ada, b_ada.reshape(depth, 1, n))


class _Layout:
    def __init__(self, batch, seq, ctx, tile):
        assert seq % tile == 0 and ctx % tile == 0
        self.batch, self.seq, self.ctx, self.tile = batch, seq, ctx, tile
        self.lat_tiles = seq // tile
        self.ctx_tiles = ctx // tile
        self.n_lat = batch * self.lat_tiles
        self.n_tiles = self.n_lat + batch * self.ctx_tiles
        self.n_tok = self.n_tiles * tile

    def batch_of(self, t):
        return jnp.where(t < self.n_lat, t // self.lat_tiles, (t - self.n_lat) // self.ctx_tiles)

    def mod_row(self, t):
        return jnp.where(t < self.n_lat, t // self.lat_tiles, self.batch)

    def seq_tile(self, t):
        return jnp.where(t < self.n_lat, t % self.lat_tiles,
                         self.lat_tiles + (t - self.n_lat) % self.ctx_tiles)


def _rope(x, cos, sin_signed):
    n = x.shape[-1]
    lane = lax.broadcasted_iota(jnp.int32, x.shape, 1)
    half = AXIS_DIM // 2
    partner = jnp.where((lane & half) == 0, pltpu.roll(x, n - half, 1), pltpu.roll(x, half, 1))
    return x * cos + partner * sin_signed


def _inproj_kernel(x_ref, g_ref, mod_ref, w_ref, wd_ref, wu_ref, bdec_ref, cs_ref, cos_ref, sin_ref,
                   ka_ref, va_ref, kc_ref, vc_ref, qa_ref, ra_ref, ua_ref, us_ref, qc_ref, gl_ref,
                   laf_ref, lab_ref):
    d = D_MODEL
    x = x_ref[...]
    shift = mod_ref[:, 0:d]
    scale = mod_ref[:, d:2 * d]
    h = x * lax.rsqrt(jnp.mean(x * x, axis=-1, keepdims=True) + EPS) * g_ref[...]
    hb = (h * (1.0 + scale) + shift).astype(BF16)

    def proj(i):
        return _dot(hb, w_ref[:, IN_OFFS[i]:IN_OFFS[i + 1]])

    ka_ref[...] = proj(0).astype(BF16)
    va_ref[...] = proj(1).astype(BF16)
    cos = cos_ref[...]
    sin = sin_ref[...]
    kc_ref[...] = _rope(proj(2), cos, sin).astype(BF16)
    vc_ref[...] = proj(3).astype(BF16)
    qa_ref[...] = proj(4).astype(BF16)
    r = proj(5)
    ra_ref[...] = (r * _sigmoid(r)).astype(BF16)
    u = proj(6).astype(BF16)
    for g in range(FNET_GROUPS):
        sl = slice(g * FNET_GROUP_W, (g + 1) * FNET_GROUP_W)
        ab = _dot(u[:, sl], cs_ref[...])
        ua_ref[:, sl] = ab[:, :FNET_GROUP_W].astype(BF16)
        us_ref[:, sl] = ab[:, FNET_GROUP_W:].astype(BF16)
    reps = SWA_Q_W // LANES
    qc_ref[...] = (_rope(proj(7), jnp.concatenate([cos] * reps, axis=1), jnp.concatenate([sin] * reps, axis=1))
                   * SWA_HEAD_DIM ** -0.5).astype(BF16)
    gl_ref[...] = _sigmoid(proj(8)).astype(BF16)
    low = _dot(hb, wd_ref[...]).astype(BF16)
    z = _dot(low, wu_ref[...]) + bdec_ref[...]
    la = (jnp.minimum(z, 0.0) - jnp.log(1.0 + jnp.exp(-jnp.abs(z)))) * (1.0 / GLA_TAU)
    laf_ref[...] = la[:, :GLA_QK_W]
    lab_ref[...] = la[:, GLA_QK_W:]


def _inproj(lay, x, g_mix, mods, layer, w_in, w_down, w_up, b_dec, cs, cos_t, sin_t):
    tm, d = lay.tile, D_MODEL
    n = lay.n_tok
    bf = lambda w: jax.ShapeDtypeStruct((n, w), BF16)
    tok = lambda w: pl.BlockSpec((tm, w), lambda t: (t, 0))
    const = lambda a: pl.BlockSpec(a.shape, lambda t: (0,) * a.ndim)
    seq_rows = lay.seq + lay.ctx
    fnet_shape = jax.ShapeDtypeStruct((seq_rows, lay.batch * FNET_W), BF16)
    fnet_spec = pl.BlockSpec((tm, FNET_W), lambda t: (lay.seq_tile(t), lay.batch_of(t)))
    pos_spec = pl.BlockSpec((tm, LANES), lambda t: (lay.seq_tile(t), 0))
    return pl.pallas_call(
        _inproj_kernel,
        out_shape=(bf(GLA_QK_W), bf(GLA_V_W), bf(SWA_KV_W), bf(SWA_KV_W), bf(GLA_QK_W), bf(GLA_V_W),
                   fnet_shape, fnet_shape, bf(SWA_Q_W), bf(3 * d),
                   jax.ShapeDtypeStruct((n, GLA_QK_W), F32), jax.ShapeDtypeStruct((n, GLA_QK_W), F32)),
        grid=(lay.n_tiles,),
        in_specs=[tok(d), const(g_mix),
                  pl.BlockSpec((None, 1, N_MOD * d), lambda t: (lay.mod_row(t), 0, 0)),
                  pl.BlockSpec((None,) + w_in.shape[1:], lambda t: (layer, 0, 0)),
                  const(w_down), const(w_up), const(b_dec), const(cs), pos_spec, pos_spec],
        out_specs=(tok(GLA_QK_W), tok(GLA_V_W), tok(SWA_KV_W), tok(SWA_KV_W), tok(GLA_QK_W), tok(GLA_V_W),
                   fnet_spec, fnet_spec, tok(SWA_Q_W), tok(3 * d), tok(GLA_QK_W), tok(GLA_QK_W)),
        compiler_params=_params(("parallel",)),
        name="inproj",
    )(x, g_mix, mods, w_in, w_down, w_up, b_dec, cs, cos_t, sin_t)


def _split2(x):
    hi = x.astype(BF16)
    return hi, (x - hi.astype(F32)).astype(BF16)


def _gla_direction(q_ref, k_ref, v_ref, la_ref, o_ref, s_ref, rev):
    t_rows = q_ref.shape[0]
    c = GLA_CHUNK
    n_sub = t_rows // c
    hk = GLA_QK_W
    la_hi, la_lo = _split2(la_ref[...])
    row = lax.broadcasted_iota(jnp.int32, (t_rows, t_rows), 0)
    col = lax.broadcasted_iota(jnp.int32, (t_rows, t_rows), 1)
    tri = ((row // c) == (col // c)) & ((col >= row) if rev else (col <= row))
    tri = jnp.where(tri, 1.0, 0.0).astype(BF16)
    lc = _dot(tri, la_hi) + _dot(tri, la_lo)
    q = q_ref[...].astype(F32)
    k = k_ref[...].astype(F32)
    qd = (q * jnp.exp(lc) * GLA_SCALE).astype(BF16)
    kd = (k * jnp.exp(-lc)).astype(BF16)
    hrow = lax.broadcasted_iota(jnp.int32, (GLA_HEADS * c, hk), 0) // c
    hcol = lax.broadcasted_iota(jnp.int32, (GLA_HEADS * c, hk), 1) // GLA_DK
    head_mask = hrow == hcol
    trow = lax.broadcasted_iota(jnp.int32, (GLA_HEADS * c, c), 0) % c
    scol = lax.broadcasted_iota(jnp.int32, (GLA_HEADS * c, c), 1)
    causal = (scol >= trow) if rev else (scol <= trow)
    edge = 0 if rev else c - 1
    lasts = [lc[i * c + edge:i * c + edge + 1] for i in range(n_sub)]
    to_end = jnp.concatenate([jnp.broadcast_to(l, (c, hk)) for l in lasts], axis=0) - lc
    kh_t = (k * jnp.exp(to_end)).T
    pad = jnp.zeros((LANES - n_sub, hk), F32)
    decay_t = jnp.exp(jnp.concatenate(lasts + [pad], axis=0).T)
    yield
    chunk_of = lax.broadcasted_iota(jnp.int32, (1, t_rows), 1) // c
    kv = []
    for h in range(GLA_HEADS):
        kh_h = kh_t[h * GLA_DK:(h + 1) * GLA_DK]
        stack = jnp.concatenate([jnp.where(chunk_of == i, kh_h, 0.0) for i in range(n_sub)], axis=0)
        kv.append(_dot(stack.astype(BF16), v_ref[:, h * GLA_DV:(h + 1) * GLA_DV]))
    yield
    order = range(n_sub - 1, -1, -1) if rev else range(n_sub)
    state = s_ref[...]
    state_at = {}
    for i in order:
        state_at[i] = state
        inc = jnp.concatenate([kv[h][i * GLA_DK:(i + 1) * GLA_DK] for h in range(GLA_HEADS)], axis=0)
        state = decay_t[:, i:i + 1] * state + inc
    s_ref[...] = state
    yield
    inters, scoress = {}, {}
    for i in order:
        rows = slice(i * c, (i + 1) * c)
        q_stack = jnp.where(head_mask, jnp.concatenate([qd[rows]] * GLA_HEADS, axis=0), 0.0).astype(BF16)
        inters[i] = _dot(q_stack, state_at[i].astype(BF16))
        scoress[i] = jnp.where(causal, _dot_nt(q_stack, kd[rows]), 0.0).astype(BF16)
    for i in order:
        yield
        rows = slice(i * c, (i + 1) * c)
        v_i = v_ref[rows, :]
        inter, scores = inters[i], scoress[i]
        outs = []
        for h in range(GLA_HEADS):
            hr = slice(h * c, (h + 1) * c)
            hv = slice(h * GLA_DV, (h + 1) * GLA_DV)
            outs.append(inter[hr] + _dot(scores[hr], v_i[:, hv]))
        o_ref[rows, :] = jnp.concatenate(outs, axis=1).astype(BF16)


def _gla_kernel(qf_ref, kf_ref, vf_ref, laf_ref, qb_ref, kb_ref, vb_ref, lab_ref,
                of_ref, ob_ref, sf_ref, sb_ref):
    @pl.when(pl.program_id(1) == 0)
    def _():
        sf_ref[...] = jnp.zeros_like(sf_ref)
        sb_ref[...] = jnp.zeros_like(sb_ref)

    live = [_gla_direction(qf_ref, kf_ref, vf_ref, laf_ref, of_ref, sf_ref, False),
            _gla_direction(qb_ref, kb_ref, vb_ref, lab_ref, ob_ref, sb_ref, True)]
    while live:
        live = [g for g in live if next(g, "done") != "done"]


def _gla(lay, q_a, k_a, v_a, la_f, la_b):
    tm = lay.tile
    nc, nl = lay.ctx_tiles, lay.lat_tiles

    def fwd(b, j):
        return jnp.where(j < nc, lay.n_lat + b * nc + j, b * nl + (j - nc))

    def bwd(b, j):
        return jnp.where(j < nc, lay.n_lat + b * nc + (nc - 1 - j), b * nl + (nl - 1 - (j - nc)))

    spec = lambda w, f: pl.BlockSpec((tm, w), lambda b, j: (f(b, j), 0))
    out = jax.ShapeDtypeStruct((lay.n_tok, GLA_V_W), BF16)
    return pl.pallas_call(
        _gla_kernel,
        out_shape=(out, out),
        grid=(lay.batch, nc + nl),
        in_specs=[spec(GLA_QK_W, fwd), spec(GLA_QK_W, fwd), spec(GLA_V_W, fwd), spec(GLA_QK_W, fwd),
                  spec(GLA_QK_W, bwd), spec(GLA_QK_W, bwd), spec(GLA_V_W, bwd), spec(GLA_QK_W, bwd)],
        out_specs=(spec(GLA_V_W, fwd), spec(GLA_V_W, bwd)),
        scratch_shapes=[pltpu.VMEM((GLA_QK_W, GLA_DV), F32), pltpu.VMEM((GLA_QK_W, GLA_DV), F32)],
        compiler_params=_params(("parallel", "arbitrary")),
        name="gla",
    )(q_a, k_a, v_a, la_f, q_a, k_a, v_a, la_b)


def _dft_kernel(c_ref, s_ref, a_ref, b_ref, o_ref, acc_ref, *, scale):
    k = pl.program_id(1)

    @pl.when(k == 0)
    def _():
        acc_ref[...] = jnp.zeros_like(acc_ref)

    acc_ref[...] += _dot(c_ref[...], a_ref[...]) + _dot(s_ref[...], b_ref[...])

    @pl.when(k == pl.num_programs(1) - 1)
    def _():
        o_ref[...] = (acc_ref[...] * scale).astype(BF16)


def _dft_mats(n):
    f = GRID_W
    assert n % f == 0
    k = jnp.arange(n, dtype=jnp.int32)[None, :]

    def table(rows, period):
        ang = ((jnp.arange(rows, dtype=jnp.int32)[:, None] * k) % period).astype(F32) * (2.0 * math.pi / period)
        return jnp.cos(ang), jnp.sin(ang)

    ca, sa = table(n // f, n // f)
    cb, sb = table(f, n)
    cos = ca[:, None, :] * cb[None, :, :] - sa[:, None, :] * sb[None, :, :]
    sin = sa[:, None, :] * cb[None, :, :] + ca[:, None, :] * sb[None, :, :]
    return cos.reshape(n, n).astype(BF16), (-sin).reshape(n, n).astype(BF16)


def _dft_half_kernel(c_ref, s_ref, cx_ref, sx_ref, a_ref, b_ref, j_ref, o_ref, *, scale):
    a = a_ref[...]
    b = b_ref[...]
    p = _dot(c_ref[...], a)
    q = _dot(s_ref[...], b)
    o_ref[0] = ((p + q) * scale).astype(BF16)
    mirror = ((p - q) * scale).astype(BF16)
    flipped = _dot(j_ref[...], mirror)
    first = (_dot(cx_ref[...], a) - _dot(sx_ref[...], b))[0:1, :] * scale
    row = lax.broadcasted_iota(jnp.int32, (flipped.shape[0], 1), 0)
    o_ref[1] = jnp.where(row == 0, first, flipped).astype(BF16)


def _dft_half_mats(n, tm):
    f = GRID_W
    half = n // 2
    assert half % f == 0 and half % tm == 0
    k = jnp.arange(n, dtype=jnp.int32)[None, :]

    def table(rows, period):
        ang = ((rows[:, None] * k) % period).astype(F32) * (2.0 * math.pi / period)
        return jnp.cos(ang), jnp.sin(ang)

    ca, sa = table(jnp.arange(half // f, dtype=jnp.int32), n // f)
    cb, sb = table(jnp.arange(f, dtype=jnp.int32), n)
    cos = ca[:, None, :] * cb[None, :, :] - sa[:, None, :] * sb[None, :, :]
    sin = sa[:, None, :] * cb[None, :, :] + ca[:, None, :] * sb[None, :, :]
    n_it = half // tm
    cx, sx = table(jnp.arange(1, n_it + 1, dtype=jnp.int32) * tm, n)
    spread = lambda m: jnp.zeros((n_it, 8, n), F32).at[:, 0, :].set(m).reshape(n_it * 8, n).astype(BF16)
    return (cos.reshape(half, n).astype(BF16), (-sin).reshape(half, n).astype(BF16), spread(cx), spread(-sx))


def _fourier_half(ua, us, mats, length, tm):
    cmat, smat, cx, sx = mats
    width = ua.shape[1]
    half = length // 2
    scale = 1.0 / math.sqrt(length * FNET_GROUP_W)
    r = lax.broadcasted_iota(jnp.int32, (tm, tm), 0)
    c = lax.broadcasted_iota(jnp.int32, (tm, tm), 1)
    jmat = jnp.where(c == tm - r, 1.0, 0.0).astype(BF16)
    once = pl.Buffered(1)
    cols = width // FOURIER_COL_SPLIT
    return pl.pallas_call(
        functools.partial(_dft_half_kernel, scale=scale),
        out_shape=jax.ShapeDtypeStruct((2, half, width), BF16),
        grid=(FOURIER_COL_SPLIT, half // tm),
        in_specs=[pl.BlockSpec((tm, length), lambda j, i: (i, 0)),
                  pl.BlockSpec((tm, length), lambda j, i: (i, 0)),
                  pl.BlockSpec((8, length), lambda j, i: (i, 0)),
                  pl.BlockSpec((8, length), lambda j, i: (i, 0)),
                  pl.BlockSpec((length, cols), lambda j, i: (0, j), pipeline_mode=once),
                  pl.BlockSpec((length, cols), lambda j, i: (0, j), pipeline_mode=once),
                  pl.BlockSpec((tm, tm), lambda j, i: (0, 0), pipeline_mode=once)],
        out_specs=pl.BlockSpec((2, tm, cols), lambda j, i: (0, i, j)),
        compiler_params=_params(("parallel", "parallel")),
        name="fourier_half",
    )(cmat, smat, cx, sx, ua, us, jmat)


def _fourier(ua, us, cmat, smat, row0, length, tm, tk):
    width = ua.shape[1]
    off = row0 // tk
    scale = 1.0 / math.sqrt(length * FNET_GROUP_W)
    return pl.pallas_call(
        functools.partial(_dft_kernel, scale=scale),
        out_shape=jax.ShapeDtypeStruct((length, width), BF16),
        grid=(length // tm, length // tk),
        in_specs=[pl.BlockSpec((tm, tk), lambda i, k: (i, k)),
                  pl.BlockSpec((tm, tk), lambda i, k: (i, k)),
                  pl.BlockSpec((tk, width), lambda i, k: (off + k, 0)),
                  pl.BlockSpec((tk, width), lambda i, k: (off + k, 0))],
        out_specs=pl.BlockSpec((tm, width), lambda i, k: (i, 0)),
        scratch_shapes=[pltpu.VMEM((tm, width), F32)],
        compiler_params=_params(("parallel", "arbitrary")),
        name="fourier",
    )(cmat, smat, ua, us)


def _attend(q_ref, k_all, v_all, valid, sink_ref, o_ref):
    hd = SWA_HEAD_DIM
    heads_per_kv = SWA_HEADS // SWA_KV_HEADS
    k_sw = jnp.concatenate([k_all[:, hd:], k_all[:, :hd]], axis=1)
    v_t = v_all.astype(F32).T
    v_t_sw = jnp.concatenate([v_t[hd:], v_t[:hd]], axis=0)
    lane = lax.broadcasted_iota(jnp.int32, (1, LANES), 1)
    lane_half = (lane < hd, lane >= hd)
    row = lax.broadcasted_iota(jnp.int32, (LANES, 1), 0)
    row_half = (row < hd, row >= hd)
    zero = jnp.zeros((), BF16)
    for p in range(SWA_HEADS // 2):
        acc = None
        for par in range(2):
            head = 2 * p + par
            aligned = head // heads_per_kv == par
            qh = jnp.where(lane_half[par], q_ref[:, p * LANES:(p + 1) * LANES], zero)
            s = _dot_nt(k_all if aligned else k_sw, qh)
            if valid is not None:
                kw = valid.shape[0]
                s = jnp.concatenate([jnp.where(valid, s[:kw], -jnp.inf), s[kw:]], axis=0)
            sink = sink_ref[head:head + 1, 0:1]
            m = jnp.maximum(jnp.max(s, axis=0, keepdims=True), sink)
            e = jnp.exp(s - m)
            denom = jnp.sum(e, axis=0, keepdims=True) + jnp.exp(sink - m)
            v_use = jnp.where(row_half[par], v_t if aligned else v_t_sw, 0.0).astype(BF16)
            part = _dot(v_use, e.astype(BF16)) * (1.0 / denom)
            acc = part if acc is None else acc + part
        o_ref[:, p * LANES:(p + 1) * LANES] = acc.T.astype(BF16)


def _swa_kernel(q_ref, kp_ref, kc_ref, kn_ref, kx_ref, vp_ref, vc_ref, vn_ref, vx_ref, sink_ref, o_ref,
                *, seq):
    n = pl.program_id(1)
    w = WINDOW
    k_blocks = [kp_ref[...], kc_ref[0:w, :], kc_ref[w:2 * w, :], kn_ref[...]]
    v_blocks = [vp_ref[...], vc_ref[0:w, :], vc_ref[w:2 * w, :], vn_ref[...]]
    j = lax.broadcasted_iota(jnp.int32, (3 * w, w), 0)
    a = lax.broadcasted_iota(jnp.int32, (3 * w, w), 1)
    band = (j >= a) & (j - a <= 2 * w)
    for half in range(2):
        key_pos = (2 * n + half - 1) * w + j
        valid = band & (key_pos >= 0) & (key_pos < seq)
        k_all = jnp.concatenate(k_blocks[half:half + 3] + [kx_ref[...]], axis=0)
        v_all = jnp.concatenate(v_blocks[half:half + 3] + [vx_ref[...]], axis=0)
        rows = pl.ds(half * w, w)
        _attend(q_ref.at[rows], k_all, v_all, valid, sink_ref, o_ref.at[rows])


def _swa(lay, q_c, k_c, v_c, sink_b):
    w = WINDOW
    assert lay.seq % (2 * w) == 0
    nq = lay.seq // w
    steps = nq // 2
    ctx_blk = (lay.batch * lay.seq) // lay.ctx

    def edge(f):
        return pl.BlockSpec((w, SWA_KV_W), lambda b, n: (b * nq + f(n), 0))

    prev = edge(lambda n: jnp.maximum(2 * n - 1, 0))
    nxt = edge(lambda n: jnp.minimum(2 * n + 2, nq - 1))
    cur = pl.BlockSpec((2 * w, SWA_KV_W), lambda b, n: (b * steps + n, 0))
    ctx_spec = pl.BlockSpec((lay.ctx, SWA_KV_W), lambda b, n: (ctx_blk + b, 0))
    q_spec = pl.BlockSpec((2 * w, SWA_Q_W), lambda b, n: (b * steps + n, 0))
    return pl.pallas_call(
        functools.partial(_swa_kernel, seq=lay.seq),
        out_shape=jax.ShapeDtypeStruct((lay.batch * lay.seq, SWA_Q_W), BF16),
        grid=(lay.batch, steps),
        in_specs=[q_spec, prev, cur, nxt, ctx_spec, prev, cur, nxt, ctx_spec,
                  pl.BlockSpec(sink_b.shape, lambda b, n: (0, 0))],
        out_specs=q_spec,
        compiler_params=_params(("parallel", "parallel")),
        name="swa",
    )(q_c, k_c, k_c, k_c, k_c, v_c, v_c, v_c, v_c, sink_b)


def _ctx_attn_kernel(q_ref, kx_ref, vx_ref, sink_ref, o_ref):
    _attend(q_ref, kx_ref[...], vx_ref[...], None, sink_ref, o_ref)


def _ctx_attn(lay, q_c, k_c, v_c, sink_b):
    ctx_blk = (lay.batch * lay.seq) // lay.ctx
    spec = lambda wd: pl.BlockSpec((lay.ctx, wd), lambda b: (ctx_blk + b, 0))
    return pl.pallas_call(
        _ctx_attn_kernel,
        out_shape=jax.ShapeDtypeStruct((lay.batch * lay.ctx, SWA_Q_W), BF16),
        grid=(lay.batch,),
        in_specs=[spec(SWA_Q_W), spec(SWA_KV_W), spec(SWA_KV_W),
                  pl.BlockSpec(sink_b.shape, lambda b: (0, 0))],
        out_specs=pl.BlockSpec((lay.ctx, SWA_Q_W), lambda b: (b, 0)),
        compiler_params=_params(("parallel",)),
        name="ctx_attn",
    )(q_c, k_c, v_c, sink_b)


def _merge_kernel(x_ref, of_ref, ob_ref, ra_ref, ybl_ref, ybc_ref, ycl_ref, ycc_ref, gl_ref, mod_ref,
                  ggla_ref, gffn_ref, wpa_ref, wpb_ref, wpc_ref, wout_ref, wr_ref,
                  x1_ref, h2_ref, gates_ref, count_ref, *, n_lat):
    d = D_MODEL
    is_ctx = pl.program_id(0) >= n_lat
    o = of_ref[...].astype(F32) + ob_ref[...].astype(F32)
    r = ra_ref[...].astype(F32)
    parts = []
    for h in range(GLA_HEADS):
        sl = slice(h * GLA_DV, (h + 1) * GLA_DV)
        oh = o[:, sl]
        parts.append(oh * lax.rsqrt(jnp.mean(oh * oh, axis=-1, keepdims=True) + EPS) * ggla_ref[...])
    y_a = (jnp.concatenate(parts, axis=1) * r).astype(BF16)
    y_b = jnp.where(is_ctx, ybc_ref[...], ybl_ref[...])
    y_c = jnp.where(is_ctx, ycc_ref[...], ycl_ref[...])
    gl = gl_ref[...].astype(F32)
    mix = (gl[:, 0:d] * _dot(y_a, wpa_ref[...])
           + gl[:, d:2 * d] * _dot(y_b, wpb_ref[...])
           + gl[:, 2 * d:3 * d] * _dot(y_c, wpc_ref[...]))
    y = _dot(mix.astype(BF16), wout_ref[...])
    x1 = x_ref[...] + mod_ref[:, 2 * d:3 * d] * y
    x1_ref[...] = x1
    h2 = x1 * lax.rsqrt(jnp.mean(x1 * x1, axis=-1, keepdims=True) + EPS) * gffn_ref[...]
    h2 = h2 * (1.0 + mod_ref[:, 4 * d:5 * d]) + mod_ref[:, 3 * d:4 * d]
    h2_hi = h2.astype(BF16)
    h2_ref[...] = h2_hi
    h2_lo = (h2 - h2_hi.astype(F32)).astype(BF16)
    both = _dot(h2_hi, wr_ref[...])
    logits = both[:, :ROUTER_PAD] + both[:, ROUTER_PAD:] + _dot(h2_lo, wr_ref[:, :ROUTER_PAD])
    lt = logits.T
    sub = lax.broadcasted_iota(jnp.int32, lt.shape, 0)
    sub_f = sub.astype(F32)
    neg = -jnp.inf
    big = float(ROUTER_PAD)
    is_group = (sub >= MOE_EXPERTS) & (sub < MOE_EXPERTS + MOE_GROUPS)
    gl_m = jnp.where(is_group, lt, neg)
    g_max = jnp.max(gl_m, axis=0, keepdims=True)
    g_sel = jnp.min(jnp.where(gl_m == g_max, sub_f, big), axis=0, keepdims=True) - MOE_EXPERTS
    g_gate = 1.0 / jnp.sum(jnp.where(is_group, jnp.exp(lt - g_max), 0.0), axis=0, keepdims=True)
    lo = g_sel * MOE_EXPERTS_PER_GROUP
    in_group = (sub_f >= lo) & (sub_f < lo + MOE_EXPERTS_PER_GROUP)
    e1 = jnp.where(in_group, lt, neg)
    v1 = jnp.max(e1, axis=0, keepdims=True)
    i1 = jnp.min(jnp.where(e1 == v1, sub_f, big), axis=0, keepdims=True)
    e2 = jnp.where(sub_f == i1, neg, e1)
    v2 = jnp.max(e2, axis=0, keepdims=True)
    i2 = jnp.min(jnp.where(e2 == v2, sub_f, big), axis=0, keepdims=True)
    t = jnp.exp(v2 - v1)
    w1 = g_gate / (1.0 + t)
    w2 = g_gate * t / (1.0 + t)
    route_t = jnp.where(sub == ROUTE_ID, i1, 0.0) + jnp.where(sub == ROUTE_ID + 1, i2, 0.0)
    route_t = route_t + jnp.where(sub == ROUTE_W, w1, 0.0) + jnp.where(sub == ROUTE_W + 1, w2, 0.0)
    route = route_t.T
    gates_ref[...] = route
    lane_f = lax.broadcasted_iota(jnp.int32, route.shape, 1).astype(F32)
    hit = jnp.where((lane_f == route[:, ROUTE_ID:ROUTE_ID + 1]) | (lane_f == route[:, ROUTE_ID + 1:ROUTE_ID + 2]),
                    1.0, 0.0)
    count_ref[...] = jnp.sum(hit, axis=0, keepdims=True).astype(jnp.int32)


def _merge(lay, x, o_f, o_b, r_a, yb_lat, fourier_tile, yb_ctx, yc_lat, yc_ctx, gate_logits, mods, g_gla, g_ffn,
           w_pa, w_pb, w_pc, w_out, w_router):
    tm, d = lay.tile, D_MODEL
    n = lay.n_tok
    tok = lambda w: pl.BlockSpec((tm, w), lambda t: (t, 0))
    const = lambda a: pl.BlockSpec(a.shape, lambda t: (0,) * a.ndim)
    lt, ct = lay.lat_tiles, lay.ctx_tiles
    sub = fourier_tile // tm

    def yb_lat_index(t):
        tl = jnp.minimum(t, lay.n_lat - 1)
        s = tl % lt
        u = lt - 1 - s
        upper = s >= lt // 2
        blk = jnp.where(upper, (u // sub) * sub + sub - 1 - u % sub, s)
        return (upper.astype(jnp.int32), blk, tl // lt)

    yb_lat_spec = pl.BlockSpec((None, tm, FNET_W), yb_lat_index)
    yb_ctx_spec = pl.BlockSpec((tm, FNET_W), lambda t: (jnp.maximum(t - lay.n_lat, 0) % ct,
                                                         jnp.maximum(t - lay.n_lat, 0) // ct))
    yc_lat_spec = pl.BlockSpec((tm, SWA_Q_W), lambda t: (jnp.minimum(t, lay.n_lat - 1), 0))
    yc_ctx_spec = pl.BlockSpec((tm, SWA_Q_W), lambda t: (jnp.maximum(t - lay.n_lat, 0), 0))
    return pl.pallas_call(
        functools.partial(_merge_kernel, n_lat=lay.n_lat),
        out_shape=(jax.ShapeDtypeStruct((n, d), F32), jax.ShapeDtypeStruct((n, d), BF16),
                   jax.ShapeDtypeStruct((n, ROUTER_PAD), F32),
                   jax.ShapeDtypeStruct((lay.n_tiles, 1, ROUTER_PAD), jnp.int32)),
        grid=(lay.n_tiles,),
        in_specs=[tok(d), tok(GLA_V_W), tok(GLA_V_W), tok(GLA_V_W), yb_lat_spec, yb_ctx_spec,
                  yc_lat_spec, yc_ctx_spec, tok(3 * d),
                  pl.BlockSpec((None, 1, N_MOD * d), lambda t: (lay.mod_row(t), 0, 0)),
                  const(g_gla), const(g_ffn), const(w_pa), const(w_pb), const(w_pc), const(w_out),
                  const(w_router)],
        out_specs=(tok(d), tok(d), tok(ROUTER_PAD),
                   pl.BlockSpec((None, 1, ROUTER_PAD), lambda t: (t, 0, 0))),
        compiler_params=_params(("parallel",)),
        name="merge",
    )(x, o_f, o_b, r_a, yb_lat, yb_ctx, yc_lat, yc_ctx, gate_logits, mods, g_gla, g_ffn,
      w_pa, w_pb, w_pc, w_out, w_router)


def _route_ids(route, axis):
    take = (lambda i: route[:, i:i + 1]) if axis == 1 else (lambda i: route[i:i + 1, :])
    return take(ROUTE_ID).astype(jnp.int32), take(ROUTE_ID + 1).astype(jnp.int32)


def _moe_plan(counts, n_blocks):
    cnt = counts
    pc = (cnt + ROW_CHUNK - 1) // ROW_CHUNK * ROW_CHUNK
    lstart = jnp.cumsum(pc, axis=1) - pc
    tot = jnp.sum(pc, axis=0)
    tot_pad = (tot + FFN_BLOCK - 1) // FFN_BLOCK * FFN_BLOCK
    eend = jnp.cumsum(tot_pad)
    estart = eend - tot_pad
    base = estart[None, :] + jnp.cumsum(pc, axis=0) - pc
    n_used = eend[-1] // FFN_BLOCK
    blk = jnp.minimum(jnp.arange(n_blocks, dtype=jnp.int32), n_used - 1)
    bexp = jnp.sum((blk[:, None] * FFN_BLOCK >= eend[None, :]).astype(jnp.int32), axis=1)
    flat = lambda a: a.reshape(-1).astype(jnp.int32)
    return dict(base=flat(base), lstart=flat(lstart), nch=flat(pc // ROW_CHUNK),
                gap0=flat(estart + tot), gapn=flat((tot_pad - tot) // ROW_CHUNK),
                bexp=flat(bexp), nused=flat(n_used))


def _segment_copies(t, base_ref, lstart_ref, nch_ref, make, start):
    per_big = BIG_CHUNK // ROW_CHUNK

    def per_expert(e, carry):
        idx = t * MOE_EXPERTS + e
        loc = lstart_ref[idx]
        glob = base_ref[idx]
        n_big = nch_ref[idx] // per_big
        n_small = nch_ref[idx] - n_big * per_big

        def piece(rows, first):
            def body(i, c):
                off = first + i * rows
                cp = make(pl.multiple_of(loc + off, ROW_CHUNK), pl.multiple_of(glob + off, ROW_CHUNK), rows)
                cp.start() if start else cp.wait()
                return c
            return body

        carry = lax.fori_loop(0, n_big, piece(BIG_CHUNK, 0), carry)
        return lax.fori_loop(0, n_small, piece(ROW_CHUNK, n_big * BIG_CHUNK), carry)

    lax.fori_loop(0, MOE_EXPERTS, per_expert, 0)


def _dispatch_kernel(base_ref, lstart_ref, nch_ref, gap0_ref, gapn_ref, nused_ref, h_ref, route_ref, xs_ref,
                     buf_ref, sem):
    t = pl.program_id(0)
    tile = h_ref.shape[0]
    slots = buf_ref.shape[1]
    rt = route_ref[...].T
    e1, e2 = _route_ids(rt, 0)
    sub = lax.broadcasted_iota(jnp.int32, rt.shape, 0)
    oh1, oh2 = sub == e1, sub == e2
    hit = jnp.where(oh1 | oh2, 1.0, 0.0).astype(BF16)
    before = (lax.broadcasted_iota(jnp.int32, (tile, tile), 0)
              < lax.broadcasted_iota(jnp.int32, (tile, tile), 1))
    rank = _dot(hit, jnp.where(before, 1.0, 0.0).astype(BF16))
    sub1 = lax.broadcasted_iota(jnp.int32, (rt.shape[0], 1), 0)
    seg = jnp.zeros((rt.shape[0], 1), F32)
    for e in range(MOE_EXPERTS):
        seg = jnp.where(sub1 == e, lstart_ref[t * MOE_EXPERTS + e].astype(F32), seg)
    slot_of = rank + seg
    pos1 = jnp.sum(jnp.where(oh1, slot_of, 0.0), axis=0, keepdims=True).astype(jnp.int32)
    pos2 = jnp.sum(jnp.where(oh2, slot_of, 0.0), axis=0, keepdims=True).astype(jnp.int32)
    slot = lax.broadcasted_iota(jnp.int32, (slots, tile), 0)
    perm = jnp.where((slot == pos1) | (slot == pos2), 1.0, 0.0).astype(BF16)
    cur = t % 2
    buf_ref[cur] = _dot(perm, h_ref[...]).astype(BF16)

    def maker(b):
        def make(loc, glob, rows):
            return pltpu.make_async_copy(buf_ref.at[b, pl.ds(loc, rows)], xs_ref.at[pl.ds(glob, rows)], sem.at[b])
        return make

    _segment_copies(t, base_ref, lstart_ref, nch_ref, maker(cur), True)

    @pl.when(t > 0)
    def _():
        _segment_copies(t - 1, base_ref, lstart_ref, nch_ref, maker(1 - cur), False)

    @pl.when(t == pl.num_programs(0) - 1)
    def _():
        _segment_copies(t, base_ref, lstart_ref, nch_ref, maker(cur), False)
        buf_ref[cur, 0:FFN_BLOCK, :] = jnp.zeros((FFN_BLOCK, buf_ref.shape[2]), BF16)
        n_blocks = xs_ref.shape[0] // FFN_BLOCK

        def fill(start):
            def per_expert(e, carry):
                def chunk(i, c):
                    row = pl.multiple_of(gap0_ref[e] + i * ROW_CHUNK, ROW_CHUNK)
                    cp = pltpu.make_async_copy(buf_ref.at[cur, 0:ROW_CHUNK], xs_ref.at[pl.ds(row, ROW_CHUNK)],
                                               sem.at[cur])
                    cp.start() if start else cp.wait()
                    return c
                return lax.fori_loop(0, gapn_ref[e], chunk, carry)

            def tail(b, c):
                row = pl.multiple_of(b * FFN_BLOCK, FFN_BLOCK)
                cp = pltpu.make_async_copy(buf_ref.at[cur, 0:FFN_BLOCK], xs_ref.at[pl.ds(row, FFN_BLOCK)],
                                           sem.at[cur])
                cp.start() if start else cp.wait()
                return c

            lax.fori_loop(0, MOE_EXPERTS, per_expert, 0)
            lax.fori_loop(nused_ref[0], n_blocks, tail, 0)

        fill(True)
        fill(False)


def _dispatch(plan, h2, route, n_rows):
    tile, d = DISPATCH_TILE, D_MODEL
    n_tiles = h2.shape[0] // tile
    return pl.pallas_call(
        _dispatch_kernel,
        out_shape=jax.ShapeDtypeStruct((n_rows, d), BF16),
        grid_spec=pltpu.PrefetchScalarGridSpec(
            num_scalar_prefetch=6,
            grid=(n_tiles,),
            in_specs=[pl.BlockSpec((tile, d), lambda t, *_: (t, 0)),
                      pl.BlockSpec((tile, ROUTER_PAD), lambda t, *_: (t, 0))],
            out_specs=pl.BlockSpec(memory_space=pl.ANY),
            scratch_shapes=[pltpu.VMEM((2, SLOT_ROWS, d), BF16), pltpu.SemaphoreType.DMA((2,))],
        ),
        compiler_params=_params(("arbitrary",)),
        name="moe_dispatch",
    )(plan["base"], plan["lstart"], plan["nch"], plan["gap0"], plan["gapn"], plan["nused"], h2, route)


def _ffn_kernel(bexp_ref, nused_ref, x_ref, w1_ref, w3_ref, w2_ref, y_ref, w13_s, w2_s):
    b = pl.program_id(0)
    used = b < nused_ref[0]
    de = D_EXPERT

    @pl.when(used & ((b == 0) | (bexp_ref[b] != bexp_ref[jnp.maximum(b - 1, 0)])))
    def _():
        w13_s[:, :de] = w1_ref[...].astype(BF16)
        w13_s[:, de:] = w3_ref[...].astype(BF16)
        w2_s[...] = w2_ref[...].astype(BF16)

    @pl.when(used)
    def _():
        ab = _dot(x_ref[...], w13_s[...])
        a = ab[:, :de]
        hid = (a * _sigmoid(a)) * ab[:, de:]
        y_ref[...] = _dot(hid.astype(BF16), w2_s[...]).astype(BF16)

    @pl.when(jnp.logical_not(used))
    def _():
        y_ref[...] = jnp.zeros_like(y_ref)


def _expert_ffn(plan, xs, layer, w1, w3, w2):
    d = D_MODEL
    n_blocks = xs.shape[0] // FFN_BLOCK
    row = lambda b, bexp, nused: (jnp.minimum(b, nused[0] - 1), 0)
    wsel = lambda b, bexp, nused: (layer, bexp[b], 0, 0)
    return pl.pallas_call(
        _ffn_kernel,
        out_shape=jax.ShapeDtypeStruct(xs.shape, BF16),
        grid_spec=pltpu.PrefetchScalarGridSpec(
            num_scalar_prefetch=2,
            grid=(n_blocks,),
            in_specs=[pl.BlockSpec((FFN_BLOCK, d), row),
                      pl.BlockSpec((None, None, d, D_EXPERT), wsel),
                      pl.BlockSpec((None, None, d, D_EXPERT), wsel),
                      pl.BlockSpec((None, None, D_EXPERT, d), wsel)],
            out_specs=pl.BlockSpec((FFN_BLOCK, d), lambda b, bexp, nused: (b, 0)),
            scratch_shapes=[pltpu.VMEM((d, 2 * D_EXPERT), BF16), pltpu.VMEM((D_EXPERT, d), BF16)],
        ),
        compiler_params=_params(("arbitrary",)),
        name="moe_ffn",
    )(plan["bexp"], plan["nused"], xs, w1, w3, w2)


def _combine_kernel(base_ref, lstart_ref, nch_ref, route_ref, x1_ref, mod_ref, ys_ref, *rest, n_out_tiles):
    g_ref = rest[0] if len(rest) == 4 else None
    o_ref, buf_ref, sem = rest[-3:]
    d = D_MODEL
    t = pl.program_id(0)
    tile = route_ref.shape[0]
    slots = buf_ref.shape[1]
    cur = t % 2

    def maker(b):
        def make(loc, glob, rows):
            return pltpu.make_async_copy(ys_ref.at[pl.ds(glob, rows)], buf_ref.at[b, pl.ds(loc, rows)], sem.at[b])
        return make

    @pl.when(t == 0)
    def _():
        buf_ref[...] = jnp.zeros_like(buf_ref)
        _segment_copies(t, base_ref, lstart_ref, nch_ref, maker(cur), True)

    @pl.when(t + 1 < pl.num_programs(0))
    def _():
        _segment_copies(t + 1, base_ref, lstart_ref, nch_ref, maker(1 - cur), True)

    route = route_ref[...]
    e1, e2 = _route_ids(route, 1)
    lane = lax.broadcasted_iota(jnp.int32, route.shape, 1)
    oh1, oh2 = lane == e1, lane == e2
    hit = jnp.where(oh1 | oh2, 1.0, 0.0).astype(BF16)
    before = (lax.broadcasted_iota(jnp.int32, (tile, tile), 1)
              < lax.broadcasted_iota(jnp.int32, (tile, tile), 0))
    rank = _dot(jnp.where(before, 1.0, 0.0).astype(BF16), hit)
    lane1 = lax.broadcasted_iota(jnp.int32, (1, route.shape[1]), 1)
    seg = jnp.zeros((1, route.shape[1]), F32)
    for e in range(MOE_EXPERTS):
        seg = jnp.where(lane1 == e, lstart_ref[t * MOE_EXPERTS + e].astype(F32), seg)
    slot_of = rank + seg
    pos1 = jnp.sum(jnp.where(oh1, slot_of, 0.0), axis=1, keepdims=True).astype(jnp.int32)
    pos2 = jnp.sum(jnp.where(oh2, slot_of, 0.0), axis=1, keepdims=True).astype(jnp.int32)
    slot = lax.broadcasted_iota(jnp.int32, (tile, slots), 1)
    w1 = route[:, ROUTE_W:ROUTE_W + 1]
    w2 = route[:, ROUTE_W + 1:ROUTE_W + 2]
    comb = (jnp.where(slot == pos1, w1, 0.0) + jnp.where(slot == pos2, w2, 0.0)).astype(BF16)
    _segment_copies(t, base_ref, lstart_ref, nch_ref, maker(cur), False)
    moe = _dot(comb, buf_ref[cur])
    x2 = x1_ref[...] + mod_ref[:, 5 * d:6 * d] * moe
    if g_ref is None:
        o_ref[...] = x2
    else:
        @pl.when(t < n_out_tiles)
        def _():
            o_ref[...] = x2 * lax.rsqrt(jnp.mean(x2 * x2, axis=-1, keepdims=True) + EPS) * g_ref[...]


def _combine(lay, plan, route, x1, mods, ys, g_final):
    tile, d = DISPATCH_TILE, D_MODEL
    assert lay.seq % tile == 0 and (lay.batch * lay.ctx) % tile == 0
    lat_tiles = lay.seq // tile
    n_lat = lay.batch * lat_tiles
    n_tiles = lay.n_tok // tile
    row = lambda t: jnp.where(t < n_lat, t // lat_tiles, lay.batch)
    final = g_final is not None
    n_out_tiles = n_lat if final else n_tiles
    extra_specs = [pl.BlockSpec((1, d), lambda t, *_: (0, 0))] if final else []
    extra_args = [g_final] if final else []
    return pl.pallas_call(
        functools.partial(_combine_kernel, n_out_tiles=n_out_tiles),
        out_shape=jax.ShapeDtypeStruct((n_out_tiles * tile, d), F32),
        grid_spec=pltpu.PrefetchScalarGridSpec(
            num_scalar_prefetch=3,
            grid=(n_tiles,),
            in_specs=[pl.BlockSpec((tile, ROUTER_PAD), lambda t, *_: (t, 0)),
                      pl.BlockSpec((tile, d), lambda t, *_: (t, 0)),
                      pl.BlockSpec((None, 1, N_MOD * d), lambda t, *_: (row(t), 0, 0)),
                      pl.BlockSpec(memory_space=pl.ANY)] + extra_specs,
            out_specs=pl.BlockSpec((tile, d), lambda t, *_: (jnp.minimum(t, n_out_tiles - 1), 0)),
            scratch_shapes=[pltpu.VMEM((2, SLOT_ROWS, d), BF16), pltpu.SemaphoreType.DMA((2,))],
        ),
        compiler_params=_params(("arbitrary",)),
        name="moe_combine",
    )(plan["base"], plan["lstart"], plan["nch"], route, x1, mods, ys, *extra_args)


def _moe(lay, h2, x1, route, tile_counts, mods, layer, w1, w3, w2, g_final=None):
    n_tiles = lay.n_tok // DISPATCH_TILE
    max_rows = (MOE_TOPK * lay.n_tok + n_tiles * MOE_EXPERTS * (ROW_CHUNK - 1)
                + MOE_EXPERTS * (FFN_BLOCK - 1))
    n_blocks = -(-max_rows // FFN_BLOCK)
    counts = jnp.sum(tile_counts[:, 0, :MOE_EXPERTS].reshape(n_tiles, DISPATCH_TILE // lay.tile, MOE_EXPERTS), axis=1)
    plan = _moe_plan(counts, n_blocks)
    xs = _dispatch(plan, h2, route, n_blocks * FFN_BLOCK)
    ys = _expert_ffn(plan, xs, layer, w1, w3, w2)
    return _combine(lay, plan, route, x1, mods, ys, g_final)


def _rope_tables(seq, ctx):
    pos = jnp.arange(seq, dtype=jnp.int32)
    inv_freq = ROPE_BASE ** (-jnp.arange(0, AXIS_DIM, 2, dtype=F32) / AXIS_DIM)
    ang_row = (pos // GRID_W).astype(F32)[:, None] * inv_freq
    ang_col = (pos % GRID_W).astype(F32)[:, None] * inv_freq
    cos_h = jnp.concatenate([jnp.cos(ang_row)] * 2 + [jnp.cos(ang_col)] * 2, axis=1)
    sin_h = jnp.concatenate([-jnp.sin(ang_row), jnp.sin(ang_row), -jnp.sin(ang_col), jnp.sin(ang_col)], axis=1)
    reps = LANES // SWA_HEAD_DIM
    cos_t = jnp.concatenate([jnp.tile(cos_h, (1, reps)), jnp.ones((ctx, LANES), F32)], axis=0)
    sin_t = jnp.concatenate([jnp.tile(sin_h, (1, reps)), jnp.zeros((ctx, LANES), F32)], axis=0)
    return cos_t, sin_t


def _channel_dft():
    i = jnp.arange(FNET_GROUP_W, dtype=jnp.int32)
    ang = ((i[:, None] * i[None, :]) % FNET_GROUP_W).astype(F32) * (2.0 * math.pi / FNET_GROUP_W)
    return jnp.concatenate([jnp.cos(ang), jnp.sin(ang)], axis=1).astype(BF16)


def kernel(x, c, ctx, c_ctx, w_ada, b_ada, g_mix, g_ffn, w_in, w_decay_down, w_decay_up, b_decay, g_gla, sink,
           w_pa, w_pb, w_pc, w_out, w_router_group, w_router_expert, w1, w3, w2, g_final):
    batch, seq, d = x.shape
    n_ctx = ctx.shape[1]
    depth = w_ada.shape[0]
    assert d == D_MODEL and seq % GRID_W == 0 and seq % n_ctx == 0
    lay = _Layout(batch, seq, n_ctx, TOKEN_TILE)

    rows = -(-(batch + 1) // 8) * 8
    c_all = jnp.zeros((rows, d), F32).at[:batch].set(c).at[batch].set(c_ctx)
    mods_all = _adaln(c_all, w_ada, b_ada).reshape(depth, rows, 1, N_MOD * d)

    cos_t, sin_t = _rope_tables(seq, n_ctx)
    cs = _channel_dft()
    fourier_tile = min(seq // 2, FOURIER_TILE)
    assert fourier_tile % TOKEN_TILE == 0 and seq % (2 * fourier_tile) == 0
    lat_mats = _dft_half_mats(seq, fourier_tile)
    c_ctx_m, s_ctx_m = _dft_mats(n_ctx)

    xs = jnp.concatenate([x.reshape(batch * seq, d), ctx.reshape(batch * n_ctx, d)], axis=0)
    rank = w_decay_down.shape[-1]
    w_in_bf = w_in.astype(BF16)
    for l in range(depth):
        last = l == depth - 1
        mods = mods_all[l]
        down = jnp.concatenate([w_decay_down[l, 0], w_decay_down[l, 1]], axis=1)
        down = jnp.pad(down, ((0, 0), (0, DECAY_PAD - 2 * rank))).astype(BF16)
        w_up = jnp.zeros((DECAY_PAD, 2 * GLA_QK_W), F32)
        w_up = w_up.at[:rank, :GLA_QK_W].set(w_decay_up[l, 0]).at[rank:2 * rank, GLA_QK_W:].set(w_decay_up[l, 1])
        b_dec = b_decay[l].reshape(1, 2 * GLA_QK_W)
        (k_a, v_a, k_c, v_c, q_a, r_a, u_cos, u_sin, q_c, gate_logits, la_f, la_b) = _inproj(
            lay, xs, g_mix[l].reshape(1, d), mods, l, w_in_bf, down, w_up.astype(BF16), b_dec, cs, cos_t, sin_t)

        o_f, o_b = _gla(lay, q_a, k_a, v_a, la_f, la_b)
        yb_lat = _fourier_half(u_cos, u_sin, lat_mats, seq, fourier_tile)
        yc_lat = _swa(lay, q_c, k_c, v_c, jnp.broadcast_to(sink[l][:, None], (SWA_HEADS, LANES)))
        if last:
            yb_ctx = jnp.zeros((n_ctx, batch * FNET_W), BF16)
            yc_ctx = jnp.zeros((batch * n_ctx, SWA_Q_W), BF16)
        else:
            yb_ctx = _fourier(u_cos, u_sin, c_ctx_m, s_ctx_m, seq, n_ctx, n_ctx, n_ctx)
            yc_ctx = _ctx_attn(lay, q_c, k_c, v_c, jnp.broadcast_to(sink[l][:, None], (SWA_HEADS, LANES)))

        w_router = jnp.zeros((d, ROUTER_PAD), F32)
        w_router = w_router.at[:, :MOE_EXPERTS].set(w_router_expert[l])
        w_router = w_router.at[:, MOE_EXPERTS:MOE_EXPERTS + MOE_GROUPS].set(w_router_group[l])
        w_router_hi = w_router.astype(BF16)
        w_router = jnp.concatenate([w_router_hi, (w_router - w_router_hi.astype(F32)).astype(BF16)], axis=1)
        x1, h2, gates, tile_counts = _merge(lay, xs, o_f, o_b, r_a, yb_lat, fourier_tile, yb_ctx, yc_lat, yc_ctx,
                               gate_logits, mods,
                               g_gla[l].reshape(1, GLA_DV), g_ffn[l].reshape(1, d),
                               w_pa[l].astype(BF16), w_pb[l].astype(BF16), w_pc[l].astype(BF16),
                               w_out[l].astype(BF16), w_router)
        xs = _moe(lay, h2, x1, gates, tile_counts, mods, l, w1, w3, w2,
                  g_final.reshape(1, d) if last else None)

    return xs.reshape(batch, seq, d)
```

```python
import functools
import math

import jax
import jax.numpy as jnp
from jax import lax
from jax.experimental import pallas as pl
from jax.experimental.pallas import tpu as pltpu

F32 = jnp.float32
BF16 = jnp.bfloat16
HIGHEST = lax.Precision.HIGHEST

D_MODEL = 1024
GRID_W = 64
EPS = 1e-6
N_MOD = 6
GLA_HEADS = 4
GLA_DK = 64
GLA_DV = 128
GLA_TAU = 16.0
GLA_CHUNK = 64
GLA_SCALE = GLA_DK ** -0.5
FNET_GROUPS = 4
FNET_GROUP_W = 128
SWA_HEADS = 8
SWA_KV_HEADS = 2
SWA_HEAD_DIM = 64
WINDOW = 128
ROPE_BASE = 10000.0
AXIS_DIM = SWA_HEAD_DIM // 2
MOE_GROUPS = 4
MOE_EXPERTS_PER_GROUP = 4
MOE_EXPERTS = MOE_GROUPS * MOE_EXPERTS_PER_GROUP
MOE_TOPK = 2
D_EXPERT = 512

GLA_QK_W = GLA_HEADS * GLA_DK
GLA_V_W = GLA_HEADS * GLA_DV
FNET_W = FNET_GROUPS * FNET_GROUP_W
SWA_Q_W = SWA_HEADS * SWA_HEAD_DIM
SWA_KV_W = SWA_KV_HEADS * SWA_HEAD_DIM
IN_SIZES = (GLA_QK_W, GLA_V_W, SWA_KV_W, SWA_KV_W, GLA_QK_W, GLA_V_W, FNET_W, SWA_Q_W, 3 * D_MODEL)
IN_OFFS = tuple(int(sum(IN_SIZES[:i])) for i in range(len(IN_SIZES) + 1))
IN_COLS = IN_OFFS[-1]

LANES = 128
TOKEN_TILE = 256
DECAY_PAD = LANES
ROUTER_PAD = LANES
ROUTE_ID = 0
ROUTE_W = 2
DISPATCH_TILE = 512
ROW_CHUNK = 16
BIG_CHUNK = 64
FOURIER_TILE = 512
FOURIER_COL_SPLIT = 2
FFN_BLOCK = 512
SLOT_ROWS = -(-(MOE_TOPK * DISPATCH_TILE + MOE_EXPERTS * (ROW_CHUNK - 1)) // LANES) * LANES
VMEM_LIMIT = 56 * 1024 * 1024


def _params(sem, vmem=VMEM_LIMIT):
    return pltpu.CompilerParams(dimension_semantics=sem, vmem_limit_bytes=vmem)


def _sigmoid(x):
    return 0.5 * jnp.tanh(0.5 * x) + 0.5


def _dot(a, b):
    return jnp.dot(a, b, preferred_element_type=F32)


def _dot_nt(a, b):
    return lax.dot_general(a, b, (((1,), (1,)), ((), ())), preferred_element_type=F32)


def _ada_kernel(c_ref, w_ref, b_ref, o_ref):
    c = c_ref[...]
    a = c * _sigmoid(c)
    o_ref[...] = jnp.dot(a, w_ref[...], preferred_element_type=F32, precision=HIGHEST) + b_ref[...]


def _adaln(c_all, w_ada, b_ada):
    depth, d, n = w_ada.shape
    rows = c_all.shape[0]
    tn = 1536
    return pl.pallas_call(
        _ada_kernel,
        out_shape=jax.ShapeDtypeStruct((depth, rows, n), F32),
        grid=(depth, n // tn),
        in_specs=[pl.BlockSpec((rows, d), lambda l, j: (0, 0)),
                  pl.BlockSpec((None, d, tn), lambda l, j: (l, 0, j)),
                  pl.BlockSpec((None, 1, tn), lambda l, j: (l, 0, j))],
        out_specs=pl.BlockSpec((None, rows, tn), lambda l, j: (l, 0, j)),
        compiler_params=_params(("parallel", "parallel")),
        name="adaln",
    )(c_all, w_ada, b_ada.reshape(depth, 1, n))


class _Layout:
    def __init__(self, batch, seq, ctx, tile):
        assert seq % tile == 0 and ctx % tile == 0
        self.batch, self.seq, self.ctx, self.tile = batch, seq, ctx, tile
        self.lat_tiles = seq // tile
        self.ctx_tiles = ctx // tile
        self.n_lat = batch * self.lat_tiles
        self.n_tiles = self.n_lat + batch * self.ctx_tiles
        self.n_tok = self.n_tiles * tile

    def batch_of(self, t):
        return jnp.where(t < self.n_lat, t // self.lat_tiles, (t - self.n_lat) // self.ctx_tiles)

    def mod_row(self, t):
        return jnp.where(t < self.n_lat, t // self.lat_tiles, self.batch)

    def seq_tile(self, t):
        return jnp.where(t < self.n_lat, t % self.lat_tiles,
                         self.lat_tiles + (t - self.n_lat) % self.ctx_tiles)


def _rope(x, cos, sin_signed):
    n = x.shape[-1]
    lane = lax.broadcasted_iota(jnp.int32, x.shape, 1)
    half = AXIS_DIM // 2
    partner = jnp.where((lane & half) == 0, pltpu.roll(x, n - half, 1), pltpu.roll(x, half, 1))
    return x * cos + partner * sin_signed


def _inproj_kernel(x_ref, g_ref, mod_ref, w_ref, wd_ref, wu_ref, bdec_ref, cs_ref, cos_ref, sin_ref,
                   ka_ref, va_ref, kc_ref, vc_ref, qa_ref, ra_ref, ua_ref, us_ref, qc_ref, gl_ref,
                   laf_ref, lab_ref):
    d = D_MODEL
    x = x_ref[...]
    shift = mod_ref[:, 0:d]
    scale = mod_ref[:, d:2 * d]
    h = x * lax.rsqrt(jnp.mean(x * x, axis=-1, keepdims=True) + EPS) * g_ref[...]
    hb = (h * (1.0 + scale) + shift).astype(BF16)

    def proj(i):
        return _dot(hb, w_ref[:, IN_OFFS[i]:IN_OFFS[i + 1]])

    ka_ref[...] = proj(0).astype(BF16)
    va_ref[...] = proj(1).astype(BF16)
    cos = cos_ref[...]
    sin = sin_ref[...]
    kc_ref[...] = _rope(proj(2), cos, sin).astype(BF16)
    vc_ref[...] = proj(3).astype(BF16)
    qa_ref[...] = proj(4).astype(BF16)
    r = proj(5)
    ra_ref[...] = (r * _sigmoid(r)).astype(BF16)
    u = proj(6).astype(BF16)
    for g in range(FNET_GROUPS):
        sl = slice(g * FNET_GROUP_W, (g + 1) * FNET_GROUP_W)
        ab = _dot(u[:, sl], cs_ref[...])
        ua_ref[:, sl] = ab[:, :FNET_GROUP_W].astype(BF16)
        us_ref[:, sl] = ab[:, FNET_GROUP_W:].astype(BF16)
    reps = SWA_Q_W // LANES
    qc_ref[...] = (_rope(proj(7), jnp.concatenate([cos] * reps, axis=1), jnp.concatenate([sin] * reps, axis=1))
                   * SWA_HEAD_DIM ** -0.5).astype(BF16)
    gl_ref[...] = _sigmoid(proj(8)).astype(BF16)
    low = _dot(hb, wd_ref[...]).astype(BF16)
    z = _dot(low, wu_ref[...]) + bdec_ref[...]
    la = (jnp.minimum(z, 0.0) - jnp.log(1.0 + jnp.exp(-jnp.abs(z)))) * (1.0 / GLA_TAU)
    laf_ref[...] = la[:, :GLA_QK_W]
    lab_ref[...] = la[:, GLA_QK_W:]


def _inproj(lay, x, g_mix, mods, layer, w_in, w_down, w_up, b_dec, cs, cos_t, sin_t):
    tm, d = lay.tile, D_MODEL
    n = lay.n_tok
    bf = lambda w: jax.ShapeDtypeStruct((n, w), BF16)
    tok = lambda w: pl.BlockSpec((tm, w), lambda t: (t, 0))
    const = lambda a: pl.BlockSpec(a.shape, lambda t: (0,) * a.ndim)
    seq_rows = lay.seq + lay.ctx
    fnet_shape = jax.ShapeDtypeStruct((seq_rows, lay.batch * FNET_W), BF16)
    fnet_spec = pl.BlockSpec((tm, FNET_W), lambda t: (lay.seq_tile(t), lay.batch_of(t)))
    pos_spec = pl.BlockSpec((tm, LANES), lambda t: (lay.seq_tile(t), 0))
    return pl.pallas_call(
        _inproj_kernel,
        out_shape=(bf(GLA_QK_W), bf(GLA_V_W), bf(SWA_KV_W), bf(SWA_KV_W), bf(GLA_QK_W), bf(GLA_V_W),
                   fnet_shape, fnet_shape, bf(SWA_Q_W), bf(3 * d),
                   jax.ShapeDtypeStruct((n, GLA_QK_W), F32), jax.ShapeDtypeStruct((n, GLA_QK_W), F32)),
        grid=(lay.n_tiles,),
        in_specs=[tok(d), const(g_mix),
                  pl.BlockSpec((None, 1, N_MOD * d), lambda t: (lay.mod_row(t), 0, 0)),
                  pl.BlockSpec((None,) + w_in.shape[1:], lambda t: (layer, 0, 0)),
                  const(w_down), const(w_up), const(b_dec), const(cs), pos_spec, pos_spec],
        out_specs=(tok(GLA_QK_W), tok(GLA_V_W), tok(SWA_KV_W), tok(SWA_KV_W), tok(GLA_QK_W), tok(GLA_V_W),
                   fnet_spec, fnet_spec, tok(SWA_Q_W), tok(3 * d), tok(GLA_QK_W), tok(GLA_QK_W)),
        compiler_params=_params(("parallel",)),
        name="inproj",
    )(x, g_mix, mods, w_in, w_down, w_up, b_dec, cs, cos_t, sin_t)


def _split2(x):
    hi = x.astype(BF16)
    return hi, (x - hi.astype(F32)).astype(BF16)


def _gla_direction(q_ref, k_ref, v_ref, la_ref, o_ref, s_ref, rev):
    t_rows = q_ref.shape[0]
    c = GLA_CHUNK
    n_sub = t_rows // c
    hk = GLA_QK_W
    la_hi, la_lo = _split2(la_ref[...])
    row = lax.broadcasted_iota(jnp.int32, (t_rows, t_rows), 0)
    col = lax.broadcasted_iota(jnp.int32, (t_rows, t_rows), 1)
    tri = ((row // c) == (col // c)) & ((col >= row) if rev else (col <= row))
    tri = jnp.where(tri, 1.0, 0.0).astype(BF16)
    lc = _dot(tri, la_hi) + _dot(tri, la_lo)
    q = q_ref[...].astype(F32)
    k = k_ref[...].astype(F32)
    qd = (q * jnp.exp(lc) * GLA_SCALE).astype(BF16)
    kd = (k * jnp.exp(-lc)).astype(BF16)
    hrow = lax.broadcasted_iota(jnp.int32, (GLA_HEADS * c, hk), 0) // c
    hcol = lax.broadcasted_iota(jnp.int32, (GLA_HEADS * c, hk), 1) // GLA_DK
    head_mask = hrow == hcol
    trow = lax.broadcasted_iota(jnp.int32, (GLA_HEADS * c, c), 0) % c
    scol = lax.broadcasted_iota(jnp.int32, (GLA_HEADS * c, c), 1)
    causal = (scol >= trow) if rev else (scol <= trow)
    edge = 0 if rev else c - 1
    lasts = [lc[i * c + edge:i * c + edge + 1] for i in range(n_sub)]
    to_end = jnp.concatenate([jnp.broadcast_to(l, (c, hk)) for l in lasts], axis=0) - lc
    kh_t = (k * jnp.exp(to_end)).T
    pad = jnp.zeros((LANES - n_sub, hk), F32)
    decay_t = jnp.exp(jnp.concatenate(lasts + [pad], axis=0).T)
    yield
    chunk_of = lax.broadcasted_iota(jnp.int32, (1, t_rows), 1) // c
    kv = []
    for h in range(GLA_HEADS):
        kh_h = kh_t[h * GLA_DK:(h + 1) * GLA_DK]
        stack = jnp.concatenate([jnp.where(chunk_of == i, kh_h, 0.0) for i in range(n_sub)], axis=0)
        kv.append(_dot(stack.astype(BF16), v_ref[:, h * GLA_DV:(h + 1) * GLA_DV]))
    yield
    order = range(n_sub - 1, -1, -1) if rev else range(n_sub)
    state = s_ref[...]
    state_at = {}
    for i in order:
        state_at[i] = state
        inc = jnp.concatenate([kv[h][i * GLA_DK:(i + 1) * GLA_DK] for h in range(GLA_HEADS)], axis=0)
        state = decay_t[:, i:i + 1] * state + inc
    s_ref[...] = state
    yield
    inters, scoress = {}, {}
    for i in order:
        rows = slice(i * c, (i + 1) * c)
        q_stack = jnp.where(head_mask, jnp.concatenate([qd[rows]] * GLA_HEADS, axis=0), 0.0).astype(BF16)
        inters[i] = _dot(q_stack, state_at[i].astype(BF16))
        scoress[i] = jnp.where(causal, _dot_nt(q_stack, kd[rows]), 0.0).astype(BF16)
    for i in order:
        yield
        rows = slice(i * c, (i + 1) * c)
        v_i = v_ref[rows, :]
        inter, scores = inters[i], scoress[i]
        outs = []
        for h in range(GLA_HEADS):
            hr = slice(h * c, (h + 1) * c)
            hv = slice(h * GLA_DV, (h + 1) * GLA_DV)
            outs.append(inter[hr] + _dot(scores[hr], v_i[:, hv]))
        o_ref[rows, :] = jnp.concatenate(outs, axis=1).astype(BF16)


def _gla_kernel(qf_ref, kf_ref, vf_ref, laf_ref, qb_ref, kb_ref, vb_ref, lab_ref,
                of_ref, ob_ref, sf_ref, sb_ref):
    @pl.when(pl.program_id(1) == 0)
    def _():
        sf_ref[...] = jnp.zeros_like(sf_ref)
        sb_ref[...] = jnp.zeros_like(sb_ref)

    live = [_gla_direction(qf_ref, kf_ref, vf_ref, laf_ref, of_ref, sf_ref, False),
            _gla_direction(qb_ref, kb_ref, vb_ref, lab_ref, ob_ref, sb_ref, True)]
    while live:
        live = [g for g in live if next(g, "done") != "done"]


def _gla(lay, q_a, k_a, v_a, la_f, la_b):
    tm = lay.tile
    nc, nl = lay.ctx_tiles, lay.lat_tiles

    def fwd(b, j):
        return jnp.where(j < nc, lay.n_lat + b * nc + j, b * nl + (j - nc))

    def bwd(b, j):
        return jnp.where(j < nc, lay.n_lat + b * nc + (nc - 1 - j), b * nl + (nl - 1 - (j - nc)))

    spec = lambda w, f: pl.BlockSpec((tm, w), lambda b, j: (f(b, j), 0))
    out = jax.ShapeDtypeStruct((lay.n_tok, GLA_V_W), BF16)
    return pl.pallas_call(
        _gla_kernel,
        out_shape=(out, out),
        grid=(lay.batch, nc + nl),
        in_specs=[spec(GLA_QK_W, fwd), spec(GLA_QK_W, fwd), spec(GLA_V_W, fwd), spec(GLA_QK_W, fwd),
                  spec(GLA_QK_W, bwd), spec(GLA_QK_W, bwd), spec(GLA_V_W, bwd), spec(GLA_QK_W, bwd)],
        out_specs=(spec(GLA_V_W, fwd), spec(GLA_V_W, bwd)),
        scratch_shapes=[pltpu.VMEM((GLA_QK_W, GLA_DV), F32), pltpu.VMEM((GLA_QK_W, GLA_DV), F32)],
        compiler_params=_params(("parallel", "arbitrary")),
        name="gla",
    )(q_a, k_a, v_a, la_f, q_a, k_a, v_a, la_b)


def _dft_kernel(c_ref, s_ref, a_ref, b_ref, o_ref, acc_ref, *, scale):
    k = pl.program_id(1)

    @pl.when(k == 0)
    def _():
        acc_ref[...] = jnp.zeros_like(acc_ref)

    acc_ref[...] += _dot(c_ref[...], a_ref[...]) + _dot(s_ref[...], b_ref[...])

    @pl.when(k == pl.num_programs(1) - 1)
    def _():
        o_ref[...] = (acc_ref[...] * scale).astype(BF16)


def _dft_mats(n):
    f = GRID_W
    assert n % f == 0
    k = jnp.arange(n, dtype=jnp.int32)[None, :]

    def table(rows, period):
        ang = ((jnp.arange(rows, dtype=jnp.int32)[:, None] * k) % period).astype(F32) * (2.0 * math.pi / period)
        return jnp.cos(ang), jnp.sin(ang)

    ca, sa = table(n // f, n // f)
    cb, sb = table(f, n)
    cos = ca[:, None, :] * cb[None, :, :] - sa[:, None, :] * sb[None, :, :]
    sin = sa[:, None, :] * cb[None, :, :] + ca[:, None, :] * sb[None, :, :]
    return cos.reshape(n, n).astype(BF16), (-sin).reshape(n, n).astype(BF16)


def _dft_half_kernel(c_ref, s_ref, cx_ref, sx_ref, a_ref, b_ref, j_ref, o_ref, af_ref, bf_ref, *, scale):
    it = pl.program_id(1)
    tm = o_ref.shape[1]
    length = a_ref.shape[0]
    half = length // 2
    sub = 16
    jm = j_ref[...]
    row = lax.broadcasted_iota(jnp.int32, (tm, 1), 0)

    @pl.when(it == 0)
    def _():
        for m in range(half // tm):
            lo = slice(tm * m, tm * (m + 1))
            hi = slice(length - tm * (m + 1), length - tm * m)
            for src, dst, sign in ((a_ref, af_ref, 1.0), (b_ref, bf_ref, -1.0)):
                mirrored = _dot(jm, src[hi, :])
                if m > 0:
                    edge = src[length - tm * m:length - tm * m + sub, :][0:1, :].astype(F32)
                    mirrored = jnp.where(row == 0, edge, mirrored)
                dst[lo, :] = (src[lo, :].astype(F32) + sign * mirrored).astype(BF16)

    af = af_ref[...]
    bf = bf_ref[...]
    nyquist = a_ref[half:half + sub, :][0:1, :].astype(F32)
    p = _dot(c_ref[...], af) + jnp.where((row & 1) == 0, nyquist, -nyquist)
    q = _dot(s_ref[...], bf)
    o_ref[0] = ((p + q) * scale).astype(BF16)
    mirror = ((p - q) * scale).astype(BF16)
    flipped = _dot(jm, mirror)
    first = ((_dot(cx_ref[...], af) - _dot(sx_ref[...], bf))[0:1, :] + nyquist) * scale
    o_ref[1] = jnp.where(row == 0, first, flipped).astype(BF16)


def _dft_half_mats(n, tm):
    f = GRID_W
    half = n // 2
    assert half % f == 0 and half % tm == 0
    k = jnp.arange(half, dtype=jnp.int32)[None, :]

    def table(rows, period):
        ang = ((rows[:, None] * k) % period).astype(F32) * (2.0 * math.pi / period)
        return jnp.cos(ang), jnp.sin(ang)

    ca, sa = table(jnp.arange(half // f, dtype=jnp.int32), n // f)
    cb, sb = table(jnp.arange(f, dtype=jnp.int32), n)
    cos = ca[:, None, :] * cb[None, :, :] - sa[:, None, :] * sb[None, :, :]
    sin = sa[:, None, :] * cb[None, :, :] + ca[:, None, :] * sb[None, :, :]
    n_it = half // tm
    cx, sx = table(jnp.arange(1, n_it + 1, dtype=jnp.int32) * tm, n)
    spread = lambda m: jnp.zeros((n_it, 8, half), F32).at[:, 0, :].set(m).reshape(n_it * 8, half).astype(BF16)
    return (cos.reshape(half, half).astype(BF16), (-sin).reshape(half, half).astype(BF16), spread(cx), spread(-sx))


def _fourier_half(ua, us, mats, length, tm):
    cmat, smat, cx, sx = mats
    width = ua.shape[1]
    half = length // 2
    scale = 1.0 / math.sqrt(length * FNET_GROUP_W)
    r = lax.broadcasted_iota(jnp.int32, (tm, tm), 0)
    c = lax.broadcasted_iota(jnp.int32, (tm, tm), 1)
    jmat = jnp.where(c == tm - r, 1.0, 0.0).astype(BF16)
    once = pl.Buffered(1)
    cols = width // FOURIER_COL_SPLIT
    return pl.pallas_call(
        functools.partial(_dft_half_kernel, scale=scale),
        out_shape=jax.ShapeDtypeStruct((2, half, width), BF16),
        grid=(FOURIER_COL_SPLIT, half // tm),
        in_specs=[pl.BlockSpec((tm, half), lambda j, i: (i, 0)),
                  pl.BlockSpec((tm, half), lambda j, i: (i, 0)),
                  pl.BlockSpec((8, half), lambda j, i: (i, 0)),
                  pl.BlockSpec((8, half), lambda j, i: (i, 0)),
                  pl.BlockSpec((length, cols), lambda j, i: (0, j), pipeline_mode=once),
                  pl.BlockSpec((length, cols), lambda j, i: (0, j), pipeline_mode=once),
                  pl.BlockSpec((tm, tm), lambda j, i: (0, 0), pipeline_mode=once)],
        out_specs=pl.BlockSpec((2, tm, cols), lambda j, i: (0, i, j)),
        scratch_shapes=[pltpu.VMEM((half, cols), BF16), pltpu.VMEM((half, cols), BF16)],
        compiler_params=_params(("arbitrary", "arbitrary")),
        name="fourier_half",
    )(cmat, smat, cx, sx, ua, us, jmat)


def _fourier(ua, us, cmat, smat, row0, length, tm, tk):
    width = ua.shape[1]
    off = row0 // tk
    scale = 1.0 / math.sqrt(length * FNET_GROUP_W)
    return pl.pallas_call(
        functools.partial(_dft_kernel, scale=scale),
        out_shape=jax.ShapeDtypeStruct((length, width), BF16),
        grid=(length // tm, length // tk),
        in_specs=[pl.BlockSpec((tm, tk), lambda i, k: (i, k)),
                  pl.BlockSpec((tm, tk), lambda i, k: (i, k)),
                  pl.BlockSpec((tk, width), lambda i, k: (off + k, 0)),
                  pl.BlockSpec((tk, width), lambda i, k: (off + k, 0))],
        out_specs=pl.BlockSpec((tm, width), lambda i, k: (i, 0)),
        scratch_shapes=[pltpu.VMEM((tm, width), F32)],
        compiler_params=_params(("parallel", "arbitrary")),
        name="fourier",
    )(cmat, smat, ua, us)


def _attend(q_ref, k_all, v_all, valid, sink_ref, o_ref):
    hd = SWA_HEAD_DIM
    heads_per_kv = SWA_HEADS // SWA_KV_HEADS
    k_sw = jnp.concatenate([k_all[:, hd:], k_all[:, :hd]], axis=1)
    v_t = v_all.astype(F32).T
    v_t_sw = jnp.concatenate([v_t[hd:], v_t[:hd]], axis=0)
    lane = lax.broadcasted_iota(jnp.int32, (1, LANES), 1)
    lane_half = (lane < hd, lane >= hd)
    row = lax.broadcasted_iota(jnp.int32, (LANES, 1), 0)
    row_half = (row < hd, row >= hd)
    zero = jnp.zeros((), BF16)
    for p in range(SWA_HEADS // 2):
        acc = None
        for par in range(2):
            head = 2 * p + par
            aligned = head // heads_per_kv == par
            qh = jnp.where(lane_half[par], q_ref[:, p * LANES:(p + 1) * LANES], zero)
            s = _dot_nt(k_all if aligned else k_sw, qh)
            if valid is not None:
                kw = valid.shape[0]
                s = jnp.concatenate([jnp.where(valid, s[:kw], -jnp.inf), s[kw:]], axis=0)
            sink = sink_ref[head:head + 1, 0:1]
            m = jnp.maximum(jnp.max(s, axis=0, keepdims=True), sink)
            e = jnp.exp(s - m)
            denom = jnp.sum(e, axis=0, keepdims=True) + jnp.exp(sink - m)
            v_use = jnp.where(row_half[par], v_t if aligned else v_t_sw, 0.0).astype(BF16)
            part = _dot(v_use, e.astype(BF16)) * (1.0 / denom)
            acc = part if acc is None else acc + part
        o_ref[:, p * LANES:(p + 1) * LANES] = acc.T.astype(BF16)


def _swa_kernel(q_ref, kp_ref, kc_ref, kn_ref, kx_ref, vp_ref, vc_ref, vn_ref, vx_ref, sink_ref, o_ref,
                *, seq):
    n = pl.program_id(1)
    w = WINDOW
    k_blocks = [kp_ref[...], kc_ref[0:w, :], kc_ref[w:2 * w, :], kn_ref[...]]
    v_blocks = [vp_ref[...], vc_ref[0:w, :], vc_ref[w:2 * w, :], vn_ref[...]]
    j = lax.broadcasted_iota(jnp.int32, (3 * w, w), 0)
    a = lax.broadcasted_iota(jnp.int32, (3 * w, w), 1)
    band = (j >= a) & (j - a <= 2 * w)
    for half in range(2):
        key_pos = (2 * n + half - 1) * w + j
        valid = band & (key_pos >= 0) & (key_pos < seq)
        k_all = jnp.concatenate(k_blocks[half:half + 3] + [kx_ref[...]], axis=0)
        v_all = jnp.concatenate(v_blocks[half:half + 3] + [vx_ref[...]], axis=0)
        rows = pl.ds(half * w, w)
        _attend(q_ref.at[rows], k_all, v_all, valid, sink_ref, o_ref.at[rows])


def _swa(lay, q_c, k_c, v_c, sink_b):
    w = WINDOW
    assert lay.seq % (2 * w) == 0
    nq = lay.seq // w
    steps = nq // 2
    ctx_blk = (lay.batch * lay.seq) // lay.ctx

    def edge(f):
        return pl.BlockSpec((w, SWA_KV_W), lambda b, n: (b * nq + f(n), 0))

    prev = edge(lambda n: jnp.maximum(2 * n - 1, 0))
    nxt = edge(lambda n: jnp.minimum(2 * n + 2, nq - 1))
    cur = pl.BlockSpec((2 * w, SWA_KV_W), lambda b, n: (b * steps + n, 0))
    ctx_spec = pl.BlockSpec((lay.ctx, SWA_KV_W), lambda b, n: (ctx_blk + b, 0))
    q_spec = pl.BlockSpec((2 * w, SWA_Q_W), lambda b, n: (b * steps + n, 0))
    return pl.pallas_call(
        functools.partial(_swa_kernel, seq=lay.seq),
        out_shape=jax.ShapeDtypeStruct((lay.batch * lay.seq, SWA_Q_W), BF16),
        grid=(lay.batch, steps),
        in_specs=[q_spec, prev, cur, nxt, ctx_spec, prev, cur, nxt, ctx_spec,
                  pl.BlockSpec(sink_b.shape, lambda b, n: (0, 0))],
        out_specs=q_spec,
        compiler_params=_params(("parallel", "parallel")),
        name="swa",
    )(q_c, k_c, k_c, k_c, k_c, v_c, v_c, v_c, v_c, sink_b)


def _ctx_attn_kernel(q_ref, kx_ref, vx_ref, sink_ref, o_ref):
    _attend(q_ref, kx_ref[...], vx_ref[...], None, sink_ref, o_ref)


def _ctx_attn(lay, q_c, k_c, v_c, sink_b):
    ctx_blk = (lay.batch * lay.seq) // lay.ctx
    spec = lambda wd: pl.BlockSpec((lay.ctx, wd), lambda b: (ctx_blk + b, 0))
    return pl.pallas_call(
        _ctx_attn_kernel,
        out_shape=jax.ShapeDtypeStruct((lay.batch * lay.ctx, SWA_Q_W), BF16),
        grid=(lay.batch,),
        in_specs=[spec(SWA_Q_W), spec(SWA_KV_W), spec(SWA_KV_W),
                  pl.BlockSpec(sink_b.shape, lambda b: (0, 0))],
        out_specs=pl.BlockSpec((lay.ctx, SWA_Q_W), lambda b: (b, 0)),
        compiler_params=_params(("parallel",)),
        name="ctx_attn",
    )(q_c, k_c, v_c, sink_b)


def _merge_kernel(x_ref, of_ref, ob_ref, ra_ref, ybl_ref, ybc_ref, ycl_ref, ycc_ref, gl_ref, mod_ref,
                  ggla_ref, gffn_ref, wpa_ref, wpb_ref, wpc_ref, wout_ref, wr_ref,
                  x1_ref, h2_ref, gates_ref, count_ref, *, n_lat):
    d = D_MODEL
    is_ctx = pl.program_id(0) >= n_lat
    o = of_ref[...].astype(F32) + ob_ref[...].astype(F32)
    r = ra_ref[...].astype(F32)
    parts = []
    for h in range(GLA_HEADS):
        sl = slice(h * GLA_DV, (h + 1) * GLA_DV)
        oh = o[:, sl]
        parts.append(oh * lax.rsqrt(jnp.mean(oh * oh, axis=-1, keepdims=True) + EPS) * ggla_ref[...])
    y_a = (jnp.concatenate(parts, axis=1) * r).astype(BF16)
    y_b = jnp.where(is_ctx, ybc_ref[...], ybl_ref[...])
    y_c = jnp.where(is_ctx, ycc_ref[...], ycl_ref[...])
    gl = gl_ref[...].astype(F32)
    mix = (gl[:, 0:d] * _dot(y_a, wpa_ref[...])
           + gl[:, d:2 * d] * _dot(y_b, wpb_ref[...])
           + gl[:, 2 * d:3 * d] * _dot(y_c, wpc_ref[...]))
    y = _dot(mix.astype(BF16), wout_ref[...])
    x1 = x_ref[...] + mod_ref[:, 2 * d:3 * d] * y
    x1_ref[...] = x1
    h2 = x1 * lax.rsqrt(jnp.mean(x1 * x1, axis=-1, keepdims=True) + EPS) * gffn_ref[...]
    h2 = h2 * (1.0 + mod_ref[:, 4 * d:5 * d]) + mod_ref[:, 3 * d:4 * d]
    h2_hi = h2.astype(BF16)
    h2_ref[...] = h2_hi
    h2_lo = (h2 - h2_hi.astype(F32)).astype(BF16)
    both = _dot(h2_hi, wr_ref[...])
    logits = both[:, :ROUTER_PAD] + both[:, ROUTER_PAD:] + _dot(h2_lo, wr_ref[:, :ROUTER_PAD])
    lt = logits.T
    sub = lax.broadcasted_iota(jnp.int32, lt.shape, 0)
    sub_f = sub.astype(F32)
    neg = -jnp.inf
    big = float(ROUTER_PAD)
    is_group = (sub >= MOE_EXPERTS) & (sub < MOE_EXPERTS + MOE_GROUPS)
    gl_m = jnp.where(is_group, lt, neg)
    g_max = jnp.max(gl_m, axis=0, keepdims=True)
    g_sel = jnp.min(jnp.where(gl_m == g_max, sub_f, big), axis=0, keepdims=True) - MOE_EXPERTS
    g_gate = 1.0 / jnp.sum(jnp.where(is_group, jnp.exp(lt - g_max), 0.0), axis=0, keepdims=True)
    lo = g_sel * MOE_EXPERTS_PER_GROUP
    in_group = (sub_f >= lo) & (sub_f < lo + MOE_EXPERTS_PER_GROUP)
    e1 = jnp.where(in_group, lt, neg)
    v1 = jnp.max(e1, axis=0, keepdims=True)
    i1 = jnp.min(jnp.where(e1 == v1, sub_f, big), axis=0, keepdims=True)
    e2 = jnp.where(sub_f == i1, neg, e1)
    v2 = jnp.max(e2, axis=0, keepdims=True)
    i2 = jnp.min(jnp.where(e2 == v2, sub_f, big), axis=0, keepdims=True)
    t = jnp.exp(v2 - v1)
    w1 = g_gate / (1.0 + t)
    w2 = g_gate * t / (1.0 + t)
    route_t = jnp.where(sub == ROUTE_ID, i1, 0.0) + jnp.where(sub == ROUTE_ID + 1, i2, 0.0)
    route_t = route_t + jnp.where(sub == ROUTE_W, w1, 0.0) + jnp.where(sub == ROUTE_W + 1, w2, 0.0)
    route = route_t.T
    gates_ref[...] = route
    lane_f = lax.broadcasted_iota(jnp.int32, route.shape, 1).astype(F32)
    hit = jnp.where((lane_f == route[:, ROUTE_ID:ROUTE_ID + 1]) | (lane_f == route[:, ROUTE_ID + 1:ROUTE_ID + 2]),
                    1.0, 0.0)
    count_ref[...] = jnp.sum(hit, axis=0, keepdims=True).astype(jnp.int32)


def _merge(lay, x, o_f, o_b, r_a, yb_lat, fourier_tile, yb_ctx, yc_lat, yc_ctx, gate_logits, mods, g_gla, g_ffn,
           w_pa, w_pb, w_pc, w_out, w_router):
    tm, d = lay.tile, D_MODEL
    n = lay.n_tok
    tok = lambda w: pl.BlockSpec((tm, w), lambda t: (t, 0))
    const = lambda a: pl.BlockSpec(a.shape, lambda t: (0,) * a.ndim)
    lt, ct = lay.lat_tiles, lay.ctx_tiles
    sub = fourier_tile // tm

    def yb_lat_index(t):
        tl = jnp.minimum(t, lay.n_lat - 1)
        s = tl % lt
        u = lt - 1 - s
        upper = s >= lt // 2
        blk = jnp.where(upper, (u // sub) * sub + sub - 1 - u % sub, s)
        return (upper.astype(jnp.int32), blk, tl // lt)

    yb_lat_spec = pl.BlockSpec((None, tm, FNET_W), yb_lat_index)
    yb_ctx_spec = pl.BlockSpec((tm, FNET_W), lambda t: (jnp.maximum(t - lay.n_lat, 0) % ct,
                                                         jnp.maximum(t - lay.n_lat, 0) // ct))
    yc_lat_spec = pl.BlockSpec((tm, SWA_Q_W), lambda t: (jnp.minimum(t, lay.n_lat - 1), 0))
    yc_ctx_spec = pl.BlockSpec((tm, SWA_Q_W), lambda t: (jnp.maximum(t - lay.n_lat, 0), 0))
    return pl.pallas_call(
        functools.partial(_merge_kernel, n_lat=lay.n_lat),
        out_shape=(jax.ShapeDtypeStruct((n, d), F32), jax.ShapeDtypeStruct((n, d), BF16),
                   jax.ShapeDtypeStruct((n, ROUTER_PAD), F32),
                   jax.ShapeDtypeStruct((lay.n_tiles, 1, ROUTER_PAD), jnp.int32)),
        grid=(lay.n_tiles,),
        in_specs=[tok(d), tok(GLA_V_W), tok(GLA_V_W), tok(GLA_V_W), yb_lat_spec, yb_ctx_spec,
                  yc_lat_spec, yc_ctx_spec, tok(3 * d),
                  pl.BlockSpec((None, 1, N_MOD * d), lambda t: (lay.mod_row(t), 0, 0)),
                  const(g_gla), const(g_ffn), const(w_pa), const(w_pb), const(w_pc), const(w_out),
                  const(w_router)],
        out_specs=(tok(d), tok(d), tok(ROUTER_PAD),
                   pl.BlockSpec((None, 1, ROUTER_PAD), lambda t: (t, 0, 0))),
        compiler_params=_params(("parallel",)),
        name="merge",
    )(x, o_f, o_b, r_a, yb_lat, yb_ctx, yc_lat, yc_ctx, gate_logits, mods, g_gla, g_ffn,
      w_pa, w_pb, w_pc, w_out, w_router)


def _route_ids(route, axis):
    take = (lambda i: route[:, i:i + 1]) if axis == 1 else (lambda i: route[i:i + 1, :])
    return take(ROUTE_ID).astype(jnp.int32), take(ROUTE_ID + 1).astype(jnp.int32)


def _moe_plan(counts, n_blocks):
    cnt = counts
    pc = (cnt + ROW_CHUNK - 1) // ROW_CHUNK * ROW_CHUNK
    lstart = jnp.cumsum(pc, axis=1) - pc
    tot = jnp.sum(pc, axis=0)
    tot_pad = (tot + FFN_BLOCK - 1) // FFN_BLOCK * FFN_BLOCK
    eend = jnp.cumsum(tot_pad)
    estart = eend - tot_pad
    base = estart[None, :] + jnp.cumsum(pc, axis=0) - pc
    n_used = eend[-1] // FFN_BLOCK
    blk = jnp.minimum(jnp.arange(n_blocks, dtype=jnp.int32), n_used - 1)
    bexp = jnp.sum((blk[:, None] * FFN_BLOCK >= eend[None, :]).astype(jnp.int32), axis=1)
    flat = lambda a: a.reshape(-1).astype(jnp.int32)
    nch = pc // ROW_CHUNK
    per_big = BIG_CHUNK // ROW_CHUNK
    nch = jnp.concatenate([flat(nch), flat(jnp.sum(nch // per_big, axis=1)), flat(jnp.sum(nch % per_big, axis=1))])
    return dict(base=flat(base), lstart=flat(lstart), nch=nch,
                gap0=flat(estart + tot), gapn=flat((tot_pad - tot) // ROW_CHUNK),
                bexp=flat(bexp), nused=flat(n_used))


def _segment_copies(t, base_ref, lstart_ref, nch_ref, make, start):
    per_big = BIG_CHUNK // ROW_CHUNK
    if not start:
        n_tiles = nch_ref.shape[0] // (MOE_EXPERTS + 2)

        def wait(rows):
            def body(i, c):
                make(0, 0, rows).wait()
                return c
            return body

        lax.fori_loop(0, nch_ref[n_tiles * MOE_EXPERTS + t], wait(BIG_CHUNK), 0)
        lax.fori_loop(0, nch_ref[n_tiles * (MOE_EXPERTS + 1) + t], wait(ROW_CHUNK), 0)
        return

    def per_expert(e, carry):
        idx = t * MOE_EXPERTS + e
        loc = lstart_ref[idx]
        glob = base_ref[idx]
        n_big = nch_ref[idx] // per_big
        n_small = nch_ref[idx] - n_big * per_big

        def piece(rows, first):
            def body(i, c):
                off = first + i * rows
                cp = make(pl.multiple_of(loc + off, ROW_CHUNK), pl.multiple_of(glob + off, ROW_CHUNK), rows)
                cp.start() if start else cp.wait()
                return c
            return body

        carry = lax.fori_loop(0, n_big, piece(BIG_CHUNK, 0), carry)
        return lax.fori_loop(0, n_small, piece(ROW_CHUNK, n_big * BIG_CHUNK), carry)

    lax.fori_loop(0, MOE_EXPERTS, per_expert, 0)


def _dispatch_kernel(base_ref, lstart_ref, nch_ref, gap0_ref, gapn_ref, nused_ref, h_ref, route_ref, xs_ref,
                     buf_ref, sem):
    t = pl.program_id(0)
    tile = h_ref.shape[0]
    slots = buf_ref.shape[1]
    rt = route_ref[...].T
    e1, e2 = _route_ids(rt, 0)
    sub = lax.broadcasted_iota(jnp.int32, rt.shape, 0)
    oh1, oh2 = sub == e1, sub == e2
    hit = jnp.where(oh1 | oh2, 1.0, 0.0).astype(BF16)
    before = (lax.broadcasted_iota(jnp.int32, (tile, tile), 0)
              < lax.broadcasted_iota(jnp.int32, (tile, tile), 1))
    rank = _dot(hit, jnp.where(before, 1.0, 0.0).astype(BF16))
    sub1 = lax.broadcasted_iota(jnp.int32, (rt.shape[0], 1), 0)
    seg = jnp.zeros((rt.shape[0], 1), F32)
    for e in range(MOE_EXPERTS):
        seg = jnp.where(sub1 == e, lstart_ref[t * MOE_EXPERTS + e].astype(F32), seg)
    slot_of = rank + seg
    pos1 = jnp.sum(jnp.where(oh1, slot_of, 0.0), axis=0, keepdims=True).astype(jnp.int32)
    pos2 = jnp.sum(jnp.where(oh2, slot_of, 0.0), axis=0, keepdims=True).astype(jnp.int32)
    slot = lax.broadcasted_iota(jnp.int32, (slots, tile), 0)
    perm = jnp.where((slot == pos1) | (slot == pos2), 1.0, 0.0).astype(BF16)
    cur = t % 2
    buf_ref[cur] = _dot(perm, h_ref[...]).astype(BF16)

    def maker(b):
        def make(loc, glob, rows):
            return pltpu.make_async_copy(buf_ref.at[b, pl.ds(loc, rows)], xs_ref.at[pl.ds(glob, rows)], sem.at[b])
        return make

    _segment_copies(t, base_ref, lstart_ref, nch_ref, maker(cur), True)

    @pl.when(t > 0)
    def _():
        _segment_copies(t - 1, base_ref, lstart_ref, nch_ref, maker(1 - cur), False)

    @pl.when(t == pl.num_programs(0) - 1)
    def _():
        _segment_copies(t, base_ref, lstart_ref, nch_ref, maker(cur), False)
        buf_ref[cur, 0:FFN_BLOCK, :] = jnp.zeros((FFN_BLOCK, buf_ref.shape[2]), BF16)
        n_blocks = xs_ref.shape[0] // FFN_BLOCK

        def fill(start):
            def per_expert(e, carry):
                def chunk(i, c):
                    row = pl.multiple_of(gap0_ref[e] + i * ROW_CHUNK, ROW_CHUNK)
                    cp = pltpu.make_async_copy(buf_ref.at[cur, 0:ROW_CHUNK], xs_ref.at[pl.ds(row, ROW_CHUNK)],
                                               sem.at[cur])
                    cp.start() if start else cp.wait()
                    return c
                return lax.fori_loop(0, gapn_ref[e], chunk, carry)

            def tail(b, c):
                row = pl.multiple_of(b * FFN_BLOCK, FFN_BLOCK)
                cp = pltpu.make_async_copy(buf_ref.at[cur, 0:FFN_BLOCK], xs_ref.at[pl.ds(row, FFN_BLOCK)],
                                           sem.at[cur])
                cp.start() if start else cp.wait()
                return c

            lax.fori_loop(0, MOE_EXPERTS, per_expert, 0)
            lax.fori_loop(nused_ref[0], n_blocks, tail, 0)

        fill(True)
        fill(False)


def _dispatch(plan, h2, route, n_rows):
    tile, d = DISPATCH_TILE, D_MODEL
    n_tiles = h2.shape[0] // tile
    return pl.pallas_call(
        _dispatch_kernel,
        out_shape=jax.ShapeDtypeStruct((n_rows, d), BF16),
        grid_spec=pltpu.PrefetchScalarGridSpec(
            num_scalar_prefetch=6,
            grid=(n_tiles,),
            in_specs=[pl.BlockSpec((tile, d), lambda t, *_: (t, 0)),
                      pl.BlockSpec((tile, ROUTER_PAD), lambda t, *_: (t, 0))],
            out_specs=pl.BlockSpec(memory_space=pl.ANY),
            scratch_shapes=[pltpu.VMEM((2, SLOT_ROWS, d), BF16), pltpu.SemaphoreType.DMA((2,))],
        ),
        compiler_params=_params(("arbitrary",)),
        name="moe_dispatch",
    )(plan["base"], plan["lstart"], plan["nch"], plan["gap0"], plan["gapn"], plan["nused"], h2, route)


def _ffn_kernel(bexp_ref, nused_ref, x_ref, w1_ref, w3_ref, w2_ref, y_ref, w13_s, w2_s):
    b = pl.program_id(0)
    used = b < nused_ref[0]
    de = D_EXPERT

    @pl.when(used & ((b == 0) | (bexp_ref[b] != bexp_ref[jnp.maximum(b - 1, 0)])))
    def _():
        w13_s[:, :de] = w1_ref[...].astype(BF16)
        w13_s[:, de:] = w3_ref[...].astype(BF16)
        w2_s[...] = w2_ref[...].astype(BF16)

    @pl.when(used)
    def _():
        ab = _dot(x_ref[...], w13_s[...])
        a = ab[:, :de]
        hid = (a * _sigmoid(a)) * ab[:, de:]
        y_ref[...] = _dot(hid.astype(BF16), w2_s[...]).astype(BF16)

    @pl.when(jnp.logical_not(used))
    def _():
        y_ref[...] = jnp.zeros_like(y_ref)


def _expert_ffn(plan, xs, layer, w1, w3, w2):
    d = D_MODEL
    n_blocks = xs.shape[0] // FFN_BLOCK
    row = lambda b, bexp, nused: (jnp.minimum(b, nused[0] - 1), 0)
    wsel = lambda b, bexp, nused: (layer, bexp[b], 0, 0)
    return pl.pallas_call(
        _ffn_kernel,
        out_shape=jax.ShapeDtypeStruct(xs.shape, BF16),
        grid_spec=pltpu.PrefetchScalarGridSpec(
            num_scalar_prefetch=2,
            grid=(n_blocks,),
            in_specs=[pl.BlockSpec((FFN_BLOCK, d), row),
                      pl.BlockSpec((None, None, d, D_EXPERT), wsel),
                      pl.BlockSpec((None, None, d, D_EXPERT), wsel),
                      pl.BlockSpec((None, None, D_EXPERT, d), wsel)],
            out_specs=pl.BlockSpec((FFN_BLOCK, d), lambda b, bexp, nused: (b, 0)),
            scratch_shapes=[pltpu.VMEM((d, 2 * D_EXPERT), BF16), pltpu.VMEM((D_EXPERT, d), BF16)],
        ),
        compiler_params=_params(("arbitrary",)),
        name="moe_ffn",
    )(plan["bexp"], plan["nused"], xs, w1, w3, w2)


def _combine_kernel(base_ref, lstart_ref, nch_ref, route_ref, x1_ref, mod_ref, ys_ref, *rest, n_out_tiles):
    g_ref = rest[0] if len(rest) == 4 else None
    o_ref, buf_ref, sem = rest[-3:]
    d = D_MODEL
    t = pl.program_id(0)
    tile = route_ref.shape[0]
    slots = buf_ref.shape[1]
    cur = t % 2

    def maker(b):
        def make(loc, glob, rows):
            return pltpu.make_async_copy(ys_ref.at[pl.ds(glob, rows)], buf_ref.at[b, pl.ds(loc, rows)], sem.at[b])
        return make

    @pl.when(t == 0)
    def _():
        buf_ref[...] = jnp.zeros_like(buf_ref)
        _segment_copies(t, base_ref, lstart_ref, nch_ref, maker(cur), True)

    @pl.when(t + 1 < pl.num_programs(0))
    def _():
        _segment_copies(t + 1, base_ref, lstart_ref, nch_ref, maker(1 - cur), True)

    route = route_ref[...]
    e1, e2 = _route_ids(route, 1)
    lane = lax.broadcasted_iota(jnp.int32, route.shape, 1)
    oh1, oh2 = lane == e1, lane == e2
    hit = jnp.where(oh1 | oh2, 1.0, 0.0).astype(BF16)
    before = (lax.broadcasted_iota(jnp.int32, (tile, tile), 1)
              < lax.broadcasted_iota(jnp.int32, (tile, tile), 0))
    rank = _dot(jnp.where(before, 1.0, 0.0).astype(BF16), hit)
    lane1 = lax.broadcasted_iota(jnp.int32, (1, route.shape[1]), 1)
    seg = jnp.zeros((1, route.shape[1]), F32)
    for e in range(MOE_EXPERTS):
        seg = jnp.where(lane1 == e, lstart_ref[t * MOE_EXPERTS + e].astype(F32), seg)
    slot_of = rank + seg
    pos1 = jnp.sum(jnp.where(oh1, slot_of, 0.0), axis=1, keepdims=True).astype(jnp.int32)
    pos2 = jnp.sum(jnp.where(oh2, slot_of, 0.0), axis=1, keepdims=True).astype(jnp.int32)
    slot = lax.broadcasted_iota(jnp.int32, (tile, slots), 1)
    w1 = route[:, ROUTE_W:ROUTE_W + 1]
    w2 = route[:, ROUTE_W + 1:ROUTE_W + 2]
    comb = (jnp.where(slot == pos1, w1, 0.0) + jnp.where(slot == pos2, w2, 0.0)).astype(BF16)
    _segment_copies(t, base_ref, lstart_ref, nch_ref, maker(cur), False)
    moe = _dot(comb, buf_ref[cur])
    x2 = x1_ref[...] + mod_ref[:, 5 * d:6 * d] * moe
    if g_ref is None:
        o_ref[...] = x2
    else:
        @pl.when(t < n_out_tiles)
        def _():
            o_ref[...] = x2 * lax.rsqrt(jnp.mean(x2 * x2, axis=-1, keepdims=True) + EPS) * g_ref[...]


def _combine(lay, plan, route, x1, mods, ys, g_final):
    tile, d = DISPATCH_TILE, D_MODEL
    assert lay.seq % tile == 0 and (lay.batch * lay.ctx) % tile == 0
    lat_tiles = lay.seq // tile
    n_lat = lay.batch * lat_tiles
    n_tiles = lay.n_tok // tile
    row = lambda t: jnp.where(t < n_lat, t // lat_tiles, lay.batch)
    final = g_final is not None
    n_out_tiles = n_lat if final else n_tiles
    extra_specs = [pl.BlockSpec((1, d), lambda t, *_: (0, 0))] if final else []
    extra_args = [g_final] if final else []
    return pl.pallas_call(
        functools.partial(_combine_kernel, n_out_tiles=n_out_tiles),
        out_shape=jax.ShapeDtypeStruct((n_out_tiles * tile, d), F32),
        grid_spec=pltpu.PrefetchScalarGridSpec(
            num_scalar_prefetch=3,
            grid=(n_tiles,),
            in_specs=[pl.BlockSpec((tile, ROUTER_PAD), lambda t, *_: (t, 0)),
                      pl.BlockSpec((tile, d), lambda t, *_: (t, 0)),
                      pl.BlockSpec((None, 1, N_MOD * d), lambda t, *_: (row(t), 0, 0)),
                      pl.BlockSpec(memory_space=pl.ANY)] + extra_specs,
            out_specs=pl.BlockSpec((tile, d), lambda t, *_: (jnp.minimum(t, n_out_tiles - 1), 0)),
            scratch_shapes=[pltpu.VMEM((2, SLOT_ROWS, d), BF16), pltpu.SemaphoreType.DMA((2,))],
        ),
        compiler_params=_params(("arbitrary",)),
        name="moe_combine",
    )(plan["base"], plan["lstart"], plan["nch"], route, x1, mods, ys, *extra_args)


def _moe(lay, h2, x1, route, tile_counts, mods, layer, w1, w3, w2, g_final=None):
    n_tiles = lay.n_tok // DISPATCH_TILE
    max_rows = (MOE_TOPK * lay.n_tok + n_tiles * MOE_EXPERTS * (ROW_CHUNK - 1)
                + MOE_EXPERTS * (FFN_BLOCK - 1))
    n_blocks = -(-max_rows // FFN_BLOCK)
    counts = jnp.sum(tile_counts[:, 0, :MOE_EXPERTS].reshape(n_tiles, DISPATCH_TILE // lay.tile, MOE_EXPERTS), axis=1)
    plan = _moe_plan(counts, n_blocks)
    xs = _dispatch(plan, h2, route, n_blocks * FFN_BLOCK)
    ys = _expert_ffn(plan, xs, layer, w1, w3, w2)
    return _combine(lay, plan, route, x1, mods, ys, g_final)


def _rope_tables(seq, ctx):
    pos = jnp.arange(seq, dtype=jnp.int32)
    inv_freq = ROPE_BASE ** (-jnp.arange(0, AXIS_DIM, 2, dtype=F32) / AXIS_DIM)
    ang_row = (pos // GRID_W).astype(F32)[:, None] * inv_freq
    ang_col = (pos % GRID_W).astype(F32)[:, None] * inv_freq
    cos_h = jnp.concatenate([jnp.cos(ang_row)] * 2 + [jnp.cos(ang_col)] * 2, axis=1)
    sin_h = jnp.concatenate([-jnp.sin(ang_row), jnp.sin(ang_row), -jnp.sin(ang_col), jnp.sin(ang_col)], axis=1)
    reps = LANES // SWA_HEAD_DIM
    cos_t = jnp.concatenate([jnp.tile(cos_h, (1, reps)), jnp.ones((ctx, LANES), F32)], axis=0)
    sin_t = jnp.concatenate([jnp.tile(sin_h, (1, reps)), jnp.zeros((ctx, LANES), F32)], axis=0)
    return cos_t, sin_t


def _channel_dft():
    i = jnp.arange(FNET_GROUP_W, dtype=jnp.int32)
    ang = ((i[:, None] * i[None, :]) % FNET_GROUP_W).astype(F32) * (2.0 * math.pi / FNET_GROUP_W)
    return jnp.concatenate([jnp.cos(ang), jnp.sin(ang)], axis=1).astype(BF16)


def kernel(x, c, ctx, c_ctx, w_ada, b_ada, g_mix, g_ffn, w_in, w_decay_down, w_decay_up, b_decay, g_gla, sink,
           w_pa, w_pb, w_pc, w_out, w_router_group, w_router_expert, w1, w3, w2, g_final):
    batch, seq, d = x.shape
    n_ctx = ctx.shape[1]
    depth = w_ada.shape[0]
    assert d == D_MODEL and seq % GRID_W == 0 and seq % n_ctx == 0
    lay = _Layout(batch, seq, n_ctx, TOKEN_TILE)

    rows = -(-(batch + 1) // 8) * 8
    c_all = jnp.zeros((rows, d), F32).at[:batch].set(c).at[batch].set(c_ctx)
    mods_all = _adaln(c_all, w_ada, b_ada).reshape(depth, rows, 1, N_MOD * d)

    cos_t, sin_t = _rope_tables(seq, n_ctx)
    cs = _channel_dft()
    fourier_tile = min(seq // 2, FOURIER_TILE)
    assert fourier_tile % TOKEN_TILE == 0 and seq % (2 * fourier_tile) == 0
    lat_mats = _dft_half_mats(seq, fourier_tile)
    c_ctx_m, s_ctx_m = _dft_mats(n_ctx)

    xs = jnp.concatenate([x.reshape(batch * seq, d), ctx.reshape(batch * n_ctx, d)], axis=0)
    rank = w_decay_down.shape[-1]
    w_in_bf = w_in.astype(BF16)
    for l in range(depth):
        last = l == depth - 1
        mods = mods_all[l]
        down = jnp.concatenate([w_decay_down[l, 0], w_decay_down[l, 1]], axis=1)
        down = jnp.pad(down, ((0, 0), (0, DECAY_PAD - 2 * rank))).astype(BF16)
        w_up = jnp.zeros((DECAY_PAD, 2 * GLA_QK_W), F32)
        w_up = w_up.at[:rank, :GLA_QK_W].set(w_decay_up[l, 0]).at[rank:2 * rank, GLA_QK_W:].set(w_decay_up[l, 1])
        b_dec = b_decay[l].reshape(1, 2 * GLA_QK_W)
        (k_a, v_a, k_c, v_c, q_a, r_a, u_cos, u_sin, q_c, gate_logits, la_f, la_b) = _inproj(
            lay, xs, g_mix[l].reshape(1, d), mods, l, w_in_bf, down, w_up.astype(BF16), b_dec, cs, cos_t, sin_t)

        o_f, o_b = _gla(lay, q_a, k_a, v_a, la_f, la_b)
        yb_lat = _fourier_half(u_cos, u_sin, lat_mats, seq, fourier_tile)
        yc_lat = _swa(lay, q_c, k_c, v_c, jnp.broadcast_to(sink[l][:, None], (SWA_HEADS, LANES)))
        if last:
            yb_ctx = jnp.zeros((n_ctx, batch * FNET_W), BF16)
            yc_ctx = jnp.zeros((batch * n_ctx, SWA_Q_W), BF16)
        else:
            yb_ctx = _fourier(u_cos, u_sin, c_ctx_m, s_ctx_m, seq, n_ctx, n_ctx, n_ctx)
            yc_ctx = _ctx_attn(lay, q_c, k_c, v_c, jnp.broadcast_to(sink[l][:, None], (SWA_HEADS, LANES)))

        w_router = jnp.zeros((d, ROUTER_PAD), F32)
        w_router = w_router.at[:, :MOE_EXPERTS].set(w_router_expert[l])
        w_router = w_router.at[:, MOE_EXPERTS:MOE_EXPERTS + MOE_GROUPS].set(w_router_group[l])
        w_router_hi = w_router.astype(BF16)
        w_router = jnp.concatenate([w_router_hi, (w_router - w_router_hi.astype(F32)).astype(BF16)], axis=1)
        x1, h2, gates, tile_counts = _merge(lay, xs, o_f, o_b, r_a, yb_lat, fourier_tile, yb_ctx, yc_lat, yc_ctx,
                               gate_logits, mods,
                               g_gla[l].reshape(1, GLA_DV), g_ffn[l].reshape(1, d),
                               w_pa[l].astype(BF16), w_pb[l].astype(BF16), w_pc[l].astype(BF16),
                               w_out[l].astype(BF16), w_router)
        xs = _moe(lay, h2, x1, gates, tile_counts, mods, l, w1, w3, w2,
                  g_final.reshape(1, d) if last else None)

    return xs.reshape(batch, seq, d)
```

```python
import functools
import math

import jax
import jax.numpy as jnp
from jax import lax
from jax.experimental import pallas as pl
from jax.experimental.pallas import tpu as pltpu

F32 = jnp.float32
BF16 = jnp.bfloat16
HIGHEST = lax.Precision.HIGHEST
LOG2E = math.log2(math.e)

D_MODEL = 1024
GRID_W = 64
EPS = 1e-6
N_MOD = 6
GLA_HEADS = 4
GLA_DK = 64
GLA_DV = 128
GLA_TAU = 16.0
GLA_CHUNK = 64
GLA_SCALE = GLA_DK ** -0.5
FNET_GROUPS = 4
FNET_GROUP_W = 128
SWA_HEADS = 8
SWA_KV_HEADS = 2
SWA_HEAD_DIM = 64
WINDOW = 128
ROPE_BASE = 10000.0
AXIS_DIM = SWA_HEAD_DIM // 2
MOE_GROUPS = 4
MOE_EXPERTS_PER_GROUP = 4
MOE_EXPERTS = MOE_GROUPS * MOE_EXPERTS_PER_GROUP
MOE_TOPK = 2
D_EXPERT = 512

GLA_QK_W = GLA_HEADS * GLA_DK
GLA_V_W = GLA_HEADS * GLA_DV
FNET_W = FNET_GROUPS * FNET_GROUP_W
SWA_Q_W = SWA_HEADS * SWA_HEAD_DIM
SWA_KV_W = SWA_KV_HEADS * SWA_HEAD_DIM
IN_SIZES = (GLA_QK_W, GLA_V_W, SWA_KV_W, SWA_KV_W, GLA_QK_W, GLA_V_W, FNET_W, SWA_Q_W, 3 * D_MODEL)
IN_OFFS = tuple(int(sum(IN_SIZES[:i])) for i in range(len(IN_SIZES) + 1))
IN_COLS = IN_OFFS[-1]

LANES = 128
TOKEN_TILE = 256
DECAY_PAD = LANES
ROUTER_PAD = LANES
ROUTE_ID = 0
ROUTE_W = 2
DISPATCH_TILE = 512
ROW_CHUNK = 16
BIG_CHUNK = 64
FOURIER_TILE = 512
FOURIER_COL_SPLIT = 2
FFN_BLOCK = 512
SLOT_ROWS = -(-(MOE_TOPK * DISPATCH_TILE + MOE_EXPERTS * (ROW_CHUNK - 1)) // LANES) * LANES
VMEM_LIMIT = 56 * 1024 * 1024


def _layer_block(a, layer):
    return pl.BlockSpec((None,) + a.shape[1:], lambda *_: (layer,) + (0,) * (a.ndim - 1))


def _params(sem, vmem=VMEM_LIMIT):
    return pltpu.CompilerParams(dimension_semantics=sem, vmem_limit_bytes=vmem)


def _sigmoid(x):
    return 0.5 * jnp.tanh(0.5 * x) + 0.5


def _dot(a, b):
    return jnp.dot(a, b, preferred_element_type=F32)


def _dot_nt(a, b):
    return lax.dot_general(a, b, (((1,), (1,)), ((), ())), preferred_element_type=F32)


def _ada_kernel(c_ref, w_ref, b_ref, o_ref):
    c = c_ref[...]
    a = c * _sigmoid(c)
    o_ref[...] = jnp.dot(a, w_ref[...], preferred_element_type=F32, precision=HIGHEST) + b_ref[...]


def _adaln(c_all, w_ada, b_ada):
    depth, d, n = w_ada.shape
    rows = c_all.shape[0]
    tn = 1536
    return pl.pallas_call(
        _ada_kernel,
        out_shape=jax.ShapeDtypeStruct((depth, rows, n), F32),
        grid=(depth, n // tn),
        in_specs=[pl.BlockSpec((rows, d), lambda l, j: (0, 0)),
                  pl.BlockSpec((None, d, tn), lambda l, j: (l, 0, j)),
                  pl.BlockSpec((None, 1, tn), lambda l, j: (l, 0, j))],
        out_specs=pl.BlockSpec((None, rows, tn), lambda l, j: (l, 0, j)),
        compiler_params=_params(("parallel", "parallel")),
        name="adaln",
    )(c_all, w_ada, b_ada.reshape(depth, 1, n))


class _Layout:
    def __init__(self, batch, seq, ctx, tile):
        assert seq % tile == 0 and ctx % tile == 0
        self.batch, self.seq, self.ctx, self.tile = batch, seq, ctx, tile
        self.lat_tiles = seq // tile
        self.ctx_tiles = ctx // tile
        self.n_lat = batch * self.lat_tiles
        self.n_tiles = self.n_lat + batch * self.ctx_tiles
        self.n_tok = self.n_tiles * tile

    def batch_of(self, t):
        return jnp.where(t < self.n_lat, t // self.lat_tiles, (t - self.n_lat) // self.ctx_tiles)

    def mod_row(self, t):
        return jnp.where(t < self.n_lat, t // self.lat_tiles, self.batch)

    def stream_specs(self, width, ctx_first):
        if ctx_first is None:
            return [pl.BlockSpec((self.tile, width), lambda t: (t, 0))]
        lat = pl.BlockSpec((self.tile, width), lambda t: (jnp.minimum(t, self.n_lat - 1), 0))
        ctx = pl.BlockSpec((self.tile, width), lambda t: (ctx_first + jnp.maximum(t - self.n_lat, 0), 0))
        return [lat, ctx]

    def seq_tile(self, t):
        return jnp.where(t < self.n_lat, t % self.lat_tiles,
                         self.lat_tiles + (t - self.n_lat) % self.ctx_tiles)


def _rope(x, cos, sin_signed):
    n = x.shape[-1]
    lane = lax.broadcasted_iota(jnp.int32, x.shape, 1)
    half = AXIS_DIM // 2
    partner = jnp.where((lane & half) == 0, pltpu.roll(x, n - half, 1), pltpu.roll(x, half, 1))
    return x * cos + partner * sin_signed


def _read_stream(x_refs, n_lat):
    if len(x_refs) == 1:
        return x_refs[0][...]
    return jnp.where(pl.program_id(0) >= n_lat, x_refs[1][...], x_refs[0][...])


def _inproj_kernel(*refs, n_lat, n_x):
    _inproj_body(_read_stream(refs[:n_x], n_lat), *refs[n_x:])


def _inproj_body(x, g_ref, mod_ref, w_ref, wd_ref, wu_ref, bdec_ref, cs_ref, cos_ref, sin_ref,
                 ka_ref, va_ref, kc_ref, vc_ref, qa_ref, ra_ref, ua_ref, us_ref, qc_ref, gl_ref,
                 laf_ref, lab_ref):
    d = D_MODEL
    shift = mod_ref[:, 0:d]
    scale = mod_ref[:, d:2 * d]
    h = x * lax.rsqrt(jnp.mean(x * x, axis=-1, keepdims=True) + EPS) * g_ref[...]
    hb = (h * (1.0 + scale) + shift).astype(BF16)

    def proj(i):
        return _dot(hb, w_ref[:, IN_OFFS[i]:IN_OFFS[i + 1]])

    ka_ref[...] = proj(0).astype(BF16)
    va_ref[...] = proj(1).astype(BF16)
    cos = cos_ref[...]
    sin = sin_ref[...]
    kc_ref[...] = _rope(proj(2), cos, sin).astype(BF16)
    vc_ref[...] = proj(3).astype(BF16)
    qa_ref[...] = proj(4).astype(BF16)
    r = proj(5)
    ra_ref[...] = (r * _sigmoid(r)).astype(BF16)
    u = proj(6).astype(BF16)
    for g in range(FNET_GROUPS):
        sl = slice(g * FNET_GROUP_W, (g + 1) * FNET_GROUP_W)
        ab = _dot(u[:, sl], cs_ref[...])
        ua_ref[:, sl] = ab[:, :FNET_GROUP_W].astype(BF16)
        us_ref[:, sl] = ab[:, FNET_GROUP_W:].astype(BF16)
    reps = SWA_Q_W // LANES
    qc_ref[...] = (_rope(proj(7), jnp.concatenate([cos] * reps, axis=1), jnp.concatenate([sin] * reps, axis=1))
                   * (SWA_HEAD_DIM ** -0.5 * LOG2E)).astype(BF16)
    gl_ref[...] = _sigmoid(proj(8)).astype(BF16)
    low = _dot(hb, wd_ref[...]).astype(BF16)
    z = _dot(low, wu_ref[...]) + bdec_ref[...]
    la = (jnp.minimum(z, 0.0) - jnp.log(1.0 + jnp.exp(-jnp.abs(z)))) * (1.0 / GLA_TAU)
    laf_ref[...] = la[:, :GLA_QK_W]
    lab_ref[...] = la[:, GLA_QK_W:]


def _inproj(lay, x, g_mix, mods, layer, w_in, w_down, w_up, b_dec, cs, cos_t, sin_t):
    tm, d = lay.tile, D_MODEL
    n = lay.n_tok
    bf = lambda w: jax.ShapeDtypeStruct((n, w), BF16)
    tok = lambda w: pl.BlockSpec((tm, w), lambda t: (t, 0))
    const = lambda a: pl.BlockSpec(a.shape, lambda t: (0,) * a.ndim)
    seq_rows = lay.seq + lay.ctx
    fnet_shape = jax.ShapeDtypeStruct((seq_rows, lay.batch * FNET_W), BF16)
    fnet_spec = pl.BlockSpec((tm, FNET_W), lambda t: (lay.seq_tile(t), lay.batch_of(t)))
    pos_spec = pl.BlockSpec((tm, LANES), lambda t: (lay.seq_tile(t), 0))
    return pl.pallas_call(
        functools.partial(_inproj_kernel, n_lat=lay.n_lat, n_x=len(x[0])),
        out_shape=(bf(GLA_QK_W), bf(GLA_V_W), bf(SWA_KV_W), bf(SWA_KV_W), bf(GLA_QK_W), bf(GLA_V_W),
                   fnet_shape, fnet_shape, bf(SWA_Q_W), bf(3 * d),
                   jax.ShapeDtypeStruct((n, GLA_QK_W), F32), jax.ShapeDtypeStruct((n, GLA_QK_W), F32)),
        grid=(lay.n_tiles,),
        in_specs=lay.stream_specs(d, x[1]) + [_layer_block(g_mix, layer),
                  pl.BlockSpec((None, None, 1, N_MOD * d), lambda t: (layer, lay.mod_row(t), 0, 0)),
                  pl.BlockSpec((None,) + w_in.shape[1:], lambda t: (layer, 0, 0)),
                  _layer_block(w_down, layer), _layer_block(w_up, layer), _layer_block(b_dec, layer), const(cs),
                  pos_spec, pos_spec],
        out_specs=(tok(GLA_QK_W), tok(GLA_V_W), tok(SWA_KV_W), tok(SWA_KV_W), tok(GLA_QK_W), tok(GLA_V_W),
                   fnet_spec, fnet_spec, tok(SWA_Q_W), tok(3 * d), tok(GLA_QK_W), tok(GLA_QK_W)),
        compiler_params=_params(("parallel",)),
        name="inproj",
    )(*x[0], g_mix, mods, w_in, w_down, w_up, b_dec, cs, cos_t, sin_t)


def _split2(x):
    hi = x.astype(BF16)
    return hi, (x - hi.astype(F32)).astype(BF16)


def _gla_direction(q_ref, k_ref, v_ref, la_ref, o_ref, s_ref, rev):
    t_rows = q_ref.shape[0]
    c = GLA_CHUNK
    n_sub = t_rows // c
    hk = GLA_QK_W
    la_hi, la_lo = _split2(la_ref[...])
    row = lax.broadcasted_iota(jnp.int32, (t_rows, t_rows), 0)
    col = lax.broadcasted_iota(jnp.int32, (t_rows, t_rows), 1)
    tri = ((row // c) == (col // c)) & ((col >= row) if rev else (col <= row))
    tri = jnp.where(tri, 1.0, 0.0).astype(BF16)
    lc = _dot(tri, la_hi) + _dot(tri, la_lo)
    q = q_ref[...].astype(F32)
    k = k_ref[...].astype(F32)
    qd = (q * jnp.exp(lc) * GLA_SCALE).astype(BF16)
    kd = (k * jnp.exp(-lc)).astype(BF16)
    hrow = lax.broadcasted_iota(jnp.int32, (GLA_HEADS * c, hk), 0) // c
    hcol = lax.broadcasted_iota(jnp.int32, (GLA_HEADS * c, hk), 1) // GLA_DK
    head_mask = hrow == hcol
    trow = lax.broadcasted_iota(jnp.int32, (GLA_HEADS * c, c), 0) % c
    scol = lax.broadcasted_iota(jnp.int32, (GLA_HEADS * c, c), 1)
    causal = (scol >= trow) if rev else (scol <= trow)
    edge = 0 if rev else c - 1
    lasts = [lc[i * c + edge:i * c + edge + 1] for i in range(n_sub)]
    to_end = jnp.concatenate([jnp.broadcast_to(l, (c, hk)) for l in lasts], axis=0) - lc
    kh_t = (k * jnp.exp(to_end)).T
    pad = jnp.zeros((LANES - n_sub, hk), F32)
    decay_t = jnp.exp(jnp.concatenate(lasts + [pad], axis=0).T)
    yield
    chunk_of = lax.broadcasted_iota(jnp.int32, (1, t_rows), 1) // c
    kv = []
    for h in range(GLA_HEADS):
        kh_h = kh_t[h * GLA_DK:(h + 1) * GLA_DK]
        stack = jnp.concatenate([jnp.where(chunk_of == i, kh_h, 0.0) for i in range(n_sub)], axis=0)
        kv.append(_dot(stack.astype(BF16), v_ref[:, h * GLA_DV:(h + 1) * GLA_DV]))
    yield
    order = range(n_sub - 1, -1, -1) if rev else range(n_sub)
    state = s_ref[...]
    state_at = {}
    for i in order:
        state_at[i] = state
        inc = jnp.concatenate([kv[h][i * GLA_DK:(i + 1) * GLA_DK] for h in range(GLA_HEADS)], axis=0)
        state = decay_t[:, i:i + 1] * state + inc
    s_ref[...] = state
    yield
    inters, scoress = {}, {}
    for i in order:
        rows = slice(i * c, (i + 1) * c)
        q_stack = jnp.where(head_mask, jnp.concatenate([qd[rows]] * GLA_HEADS, axis=0), 0.0).astype(BF16)
        inters[i] = _dot(q_stack, state_at[i].astype(BF16))
        scoress[i] = jnp.where(causal, _dot_nt(q_stack, kd[rows]), 0.0).astype(BF16)
    for i in order:
        yield
        rows = slice(i * c, (i + 1) * c)
        v_i = v_ref[rows, :]
        inter, scores = inters[i], scoress[i]
        outs = []
        for h in range(GLA_HEADS):
            hr = slice(h * c, (h + 1) * c)
            hv = slice(h * GLA_DV, (h + 1) * GLA_DV)
            outs.append(inter[hr] + _dot(scores[hr], v_i[:, hv]))
        o_ref[rows, :] = jnp.concatenate(outs, axis=1).astype(BF16)


def _gla_kernel(qf_ref, kf_ref, vf_ref, laf_ref, qb_ref, kb_ref, vb_ref, lab_ref,
                of_ref, ob_ref, sf_ref, sb_ref):
    @pl.when(pl.program_id(1) == 0)
    def _():
        sf_ref[...] = jnp.zeros_like(sf_ref)
        sb_ref[...] = jnp.zeros_like(sb_ref)

    live = [_gla_direction(qf_ref, kf_ref, vf_ref, laf_ref, of_ref, sf_ref, False),
            _gla_direction(qb_ref, kb_ref, vb_ref, lab_ref, ob_ref, sb_ref, True)]
    while live:
        live = [g for g in live if next(g, "done") != "done"]


def _gla(lay, q_a, k_a, v_a, la_f, la_b):
    tm = lay.tile
    nc, nl = lay.ctx_tiles, lay.lat_tiles

    def fwd(b, j):
        return jnp.where(j < nc, lay.n_lat + b * nc + j, b * nl + (j - nc))

    def bwd(b, j):
        return jnp.where(j < nc, lay.n_lat + b * nc + (nc - 1 - j), b * nl + (nl - 1 - (j - nc)))

    spec = lambda w, f: pl.BlockSpec((tm, w), lambda b, j: (f(b, j), 0))
    out = jax.ShapeDtypeStruct((lay.n_tok, GLA_V_W), BF16)
    return pl.pallas_call(
        _gla_kernel,
        out_shape=(out, out),
        grid=(lay.batch, nc + nl),
        in_specs=[spec(GLA_QK_W, fwd), spec(GLA_QK_W, fwd), spec(GLA_V_W, fwd), spec(GLA_QK_W, fwd),
                  spec(GLA_QK_W, bwd), spec(GLA_QK_W, bwd), spec(GLA_V_W, bwd), spec(GLA_QK_W, bwd)],
        out_specs=(spec(GLA_V_W, fwd), spec(GLA_V_W, bwd)),
        scratch_shapes=[pltpu.VMEM((GLA_QK_W, GLA_DV), F32), pltpu.VMEM((GLA_QK_W, GLA_DV), F32)],
        compiler_params=_params(("parallel", "arbitrary")),
        name="gla",
    )(q_a, k_a, v_a, la_f, q_a, k_a, v_a, la_b)


def _dft_kernel(c_ref, s_ref, a_ref, b_ref, o_ref, acc_ref, *, scale):
    k = pl.program_id(1)

    @pl.when(k == 0)
    def _():
        acc_ref[...] = jnp.zeros_like(acc_ref)

    acc_ref[...] += _dot(c_ref[...], a_ref[...]) + _dot(s_ref[...], b_ref[...])

    @pl.when(k == pl.num_programs(1) - 1)
    def _():
        o_ref[...] = (acc_ref[...] * scale).astype(BF16)


def _dft_mats(n):
    f = GRID_W
    assert n % f == 0
    k = jnp.arange(n, dtype=jnp.int32)[None, :]

    def table(rows, period):
        ang = ((jnp.arange(rows, dtype=jnp.int32)[:, None] * k) % period).astype(F32) * (2.0 * math.pi / period)
        return jnp.cos(ang), jnp.sin(ang)

    ca, sa = table(n // f, n // f)
    cb, sb = table(f, n)
    cos = ca[:, None, :] * cb[None, :, :] - sa[:, None, :] * sb[None, :, :]
    sin = sa[:, None, :] * cb[None, :, :] + ca[:, None, :] * sb[None, :, :]
    return cos.reshape(n, n).astype(BF16), (-sin).reshape(n, n).astype(BF16)


def _dft_half_kernel(c_ref, s_ref, cx_ref, sx_ref, a_ref, b_ref, j_ref, o_ref, af_ref, bf_ref, *, scale):
    it = pl.program_id(1)
    tm = o_ref.shape[1]
    length = a_ref.shape[0]
    half = length // 2
    sub = 16
    jm = j_ref[...]
    row = lax.broadcasted_iota(jnp.int32, (tm, 1), 0)

    @pl.when(it == 0)
    def _():
        for m in range(half // tm):
            lo = slice(tm * m, tm * (m + 1))
            hi = slice(length - tm * (m + 1), length - tm * m)
            for src, dst, sign in ((a_ref, af_ref, 1.0), (b_ref, bf_ref, -1.0)):
                mirrored = _dot(jm, src[hi, :])
                if m > 0:
                    edge = src[length - tm * m:length - tm * m + sub, :][0:1, :].astype(F32)
                    mirrored = jnp.where(row == 0, edge, mirrored)
                dst[lo, :] = (src[lo, :].astype(F32) + sign * mirrored).astype(BF16)

    af = af_ref[...]
    bf = bf_ref[...]
    nyquist = a_ref[half:half + sub, :][0:1, :].astype(F32)
    p = _dot(c_ref[...], af) + jnp.where((row & 1) == 0, nyquist, -nyquist)
    q = _dot(s_ref[...], bf)
    o_ref[0] = ((p + q) * scale).astype(BF16)
    mirror = ((p - q) * scale).astype(BF16)
    flipped = _dot(jm, mirror)
    first = ((_dot(cx_ref[...], af) - _dot(sx_ref[...], bf))[0:1, :] + nyquist) * scale
    o_ref[1] = jnp.where(row == 0, first, flipped).astype(BF16)


def _dft_half_mats(n, tm):
    f = GRID_W
    half = n // 2
    assert half % f == 0 and half % tm == 0
    k = jnp.arange(half, dtype=jnp.int32)[None, :]

    def table(rows, period):
        ang = ((rows[:, None] * k) % period).astype(F32) * (2.0 * math.pi / period)
        return jnp.cos(ang), jnp.sin(ang)

    ca, sa = table(jnp.arange(half // f, dtype=jnp.int32), n // f)
    cb, sb = table(jnp.arange(f, dtype=jnp.int32), n)
    cos = ca[:, None, :] * cb[None, :, :] - sa[:, None, :] * sb[None, :, :]
    sin = sa[:, None, :] * cb[None, :, :] + ca[:, None, :] * sb[None, :, :]
    n_it = half // tm
    cx, sx = table(jnp.arange(1, n_it + 1, dtype=jnp.int32) * tm, n)
    spread = lambda m: jnp.zeros((n_it, 8, half), F32).at[:, 0, :].set(m).reshape(n_it * 8, half).astype(BF16)
    return (cos.reshape(half, half).astype(BF16), (-sin).reshape(half, half).astype(BF16), spread(cx), spread(-sx))


def _fourier_half(ua, us, mats, length, tm):
    cmat, smat, cx, sx = mats
    width = ua.shape[1]
    half = length // 2
    scale = 1.0 / math.sqrt(length * FNET_GROUP_W)
    r = lax.broadcasted_iota(jnp.int32, (tm, tm), 0)
    c = lax.broadcasted_iota(jnp.int32, (tm, tm), 1)
    jmat = jnp.where(c == tm - r, 1.0, 0.0).astype(BF16)
    once = pl.Buffered(1)
    cols = width // FOURIER_COL_SPLIT
    return pl.pallas_call(
        functools.partial(_dft_half_kernel, scale=scale),
        out_shape=jax.ShapeDtypeStruct((2, half, width), BF16),
        grid=(FOURIER_COL_SPLIT, half // tm),
        in_specs=[pl.BlockSpec((tm, half), lambda j, i: (i, 0)),
                  pl.BlockSpec((tm, half), lambda j, i: (i, 0)),
                  pl.BlockSpec((8, half), lambda j, i: (i, 0)),
                  pl.BlockSpec((8, half), lambda j, i: (i, 0)),
                  pl.BlockSpec((length, cols), lambda j, i: (0, j), pipeline_mode=once),
                  pl.BlockSpec((length, cols), lambda j, i: (0, j), pipeline_mode=once),
                  pl.BlockSpec((tm, tm), lambda j, i: (0, 0), pipeline_mode=once)],
        out_specs=pl.BlockSpec((2, tm, cols), lambda j, i: (0, i, j)),
        scratch_shapes=[pltpu.VMEM((half, cols), BF16), pltpu.VMEM((half, cols), BF16)],
        compiler_params=_params(("arbitrary", "arbitrary")),
        name="fourier_half",
    )(cmat, smat, cx, sx, ua, us, jmat)


def _fourier(ua, us, cmat, smat, row0, length, tm, tk):
    width = ua.shape[1]
    off = row0 // tk
    scale = 1.0 / math.sqrt(length * FNET_GROUP_W)
    return pl.pallas_call(
        functools.partial(_dft_kernel, scale=scale),
        out_shape=jax.ShapeDtypeStruct((length, width), BF16),
        grid=(length // tm, length // tk),
        in_specs=[pl.BlockSpec((tm, tk), lambda i, k: (i, k)),
                  pl.BlockSpec((tm, tk), lambda i, k: (i, k)),
                  pl.BlockSpec((tk, width), lambda i, k: (off + k, 0)),
                  pl.BlockSpec((tk, width), lambda i, k: (off + k, 0))],
        out_specs=pl.BlockSpec((tm, width), lambda i, k: (i, 0)),
        scratch_shapes=[pltpu.VMEM((tm, width), F32)],
        compiler_params=_params(("parallel", "arbitrary")),
        name="fourier",
    )(cmat, smat, ua, us)


def _attend(q_ref, k_all, v_all, valid, sink_ref, o_ref):
    hd = SWA_HEAD_DIM
    heads_per_kv = SWA_HEADS // SWA_KV_HEADS
    k_sw = jnp.concatenate([k_all[:, hd:], k_all[:, :hd]], axis=1)
    v_t = v_all.astype(F32).T
    v_t_sw = jnp.concatenate([v_t[hd:], v_t[:hd]], axis=0)
    lane = lax.broadcasted_iota(jnp.int32, (1, LANES), 1)
    lane_half = (lane < hd, lane >= hd)
    row = lax.broadcasted_iota(jnp.int32, (LANES, 1), 0)
    row_half = (row < hd, row >= hd)
    zero = jnp.zeros((), BF16)
    for p in range(SWA_HEADS // 2):
        acc = None
        for par in range(2):
            head = 2 * p + par
            aligned = head // heads_per_kv == par
            qh = jnp.where(lane_half[par], q_ref[:, p * LANES:(p + 1) * LANES], zero)
            s = _dot_nt(k_all if aligned else k_sw, qh)
            if valid is not None:
                kw = valid.shape[0]
                s = jnp.concatenate([jnp.where(valid, s[:kw], -jnp.inf), s[kw:]], axis=0)
            sink = sink_ref[head:head + 1, 0:1] * LOG2E
            m = jnp.maximum(jnp.max(s, axis=0, keepdims=True), sink)
            e = jnp.exp2(s - m)
            denom = jnp.sum(e, axis=0, keepdims=True) + jnp.exp2(sink - m)
            v_use = jnp.where(row_half[par], v_t if aligned else v_t_sw, 0.0).astype(BF16)
            part = _dot(v_use, e.astype(BF16)) * (1.0 / denom)
            acc = part if acc is None else acc + part
        o_ref[:, p * LANES:(p + 1) * LANES] = acc.T.astype(BF16)


def _swa_kernel(q_ref, kp_ref, kc_ref, kn_ref, kx_ref, vp_ref, vc_ref, vn_ref, vx_ref, sink_ref, o_ref,
                *, seq):
    n = pl.program_id(1)
    w = WINDOW
    k_blocks = [kp_ref[...], kc_ref[0:w, :], kc_ref[w:2 * w, :], kn_ref[...]]
    v_blocks = [vp_ref[...], vc_ref[0:w, :], vc_ref[w:2 * w, :], vn_ref[...]]
    j = lax.broadcasted_iota(jnp.int32, (3 * w, w), 0)
    a = lax.broadcasted_iota(jnp.int32, (3 * w, w), 1)
    band = (j >= a) & (j - a <= 2 * w)
    for half in range(2):
        key_pos = (2 * n + half - 1) * w + j
        valid = band & (key_pos >= 0) & (key_pos < seq)
        k_all = jnp.concatenate(k_blocks[half:half + 3] + [kx_ref[...]], axis=0)
        v_all = jnp.concatenate(v_blocks[half:half + 3] + [vx_ref[...]], axis=0)
        rows = pl.ds(half * w, w)
        _attend(q_ref.at[rows], k_all, v_all, valid, sink_ref, o_ref.at[rows])


def _swa(lay, q_c, k_c, v_c, sink_b, layer):
    w = WINDOW
    assert lay.seq % (2 * w) == 0
    nq = lay.seq // w
    steps = nq // 2
    ctx_blk = (lay.batch * lay.seq) // lay.ctx

    def edge(f):
        return pl.BlockSpec((w, SWA_KV_W), lambda b, n: (b * nq + f(n), 0))

    prev = edge(lambda n: jnp.maximum(2 * n - 1, 0))
    nxt = edge(lambda n: jnp.minimum(2 * n + 2, nq - 1))
    cur = pl.BlockSpec((2 * w, SWA_KV_W), lambda b, n: (b * steps + n, 0))
    ctx_spec = pl.BlockSpec((lay.ctx, SWA_KV_W), lambda b, n: (ctx_blk + b, 0))
    q_spec = pl.BlockSpec((2 * w, SWA_Q_W), lambda b, n: (b * steps + n, 0))
    return pl.pallas_call(
        functools.partial(_swa_kernel, seq=lay.seq),
        out_shape=jax.ShapeDtypeStruct((lay.batch * lay.seq, SWA_Q_W), BF16),
        grid=(lay.batch, steps),
        in_specs=[q_spec, prev, cur, nxt, ctx_spec, prev, cur, nxt, ctx_spec,
                  _layer_block(sink_b, layer)],
        out_specs=q_spec,
        compiler_params=_params(("parallel", "parallel")),
        name="swa",
    )(q_c, k_c, k_c, k_c, k_c, v_c, v_c, v_c, v_c, sink_b)


def _ctx_attn_kernel(q_ref, kx_ref, vx_ref, sink_ref, o_ref):
    _attend(q_ref, kx_ref[...], vx_ref[...], None, sink_ref, o_ref)


def _ctx_attn(lay, q_c, k_c, v_c, sink_b, layer):
    ctx_blk = (lay.batch * lay.seq) // lay.ctx
    spec = lambda wd: pl.BlockSpec((lay.ctx, wd), lambda b: (ctx_blk + b, 0))
    return pl.pallas_call(
        _ctx_attn_kernel,
        out_shape=jax.ShapeDtypeStruct((lay.batch * lay.ctx, SWA_Q_W), BF16),
        grid=(lay.batch,),
        in_specs=[spec(SWA_Q_W), spec(SWA_KV_W), spec(SWA_KV_W),
                  _layer_block(sink_b, layer)],
        out_specs=pl.BlockSpec((lay.ctx, SWA_Q_W), lambda b: (b, 0)),
        compiler_params=_params(("parallel",)),
        name="ctx_attn",
    )(q_c, k_c, v_c, sink_b)


def _merge_kernel(*refs, n_lat, n_x):
    _merge_body(_read_stream(refs[:n_x], n_lat), *refs[n_x:], n_lat=n_lat)


def _merge_body(x, of_ref, ob_ref, ra_ref, ybl_ref, ybc_ref, ycl_ref, ycc_ref, gl_ref, mod_ref,
                ggla_ref, gffn_ref, wpa_ref, wpb_ref, wpc_ref, wout_ref, wr_ref,
                x1_ref, h2_ref, gates_ref, count_ref, *, n_lat):
    d = D_MODEL
    is_ctx = pl.program_id(0) >= n_lat
    o = of_ref[...].astype(F32) + ob_ref[...].astype(F32)
    r = ra_ref[...].astype(F32)
    parts = []
    for h in range(GLA_HEADS):
        sl = slice(h * GLA_DV, (h + 1) * GLA_DV)
        oh = o[:, sl]
        parts.append(oh * lax.rsqrt(jnp.mean(oh * oh, axis=-1, keepdims=True) + EPS) * ggla_ref[...])
    y_a = (jnp.concatenate(parts, axis=1) * r).astype(BF16)
    y_b = jnp.where(is_ctx, ybc_ref[...], ybl_ref[...])
    y_c = jnp.where(is_ctx, ycc_ref[...], ycl_ref[...])
    gl = gl_ref[...].astype(F32)
    mix = (gl[:, 0:d] * _dot(y_a, wpa_ref[...])
           + gl[:, d:2 * d] * _dot(y_b, wpb_ref[...])
           + gl[:, 2 * d:3 * d] * _dot(y_c, wpc_ref[...]))
    y = _dot(mix.astype(BF16), wout_ref[...])
    x1 = x + mod_ref[:, 2 * d:3 * d] * y
    x1_ref[...] = x1
    h2 = x1 * lax.rsqrt(jnp.mean(x1 * x1, axis=-1, keepdims=True) + EPS) * gffn_ref[...]
    h2 = h2 * (1.0 + mod_ref[:, 4 * d:5 * d]) + mod_ref[:, 3 * d:4 * d]
    h2_hi = h2.astype(BF16)
    h2_ref[...] = h2_hi
    h2_lo = (h2 - h2_hi.astype(F32)).astype(BF16)
    both = _dot(h2_hi, wr_ref[...])
    logits = both[:, :ROUTER_PAD] + both[:, ROUTER_PAD:] + _dot(h2_lo, wr_ref[:, :ROUTER_PAD])
    lt = logits.T
    sub = lax.broadcasted_iota(jnp.int32, lt.shape, 0)
    sub_f = sub.astype(F32)
    neg = -jnp.inf
    big = float(ROUTER_PAD)
    is_group = (sub >= MOE_EXPERTS) & (sub < MOE_EXPERTS + MOE_GROUPS)
    gl_m = jnp.where(is_group, lt, neg)
    g_max = jnp.max(gl_m, axis=0, keepdims=True)
    g_sel = jnp.min(jnp.where(gl_m == g_max, sub_f, big), axis=0, keepdims=True) - MOE_EXPERTS
    g_gate = 1.0 / jnp.sum(jnp.where(is_group, jnp.exp(lt - g_max), 0.0), axis=0, keepdims=True)
    lo = g_sel * MOE_EXPERTS_PER_GROUP
    in_group = (sub_f >= lo) & (sub_f < lo + MOE_EXPERTS_PER_GROUP)
    e1 = jnp.where(in_group, lt, neg)
    v1 = jnp.max(e1, axis=0, keepdims=True)
    i1 = jnp.min(jnp.where(e1 == v1, sub_f, big), axis=0, keepdims=True)
    e2 = jnp.where(sub_f == i1, neg, e1)
    v2 = jnp.max(e2, axis=0, keepdims=True)
    i2 = jnp.min(jnp.where(e2 == v2, sub_f, big), axis=0, keepdims=True)
    t = jnp.exp(v2 - v1)
    w1 = g_gate / (1.0 + t)
    w2 = g_gate * t / (1.0 + t)
    route_t = jnp.where(sub == ROUTE_ID, i1, 0.0) + jnp.where(sub == ROUTE_ID + 1, i2, 0.0)
    route_t = route_t + jnp.where(sub == ROUTE_W, w1, 0.0) + jnp.where(sub == ROUTE_W + 1, w2, 0.0)
    route = route_t.T
    gates_ref[...] = route
    lane_f = lax.broadcasted_iota(jnp.int32, route.shape, 1).astype(F32)
    hit = jnp.where((lane_f == route[:, ROUTE_ID:ROUTE_ID + 1]) | (lane_f == route[:, ROUTE_ID + 1:ROUTE_ID + 2]),
                    1.0, 0.0)
    count_ref[...] = jnp.sum(hit, axis=0, keepdims=True).astype(jnp.int32)


def _merge(lay, x, o_f, o_b, r_a, yb_lat, fourier_tile, yb_ctx, yc_lat, yc_ctx, gate_logits, mods, layer, g_gla, g_ffn,
           w_pa, w_pb, w_pc, w_out, w_router):
    tm, d = lay.tile, D_MODEL
    n = lay.n_tok
    tok = lambda w: pl.BlockSpec((tm, w), lambda t: (t, 0))
    const = lambda a: pl.BlockSpec(a.shape, lambda t: (0,) * a.ndim)
    lt, ct = lay.lat_tiles, lay.ctx_tiles
    sub = fourier_tile // tm

    def yb_lat_index(t):
        tl = jnp.minimum(t, lay.n_lat - 1)
        s = tl % lt
        u = lt - 1 - s
        upper = s >= lt // 2
        blk = jnp.where(upper, (u // sub) * sub + sub - 1 - u % sub, s)
        return (upper.astype(jnp.int32), blk, tl // lt)

    yb_lat_spec = pl.BlockSpec((None, tm, FNET_W), yb_lat_index)
    yb_ctx_spec = pl.BlockSpec((tm, FNET_W), lambda t: (jnp.maximum(t - lay.n_lat, 0) % ct,
                                                         jnp.maximum(t - lay.n_lat, 0) // ct))
    yc_lat_spec = pl.BlockSpec((tm, SWA_Q_W), lambda t: (jnp.minimum(t, lay.n_lat - 1), 0))
    yc_ctx_spec = pl.BlockSpec((tm, SWA_Q_W), lambda t: (jnp.maximum(t - lay.n_lat, 0), 0))
    return pl.pallas_call(
        functools.partial(_merge_kernel, n_lat=lay.n_lat, n_x=len(x[0])),
        out_shape=(jax.ShapeDtypeStruct((n, d), F32), jax.ShapeDtypeStruct((n, d), BF16),
                   jax.ShapeDtypeStruct((n, ROUTER_PAD), F32),
                   jax.ShapeDtypeStruct((lay.n_tiles, 1, ROUTER_PAD), jnp.int32)),
        grid=(lay.n_tiles,),
        in_specs=lay.stream_specs(d, x[1]) + [tok(GLA_V_W), tok(GLA_V_W), tok(GLA_V_W), yb_lat_spec, yb_ctx_spec,
                  yc_lat_spec, yc_ctx_spec, tok(3 * d),
                  pl.BlockSpec((None, None, 1, N_MOD * d), lambda t: (layer, lay.mod_row(t), 0, 0)),
                  _layer_block(g_gla, layer), _layer_block(g_ffn, layer), _layer_block(w_pa, layer),
                  _layer_block(w_pb, layer), _layer_block(w_pc, layer), _layer_block(w_out, layer),
                  _layer_block(w_router, layer)],
        out_specs=(tok(d), tok(d), tok(ROUTER_PAD),
                   pl.BlockSpec((None, 1, ROUTER_PAD), lambda t: (t, 0, 0))),
        compiler_params=_params(("parallel",)),
        name="merge",
    )(*x[0], o_f, o_b, r_a, yb_lat, yb_ctx, yc_lat, yc_ctx, gate_logits, mods, g_gla, g_ffn,
      w_pa, w_pb, w_pc, w_out, w_router)


def _route_ids(route, axis):
    take = (lambda i: route[:, i:i + 1]) if axis == 1 else (lambda i: route[i:i + 1, :])
    return take(ROUTE_ID).astype(jnp.int32), take(ROUTE_ID + 1).astype(jnp.int32)


def _moe_plan(counts, n_blocks):
    cnt = counts
    pc = (cnt + ROW_CHUNK - 1) // ROW_CHUNK * ROW_CHUNK
    lstart = jnp.cumsum(pc, axis=1) - pc
    tot = jnp.sum(pc, axis=0)
    tot_pad = (tot + FFN_BLOCK - 1) // FFN_BLOCK * FFN_BLOCK
    eend = jnp.cumsum(tot_pad)
    estart = eend - tot_pad
    base = estart[None, :] + jnp.cumsum(pc, axis=0) - pc
    n_used = eend[-1] // FFN_BLOCK
    blk = jnp.minimum(jnp.arange(n_blocks, dtype=jnp.int32), n_used - 1)
    bexp = jnp.sum((blk[:, None] * FFN_BLOCK >= eend[None, :]).astype(jnp.int32), axis=1)
    flat = lambda a: a.reshape(-1).astype(jnp.int32)
    nch = pc // ROW_CHUNK
    per_big = BIG_CHUNK // ROW_CHUNK
    nch = jnp.concatenate([flat(nch), flat(jnp.sum(nch // per_big, axis=1)), flat(jnp.sum(nch % per_big, axis=1))])
    return dict(base=flat(base), lstart=flat(lstart), nch=nch,
                gap0=flat(estart + tot), gapn=flat((tot_pad - tot) // ROW_CHUNK),
                bexp=flat(bexp), nused=flat(n_used))


def _segment_copies(t, base_ref, lstart_ref, nch_ref, make, start):
    per_big = BIG_CHUNK // ROW_CHUNK
    if not start:
        n_tiles = nch_ref.shape[0] // (MOE_EXPERTS + 2)

        def wait(rows):
            def body(i, c):
                make(0, 0, rows).wait()
                return c
            return body

        lax.fori_loop(0, nch_ref[n_tiles * MOE_EXPERTS + t], wait(BIG_CHUNK), 0)
        lax.fori_loop(0, nch_ref[n_tiles * (MOE_EXPERTS + 1) + t], wait(ROW_CHUNK), 0)
        return

    def per_expert(e, carry):
        idx = t * MOE_EXPERTS + e
        loc = lstart_ref[idx]
        glob = base_ref[idx]
        n_big = nch_ref[idx] // per_big
        n_small = nch_ref[idx] - n_big * per_big

        def piece(rows, first):
            def body(i, c):
                off = first + i * rows
                cp = make(pl.multiple_of(loc + off, ROW_CHUNK), pl.multiple_of(glob + off, ROW_CHUNK), rows)
                cp.start() if start else cp.wait()
                return c
            return body

        carry = lax.fori_loop(0, n_big, piece(BIG_CHUNK, 0), carry)
        return lax.fori_loop(0, n_small, piece(ROW_CHUNK, n_big * BIG_CHUNK), carry)

    lax.fori_loop(0, MOE_EXPERTS, per_expert, 0)


def _dispatch_kernel(base_ref, lstart_ref, nch_ref, gap0_ref, gapn_ref, nused_ref, h_ref, route_ref, xs_ref,
                     buf_ref, sem):
    t = pl.program_id(0)
    tile = h_ref.shape[0]
    slots = buf_ref.shape[1]
    rt = route_ref[...].T
    e1, e2 = _route_ids(rt, 0)
    sub = lax.broadcasted_iota(jnp.int32, rt.shape, 0)
    oh1, oh2 = sub == e1, sub == e2
    hit = jnp.where(oh1 | oh2, 1.0, 0.0).astype(BF16)
    before = (lax.broadcasted_iota(jnp.int32, (tile, tile), 0)
              < lax.broadcasted_iota(jnp.int32, (tile, tile), 1))
    rank = _dot(hit, jnp.where(before, 1.0, 0.0).astype(BF16))
    sub1 = lax.broadcasted_iota(jnp.int32, (rt.shape[0], 1), 0)
    seg = jnp.zeros((rt.shape[0], 1), F32)
    for e in range(MOE_EXPERTS):
        seg = jnp.where(sub1 == e, lstart_ref[t * MOE_EXPERTS + e].astype(F32), seg)
    slot_of = rank + seg
    pos1 = jnp.sum(jnp.where(oh1, slot_of, 0.0), axis=0, keepdims=True).astype(jnp.int32)
    pos2 = jnp.sum(jnp.where(oh2, slot_of, 0.0), axis=0, keepdims=True).astype(jnp.int32)
    slot = lax.broadcasted_iota(jnp.int32, (slots, tile), 0)
    perm = jnp.where((slot == pos1) | (slot == pos2), 1.0, 0.0).astype(BF16)
    cur = t % 2
    buf_ref[cur] = _dot(perm, h_ref[...]).astype(BF16)

    def maker(b):
        def make(loc, glob, rows):
            return pltpu.make_async_copy(buf_ref.at[b, pl.ds(loc, rows)], xs_ref.at[pl.ds(glob, rows)], sem.at[b])
        return make

    _segment_copies(t, base_ref, lstart_ref, nch_ref, maker(cur), True)

    @pl.when(t > 0)
    def _():
        _segment_copies(t - 1, base_ref, lstart_ref, nch_ref, maker(1 - cur), False)

    @pl.when(t == pl.num_programs(0) - 1)
    def _():
        _segment_copies(t, base_ref, lstart_ref, nch_ref, maker(cur), False)
        buf_ref[cur, 0:FFN_BLOCK, :] = jnp.zeros((FFN_BLOCK, buf_ref.shape[2]), BF16)
        n_blocks = xs_ref.shape[0] // FFN_BLOCK

        def fill(start):
            def per_expert(e, carry):
                def chunk(i, c):
                    row = pl.multiple_of(gap0_ref[e] + i * ROW_CHUNK, ROW_CHUNK)
                    cp = pltpu.make_async_copy(buf_ref.at[cur, 0:ROW_CHUNK], xs_ref.at[pl.ds(row, ROW_CHUNK)],
                                               sem.at[cur])
                    cp.start() if start else cp.wait()
                    return c
                return lax.fori_loop(0, gapn_ref[e], chunk, carry)

            def tail(b, c):
                row = pl.multiple_of(b * FFN_BLOCK, FFN_BLOCK)
                cp = pltpu.make_async_copy(buf_ref.at[cur, 0:FFN_BLOCK], xs_ref.at[pl.ds(row, FFN_BLOCK)],
                                           sem.at[cur])
                cp.start() if start else cp.wait()
                return c

            lax.fori_loop(0, MOE_EXPERTS, per_expert, 0)
            lax.fori_loop(nused_ref[0], n_blocks, tail, 0)

        fill(True)
        fill(False)


def _dispatch(plan, h2, route, n_rows):
    tile, d = DISPATCH_TILE, D_MODEL
    n_tiles = h2.shape[0] // tile
    return pl.pallas_call(
        _dispatch_kernel,
        out_shape=jax.ShapeDtypeStruct((n_rows, d), BF16),
        grid_spec=pltpu.PrefetchScalarGridSpec(
            num_scalar_prefetch=6,
            grid=(n_tiles,),
            in_specs=[pl.BlockSpec((tile, d), lambda t, *_: (t, 0)),
                      pl.BlockSpec((tile, ROUTER_PAD), lambda t, *_: (t, 0))],
            out_specs=pl.BlockSpec(memory_space=pl.ANY),
            scratch_shapes=[pltpu.VMEM((2, SLOT_ROWS, d), BF16), pltpu.SemaphoreType.DMA((2,))],
        ),
        compiler_params=_params(("arbitrary",)),
        name="moe_dispatch",
    )(plan["base"], plan["lstart"], plan["nch"], plan["gap0"], plan["gapn"], plan["nused"], h2, route)


def _ffn_kernel(bexp_ref, nused_ref, x_ref, w1_ref, w3_ref, w2_ref, y_ref, w13_s, w2_s):
    b = pl.program_id(0)
    used = b < nused_ref[0]
    de = D_EXPERT

    @pl.when(used & ((b == 0) | (bexp_ref[b] != bexp_ref[jnp.maximum(b - 1, 0)])))
    def _():
        w13_s[:, :de] = w1_ref[...].astype(BF16)
        w13_s[:, de:] = w3_ref[...].astype(BF16)
        w2_s[...] = w2_ref[...].astype(BF16)

    @pl.when(used)
    def _():
        ab = _dot(x_ref[...], w13_s[...])
        a = ab[:, :de]
        hid = (a * _sigmoid(a)) * ab[:, de:]
        y_ref[...] = _dot(hid.astype(BF16), w2_s[...]).astype(BF16)

    @pl.when(jnp.logical_not(used))
    def _():
        y_ref[...] = jnp.zeros_like(y_ref)


def _expert_ffn(plan, xs, layer, w1, w3, w2):
    d = D_MODEL
    n_blocks = xs.shape[0] // FFN_BLOCK
    row = lambda b, bexp, nused: (jnp.minimum(b, nused[0] - 1), 0)
    wsel = lambda b, bexp, nused: (layer, bexp[b], 0, 0)
    return pl.pallas_call(
        _ffn_kernel,
        out_shape=jax.ShapeDtypeStruct(xs.shape, BF16),
        grid_spec=pltpu.PrefetchScalarGridSpec(
            num_scalar_prefetch=2,
            grid=(n_blocks,),
            in_specs=[pl.BlockSpec((FFN_BLOCK, d), row),
                      pl.BlockSpec((None, None, d, D_EXPERT), wsel),
                      pl.BlockSpec((None, None, d, D_EXPERT), wsel),
                      pl.BlockSpec((None, None, D_EXPERT, d), wsel)],
            out_specs=pl.BlockSpec((FFN_BLOCK, d), lambda b, bexp, nused: (b, 0)),
            scratch_shapes=[pltpu.VMEM((d, 2 * D_EXPERT), BF16), pltpu.VMEM((D_EXPERT, d), BF16)],
        ),
        compiler_params=_params(("arbitrary",)),
        name="moe_ffn",
    )(plan["bexp"], plan["nused"], xs, w1, w3, w2)


def _combine_kernel(base_ref, lstart_ref, nch_ref, route_ref, x1_ref, mod_ref, ys_ref, *rest, n_out_tiles):
    g_ref = rest[0] if len(rest) == 4 else None
    o_ref, buf_ref, sem = rest[-3:]
    d = D_MODEL
    t = pl.program_id(0)
    tile = route_ref.shape[0]
    slots = buf_ref.shape[1]
    cur = t % 2

    def maker(b):
        def make(loc, glob, rows):
            return pltpu.make_async_copy(ys_ref.at[pl.ds(glob, rows)], buf_ref.at[b, pl.ds(loc, rows)], sem.at[b])
        return make

    @pl.when(t == 0)
    def _():
        buf_ref[...] = jnp.zeros_like(buf_ref)
        _segment_copies(t, base_ref, lstart_ref, nch_ref, maker(cur), True)

    @pl.when(t + 1 < pl.num_programs(0))
    def _():
        _segment_copies(t + 1, base_ref, lstart_ref, nch_ref, maker(1 - cur), True)

    route = route_ref[...]
    e1, e2 = _route_ids(route, 1)
    lane = lax.broadcasted_iota(jnp.int32, route.shape, 1)
    oh1, oh2 = lane == e1, lane == e2
    hit = jnp.where(oh1 | oh2, 1.0, 0.0).astype(BF16)
    before = (lax.broadcasted_iota(jnp.int32, (tile, tile), 1)
              < lax.broadcasted_iota(jnp.int32, (tile, tile), 0))
    rank = _dot(jnp.where(before, 1.0, 0.0).astype(BF16), hit)
    lane1 = lax.broadcasted_iota(jnp.int32, (1, route.shape[1]), 1)
    seg = jnp.zeros((1, route.shape[1]), F32)
    for e in range(MOE_EXPERTS):
        seg = jnp.where(lane1 == e, lstart_ref[t * MOE_EXPERTS + e].astype(F32), seg)
    slot_of = rank + seg
    pos1 = jnp.sum(jnp.where(oh1, slot_of, 0.0), axis=1, keepdims=True).astype(jnp.int32)
    pos2 = jnp.sum(jnp.where(oh2, slot_of, 0.0), axis=1, keepdims=True).astype(jnp.int32)
    slot = lax.broadcasted_iota(jnp.int32, (tile, slots), 1)
    w1 = route[:, ROUTE_W:ROUTE_W + 1]
    w2 = route[:, ROUTE_W + 1:ROUTE_W + 2]
    comb = (jnp.where(slot == pos1, w1, 0.0) + jnp.where(slot == pos2, w2, 0.0)).astype(BF16)
    _segment_copies(t, base_ref, lstart_ref, nch_ref, maker(cur), False)
    moe = _dot(comb, buf_ref[cur])
    x2 = x1_ref[...] + mod_ref[:, 5 * d:6 * d] * moe
    if g_ref is None:
        o_ref[...] = x2
    else:
        @pl.when(t < n_out_tiles)
        def _():
            o_ref[...] = x2 * lax.rsqrt(jnp.mean(x2 * x2, axis=-1, keepdims=True) + EPS) * g_ref[...]


def _combine(lay, plan, route, x1, mods, layer, ys, g_final):
    tile, d = DISPATCH_TILE, D_MODEL
    assert lay.seq % tile == 0 and (lay.batch * lay.ctx) % tile == 0
    lat_tiles = lay.seq // tile
    n_lat = lay.batch * lat_tiles
    n_tiles = lay.n_tok // tile
    row = lambda t: jnp.where(t < n_lat, t // lat_tiles, lay.batch)
    final = g_final is not None
    n_out_tiles = n_lat if final else n_tiles
    extra_specs = [pl.BlockSpec((1, d), lambda t, *_: (0, 0))] if final else []
    extra_args = [g_final] if final else []
    return pl.pallas_call(
        functools.partial(_combine_kernel, n_out_tiles=n_out_tiles),
        out_shape=jax.ShapeDtypeStruct((n_out_tiles * tile, d), F32),
        grid_spec=pltpu.PrefetchScalarGridSpec(
            num_scalar_prefetch=3,
            grid=(n_tiles,),
            in_specs=[pl.BlockSpec((tile, ROUTER_PAD), lambda t, *_: (t, 0)),
                      pl.BlockSpec((tile, d), lambda t, *_: (t, 0)),
                      pl.BlockSpec((None, None, 1, N_MOD * d), lambda t, *_: (layer, row(t), 0, 0)),
                      pl.BlockSpec(memory_space=pl.ANY)] + extra_specs,
            out_specs=pl.BlockSpec((tile, d), lambda t, *_: (jnp.minimum(t, n_out_tiles - 1), 0)),
            scratch_shapes=[pltpu.VMEM((2, SLOT_ROWS, d), BF16), pltpu.SemaphoreType.DMA((2,))],
        ),
        compiler_params=_params(("arbitrary",)),
        name="moe_combine",
    )(plan["base"], plan["lstart"], plan["nch"], route, x1, mods, ys, *extra_args)


def _moe(lay, h2, x1, route, tile_counts, mods, layer, w1, w3, w2, g_final=None):
    n_tiles = lay.n_tok // DISPATCH_TILE
    max_rows = (MOE_TOPK * lay.n_tok + n_tiles * MOE_EXPERTS * (ROW_CHUNK - 1)
                + MOE_EXPERTS * (FFN_BLOCK - 1))
    n_blocks = -(-max_rows // FFN_BLOCK)
    counts = jnp.sum(tile_counts[:, 0, :MOE_EXPERTS].reshape(n_tiles, DISPATCH_TILE // lay.tile, MOE_EXPERTS), axis=1)
    plan = _moe_plan(counts, n_blocks)
    xs = _dispatch(plan, h2, route, n_blocks * FFN_BLOCK)
    ys = _expert_ffn(plan, xs, layer, w1, w3, w2)
    return _combine(lay, plan, route, x1, mods, layer, ys, g_final)


def _rope_tables(seq, ctx):
    pos = jnp.arange(seq, dtype=jnp.int32)
    inv_freq = ROPE_BASE ** (-jnp.arange(0, AXIS_DIM, 2, dtype=F32) / AXIS_DIM)
    ang_row = (pos // GRID_W).astype(F32)[:, None] * inv_freq
    ang_col = (pos % GRID_W).astype(F32)[:, None] * inv_freq
    cos_h = jnp.concatenate([jnp.cos(ang_row)] * 2 + [jnp.cos(ang_col)] * 2, axis=1)
    sin_h = jnp.concatenate([-jnp.sin(ang_row), jnp.sin(ang_row), -jnp.sin(ang_col), jnp.sin(ang_col)], axis=1)
    reps = LANES // SWA_HEAD_DIM
    cos_t = jnp.concatenate([jnp.tile(cos_h, (1, reps)), jnp.ones((ctx, LANES), F32)], axis=0)
    sin_t = jnp.concatenate([jnp.tile(sin_h, (1, reps)), jnp.zeros((ctx, LANES), F32)], axis=0)
    return cos_t, sin_t


def _channel_dft():
    i = jnp.arange(FNET_GROUP_W, dtype=jnp.int32)
    ang = ((i[:, None] * i[None, :]) % FNET_GROUP_W).astype(F32) * (2.0 * math.pi / FNET_GROUP_W)
    return jnp.concatenate([jnp.cos(ang), jnp.sin(ang)], axis=1).astype(BF16)


def kernel(x, c, ctx, c_ctx, w_ada, b_ada, g_mix, g_ffn, w_in, w_decay_down, w_decay_up, b_decay, g_gla, sink,
           w_pa, w_pb, w_pc, w_out, w_router_group, w_router_expert, w1, w3, w2, g_final):
    batch, seq, d = x.shape
    n_ctx = ctx.shape[1]
    depth = w_ada.shape[0]
    assert d == D_MODEL and seq % GRID_W == 0 and seq % n_ctx == 0
    lay = _Layout(batch, seq, n_ctx, TOKEN_TILE)

    rows = -(-(batch + 1) // 8) * 8
    c_all = jnp.zeros((rows, d), F32).at[:batch].set(c).at[batch].set(c_ctx)
    mods_all = _adaln(c_all, w_ada, b_ada).reshape(depth, rows, 1, N_MOD * d)

    cos_t, sin_t = _rope_tables(seq, n_ctx)
    cs = _channel_dft()
    fourier_tile = min(seq // 2, FOURIER_TILE)
    assert fourier_tile % TOKEN_TILE == 0 and seq % (2 * fourier_tile) == 0
    lat_mats = _dft_half_mats(seq, fourier_tile)
    c_ctx_m, s_ctx_m = _dft_mats(n_ctx)

    stream = ((x.reshape(batch * seq, d), ctx.reshape(batch * n_ctx, d)), 0)

    rank = w_decay_down.shape[-1]
    w_in_bf = w_in.astype(BF16)
    down = jnp.concatenate([w_decay_down[:, 0], w_decay_down[:, 1]], axis=2)
    down = jnp.pad(down, ((0, 0), (0, 0), (0, DECAY_PAD - 2 * rank))).astype(BF16)
    w_up = jnp.zeros((depth, DECAY_PAD, 2 * GLA_QK_W), F32)
    w_up = w_up.at[:, :rank, :GLA_QK_W].set(w_decay_up[:, 0]).at[:, rank:2 * rank, GLA_QK_W:].set(w_decay_up[:, 1])
    w_up = w_up.astype(BF16)
    b_dec = b_decay.reshape(depth, 1, 2 * GLA_QK_W)
    sink_b = jnp.broadcast_to(sink[:, :, None], (depth, SWA_HEADS, LANES))
    w_router = jnp.zeros((depth, d, ROUTER_PAD), F32)
    w_router = w_router.at[:, :, :MOE_EXPERTS].set(w_router_expert)
    w_router = w_router.at[:, :, MOE_EXPERTS:MOE_EXPERTS + MOE_GROUPS].set(w_router_group)
    w_router_hi = w_router.astype(BF16)
    w_router = jnp.concatenate([w_router_hi, (w_router - w_router_hi.astype(F32)).astype(BF16)], axis=2)
    g_mix3, g_ffn3, g_gla3 = g_mix[:, None, :], g_ffn[:, None, :], g_gla[:, None, :]
    w_pa_bf, w_pb_bf, w_pc_bf, w_out_bf = (w.astype(BF16) for w in (w_pa, w_pb, w_pc, w_out))

    for l in range(depth):
        last = l == depth - 1
        (k_a, v_a, k_c, v_c, q_a, r_a, u_cos, u_sin, q_c, gate_logits, la_f, la_b) = _inproj(
            lay, stream, g_mix3, mods_all, l, w_in_bf, down, w_up, b_dec, cs, cos_t, sin_t)

        o_f, o_b = _gla(lay, q_a, k_a, v_a, la_f, la_b)
        yb_lat = _fourier_half(u_cos, u_sin, lat_mats, seq, fourier_tile)
        yc_lat = _swa(lay, q_c, k_c, v_c, sink_b, l)
        if last:
            yb_ctx = jnp.zeros((n_ctx, batch * FNET_W), BF16)
            yc_ctx = jnp.zeros((batch * n_ctx, SWA_Q_W), BF16)
        else:
            yb_ctx = _fourier(u_cos, u_sin, c_ctx_m, s_ctx_m, seq, n_ctx, n_ctx, n_ctx)
            yc_ctx = _ctx_attn(lay, q_c, k_c, v_c, sink_b, l)

        x1, h2, gates, tile_counts = _merge(lay, stream, o_f, o_b, r_a, yb_lat, fourier_tile, yb_ctx, yc_lat, yc_ctx,
                                            gate_logits, mods_all, l, g_gla3, g_ffn3,
                                            w_pa_bf, w_pb_bf, w_pc_bf, w_out_bf, w_router)
        xs = _moe(lay, h2, x1, gates, tile_counts, mods_all, l, w1, w3, w2,
                  g_final.reshape(1, d) if last else None)
        stream = ((xs,), None)

    return xs.reshape(batch, seq, d)
```

```python
import functools
import math

import jax
import jax.numpy as jnp
from jax import lax
from jax.experimental import pallas as pl
from jax.experimental.pallas import tpu as pltpu

F32 = jnp.float32
BF16 = jnp.bfloat16
HIGHEST = lax.Precision.HIGHEST
LOG2E = math.log2(math.e)

D_MODEL = 1024
GRID_W = 64
EPS = 1e-6
N_MOD = 6
GLA_HEADS = 4
GLA_DK = 64
GLA_DV = 128
GLA_TAU = 16.0
GLA_CHUNK = 64
GLA_SCALE = GLA_DK ** -0.5
FNET_GROUPS = 4
FNET_GROUP_W = 128
SWA_HEADS = 8
SWA_KV_HEADS = 2
SWA_HEAD_DIM = 64
WINDOW = 128
ROPE_BASE = 10000.0
AXIS_DIM = SWA_HEAD_DIM // 2
MOE_GROUPS = 4
MOE_EXPERTS_PER_GROUP = 4
MOE_EXPERTS = MOE_GROUPS * MOE_EXPERTS_PER_GROUP
MOE_TOPK = 2
D_EXPERT = 512

GLA_QK_W = GLA_HEADS * GLA_DK
GLA_V_W = GLA_HEADS * GLA_DV
FNET_W = FNET_GROUPS * FNET_GROUP_W
SWA_Q_W = SWA_HEADS * SWA_HEAD_DIM
SWA_KV_W = SWA_KV_HEADS * SWA_HEAD_DIM
IN_SIZES = (GLA_QK_W, GLA_V_W, SWA_KV_W, SWA_KV_W, GLA_QK_W, GLA_V_W, FNET_W, SWA_Q_W, 3 * D_MODEL)
IN_OFFS = tuple(int(sum(IN_SIZES[:i])) for i in range(len(IN_SIZES) + 1))
IN_COLS = IN_OFFS[-1]

LANES = 128
TOKEN_TILE = 256
DECAY_PAD = LANES
ROUTER_PAD = LANES
ROUTE_ID = 0
ROUTE_W = 2
DISPATCH_TILE = 512
ROW_CHUNK = 16
BIG_CHUNK = 64
FOURIER_TILE = 512
FOURIER_COL_SPLIT = 2
FFN_BLOCK = 512
SLOT_ROWS = -(-(MOE_TOPK * DISPATCH_TILE + MOE_EXPERTS * (ROW_CHUNK - 1)) // LANES) * LANES
VMEM_LIMIT = 56 * 1024 * 1024


def _layer_block(a, layer):
    return pl.BlockSpec((None,) + a.shape[1:], lambda *_: (layer,) + (0,) * (a.ndim - 1))


def _params(sem, vmem=VMEM_LIMIT):
    return pltpu.CompilerParams(dimension_semantics=sem, vmem_limit_bytes=vmem)


def _sigmoid(x):
    return 0.5 * jnp.tanh(0.5 * x) + 0.5


def _dot(a, b):
    return jnp.dot(a, b, preferred_element_type=F32)


def _dot_nt(a, b):
    return lax.dot_general(a, b, (((1,), (1,)), ((), ())), preferred_element_type=F32)


def _ada_kernel(c_ref, w_ref, b_ref, o_ref):
    c = c_ref[...]
    a = c * _sigmoid(c)
    o_ref[...] = jnp.dot(a, w_ref[...], preferred_element_type=F32, precision=HIGHEST) + b_ref[...]


def _adaln(c_all, w_ada, b_ada):
    depth, d, n = w_ada.shape
    rows = c_all.shape[0]
    tn = 1536
    return pl.pallas_call(
        _ada_kernel,
        out_shape=jax.ShapeDtypeStruct((depth, rows, n), F32),
        grid=(depth, n // tn),
        in_specs=[pl.BlockSpec((rows, d), lambda l, j: (0, 0)),
                  pl.BlockSpec((None, d, tn), lambda l, j: (l, 0, j)),
                  pl.BlockSpec((None, 1, tn), lambda l, j: (l, 0, j))],
        out_specs=pl.BlockSpec((None, rows, tn), lambda l, j: (l, 0, j)),
        compiler_params=_params(("parallel", "parallel")),
        name="adaln",
    )(c_all, w_ada, b_ada.reshape(depth, 1, n))


class _Layout:
    def __init__(self, batch, seq, ctx, tile):
        assert seq % tile == 0 and ctx % tile == 0
        self.batch, self.seq, self.ctx, self.tile = batch, seq, ctx, tile
        self.lat_tiles = seq // tile
        self.ctx_tiles = ctx // tile
        self.n_lat = batch * self.lat_tiles
        self.n_tiles = self.n_lat + batch * self.ctx_tiles
        self.n_tok = self.n_tiles * tile

    def batch_of(self, t):
        return jnp.where(t < self.n_lat, t // self.lat_tiles, (t - self.n_lat) // self.ctx_tiles)

    def mod_row(self, t):
        return jnp.where(t < self.n_lat, t // self.lat_tiles, self.batch)

    def stream_specs(self, width, ctx_first):
        if ctx_first is None:
            return [pl.BlockSpec((self.tile, width), lambda t: (t, 0))]
        lat = pl.BlockSpec((self.tile, width), lambda t: (jnp.minimum(t, self.n_lat - 1), 0))
        ctx = pl.BlockSpec((self.tile, width), lambda t: (ctx_first + jnp.maximum(t - self.n_lat, 0), 0))
        return [lat, ctx]

    def seq_tile(self, t):
        return jnp.where(t < self.n_lat, t % self.lat_tiles,
                         self.lat_tiles + (t - self.n_lat) % self.ctx_tiles)


def _rope(x, cos, sin_signed):
    n = x.shape[-1]
    lane = lax.broadcasted_iota(jnp.int32, x.shape, 1)
    half = AXIS_DIM // 2
    partner = jnp.where((lane & half) == 0, pltpu.roll(x, n - half, 1), pltpu.roll(x, half, 1))
    return x * cos + partner * sin_signed


def _read_stream(x_refs, n_lat):
    if len(x_refs) == 1:
        return x_refs[0][...]
    return jnp.where(pl.program_id(0) >= n_lat, x_refs[1][...], x_refs[0][...])


def _inproj_kernel(*refs, n_lat, n_x):
    _inproj_body(_read_stream(refs[:n_x], n_lat), *refs[n_x:])


def _inproj_body(x, g_ref, mod_ref, w_ref, wd_ref, wu_ref, bdec_ref, cs_ref, cos_ref, sin_ref,
                 ka_ref, va_ref, kc_ref, vc_ref, qa_ref, ra_ref, ua_ref, us_ref, qc_ref, gl_ref,
                 laf_ref, lab_ref):
    d = D_MODEL
    shift = mod_ref[:, 0:d]
    scale = mod_ref[:, d:2 * d]
    h = x * lax.rsqrt(jnp.mean(x * x, axis=-1, keepdims=True) + EPS) * g_ref[...]
    hb = (h * (1.0 + scale) + shift).astype(BF16)

    def proj(i):
        return _dot(hb, w_ref[:, IN_OFFS[i]:IN_OFFS[i + 1]])

    ka_ref[...] = proj(0).astype(BF16)
    va_ref[...] = proj(1).astype(BF16)
    cos = cos_ref[...]
    sin = sin_ref[...]
    kc_ref[...] = _rope(proj(2), cos, sin).astype(BF16)
    vc_ref[...] = proj(3).astype(BF16)
    qa_ref[...] = proj(4).astype(BF16)
    r = proj(5)
    ra_ref[...] = (r * _sigmoid(r)).astype(BF16)
    u = proj(6).astype(BF16)
    for g in range(FNET_GROUPS):
        sl = slice(g * FNET_GROUP_W, (g + 1) * FNET_GROUP_W)
        ab = _dot(u[:, sl], cs_ref[...])
        ua_ref[:, sl] = ab[:, :FNET_GROUP_W].astype(BF16)
        us_ref[:, sl] = ab[:, FNET_GROUP_W:].astype(BF16)
    reps = SWA_Q_W // LANES
    qc_ref[...] = (_rope(proj(7), jnp.concatenate([cos] * reps, axis=1), jnp.concatenate([sin] * reps, axis=1))
                   * (SWA_HEAD_DIM ** -0.5 * LOG2E)).astype(BF16)
    gl_ref[...] = _sigmoid(proj(8)).astype(BF16)
    low = _dot(hb, wd_ref[...]).astype(BF16)
    z = _dot(low, wu_ref[...]) + bdec_ref[...]
    la = (jnp.minimum(z, 0.0) - jnp.log(1.0 + jnp.exp(-jnp.abs(z)))) * (1.0 / GLA_TAU)
    laf_ref[...] = la[:, :GLA_QK_W]
    lab_ref[...] = la[:, GLA_QK_W:]


def _inproj(lay, x, g_mix, mods, layer, w_in, w_down, w_up, b_dec, cs, cos_t, sin_t):
    tm, d = lay.tile, D_MODEL
    n = lay.n_tok
    bf = lambda w: jax.ShapeDtypeStruct((n, w), BF16)
    tok = lambda w: pl.BlockSpec((tm, w), lambda t: (t, 0))
    const = lambda a: pl.BlockSpec(a.shape, lambda t: (0,) * a.ndim)
    seq_rows = lay.seq + lay.ctx
    fnet_shape = jax.ShapeDtypeStruct((seq_rows, lay.batch * FNET_W), BF16)
    fnet_spec = pl.BlockSpec((tm, FNET_W), lambda t: (lay.seq_tile(t), lay.batch_of(t)))
    pos_spec = pl.BlockSpec((tm, LANES), lambda t: (lay.seq_tile(t), 0))
    return pl.pallas_call(
        functools.partial(_inproj_kernel, n_lat=lay.n_lat, n_x=len(x[0])),
        out_shape=(bf(GLA_QK_W), bf(GLA_V_W), bf(SWA_KV_W), bf(SWA_KV_W), bf(GLA_QK_W), bf(GLA_V_W),
                   fnet_shape, fnet_shape, bf(SWA_Q_W), bf(3 * d),
                   jax.ShapeDtypeStruct((n, GLA_QK_W), F32), jax.ShapeDtypeStruct((n, GLA_QK_W), F32)),
        grid=(lay.n_tiles,),
        in_specs=lay.stream_specs(d, x[1]) + [_layer_block(g_mix, layer),
                  pl.BlockSpec((None, None, 1, N_MOD * d), lambda t: (layer, lay.mod_row(t), 0, 0)),
                  pl.BlockSpec((None,) + w_in.shape[1:], lambda t: (layer, 0, 0)),
                  _layer_block(w_down, layer), _layer_block(w_up, layer), _layer_block(b_dec, layer), const(cs),
                  pos_spec, pos_spec],
        out_specs=(tok(GLA_QK_W), tok(GLA_V_W), tok(SWA_KV_W), tok(SWA_KV_W), tok(GLA_QK_W), tok(GLA_V_W),
                   fnet_spec, fnet_spec, tok(SWA_Q_W), tok(3 * d), tok(GLA_QK_W), tok(GLA_QK_W)),
        compiler_params=_params(("parallel",)),
        name="inproj",
    )(*x[0], g_mix, mods, w_in, w_down, w_up, b_dec, cs, cos_t, sin_t)


def _split2(x):
    hi = x.astype(BF16)
    return hi, (x - hi.astype(F32)).astype(BF16)


def _gla_direction(q_ref, k_ref, v_ref, la_ref, o_ref, s_ref, rev):
    t_rows = q_ref.shape[0]
    c = GLA_CHUNK
    n_sub = t_rows // c
    hk = GLA_QK_W
    la_hi, la_lo = _split2(la_ref[...])
    row = lax.broadcasted_iota(jnp.int32, (t_rows, t_rows), 0)
    col = lax.broadcasted_iota(jnp.int32, (t_rows, t_rows), 1)
    tri = ((row // c) == (col // c)) & ((col >= row) if rev else (col <= row))
    tri = jnp.where(tri, 1.0, 0.0).astype(BF16)
    lc = _dot(tri, la_hi) + _dot(tri, la_lo)
    q = q_ref[...].astype(F32)
    k = k_ref[...].astype(F32)
    qd = (q * jnp.exp(lc) * GLA_SCALE).astype(BF16)
    kd = (k * jnp.exp(-lc)).astype(BF16)
    hrow = lax.broadcasted_iota(jnp.int32, (GLA_HEADS * c, hk), 0) // c
    hcol = lax.broadcasted_iota(jnp.int32, (GLA_HEADS * c, hk), 1) // GLA_DK
    head_mask = hrow == hcol
    trow = lax.broadcasted_iota(jnp.int32, (GLA_HEADS * c, c), 0) % c
    scol = lax.broadcasted_iota(jnp.int32, (GLA_HEADS * c, c), 1)
    causal = (scol >= trow) if rev else (scol <= trow)
    edge = 0 if rev else c - 1
    lasts = [lc[i * c + edge:i * c + edge + 1] for i in range(n_sub)]
    to_end = jnp.concatenate([jnp.broadcast_to(l, (c, hk)) for l in lasts], axis=0) - lc
    kh_t = (k * jnp.exp(to_end)).T
    pad = jnp.zeros((LANES - n_sub, hk), F32)
    decay_t = jnp.exp(jnp.concatenate(lasts + [pad], axis=0).T)
    yield
    chunk_of = lax.broadcasted_iota(jnp.int32, (1, t_rows), 1) // c
    kv = []
    for h in range(GLA_HEADS):
        kh_h = kh_t[h * GLA_DK:(h + 1) * GLA_DK]
        stack = jnp.concatenate([jnp.where(chunk_of == i, kh_h, 0.0) for i in range(n_sub)], axis=0)
        kv.append(_dot(stack.astype(BF16), v_ref[:, h * GLA_DV:(h + 1) * GLA_DV]))
    yield
    order = range(n_sub - 1, -1, -1) if rev else range(n_sub)
    state = s_ref[...]
    state_at = {}
    for i in order:
        state_at[i] = state
        inc = jnp.concatenate([kv[h][i * GLA_DK:(i + 1) * GLA_DK] for h in range(GLA_HEADS)], axis=0)
        state = decay_t[:, i:i + 1] * state + inc
    s_ref[...] = state
    yield
    inters, scoress = {}, {}
    for i in order:
        rows = slice(i * c, (i + 1) * c)
        q_stack = jnp.where(head_mask, jnp.concatenate([qd[rows]] * GLA_HEADS, axis=0), 0.0).astype(BF16)
        inters[i] = _dot(q_stack, state_at[i].astype(BF16))
        scoress[i] = jnp.where(causal, _dot_nt(q_stack, kd[rows]), 0.0).astype(BF16)
    for i in order:
        yield
        rows = slice(i * c, (i + 1) * c)
        v_i = v_ref[rows, :]
        inter, scores = inters[i], scoress[i]
        outs = []
        for h in range(GLA_HEADS):
            hr = slice(h * c, (h + 1) * c)
            hv = slice(h * GLA_DV, (h + 1) * GLA_DV)
            outs.append(inter[hr] + _dot(scores[hr], v_i[:, hv]))
        o_ref[rows, :] = jnp.concatenate(outs, axis=1).astype(BF16)


def _gla_kernel(qf_ref, kf_ref, vf_ref, laf_ref, qb_ref, kb_ref, vb_ref, lab_ref,
                of_ref, ob_ref, sf_ref, sb_ref):
    @pl.when(pl.program_id(1) == 0)
    def _():
        sf_ref[...] = jnp.zeros_like(sf_ref)
        sb_ref[...] = jnp.zeros_like(sb_ref)

    live = [_gla_direction(qf_ref, kf_ref, vf_ref, laf_ref, of_ref, sf_ref, False),
            _gla_direction(qb_ref, kb_ref, vb_ref, lab_ref, ob_ref, sb_ref, True)]
    while live:
        live = [g for g in live if next(g, "done") != "done"]


def _gla(lay, q_a, k_a, v_a, la_f, la_b):
    tm = lay.tile
    nc, nl = lay.ctx_tiles, lay.lat_tiles

    def fwd(b, j):
        return jnp.where(j < nc, lay.n_lat + b * nc + j, b * nl + (j - nc))

    def bwd(b, j):
        return jnp.where(j < nc, lay.n_lat + b * nc + (nc - 1 - j), b * nl + (nl - 1 - (j - nc)))

    spec = lambda w, f: pl.BlockSpec((tm, w), lambda b, j: (f(b, j), 0))
    out = jax.ShapeDtypeStruct((lay.n_tok, GLA_V_W), BF16)
    return pl.pallas_call(
        _gla_kernel,
        out_shape=(out, out),
        grid=(lay.batch, nc + nl),
        in_specs=[spec(GLA_QK_W, fwd), spec(GLA_QK_W, fwd), spec(GLA_V_W, fwd), spec(GLA_QK_W, fwd),
                  spec(GLA_QK_W, bwd), spec(GLA_QK_W, bwd), spec(GLA_V_W, bwd), spec(GLA_QK_W, bwd)],
        out_specs=(spec(GLA_V_W, fwd), spec(GLA_V_W, bwd)),
        scratch_shapes=[pltpu.VMEM((GLA_QK_W, GLA_DV), F32), pltpu.VMEM((GLA_QK_W, GLA_DV), F32)],
        compiler_params=_params(("parallel", "arbitrary")),
        name="gla",
    )(q_a, k_a, v_a, la_f, q_a, k_a, v_a, la_b)


def _dft_kernel(c_ref, s_ref, a_ref, b_ref, o_ref, acc_ref, *, scale):
    k = pl.program_id(1)

    @pl.when(k == 0)
    def _():
        acc_ref[...] = jnp.zeros_like(acc_ref)

    acc_ref[...] += _dot(c_ref[...], a_ref[...]) + _dot(s_ref[...], b_ref[...])

    @pl.when(k == pl.num_programs(1) - 1)
    def _():
        o_ref[...] = (acc_ref[...] * scale).astype(BF16)


def _dft_mats(n):
    f = GRID_W
    assert n % f == 0
    k = jnp.arange(n, dtype=jnp.int32)[None, :]

    def table(rows, period):
        ang = ((jnp.arange(rows, dtype=jnp.int32)[:, None] * k) % period).astype(F32) * (2.0 * math.pi / period)
        return jnp.cos(ang), jnp.sin(ang)

    ca, sa = table(n // f, n // f)
    cb, sb = table(f, n)
    cos = ca[:, None, :] * cb[None, :, :] - sa[:, None, :] * sb[None, :, :]
    sin = sa[:, None, :] * cb[None, :, :] + ca[:, None, :] * sb[None, :, :]
    return cos.reshape(n, n).astype(BF16), (-sin).reshape(n, n).astype(BF16)


def _dft_half_kernel(c_ref, s_ref, cx_ref, sx_ref, a_ref, b_ref, j_ref, o_ref, af_ref, bf_ref, *, scale):
    it = pl.program_id(1)
    tm = o_ref.shape[1]
    length = a_ref.shape[0]
    half = length // 2
    sub = 16
    jm = j_ref[...]
    row = lax.broadcasted_iota(jnp.int32, (tm, 1), 0)

    @pl.when(it == 0)
    def _():
        for m in range(half // tm):
            lo = slice(tm * m, tm * (m + 1))
            hi = slice(length - tm * (m + 1), length - tm * m)
            for src, dst, sign in ((a_ref, af_ref, 1.0), (b_ref, bf_ref, -1.0)):
                mirrored = _dot(jm, src[hi, :])
                if m > 0:
                    edge = src[length - tm * m:length - tm * m + sub, :][0:1, :].astype(F32)
                    mirrored = jnp.where(row == 0, edge, mirrored)
                dst[lo, :] = (src[lo, :].astype(F32) + sign * mirrored).astype(BF16)

    af = af_ref[...]
    bf = bf_ref[...]
    nyquist = a_ref[half:half + sub, :][0:1, :].astype(F32)
    p = _dot(c_ref[...], af) + jnp.where((row & 1) == 0, nyquist, -nyquist)
    q = _dot(s_ref[...], bf)
    o_ref[0] = ((p + q) * scale).astype(BF16)
    mirror = ((p - q) * scale).astype(BF16)
    flipped = _dot(jm, mirror)
    first = ((_dot(cx_ref[...], af) - _dot(sx_ref[...], bf))[0:1, :] + nyquist) * scale
    o_ref[1] = jnp.where(row == 0, first, flipped).astype(BF16)


def _dft_half_mats(n, tm):
    f = GRID_W
    half = n // 2
    assert half % f == 0 and half % tm == 0
    k = jnp.arange(half, dtype=jnp.int32)[None, :]

    def table(rows, period):
        ang = ((rows[:, None] * k) % period).astype(F32) * (2.0 * math.pi / period)
        return jnp.cos(ang), jnp.sin(ang)

    ca, sa = table(jnp.arange(half // f, dtype=jnp.int32), n // f)
    cb, sb = table(jnp.arange(f, dtype=jnp.int32), n)
    cos = ca[:, None, :] * cb[None, :, :] - sa[:, None, :] * sb[None, :, :]
    sin = sa[:, None, :] * cb[None, :, :] + ca[:, None, :] * sb[None, :, :]
    n_it = half // tm
    cx, sx = table(jnp.arange(1, n_it + 1, dtype=jnp.int32) * tm, n)
    spread = lambda m: jnp.zeros((n_it, 8, half), F32).at[:, 0, :].set(m).reshape(n_it * 8, half).astype(BF16)
    return (cos.reshape(half, half).astype(BF16), (-sin).reshape(half, half).astype(BF16), spread(cx), spread(-sx))


def _fourier_half(ua, us, mats, length, tm):
    cmat, smat, cx, sx = mats
    width = ua.shape[1]
    half = length // 2
    scale = 1.0 / math.sqrt(length * FNET_GROUP_W)
    r = lax.broadcasted_iota(jnp.int32, (tm, tm), 0)
    c = lax.broadcasted_iota(jnp.int32, (tm, tm), 1)
    jmat = jnp.where(c == tm - r, 1.0, 0.0).astype(BF16)
    once = pl.Buffered(1)
    cols = width // FOURIER_COL_SPLIT
    return pl.pallas_call(
        functools.partial(_dft_half_kernel, scale=scale),
        out_shape=jax.ShapeDtypeStruct((2, half, width), BF16),
        grid=(FOURIER_COL_SPLIT, half // tm),
        in_specs=[pl.BlockSpec((tm, half), lambda j, i: (i, 0)),
                  pl.BlockSpec((tm, half), lambda j, i: (i, 0)),
                  pl.BlockSpec((8, half), lambda j, i: (i, 0)),
                  pl.BlockSpec((8, half), lambda j, i: (i, 0)),
                  pl.BlockSpec((length, cols), lambda j, i: (0, j), pipeline_mode=once),
                  pl.BlockSpec((length, cols), lambda j, i: (0, j), pipeline_mode=once),
                  pl.BlockSpec((tm, tm), lambda j, i: (0, 0), pipeline_mode=once)],
        out_specs=pl.BlockSpec((2, tm, cols), lambda j, i: (0, i, j)),
        scratch_shapes=[pltpu.VMEM((half, cols), BF16), pltpu.VMEM((half, cols), BF16)],
        compiler_params=_params(("arbitrary", "arbitrary")),
        name="fourier_half",
    )(cmat, smat, cx, sx, ua, us, jmat)


def _fourier(ua, us, cmat, smat, row0, length, tm, tk):
    width = ua.shape[1]
    off = row0 // tk
    scale = 1.0 / math.sqrt(length * FNET_GROUP_W)
    return pl.pallas_call(
        functools.partial(_dft_kernel, scale=scale),
        out_shape=jax.ShapeDtypeStruct((length, width), BF16),
        grid=(length // tm, length // tk),
        in_specs=[pl.BlockSpec((tm, tk), lambda i, k: (i, k)),
                  pl.BlockSpec((tm, tk), lambda i, k: (i, k)),
                  pl.BlockSpec((tk, width), lambda i, k: (off + k, 0)),
                  pl.BlockSpec((tk, width), lambda i, k: (off + k, 0))],
        out_specs=pl.BlockSpec((tm, width), lambda i, k: (i, 0)),
        scratch_shapes=[pltpu.VMEM((tm, width), F32)],
        compiler_params=_params(("parallel", "arbitrary")),
        name="fourier",
    )(cmat, smat, ua, us)


def _attend(q_ref, k_all, v_all, valid, sink_ref, o_ref):
    hd = SWA_HEAD_DIM
    heads_per_kv = SWA_HEADS // SWA_KV_HEADS
    k_sw = jnp.concatenate([k_all[:, hd:], k_all[:, :hd]], axis=1)
    v_t = v_all.astype(F32).T
    v_t_sw = jnp.concatenate([v_t[hd:], v_t[:hd]], axis=0)
    lane = lax.broadcasted_iota(jnp.int32, (1, LANES), 1)
    lane_half = (lane < hd, lane >= hd)
    row = lax.broadcasted_iota(jnp.int32, (LANES, 1), 0)
    row_half = (row < hd, row >= hd)
    zero = jnp.zeros((), BF16)
    for p in range(SWA_HEADS // 2):
        acc = None
        for par in range(2):
            head = 2 * p + par
            aligned = head // heads_per_kv == par
            qh = jnp.where(lane_half[par], q_ref[:, p * LANES:(p + 1) * LANES], zero)
            s = _dot_nt(k_all if aligned else k_sw, qh)
            if valid is not None:
                kw = valid.shape[0]
                s = jnp.concatenate([jnp.where(valid, s[:kw], -jnp.inf), s[kw:]], axis=0)
            sink = sink_ref[head:head + 1, 0:1] * LOG2E
            m = jnp.maximum(jnp.max(s, axis=0, keepdims=True), sink)
            e = jnp.exp2(s - m)
            denom = jnp.sum(e, axis=0, keepdims=True) + jnp.exp2(sink - m)
            v_use = jnp.where(row_half[par], v_t if aligned else v_t_sw, 0.0).astype(BF16)
            part = _dot(v_use, e.astype(BF16)) * (1.0 / denom)
            acc = part if acc is None else acc + part
        o_ref[:, p * LANES:(p + 1) * LANES] = acc.T.astype(BF16)


def _swa_kernel(q_ref, kp_ref, kc_ref, kn_ref, kx_ref, vp_ref, vc_ref, vn_ref, vx_ref, sink_ref, o_ref,
                *, seq):
    n = pl.program_id(1)
    w = WINDOW
    k_blocks = [kp_ref[...], kc_ref[0:w, :], kc_ref[w:2 * w, :], kn_ref[...]]
    v_blocks = [vp_ref[...], vc_ref[0:w, :], vc_ref[w:2 * w, :], vn_ref[...]]
    j = lax.broadcasted_iota(jnp.int32, (3 * w, w), 0)
    a = lax.broadcasted_iota(jnp.int32, (3 * w, w), 1)
    band = (j >= a) & (j - a <= 2 * w)
    for half in range(2):
        key_pos = (2 * n + half - 1) * w + j
        valid = band & (key_pos >= 0) & (key_pos < seq)
        k_all = jnp.concatenate(k_blocks[half:half + 3] + [kx_ref[...]], axis=0)
        v_all = jnp.concatenate(v_blocks[half:half + 3] + [vx_ref[...]], axis=0)
        rows = pl.ds(half * w, w)
        _attend(q_ref.at[rows], k_all, v_all, valid, sink_ref, o_ref.at[rows])


def _swa(lay, q_c, k_c, v_c, sink_b, layer):
    w = WINDOW
    assert lay.seq % (2 * w) == 0
    nq = lay.seq // w
    steps = nq // 2
    ctx_blk = (lay.batch * lay.seq) // lay.ctx

    def edge(f):
        return pl.BlockSpec((w, SWA_KV_W), lambda b, n: (b * nq + f(n), 0))

    prev = edge(lambda n: jnp.maximum(2 * n - 1, 0))
    nxt = edge(lambda n: jnp.minimum(2 * n + 2, nq - 1))
    cur = pl.BlockSpec((2 * w, SWA_KV_W), lambda b, n: (b * steps + n, 0))
    ctx_spec = pl.BlockSpec((lay.ctx, SWA_KV_W), lambda b, n: (ctx_blk + b, 0))
    q_spec = pl.BlockSpec((2 * w, SWA_Q_W), lambda b, n: (b * steps + n, 0))
    return pl.pallas_call(
        functools.partial(_swa_kernel, seq=lay.seq),
        out_shape=jax.ShapeDtypeStruct((lay.batch * lay.seq, SWA_Q_W), BF16),
        grid=(lay.batch, steps),
        in_specs=[q_spec, prev, cur, nxt, ctx_spec, prev, cur, nxt, ctx_spec,
                  _layer_block(sink_b, layer)],
        out_specs=q_spec,
        compiler_params=_params(("parallel", "parallel")),
        name="swa",
    )(q_c, k_c, k_c, k_c, k_c, v_c, v_c, v_c, v_c, sink_b)


def _ctx_attn_kernel(q_ref, kx_ref, vx_ref, sink_ref, o_ref):
    _attend(q_ref, kx_ref[...], vx_ref[...], None, sink_ref, o_ref)


def _ctx_attn(lay, q_c, k_c, v_c, sink_b, layer):
    ctx_blk = (lay.batch * lay.seq) // lay.ctx
    spec = lambda wd: pl.BlockSpec((lay.ctx, wd), lambda b: (ctx_blk + b, 0))
    return pl.pallas_call(
        _ctx_attn_kernel,
        out_shape=jax.ShapeDtypeStruct((lay.batch * lay.ctx, SWA_Q_W), BF16),
        grid=(lay.batch,),
        in_specs=[spec(SWA_Q_W), spec(SWA_KV_W), spec(SWA_KV_W),
                  _layer_block(sink_b, layer)],
        out_specs=pl.BlockSpec((lay.ctx, SWA_Q_W), lambda b: (b, 0)),
        compiler_params=_params(("parallel",)),
        name="ctx_attn",
    )(q_c, k_c, v_c, sink_b)


def _merge_kernel(*refs, n_lat, n_x):
    _merge_body(_read_stream(refs[:n_x], n_lat), *refs[n_x:], n_lat=n_lat)


def _merge_body(x, of_ref, ob_ref, ra_ref, ybl_ref, ybc_ref, ycl_ref, ycc_ref, gl_ref, mod_ref,
                ggla_ref, gffn_ref, wpa_ref, wpb_ref, wpc_ref, wout_ref, wr_ref,
                x1_ref, h2_ref, gates_ref, count_ref, *, n_lat):
    d = D_MODEL
    is_ctx = pl.program_id(0) >= n_lat
    o = of_ref[...].astype(F32) + ob_ref[...].astype(F32)
    r = ra_ref[...].astype(F32)
    parts = []
    for h in range(GLA_HEADS):
        sl = slice(h * GLA_DV, (h + 1) * GLA_DV)
        oh = o[:, sl]
        parts.append(oh * lax.rsqrt(jnp.mean(oh * oh, axis=-1, keepdims=True) + EPS) * ggla_ref[...])
    y_a = (jnp.concatenate(parts, axis=1) * r).astype(BF16)
    y_b = jnp.where(is_ctx, ybc_ref[...], ybl_ref[...])
    y_c = jnp.where(is_ctx, ycc_ref[...], ycl_ref[...])
    gl = gl_ref[...].astype(F32)
    mix = (gl[:, 0:d] * _dot(y_a, wpa_ref[...])
           + gl[:, d:2 * d] * _dot(y_b, wpb_ref[...])
           + gl[:, 2 * d:3 * d] * _dot(y_c, wpc_ref[...]))
    y = _dot(mix.astype(BF16), wout_ref[...])
    x1 = x + mod_ref[:, 2 * d:3 * d] * y
    x1_ref[...] = x1
    h2 = x1 * lax.rsqrt(jnp.mean(x1 * x1, axis=-1, keepdims=True) + EPS) * gffn_ref[...]
    h2 = h2 * (1.0 + mod_ref[:, 4 * d:5 * d]) + mod_ref[:, 3 * d:4 * d]
    h2_hi = h2.astype(BF16)
    h2_ref[...] = h2_hi
    h2_lo = (h2 - h2_hi.astype(F32)).astype(BF16)
    both = _dot(h2_hi, wr_ref[...])
    logits = both[:, :ROUTER_PAD] + both[:, ROUTER_PAD:] + _dot(h2_lo, wr_ref[:, :ROUTER_PAD])
    lt = logits.T
    sub = lax.broadcasted_iota(jnp.int32, lt.shape, 0)
    sub_f = sub.astype(F32)
    neg = -jnp.inf
    big = float(ROUTER_PAD)
    is_group = (sub >= MOE_EXPERTS) & (sub < MOE_EXPERTS + MOE_GROUPS)
    gl_m = jnp.where(is_group, lt, neg)
    g_max = jnp.max(gl_m, axis=0, keepdims=True)
    g_sel = jnp.min(jnp.where(gl_m == g_max, sub_f, big), axis=0, keepdims=True) - MOE_EXPERTS
    g_gate = 1.0 / jnp.sum(jnp.where(is_group, jnp.exp(lt - g_max), 0.0), axis=0, keepdims=True)
    lo = g_sel * MOE_EXPERTS_PER_GROUP
    in_group = (sub_f >= lo) & (sub_f < lo + MOE_EXPERTS_PER_GROUP)
    e1 = jnp.where(in_group, lt, neg)
    v1 = jnp.max(e1, axis=0, keepdims=True)
    i1 = jnp.min(jnp.where(e1 == v1, sub_f, big), axis=0, keepdims=True)
    e2 = jnp.where(sub_f == i1, neg, e1)
    v2 = jnp.max(e2, axis=0, keepdims=True)
    i2 = jnp.min(jnp.where(e2 == v2, sub_f, big), axis=0, keepdims=True)
    t = jnp.exp(v2 - v1)
    w1 = g_gate / (1.0 + t)
    w2 = g_gate * t / (1.0 + t)
    route_t = jnp.where(sub == ROUTE_ID, i1, 0.0) + jnp.where(sub == ROUTE_ID + 1, i2, 0.0)
    route_t = route_t + jnp.where(sub == ROUTE_W, w1, 0.0) + jnp.where(sub == ROUTE_W + 1, w2, 0.0)
    route = route_t.T
    gates_ref[...] = route
    lane_f = lax.broadcasted_iota(jnp.int32, route.shape, 1).astype(F32)
    hit = jnp.where((lane_f == route[:, ROUTE_ID:ROUTE_ID + 1]) | (lane_f == route[:, ROUTE_ID + 1:ROUTE_ID + 2]),
                    1.0, 0.0)
    count_ref[...] = jnp.sum(hit, axis=0, keepdims=True).astype(jnp.int32)


def _merge(lay, x, o_f, o_b, r_a, yb_lat, fourier_tile, yb_ctx, yc_lat, yc_ctx, gate_logits, mods, layer, g_gla, g_ffn,
           w_pa, w_pb, w_pc, w_out, w_router):
    tm, d = lay.tile, D_MODEL
    n = lay.n_tok
    tok = lambda w: pl.BlockSpec((tm, w), lambda t: (t, 0))
    const = lambda a: pl.BlockSpec(a.shape, lambda t: (0,) * a.ndim)
    lt, ct = lay.lat_tiles, lay.ctx_tiles
    sub = fourier_tile // tm

    def yb_lat_index(t):
        tl = jnp.minimum(t, lay.n_lat - 1)
        s = tl % lt
        u = lt - 1 - s
        upper = s >= lt // 2
        blk = jnp.where(upper, (u // sub) * sub + sub - 1 - u % sub, s)
        return (upper.astype(jnp.int32), blk, tl // lt)

    yb_lat_spec = pl.BlockSpec((None, tm, FNET_W), yb_lat_index)
    yb_ctx_spec = pl.BlockSpec((tm, FNET_W), lambda t: (jnp.maximum(t - lay.n_lat, 0) % ct,
                                                         jnp.maximum(t - lay.n_lat, 0) // ct))
    yc_lat_spec = pl.BlockSpec((tm, SWA_Q_W), lambda t: (jnp.minimum(t, lay.n_lat - 1), 0))
    yc_ctx_spec = pl.BlockSpec((tm, SWA_Q_W), lambda t: (jnp.maximum(t - lay.n_lat, 0), 0))
    return pl.pallas_call(
        functools.partial(_merge_kernel, n_lat=lay.n_lat, n_x=len(x[0])),
        out_shape=(jax.ShapeDtypeStruct((n, d), F32), jax.ShapeDtypeStruct((n, d), BF16),
                   jax.ShapeDtypeStruct((n, ROUTER_PAD), F32),
                   jax.ShapeDtypeStruct((lay.n_tiles, 1, ROUTER_PAD), jnp.int32)),
        grid=(lay.n_tiles,),
        in_specs=lay.stream_specs(d, x[1]) + [tok(GLA_V_W), tok(GLA_V_W), tok(GLA_V_W), yb_lat_spec, yb_ctx_spec,
                  yc_lat_spec, yc_ctx_spec, tok(3 * d),
                  pl.BlockSpec((None, None, 1, N_MOD * d), lambda t: (layer, lay.mod_row(t), 0, 0)),
                  _layer_block(g_gla, layer), _layer_block(g_ffn, layer), _layer_block(w_pa, layer),
                  _layer_block(w_pb, layer), _layer_block(w_pc, layer), _layer_block(w_out, layer),
                  _layer_block(w_router, layer)],
        out_specs=(tok(d), tok(d), tok(ROUTER_PAD),
                   pl.BlockSpec((None, 1, ROUTER_PAD), lambda t: (t, 0, 0))),
        compiler_params=_params(("parallel",)),
        name="merge",
    )(*x[0], o_f, o_b, r_a, yb_lat, yb_ctx, yc_lat, yc_ctx, gate_logits, mods, g_gla, g_ffn,
      w_pa, w_pb, w_pc, w_out, w_router)


def _route_ids(route, axis):
    take = (lambda i: route[:, i:i + 1]) if axis == 1 else (lambda i: route[i:i + 1, :])
    return take(ROUTE_ID).astype(jnp.int32), take(ROUTE_ID + 1).astype(jnp.int32)


def _moe_plan(counts, n_blocks):
    cnt = counts
    pc = (cnt + ROW_CHUNK - 1) // ROW_CHUNK * ROW_CHUNK
    lstart = jnp.cumsum(pc, axis=1) - pc
    tot = jnp.sum(pc, axis=0)
    tot_pad = (tot + FFN_BLOCK - 1) // FFN_BLOCK * FFN_BLOCK
    eend = jnp.cumsum(tot_pad)
    estart = eend - tot_pad
    base = estart[None, :] + jnp.cumsum(pc, axis=0) - pc
    n_used = eend[-1] // FFN_BLOCK
    blk = jnp.minimum(jnp.arange(n_blocks, dtype=jnp.int32), n_used - 1)
    bexp = jnp.sum((blk[:, None] * FFN_BLOCK >= eend[None, :]).astype(jnp.int32), axis=1)
    flat = lambda a: a.reshape(-1).astype(jnp.int32)
    nch = pc // ROW_CHUNK
    per_big = BIG_CHUNK // ROW_CHUNK
    nch = jnp.concatenate([flat(nch), flat(jnp.sum(nch // per_big, axis=1)), flat(jnp.sum(nch % per_big, axis=1))])
    return dict(base=flat(base), lstart=flat(lstart), nch=nch,
                gap0=flat(estart + tot), gapn=flat((tot_pad - tot) // ROW_CHUNK),
                bexp=flat(bexp), nused=flat(n_used))


def _segment_copies(t, base_ref, lstart_ref, nch_ref, make, start):
    per_big = BIG_CHUNK // ROW_CHUNK
    if not start:
        n_tiles = nch_ref.shape[0] // (MOE_EXPERTS + 2)

        def wait(rows):
            def body(i, c):
                make(0, 0, rows).wait()
                return c
            return body

        lax.fori_loop(0, nch_ref[n_tiles * MOE_EXPERTS + t], wait(BIG_CHUNK), 0)
        lax.fori_loop(0, nch_ref[n_tiles * (MOE_EXPERTS + 1) + t], wait(ROW_CHUNK), 0)
        return

    def per_expert(e, carry):
        idx = t * MOE_EXPERTS + e
        loc = lstart_ref[idx]
        glob = base_ref[idx]
        n_big = nch_ref[idx] // per_big
        n_small = nch_ref[idx] - n_big * per_big

        def piece(rows, first):
            def body(i, c):
                off = first + i * rows
                cp = make(pl.multiple_of(loc + off, ROW_CHUNK), pl.multiple_of(glob + off, ROW_CHUNK), rows)
                cp.start() if start else cp.wait()
                return c
            return body

        carry = lax.fori_loop(0, n_big, piece(BIG_CHUNK, 0), carry)
        return lax.fori_loop(0, n_small, piece(ROW_CHUNK, n_big * BIG_CHUNK), carry)

    lax.fori_loop(0, MOE_EXPERTS, per_expert, 0)


def _dispatch_kernel(base_ref, lstart_ref, nch_ref, gap0_ref, gapn_ref, nused_ref, h_ref, route_ref, xs_ref,
                     buf_ref, sem):
    t = pl.program_id(0)
    tile = h_ref.shape[0]
    slots = buf_ref.shape[1]
    rt = route_ref[...].T
    e1, e2 = _route_ids(rt, 0)
    sub = lax.broadcasted_iota(jnp.int32, rt.shape, 0)
    oh1, oh2 = sub == e1, sub == e2
    hit = jnp.where(oh1 | oh2, 1.0, 0.0).astype(BF16)
    before = (lax.broadcasted_iota(jnp.int32, (tile, tile), 0)
              < lax.broadcasted_iota(jnp.int32, (tile, tile), 1))
    rank = _dot(hit, jnp.where(before, 1.0, 0.0).astype(BF16))
    sub1 = lax.broadcasted_iota(jnp.int32, (rt.shape[0], 1), 0)
    seg = jnp.zeros((rt.shape[0], 1), F32)
    for e in range(MOE_EXPERTS):
        seg = jnp.where(sub1 == e, lstart_ref[t * MOE_EXPERTS + e].astype(F32), seg)
    slot_of = rank + seg
    pos1 = jnp.sum(jnp.where(oh1, slot_of, 0.0), axis=0, keepdims=True).astype(jnp.int32)
    pos2 = jnp.sum(jnp.where(oh2, slot_of, 0.0), axis=0, keepdims=True).astype(jnp.int32)
    slot = lax.broadcasted_iota(jnp.int32, (slots, tile), 0).astype(jnp.int16)
    one, zero = jnp.ones((), BF16), jnp.zeros((), BF16)
    perm = jnp.where(slot == pos1.astype(jnp.int16), one, jnp.where(slot == pos2.astype(jnp.int16), one, zero))
    cur = t % 2
    buf_ref[cur] = _dot(perm, h_ref[...]).astype(BF16)

    def maker(b):
        def make(loc, glob, rows):
            return pltpu.make_async_copy(buf_ref.at[b, pl.ds(loc, rows)], xs_ref.at[pl.ds(glob, rows)], sem.at[b])
        return make

    _segment_copies(t, base_ref, lstart_ref, nch_ref, maker(cur), True)

    @pl.when(t > 0)
    def _():
        _segment_copies(t - 1, base_ref, lstart_ref, nch_ref, maker(1 - cur), False)

    @pl.when(t == pl.num_programs(0) - 1)
    def _():
        _segment_copies(t, base_ref, lstart_ref, nch_ref, maker(cur), False)
        buf_ref[cur, 0:FFN_BLOCK, :] = jnp.zeros((FFN_BLOCK, buf_ref.shape[2]), BF16)
        n_blocks = xs_ref.shape[0] // FFN_BLOCK

        def fill(start):
            def per_expert(e, carry):
                def chunk(i, c):
                    row = pl.multiple_of(gap0_ref[e] + i * ROW_CHUNK, ROW_CHUNK)
                    cp = pltpu.make_async_copy(buf_ref.at[cur, 0:ROW_CHUNK], xs_ref.at[pl.ds(row, ROW_CHUNK)],
                                               sem.at[cur])
                    cp.start() if start else cp.wait()
                    return c
                return lax.fori_loop(0, gapn_ref[e], chunk, carry)

            def tail(b, c):
                row = pl.multiple_of(b * FFN_BLOCK, FFN_BLOCK)
                cp = pltpu.make_async_copy(buf_ref.at[cur, 0:FFN_BLOCK], xs_ref.at[pl.ds(row, FFN_BLOCK)],
                                           sem.at[cur])
                cp.start() if start else cp.wait()
                return c

            lax.fori_loop(0, MOE_EXPERTS, per_expert, 0)
            lax.fori_loop(nused_ref[0], n_blocks, tail, 0)

        fill(True)
        fill(False)


def _dispatch(plan, h2, route, n_rows):
    tile, d = DISPATCH_TILE, D_MODEL
    n_tiles = h2.shape[0] // tile
    return pl.pallas_call(
        _dispatch_kernel,
        out_shape=jax.ShapeDtypeStruct((n_rows, d), BF16),
        grid_spec=pltpu.PrefetchScalarGridSpec(
            num_scalar_prefetch=6,
            grid=(n_tiles,),
            in_specs=[pl.BlockSpec((tile, d), lambda t, *_: (t, 0)),
                      pl.BlockSpec((tile, ROUTER_PAD), lambda t, *_: (t, 0))],
            out_specs=pl.BlockSpec(memory_space=pl.ANY),
            scratch_shapes=[pltpu.VMEM((2, SLOT_ROWS, d), BF16), pltpu.SemaphoreType.DMA((2,))],
        ),
        compiler_params=_params(("arbitrary",)),
        name="moe_dispatch",
    )(plan["base"], plan["lstart"], plan["nch"], plan["gap0"], plan["gapn"], plan["nused"], h2, route)


def _ffn_kernel(bexp_ref, nused_ref, x_ref, w1_ref, w3_ref, w2_ref, y_ref, w13_s, w2_s):
    b = pl.program_id(0)
    used = b < nused_ref[0]
    de = D_EXPERT

    @pl.when(used & ((b == 0) | (bexp_ref[b] != bexp_ref[jnp.maximum(b - 1, 0)])))
    def _():
        w13_s[:, :de] = w1_ref[...].astype(BF16)
        w13_s[:, de:] = w3_ref[...].astype(BF16)
        w2_s[...] = w2_ref[...].astype(BF16)

    @pl.when(used)
    def _():
        ab = _dot(x_ref[...], w13_s[...])
        a = ab[:, :de]
        hid = (a * _sigmoid(a)) * ab[:, de:]
        y_ref[...] = _dot(hid.astype(BF16), w2_s[...]).astype(BF16)

    @pl.when(jnp.logical_not(used))
    def _():
        y_ref[...] = jnp.zeros_like(y_ref)


def _expert_ffn(plan, xs, layer, w1, w3, w2):
    d = D_MODEL
    n_blocks = xs.shape[0] // FFN_BLOCK
    row = lambda b, bexp, nused: (jnp.minimum(b, nused[0] - 1), 0)
    wsel = lambda b, bexp, nused: (layer, bexp[b], 0, 0)
    return pl.pallas_call(
        _ffn_kernel,
        out_shape=jax.ShapeDtypeStruct(xs.shape, BF16),
        grid_spec=pltpu.PrefetchScalarGridSpec(
            num_scalar_prefetch=2,
            grid=(n_blocks,),
            in_specs=[pl.BlockSpec((FFN_BLOCK, d), row),
                      pl.BlockSpec((None, None, d, D_EXPERT), wsel),
                      pl.BlockSpec((None, None, d, D_EXPERT), wsel),
                      pl.BlockSpec((None, None, D_EXPERT, d), wsel)],
            out_specs=pl.BlockSpec((FFN_BLOCK, d), lambda b, bexp, nused: (b, 0)),
            scratch_shapes=[pltpu.VMEM((d, 2 * D_EXPERT), BF16), pltpu.VMEM((D_EXPERT, d), BF16)],
        ),
        compiler_params=_params(("arbitrary",)),
        name="moe_ffn",
    )(plan["bexp"], plan["nused"], xs, w1, w3, w2)


def _combine_kernel(base_ref, lstart_ref, nch_ref, route_ref, x1_ref, mod_ref, ys_ref, *rest, n_out_tiles):
    g_ref = rest[0] if len(rest) == 4 else None
    o_ref, buf_ref, sem = rest[-3:]
    d = D_MODEL
    t = pl.program_id(0)
    tile = route_ref.shape[0]
    slots = buf_ref.shape[1]
    cur = t % 2

    def maker(b):
        def make(loc, glob, rows):
            return pltpu.make_async_copy(ys_ref.at[pl.ds(glob, rows)], buf_ref.at[b, pl.ds(loc, rows)], sem.at[b])
        return make

    @pl.when(t == 0)
    def _():
        buf_ref[...] = jnp.zeros_like(buf_ref)
        _segment_copies(t, base_ref, lstart_ref, nch_ref, maker(cur), True)

    @pl.when(t + 1 < pl.num_programs(0))
    def _():
        _segment_copies(t + 1, base_ref, lstart_ref, nch_ref, maker(1 - cur), True)

    route = route_ref[...]
    e1, e2 = _route_ids(route, 1)
    lane = lax.broadcasted_iota(jnp.int32, route.shape, 1)
    oh1, oh2 = lane == e1, lane == e2
    hit = jnp.where(oh1 | oh2, 1.0, 0.0).astype(BF16)
    before = (lax.broadcasted_iota(jnp.int32, (tile, tile), 1)
              < lax.broadcasted_iota(jnp.int32, (tile, tile), 0))
    rank = _dot(jnp.where(before, 1.0, 0.0).astype(BF16), hit)
    lane1 = lax.broadcasted_iota(jnp.int32, (1, route.shape[1]), 1)
    seg = jnp.zeros((1, route.shape[1]), F32)
    for e in range(MOE_EXPERTS):
        seg = jnp.where(lane1 == e, lstart_ref[t * MOE_EXPERTS + e].astype(F32), seg)
    slot_of = rank + seg
    pos1 = jnp.sum(jnp.where(oh1, slot_of, 0.0), axis=1, keepdims=True).astype(jnp.int32)
    pos2 = jnp.sum(jnp.where(oh2, slot_of, 0.0), axis=1, keepdims=True).astype(jnp.int32)
    slot = lax.broadcasted_iota(jnp.int32, (tile, slots), 1).astype(jnp.int16)
    w1 = route[:, ROUTE_W:ROUTE_W + 1].astype(BF16)
    w2 = route[:, ROUTE_W + 1:ROUTE_W + 2].astype(BF16)
    comb = jnp.where(slot == pos1.astype(jnp.int16), w1,
                     jnp.where(slot == pos2.astype(jnp.int16), w2, jnp.zeros((), BF16)))
    _segment_copies(t, base_ref, lstart_ref, nch_ref, maker(cur), False)
    moe = _dot(comb, buf_ref[cur])
    x2 = x1_ref[...] + mod_ref[:, 5 * d:6 * d] * moe
    if g_ref is None:
        o_ref[...] = x2
    else:
        @pl.when(t < n_out_tiles)
        def _():
            o_ref[...] = x2 * lax.rsqrt(jnp.mean(x2 * x2, axis=-1, keepdims=True) + EPS) * g_ref[...]


def _combine(lay, plan, route, x1, mods, layer, ys, g_final):
    tile, d = DISPATCH_TILE, D_MODEL
    assert lay.seq % tile == 0 and (lay.batch * lay.ctx) % tile == 0
    lat_tiles = lay.seq // tile
    n_lat = lay.batch * lat_tiles
    n_tiles = lay.n_tok // tile
    row = lambda t: jnp.where(t < n_lat, t // lat_tiles, lay.batch)
    final = g_final is not None
    n_out_tiles = n_lat if final else n_tiles
    extra_specs = [pl.BlockSpec((1, d), lambda t, *_: (0, 0))] if final else []
    extra_args = [g_final] if final else []
    return pl.pallas_call(
        functools.partial(_combine_kernel, n_out_tiles=n_out_tiles),
        out_shape=jax.ShapeDtypeStruct((n_out_tiles * tile, d), F32),
        grid_spec=pltpu.PrefetchScalarGridSpec(
            num_scalar_prefetch=3,
            grid=(n_tiles,),
            in_specs=[pl.BlockSpec((tile, ROUTER_PAD), lambda t, *_: (t, 0)),
                      pl.BlockSpec((tile, d), lambda t, *_: (t, 0)),
                      pl.BlockSpec((None, None, 1, N_MOD * d), lambda t, *_: (layer, row(t), 0, 0)),
                      pl.BlockSpec(memory_space=pl.ANY)] + extra_specs,
            out_specs=pl.BlockSpec((tile, d), lambda t, *_: (jnp.minimum(t, n_out_tiles - 1), 0)),
            scratch_shapes=[pltpu.VMEM((2, SLOT_ROWS, d), BF16), pltpu.SemaphoreType.DMA((2,))],
        ),
        compiler_params=_params(("arbitrary",)),
        name="moe_combine",
    )(plan["base"], plan["lstart"], plan["nch"], route, x1, mods, ys, *extra_args)


def _moe(lay, h2, x1, route, tile_counts, mods, layer, w1, w3, w2, g_final=None):
    n_tiles = lay.n_tok // DISPATCH_TILE
    max_rows = (MOE_TOPK * lay.n_tok + n_tiles * MOE_EXPERTS * (ROW_CHUNK - 1)
                + MOE_EXPERTS * (FFN_BLOCK - 1))
    n_blocks = -(-max_rows // FFN_BLOCK)
    counts = jnp.sum(tile_counts[:, 0, :MOE_EXPERTS].reshape(n_tiles, DISPATCH_TILE // lay.tile, MOE_EXPERTS), axis=1)
    plan = _moe_plan(counts, n_blocks)
    xs = _dispatch(plan, h2, route, n_blocks * FFN_BLOCK)
    ys = _expert_ffn(plan, xs, layer, w1, w3, w2)
    return _combine(lay, plan, route, x1, mods, layer, ys, g_final)


def _rope_tables(seq, ctx):
    pos = jnp.arange(seq, dtype=jnp.int32)
    inv_freq = ROPE_BASE ** (-jnp.arange(0, AXIS_DIM, 2, dtype=F32) / AXIS_DIM)
    ang_row = (pos // GRID_W).astype(F32)[:, None] * inv_freq
    ang_col = (pos % GRID_W).astype(F32)[:, None] * inv_freq
    cos_h = jnp.concatenate([jnp.cos(ang_row)] * 2 + [jnp.cos(ang_col)] * 2, axis=1)
    sin_h = jnp.concatenate([-jnp.sin(ang_row), jnp.sin(ang_row), -jnp.sin(ang_col), jnp.sin(ang_col)], axis=1)
    reps = LANES // SWA_HEAD_DIM
    cos_t = jnp.concatenate([jnp.tile(cos_h, (1, reps)), jnp.ones((ctx, LANES), F32)], axis=0)
    sin_t = jnp.concatenate([jnp.tile(sin_h, (1, reps)), jnp.zeros((ctx, LANES), F32)], axis=0)
    return cos_t, sin_t


def _channel_dft():
    i = jnp.arange(FNET_GROUP_W, dtype=jnp.int32)
    ang = ((i[:, None] * i[None, :]) % FNET_GROUP_W).astype(F32) * (2.0 * math.pi / FNET_GROUP_W)
    return jnp.concatenate([jnp.cos(ang), jnp.sin(ang)], axis=1).astype(BF16)


def kernel(x, c, ctx, c_ctx, w_ada, b_ada, g_mix, g_ffn, w_in, w_decay_down, w_decay_up, b_decay, g_gla, sink,
           w_pa, w_pb, w_pc, w_out, w_router_group, w_router_expert, w1, w3, w2, g_final):
    batch, seq, d = x.shape
    n_ctx = ctx.shape[1]
    depth = w_ada.shape[0]
    assert d == D_MODEL and seq % GRID_W == 0 and seq % n_ctx == 0
    lay = _Layout(batch, seq, n_ctx, TOKEN_TILE)

    rows = -(-(batch + 1) // 8) * 8
    c_all = jnp.zeros((rows, d), F32).at[:batch].set(c).at[batch].set(c_ctx)
    mods_all = _adaln(c_all, w_ada, b_ada).reshape(depth, rows, 1, N_MOD * d)

    cos_t, sin_t = _rope_tables(seq, n_ctx)
    cs = _channel_dft()
    fourier_tile = min(seq // 2, FOURIER_TILE)
    assert fourier_tile % TOKEN_TILE == 0 and seq % (2 * fourier_tile) == 0
    lat_mats = _dft_half_mats(seq, fourier_tile)
    c_ctx_m, s_ctx_m = _dft_mats(n_ctx)

    stream = ((x.reshape(batch * seq, d), ctx.reshape(batch * n_ctx, d)), 0)

    rank = w_decay_down.shape[-1]
    w_in_bf = w_in.astype(BF16)
    down = jnp.concatenate([w_decay_down[:, 0], w_decay_down[:, 1]], axis=2)
    down = jnp.pad(down, ((0, 0), (0, 0), (0, DECAY_PAD - 2 * rank))).astype(BF16)
    w_up = jnp.zeros((depth, DECAY_PAD, 2 * GLA_QK_W), F32)
    w_up = w_up.at[:, :rank, :GLA_QK_W].set(w_decay_up[:, 0]).at[:, rank:2 * rank, GLA_QK_W:].set(w_decay_up[:, 1])
    w_up = w_up.astype(BF16)
    b_dec = b_decay.reshape(depth, 1, 2 * GLA_QK_W)
    sink_b = jnp.broadcast_to(sink[:, :, None], (depth, SWA_HEADS, LANES))
    w_router = jnp.zeros((depth, d, ROUTER_PAD), F32)
    w_router = w_router.at[:, :, :MOE_EXPERTS].set(w_router_expert)
    w_router = w_router.at[:, :, MOE_EXPERTS:MOE_EXPERTS + MOE_GROUPS].set(w_router_group)
    w_router_hi = w_router.astype(BF16)
    w_router = jnp.concatenate([w_router_hi, (w_router - w_router_hi.astype(F32)).astype(BF16)], axis=2)
    g_mix3, g_ffn3, g_gla3 = g_mix[:, None, :], g_ffn[:, None, :], g_gla[:, None, :]
    w_pa_bf, w_pb_bf, w_pc_bf, w_out_bf = (w.astype(BF16) for w in (w_pa, w_pb, w_pc, w_out))

    for l in range(depth):
        last = l == depth - 1
        (k_a, v_a, k_c, v_c, q_a, r_a, u_cos, u_sin, q_c, gate_logits, la_f, la_b) = _inproj(
            lay, stream, g_mix3, mods_all, l, w_in_bf, down, w_up, b_dec, cs, cos_t, sin_t)

        o_f, o_b = _gla(lay, q_a, k_a, v_a, la_f, la_b)
        yb_lat = _fourier_half(u_cos, u_sin, lat_mats, seq, fourier_tile)
        yc_lat = _swa(lay, q_c, k_c, v_c, sink_b, l)
        if last:
            yb_ctx = jnp.zeros((n_ctx, batch * FNET_W), BF16)
            yc_ctx = jnp.zeros((batch * n_ctx, SWA_Q_W), BF16)
        else:
            yb_ctx = _fourier(u_cos, u_sin, c_ctx_m, s_ctx_m, seq, n_ctx, n_ctx, n_ctx)
            yc_ctx = _ctx_attn(lay, q_c, k_c, v_c, sink_b, l)

        x1, h2, gates, tile_counts = _merge(lay, stream, o_f, o_b, r_a, yb_lat, fourier_tile, yb_ctx, yc_lat, yc_ctx,
                                            gate_logits, mods_all, l, g_gla3, g_ffn3,
                                            w_pa_bf, w_pb_bf, w_pc_bf, w_out_bf, w_router)
        xs = _moe(lay, h2, x1, gates, tile_counts, mods_all, l, w1, w3, w2,
                  g_final.reshape(1, d) if last else None)
        stream = ((xs,), None)

    return xs.reshape(batch, seq, d)
```

```python
import functools
import math

import jax
import jax.numpy as jnp
from jax import lax
from jax.experimental import pallas as pl
from jax.experimental.pallas import tpu as pltpu

F32 = jnp.float32
BF16 = jnp.bfloat16
HIGHEST = lax.Precision.HIGHEST
LOG2E = math.log2(math.e)

D_MODEL = 1024
GRID_W = 64
EPS = 1e-6
N_MOD = 6
GLA_HEADS = 4
GLA_DK = 64
GLA_DV = 128
GLA_TAU = 16.0
GLA_CHUNK = 64
GLA_SCALE = GLA_DK ** -0.5
FNET_GROUPS = 4
FNET_GROUP_W = 128
SWA_HEADS = 8
SWA_KV_HEADS = 2
SWA_HEAD_DIM = 64
WINDOW = 128
ROPE_BASE = 10000.0
AXIS_DIM = SWA_HEAD_DIM // 2
MOE_GROUPS = 4
MOE_EXPERTS_PER_GROUP = 4
MOE_EXPERTS = MOE_GROUPS * MOE_EXPERTS_PER_GROUP
MOE_TOPK = 2
D_EXPERT = 512

GLA_QK_W = GLA_HEADS * GLA_DK
GLA_V_W = GLA_HEADS * GLA_DV
FNET_W = FNET_GROUPS * FNET_GROUP_W
SWA_Q_W = SWA_HEADS * SWA_HEAD_DIM
SWA_KV_W = SWA_KV_HEADS * SWA_HEAD_DIM
IN_SIZES = (GLA_QK_W, GLA_V_W, SWA_KV_W, SWA_KV_W, GLA_QK_W, GLA_V_W, FNET_W, SWA_Q_W, 3 * D_MODEL)
IN_OFFS = tuple(int(sum(IN_SIZES[:i])) for i in range(len(IN_SIZES) + 1))
IN_COLS = IN_OFFS[-1]

LANES = 128
TOKEN_TILE = 256
DECAY_PAD = LANES
ROUTER_PAD = LANES
ROUTE_ID = 0
ROUTE_W = 2
DISPATCH_TILE = 512
ROW_CHUNK = 16
BIG_CHUNK = 64
FOURIER_TILE = 512
FOURIER_COL_SPLIT = 2
FFN_BLOCK = 512
SLOT_ROWS = -(-(MOE_TOPK * DISPATCH_TILE + MOE_EXPERTS * (ROW_CHUNK - 1)) // LANES) * LANES
VMEM_LIMIT = 56 * 1024 * 1024


def _layer_block(a, layer):
    return pl.BlockSpec((None,) + a.shape[1:], lambda *_: (layer,) + (0,) * (a.ndim - 1))


def _params(sem, vmem=VMEM_LIMIT):
    return pltpu.CompilerParams(dimension_semantics=sem, vmem_limit_bytes=vmem)


def _sigmoid(x):
    return 0.5 * jnp.tanh(0.5 * x) + 0.5


def _dot(a, b):
    return jnp.dot(a, b, preferred_element_type=F32)


def _dot_nt(a, b):
    return lax.dot_general(a, b, (((1,), (1,)), ((), ())), preferred_element_type=F32)


def _ada_kernel(c_ref, w_ref, b_ref, o_ref):
    c = c_ref[...]
    a = c * _sigmoid(c)
    o_ref[...] = jnp.dot(a, w_ref[...], preferred_element_type=F32, precision=HIGHEST) + b_ref[...]


def _adaln(c_all, w_ada, b_ada):
    depth, d, n = w_ada.shape
    rows = c_all.shape[0]
    tn = 1536
    return pl.pallas_call(
        _ada_kernel,
        out_shape=jax.ShapeDtypeStruct((depth, rows, n), F32),
        grid=(depth, n // tn),
        in_specs=[pl.BlockSpec((rows, d), lambda l, j: (0, 0)),
                  pl.BlockSpec((None, d, tn), lambda l, j: (l, 0, j)),
                  pl.BlockSpec((None, 1, tn), lambda l, j: (l, 0, j))],
        out_specs=pl.BlockSpec((None, rows, tn), lambda l, j: (l, 0, j)),
        compiler_params=_params(("parallel", "parallel")),
        name="adaln",
    )(c_all, w_ada, b_ada.reshape(depth, 1, n))


class _Layout:
    def __init__(self, batch, seq, ctx, tile):
        assert seq % tile == 0 and ctx % tile == 0
        self.batch, self.seq, self.ctx, self.tile = batch, seq, ctx, tile
        self.lat_tiles = seq // tile
        self.ctx_tiles = ctx // tile
        self.n_lat = batch * self.lat_tiles
        self.n_tiles = self.n_lat + batch * self.ctx_tiles
        self.n_tok = self.n_tiles * tile

    def batch_of(self, t):
        return jnp.where(t < self.n_lat, t // self.lat_tiles, (t - self.n_lat) // self.ctx_tiles)

    def mod_row(self, t):
        return jnp.where(t < self.n_lat, t // self.lat_tiles, self.batch)

    def stream_specs(self, width, ctx_first):
        if ctx_first is None:
            return [pl.BlockSpec((self.tile, width), lambda t: (t, 0))]
        lat = pl.BlockSpec((self.tile, width), lambda t: (jnp.minimum(t, self.n_lat - 1), 0))
        ctx = pl.BlockSpec((self.tile, width), lambda t: (ctx_first + jnp.maximum(t - self.n_lat, 0), 0))
        return [lat, ctx]

    def seq_tile(self, t):
        return jnp.where(t < self.n_lat, t % self.lat_tiles,
                         self.lat_tiles + (t - self.n_lat) % self.ctx_tiles)


def _rope(x, cos, sin_signed):
    n = x.shape[-1]
    lane = lax.broadcasted_iota(jnp.int32, x.shape, 1)
    half = AXIS_DIM // 2
    partner = jnp.where((lane & half) == 0, pltpu.roll(x, n - half, 1), pltpu.roll(x, half, 1))
    return x * cos + partner * sin_signed


def _read_stream(x_refs, n_lat):
    if len(x_refs) == 1:
        return x_refs[0][...]
    return jnp.where(pl.program_id(0) >= n_lat, x_refs[1][...], x_refs[0][...])


def _inproj_kernel(*refs, n_lat, n_x):
    _inproj_body(_read_stream(refs[:n_x], n_lat), *refs[n_x:])


def _inproj_body(x, g_ref, mod_ref, w_ref, wd_ref, wu_ref, bdec_ref, cs_ref, cos_ref, sin_ref,
                 ka_ref, va_ref, kc_ref, vc_ref, qa_ref, ra_ref, ua_ref, us_ref, qc_ref, gl_ref,
                 laf_ref, lab_ref):
    d = D_MODEL
    shift = mod_ref[:, 0:d]
    scale = mod_ref[:, d:2 * d]
    h = x * lax.rsqrt(jnp.mean(x * x, axis=-1, keepdims=True) + EPS) * g_ref[...]
    hb = (h * (1.0 + scale) + shift).astype(BF16)

    def proj(i):
        return _dot(hb, w_ref[:, IN_OFFS[i]:IN_OFFS[i + 1]])

    ka_ref[...] = proj(0).astype(BF16)
    va_ref[...] = proj(1).astype(BF16)
    cos = cos_ref[...]
    sin = sin_ref[...]
    kc_ref[...] = _rope(proj(2), cos, sin).astype(BF16)
    vc_ref[...] = proj(3).astype(BF16)
    qa_ref[...] = proj(4).astype(BF16)
    r = proj(5)
    ra_ref[...] = (r * _sigmoid(r)).astype(BF16)
    u = proj(6).astype(BF16)
    for g in range(FNET_GROUPS):
        sl = slice(g * FNET_GROUP_W, (g + 1) * FNET_GROUP_W)
        ab = _dot(u[:, sl], cs_ref[...])
        ua_ref[:, sl] = ab[:, :FNET_GROUP_W].astype(BF16)
        us_ref[:, sl] = ab[:, FNET_GROUP_W:].astype(BF16)
    reps = SWA_Q_W // LANES
    qc_ref[...] = (_rope(proj(7), jnp.concatenate([cos] * reps, axis=1), jnp.concatenate([sin] * reps, axis=1))
                   * (SWA_HEAD_DIM ** -0.5 * LOG2E)).astype(BF16)
    gl_ref[...] = _sigmoid(proj(8)).astype(BF16)
    low = _dot(hb, wd_ref[...]).astype(BF16)
    z = _dot(low, wu_ref[...]) + bdec_ref[...]
    la = (jnp.minimum(z, 0.0) - jnp.log(1.0 + jnp.exp(-jnp.abs(z)))) * (1.0 / GLA_TAU)
    laf_ref[...] = la[:, :GLA_QK_W]
    lab_ref[...] = la[:, GLA_QK_W:]


def _inproj(lay, x, g_mix, mods, layer, w_in, w_down, w_up, b_dec, cs, cos_t, sin_t):
    tm, d = lay.tile, D_MODEL
    n = lay.n_tok
    bf = lambda w: jax.ShapeDtypeStruct((n, w), BF16)
    tok = lambda w: pl.BlockSpec((tm, w), lambda t: (t, 0))
    const = lambda a: pl.BlockSpec(a.shape, lambda t: (0,) * a.ndim)
    seq_rows = lay.seq + lay.ctx
    fnet_shape = jax.ShapeDtypeStruct((seq_rows, lay.batch * FNET_W), BF16)
    fnet_spec = pl.BlockSpec((tm, FNET_W), lambda t: (lay.seq_tile(t), lay.batch_of(t)))
    pos_spec = pl.BlockSpec((tm, LANES), lambda t: (lay.seq_tile(t), 0))
    return pl.pallas_call(
        functools.partial(_inproj_kernel, n_lat=lay.n_lat, n_x=len(x[0])),
        out_shape=(bf(GLA_QK_W), bf(GLA_V_W), bf(SWA_KV_W), bf(SWA_KV_W), bf(GLA_QK_W), bf(GLA_V_W),
                   fnet_shape, fnet_shape, bf(SWA_Q_W), bf(3 * d),
                   jax.ShapeDtypeStruct((n, GLA_QK_W), F32), jax.ShapeDtypeStruct((n, GLA_QK_W), F32)),
        grid=(lay.n_tiles,),
        in_specs=lay.stream_specs(d, x[1]) + [_layer_block(g_mix, layer),
                  pl.BlockSpec((None, None, 1, N_MOD * d), lambda t: (layer, lay.mod_row(t), 0, 0)),
                  pl.BlockSpec((None,) + w_in.shape[1:], lambda t: (layer, 0, 0)),
                  _layer_block(w_down, layer), _layer_block(w_up, layer), _layer_block(b_dec, layer), const(cs),
                  pos_spec, pos_spec],
        out_specs=(tok(GLA_QK_W), tok(GLA_V_W), tok(SWA_KV_W), tok(SWA_KV_W), tok(GLA_QK_W), tok(GLA_V_W),
                   fnet_spec, fnet_spec, tok(SWA_Q_W), tok(3 * d), tok(GLA_QK_W), tok(GLA_QK_W)),
        compiler_params=_params(("parallel",)),
        name="inproj",
    )(*x[0], g_mix, mods, w_in, w_down, w_up, b_dec, cs, cos_t, sin_t)


def _split2(x):
    hi = x.astype(BF16)
    return hi, (x - hi.astype(F32)).astype(BF16)


def _gla_direction(q_ref, k_ref, v_ref, la_ref, tri_ref, o_ref, s_ref, rev):
    t_rows = q_ref.shape[0]
    c = GLA_CHUNK
    n_sub = t_rows // c
    hk = GLA_QK_W
    la_hi, la_lo = _split2(la_ref[...])
    tri = tri_ref[...]
    lc = _dot(tri, la_hi) + _dot(tri, la_lo)
    q = q_ref[...].astype(F32)
    k = k_ref[...].astype(F32)
    qd = (q * jnp.exp(lc) * GLA_SCALE).astype(BF16)
    kd = (k * jnp.exp(-lc)).astype(BF16)
    hrow = lax.broadcasted_iota(jnp.int32, (GLA_HEADS * c, hk), 0) // c
    hcol = lax.broadcasted_iota(jnp.int32, (GLA_HEADS * c, hk), 1) // GLA_DK
    head_mask = hrow == hcol
    trow = lax.broadcasted_iota(jnp.int32, (GLA_HEADS * c, c), 0) % c
    scol = lax.broadcasted_iota(jnp.int32, (GLA_HEADS * c, c), 1)
    causal = (scol >= trow) if rev else (scol <= trow)
    edge = 0 if rev else c - 1
    lasts = [lc[i * c + edge:i * c + edge + 1] for i in range(n_sub)]
    to_end = jnp.concatenate([jnp.broadcast_to(l, (c, hk)) for l in lasts], axis=0) - lc
    kh_t = (k * jnp.exp(to_end)).T
    pad = jnp.zeros((LANES - n_sub, hk), F32)
    decay_t = jnp.exp(jnp.concatenate(lasts + [pad], axis=0).T)
    yield
    chunk_of = lax.broadcasted_iota(jnp.int32, (1, t_rows), 1) // c
    kv = []
    for h in range(GLA_HEADS):
        kh_h = kh_t[h * GLA_DK:(h + 1) * GLA_DK]
        stack = jnp.concatenate([jnp.where(chunk_of == i, kh_h, 0.0) for i in range(n_sub)], axis=0)
        kv.append(_dot(stack.astype(BF16), v_ref[:, h * GLA_DV:(h + 1) * GLA_DV]))
    yield
    order = range(n_sub - 1, -1, -1) if rev else range(n_sub)
    state = s_ref[...]
    state_at = {}
    for i in order:
        state_at[i] = state
        inc = jnp.concatenate([kv[h][i * GLA_DK:(i + 1) * GLA_DK] for h in range(GLA_HEADS)], axis=0)
        state = decay_t[:, i:i + 1] * state + inc
    s_ref[...] = state
    yield
    inters, scoress = {}, {}
    for i in order:
        rows = slice(i * c, (i + 1) * c)
        q_stack = jnp.where(head_mask, jnp.concatenate([qd[rows]] * GLA_HEADS, axis=0), 0.0).astype(BF16)
        inters[i] = _dot(q_stack, state_at[i].astype(BF16))
        scoress[i] = jnp.where(causal, _dot_nt(q_stack, kd[rows]), 0.0).astype(BF16)
    for i in order:
        yield
        rows = slice(i * c, (i + 1) * c)
        v_i = v_ref[rows, :]
        inter, scores = inters[i], scoress[i]
        outs = []
        for h in range(GLA_HEADS):
            hr = slice(h * c, (h + 1) * c)
            hv = slice(h * GLA_DV, (h + 1) * GLA_DV)
            outs.append(inter[hr] + _dot(scores[hr], v_i[:, hv]))
        o_ref[rows, :] = jnp.concatenate(outs, axis=1).astype(BF16)


def _gla_kernel(qf_ref, kf_ref, vf_ref, laf_ref, qb_ref, kb_ref, vb_ref, lab_ref, trif_ref, trib_ref,
                of_ref, ob_ref, sf_ref, sb_ref):
    @pl.when(pl.program_id(1) == 0)
    def _():
        sf_ref[...] = jnp.zeros_like(sf_ref)
        sb_ref[...] = jnp.zeros_like(sb_ref)

    live = [_gla_direction(qf_ref, kf_ref, vf_ref, laf_ref, trif_ref, of_ref, sf_ref, False),
            _gla_direction(qb_ref, kb_ref, vb_ref, lab_ref, trib_ref, ob_ref, sb_ref, True)]
    while live:
        live = [g for g in live if next(g, "done") != "done"]


def _gla_chunk_masks(tile):
    row = lax.broadcasted_iota(jnp.int32, (tile, tile), 0)
    col = lax.broadcasted_iota(jnp.int32, (tile, tile), 1)
    same = (row // GLA_CHUNK) == (col // GLA_CHUNK)
    return jnp.stack([same & (col <= row), same & (col >= row)]).astype(BF16)


def _gla(lay, q_a, k_a, v_a, la_f, la_b, masks):
    tm = lay.tile
    nc, nl = lay.ctx_tiles, lay.lat_tiles

    def fwd(b, j):
        return jnp.where(j < nc, lay.n_lat + b * nc + j, b * nl + (j - nc))

    def bwd(b, j):
        return jnp.where(j < nc, lay.n_lat + b * nc + (nc - 1 - j), b * nl + (nl - 1 - (j - nc)))

    spec = lambda w, f: pl.BlockSpec((tm, w), lambda b, j: (f(b, j), 0))
    out = jax.ShapeDtypeStruct((lay.n_tok, GLA_V_W), BF16)
    return pl.pallas_call(
        _gla_kernel,
        out_shape=(out, out),
        grid=(lay.batch, nc + nl),
        in_specs=[spec(GLA_QK_W, fwd), spec(GLA_QK_W, fwd), spec(GLA_V_W, fwd), spec(GLA_QK_W, fwd),
                  spec(GLA_QK_W, bwd), spec(GLA_QK_W, bwd), spec(GLA_V_W, bwd), spec(GLA_QK_W, bwd),
                  pl.BlockSpec((None, tm, tm), lambda b, j: (0, 0, 0)),
                  pl.BlockSpec((None, tm, tm), lambda b, j: (1, 0, 0))],
        out_specs=(spec(GLA_V_W, fwd), spec(GLA_V_W, bwd)),
        scratch_shapes=[pltpu.VMEM((GLA_QK_W, GLA_DV), F32), pltpu.VMEM((GLA_QK_W, GLA_DV), F32)],
        compiler_params=_params(("parallel", "arbitrary")),
        name="gla",
    )(q_a, k_a, v_a, la_f, q_a, k_a, v_a, la_b, masks, masks)


def _dft_kernel(c_ref, s_ref, a_ref, b_ref, o_ref, acc_ref, *, scale):
    k = pl.program_id(1)

    @pl.when(k == 0)
    def _():
        acc_ref[...] = jnp.zeros_like(acc_ref)

    acc_ref[...] += _dot(c_ref[...], a_ref[...]) + _dot(s_ref[...], b_ref[...])

    @pl.when(k == pl.num_programs(1) - 1)
    def _():
        o_ref[...] = (acc_ref[...] * scale).astype(BF16)


def _dft_mats(n):
    f = GRID_W
    assert n % f == 0
    k = jnp.arange(n, dtype=jnp.int32)[None, :]

    def table(rows, period):
        ang = ((jnp.arange(rows, dtype=jnp.int32)[:, None] * k) % period).astype(F32) * (2.0 * math.pi / period)
        return jnp.cos(ang), jnp.sin(ang)

    ca, sa = table(n // f, n // f)
    cb, sb = table(f, n)
    cos = ca[:, None, :] * cb[None, :, :] - sa[:, None, :] * sb[None, :, :]
    sin = sa[:, None, :] * cb[None, :, :] + ca[:, None, :] * sb[None, :, :]
    return cos.reshape(n, n).astype(BF16), (-sin).reshape(n, n).astype(BF16)


def _dft_half_kernel(c_ref, s_ref, cx_ref, sx_ref, a_ref, b_ref, j_ref, o_ref, af_ref, bf_ref, *, scale):
    it = pl.program_id(1)
    tm = o_ref.shape[1]
    length = a_ref.shape[0]
    half = length // 2
    sub = 16
    jm = j_ref[...]
    row = lax.broadcasted_iota(jnp.int32, (tm, 1), 0)

    @pl.when(it == 0)
    def _():
        for m in range(half // tm):
            lo = slice(tm * m, tm * (m + 1))
            hi = slice(length - tm * (m + 1), length - tm * m)
            for src, dst, sign in ((a_ref, af_ref, 1.0), (b_ref, bf_ref, -1.0)):
                mirrored = _dot(jm, src[hi, :])
                if m > 0:
                    edge = src[length - tm * m:length - tm * m + sub, :][0:1, :].astype(F32)
                    mirrored = jnp.where(row == 0, edge, mirrored)
                dst[lo, :] = (src[lo, :].astype(F32) + sign * mirrored).astype(BF16)

    af = af_ref[...]
    bf = bf_ref[...]
    nyquist = a_ref[half:half + sub, :][0:1, :].astype(F32)
    p = _dot(c_ref[...], af) + jnp.where((row & 1) == 0, nyquist, -nyquist)
    q = _dot(s_ref[...], bf)
    o_ref[0] = ((p + q) * scale).astype(BF16)
    mirror = ((p - q) * scale).astype(BF16)
    flipped = _dot(jm, mirror)
    first = ((_dot(cx_ref[...], af) - _dot(sx_ref[...], bf))[0:1, :] + nyquist) * scale
    o_ref[1] = jnp.where(row == 0, first, flipped).astype(BF16)


def _dft_half_mats(n, tm):
    f = GRID_W
    half = n // 2
    assert half % f == 0 and half % tm == 0
    k = jnp.arange(half, dtype=jnp.int32)[None, :]

    def table(rows, period):
        ang = ((rows[:, None] * k) % period).astype(F32) * (2.0 * math.pi / period)
        return jnp.cos(ang), jnp.sin(ang)

    ca, sa = table(jnp.arange(half // f, dtype=jnp.int32), n // f)
    cb, sb = table(jnp.arange(f, dtype=jnp.int32), n)
    cos = ca[:, None, :] * cb[None, :, :] - sa[:, None, :] * sb[None, :, :]
    sin = sa[:, None, :] * cb[None, :, :] + ca[:, None, :] * sb[None, :, :]
    n_it = half // tm
    cx, sx = table(jnp.arange(1, n_it + 1, dtype=jnp.int32) * tm, n)
    spread = lambda m: jnp.zeros((n_it, 8, half), F32).at[:, 0, :].set(m).reshape(n_it * 8, half).astype(BF16)
    return (cos.reshape(half, half).astype(BF16), (-sin).reshape(half, half).astype(BF16), spread(cx), spread(-sx))


def _fourier_half(ua, us, mats, length, tm):
    cmat, smat, cx, sx = mats
    width = ua.shape[1]
    half = length // 2
    scale = 1.0 / math.sqrt(length * FNET_GROUP_W)
    r = lax.broadcasted_iota(jnp.int32, (tm, tm), 0)
    c = lax.broadcasted_iota(jnp.int32, (tm, tm), 1)
    jmat = jnp.where(c == tm - r, 1.0, 0.0).astype(BF16)
    once = pl.Buffered(1)
    cols = width // FOURIER_COL_SPLIT
    return pl.pallas_call(
        functools.partial(_dft_half_kernel, scale=scale),
        out_shape=jax.ShapeDtypeStruct((2, half, width), BF16),
        grid=(FOURIER_COL_SPLIT, half // tm),
        in_specs=[pl.BlockSpec((tm, half), lambda j, i: (i, 0)),
                  pl.BlockSpec((tm, half), lambda j, i: (i, 0)),
                  pl.BlockSpec((8, half), lambda j, i: (i, 0)),
                  pl.BlockSpec((8, half), lambda j, i: (i, 0)),
                  pl.BlockSpec((length, cols), lambda j, i: (0, j), pipeline_mode=once),
                  pl.BlockSpec((length, cols), lambda j, i: (0, j), pipeline_mode=once),
                  pl.BlockSpec((tm, tm), lambda j, i: (0, 0), pipeline_mode=once)],
        out_specs=pl.BlockSpec((2, tm, cols), lambda j, i: (0, i, j)),
        scratch_shapes=[pltpu.VMEM((half, cols), BF16), pltpu.VMEM((half, cols), BF16)],
        compiler_params=_params(("arbitrary", "arbitrary")),
        name="fourier_half",
    )(cmat, smat, cx, sx, ua, us, jmat)


def _fourier(ua, us, cmat, smat, row0, length, tm, tk):
    width = ua.shape[1]
    off = row0 // tk
    scale = 1.0 / math.sqrt(length * FNET_GROUP_W)
    return pl.pallas_call(
        functools.partial(_dft_kernel, scale=scale),
        out_shape=jax.ShapeDtypeStruct((length, width), BF16),
        grid=(length // tm, length // tk),
        in_specs=[pl.BlockSpec((tm, tk), lambda i, k: (i, k)),
                  pl.BlockSpec((tm, tk), lambda i, k: (i, k)),
                  pl.BlockSpec((tk, width), lambda i, k: (off + k, 0)),
                  pl.BlockSpec((tk, width), lambda i, k: (off + k, 0))],
        out_specs=pl.BlockSpec((tm, width), lambda i, k: (i, 0)),
        scratch_shapes=[pltpu.VMEM((tm, width), F32)],
        compiler_params=_params(("parallel", "arbitrary")),
        name="fourier",
    )(cmat, smat, ua, us)


def _attend(q_ref, k_all, v_all, valid, sink_ref, o_ref):
    hd = SWA_HEAD_DIM
    heads_per_kv = SWA_HEADS // SWA_KV_HEADS
    k_sw = jnp.concatenate([k_all[:, hd:], k_all[:, :hd]], axis=1)
    v_t = v_all.astype(F32).T
    v_t_sw = jnp.concatenate([v_t[hd:], v_t[:hd]], axis=0)
    lane = lax.broadcasted_iota(jnp.int32, (1, LANES), 1)
    lane_half = (lane < hd, lane >= hd)
    row = lax.broadcasted_iota(jnp.int32, (LANES, 1), 0)
    row_half = (row < hd, row >= hd)
    zero = jnp.zeros((), BF16)
    for p in range(SWA_HEADS // 2):
        acc = None
        for par in range(2):
            head = 2 * p + par
            aligned = head // heads_per_kv == par
            qh = jnp.where(lane_half[par], q_ref[:, p * LANES:(p + 1) * LANES], zero)
            s = _dot_nt(k_all if aligned else k_sw, qh)
            if valid is not None:
                kw = valid.shape[0]
                b = kw // 3
                s = jnp.concatenate([jnp.where(valid[:b], s[:b], -jnp.inf), s[b:2 * b],
                                     jnp.where(valid[2 * b:], s[2 * b:kw], -jnp.inf), s[kw:]], axis=0)
            sink = sink_ref[head:head + 1, 0:1] * LOG2E
            m = jnp.maximum(jnp.max(s, axis=0, keepdims=True), sink)
            e = jnp.exp2(s - m)
            denom = jnp.sum(e, axis=0, keepdims=True) + jnp.exp2(sink - m)
            v_use = jnp.where(row_half[par], v_t if aligned else v_t_sw, 0.0).astype(BF16)
            part = _dot(v_use, e.astype(BF16)) * (1.0 / denom)
            acc = part if acc is None else acc + part
        o_ref[:, p * LANES:(p + 1) * LANES] = acc.T.astype(BF16)


def _swa_kernel(q_ref, kp_ref, kc_ref, kn_ref, kx_ref, vp_ref, vc_ref, vn_ref, vx_ref, sink_ref, o_ref,
                *, seq):
    n = pl.program_id(1)
    w = WINDOW
    k_blocks = [kp_ref[...], kc_ref[0:w, :], kc_ref[w:2 * w, :], kn_ref[...]]
    v_blocks = [vp_ref[...], vc_ref[0:w, :], vc_ref[w:2 * w, :], vn_ref[...]]
    j = lax.broadcasted_iota(jnp.int32, (3 * w, w), 0)
    a = lax.broadcasted_iota(jnp.int32, (3 * w, w), 1)
    band = (j >= a) & (j - a <= 2 * w)
    for half in range(2):
        key_pos = (2 * n + half - 1) * w + j
        valid = band & (key_pos >= 0) & (key_pos < seq)
        k_all = jnp.concatenate(k_blocks[half:half + 3] + [kx_ref[...]], axis=0)
        v_all = jnp.concatenate(v_blocks[half:half + 3] + [vx_ref[...]], axis=0)
        rows = pl.ds(half * w, w)
        _attend(q_ref.at[rows], k_all, v_all, valid, sink_ref, o_ref.at[rows])


def _swa(lay, q_c, k_c, v_c, sink_b, layer):
    w = WINDOW
    assert lay.seq % (2 * w) == 0
    nq = lay.seq // w
    steps = nq // 2
    ctx_blk = (lay.batch * lay.seq) // lay.ctx

    def edge(f):
        return pl.BlockSpec((w, SWA_KV_W), lambda b, n: (b * nq + f(n), 0))

    prev = edge(lambda n: jnp.maximum(2 * n - 1, 0))
    nxt = edge(lambda n: jnp.minimum(2 * n + 2, nq - 1))
    cur = pl.BlockSpec((2 * w, SWA_KV_W), lambda b, n: (b * steps + n, 0))
    ctx_spec = pl.BlockSpec((lay.ctx, SWA_KV_W), lambda b, n: (ctx_blk + b, 0))
    q_spec = pl.BlockSpec((2 * w, SWA_Q_W), lambda b, n: (b * steps + n, 0))
    return pl.pallas_call(
        functools.partial(_swa_kernel, seq=lay.seq),
        out_shape=jax.ShapeDtypeStruct((lay.batch * lay.seq, SWA_Q_W), BF16),
        grid=(lay.batch, steps),
        in_specs=[q_spec, prev, cur, nxt, ctx_spec, prev, cur, nxt, ctx_spec,
                  _layer_block(sink_b, layer)],
        out_specs=q_spec,
        compiler_params=_params(("parallel", "parallel")),
        name="swa",
    )(q_c, k_c, k_c, k_c, k_c, v_c, v_c, v_c, v_c, sink_b)


def _ctx_attn_kernel(q_ref, kx_ref, vx_ref, sink_ref, o_ref):
    _attend(q_ref, kx_ref[...], vx_ref[...], None, sink_ref, o_ref)


def _ctx_attn(lay, q_c, k_c, v_c, sink_b, layer):
    ctx_blk = (lay.batch * lay.seq) // lay.ctx
    spec = lambda wd: pl.BlockSpec((lay.ctx, wd), lambda b: (ctx_blk + b, 0))
    return pl.pallas_call(
        _ctx_attn_kernel,
        out_shape=jax.ShapeDtypeStruct((lay.batch * lay.ctx, SWA_Q_W), BF16),
        grid=(lay.batch,),
        in_specs=[spec(SWA_Q_W), spec(SWA_KV_W), spec(SWA_KV_W),
                  _layer_block(sink_b, layer)],
        out_specs=pl.BlockSpec((lay.ctx, SWA_Q_W), lambda b: (b, 0)),
        compiler_params=_params(("parallel",)),
        name="ctx_attn",
    )(q_c, k_c, v_c, sink_b)


def _merge_kernel(*refs, n_lat, n_x):
    _merge_body(_read_stream(refs[:n_x], n_lat), *refs[n_x:], n_lat=n_lat)


def _merge_body(x, of_ref, ob_ref, ra_ref, ybl_ref, ybc_ref, ycl_ref, ycc_ref, gl_ref, mod_ref,
                ggla_ref, gffn_ref, wpa_ref, wpb_ref, wpc_ref, wout_ref, wr_ref,
                x1_ref, h2_ref, gates_ref, count_ref, *, n_lat):
    d = D_MODEL
    is_ctx = pl.program_id(0) >= n_lat
    o = of_ref[...].astype(F32) + ob_ref[...].astype(F32)
    r = ra_ref[...].astype(F32)
    parts = []
    for h in range(GLA_HEADS):
        sl = slice(h * GLA_DV, (h + 1) * GLA_DV)
        oh = o[:, sl]
        parts.append(oh * lax.rsqrt(jnp.mean(oh * oh, axis=-1, keepdims=True) + EPS) * ggla_ref[...])
    y_a = (jnp.concatenate(parts, axis=1) * r).astype(BF16)
    y_b = jnp.where(is_ctx, ybc_ref[...], ybl_ref[...])
    y_c = jnp.where(is_ctx, ycc_ref[...], ycl_ref[...])
    gl = gl_ref[...].astype(F32)
    mix = (gl[:, 0:d] * _dot(y_a, wpa_ref[...])
           + gl[:, d:2 * d] * _dot(y_b, wpb_ref[...])
           + gl[:, 2 * d:3 * d] * _dot(y_c, wpc_ref[...]))
    y = _dot(mix.astype(BF16), wout_ref[...])
    x1 = x + mod_ref[:, 2 * d:3 * d] * y
    x1_ref[...] = x1
    h2 = x1 * lax.rsqrt(jnp.mean(x1 * x1, axis=-1, keepdims=True) + EPS) * gffn_ref[...]
    h2 = h2 * (1.0 + mod_ref[:, 4 * d:5 * d]) + mod_ref[:, 3 * d:4 * d]
    h2_hi = h2.astype(BF16)
    h2_ref[...] = h2_hi
    h2_lo = (h2 - h2_hi.astype(F32)).astype(BF16)
    both = _dot(h2_hi, wr_ref[...])
    logits = both[:, :ROUTER_PAD] + both[:, ROUTER_PAD:] + _dot(h2_lo, wr_ref[:, :ROUTER_PAD])
    lt = logits.T
    sub = lax.broadcasted_iota(jnp.int32, lt.shape, 0)
    sub_f = sub.astype(F32)
    neg = -jnp.inf
    big = float(ROUTER_PAD)
    is_group = (sub >= MOE_EXPERTS) & (sub < MOE_EXPERTS + MOE_GROUPS)
    gl_m = jnp.where(is_group, lt, neg)
    g_max = jnp.max(gl_m, axis=0, keepdims=True)
    g_sel = jnp.min(jnp.where(gl_m == g_max, sub_f, big), axis=0, keepdims=True) - MOE_EXPERTS
    g_gate = 1.0 / jnp.sum(jnp.where(is_group, jnp.exp(lt - g_max), 0.0), axis=0, keepdims=True)
    lo = g_sel * MOE_EXPERTS_PER_GROUP
    in_group = (sub_f >= lo) & (sub_f < lo + MOE_EXPERTS_PER_GROUP)
    e1 = jnp.where(in_group, lt, neg)
    v1 = jnp.max(e1, axis=0, keepdims=True)
    i1 = jnp.min(jnp.where(e1 == v1, sub_f, big), axis=0, keepdims=True)
    e2 = jnp.where(sub_f == i1, neg, e1)
    v2 = jnp.max(e2, axis=0, keepdims=True)
    i2 = jnp.min(jnp.where(e2 == v2, sub_f, big), axis=0, keepdims=True)
    t = jnp.exp(v2 - v1)
    w1 = g_gate / (1.0 + t)
    w2 = g_gate * t / (1.0 + t)
    route_t = jnp.where(sub == ROUTE_ID, i1, 0.0) + jnp.where(sub == ROUTE_ID + 1, i2, 0.0)
    route_t = route_t + jnp.where(sub == ROUTE_W, w1, 0.0) + jnp.where(sub == ROUTE_W + 1, w2, 0.0)
    route = route_t.T
    gates_ref[...] = route
    lane_f = lax.broadcasted_iota(jnp.int32, route.shape, 1).astype(F32)
    hit = jnp.where((lane_f == route[:, ROUTE_ID:ROUTE_ID + 1]) | (lane_f == route[:, ROUTE_ID + 1:ROUTE_ID + 2]),
                    1.0, 0.0)
    count_ref[...] = jnp.sum(hit, axis=0, keepdims=True).astype(jnp.int32)


def _merge(lay, x, o_f, o_b, r_a, yb_lat, fourier_tile, yb_ctx, yc_lat, yc_ctx, gate_logits, mods, layer, g_gla, g_ffn,
           w_pa, w_pb, w_pc, w_out, w_router):
    tm, d = lay.tile, D_MODEL
    n = lay.n_tok
    tok = lambda w: pl.BlockSpec((tm, w), lambda t: (t, 0))
    const = lambda a: pl.BlockSpec(a.shape, lambda t: (0,) * a.ndim)
    lt, ct = lay.lat_tiles, lay.ctx_tiles
    sub = fourier_tile // tm

    def yb_lat_index(t):
        tl = jnp.minimum(t, lay.n_lat - 1)
        s = tl % lt
        u = lt - 1 - s
        upper = s >= lt // 2
        blk = jnp.where(upper, (u // sub) * sub + sub - 1 - u % sub, s)
        return (upper.astype(jnp.int32), blk, tl // lt)

    yb_lat_spec = pl.BlockSpec((None, tm, FNET_W), yb_lat_index)
    yb_ctx_spec = pl.BlockSpec((tm, FNET_W), lambda t: (jnp.maximum(t - lay.n_lat, 0) % ct,
                                                         jnp.maximum(t - lay.n_lat, 0) // ct))
    yc_lat_spec = pl.BlockSpec((tm, SWA_Q_W), lambda t: (jnp.minimum(t, lay.n_lat - 1), 0))
    yc_ctx_spec = pl.BlockSpec((tm, SWA_Q_W), lambda t: (jnp.maximum(t - lay.n_lat, 0), 0))
    return pl.pallas_call(
        functools.partial(_merge_kernel, n_lat=lay.n_lat, n_x=len(x[0])),
        out_shape=(jax.ShapeDtypeStruct((n, d), F32), jax.ShapeDtypeStruct((n, d), BF16),
                   jax.ShapeDtypeStruct((n, ROUTER_PAD), F32),
                   jax.ShapeDtypeStruct((lay.n_tiles, 1, ROUTER_PAD), jnp.int32)),
        grid=(lay.n_tiles,),
        in_specs=lay.stream_specs(d, x[1]) + [tok(GLA_V_W), tok(GLA_V_W), tok(GLA_V_W), yb_lat_spec, yb_ctx_spec,
                  yc_lat_spec, yc_ctx_spec, tok(3 * d),
                  pl.BlockSpec((None, None, 1, N_MOD * d), lambda t: (layer, lay.mod_row(t), 0, 0)),
                  _layer_block(g_gla, layer), _layer_block(g_ffn, layer), _layer_block(w_pa, layer),
                  _layer_block(w_pb, layer), _layer_block(w_pc, layer), _layer_block(w_out, layer),
                  _layer_block(w_router, layer)],
        out_specs=(tok(d), tok(d), tok(ROUTER_PAD),
                   pl.BlockSpec((None, 1, ROUTER_PAD), lambda t: (t, 0, 0))),
        compiler_params=_params(("parallel",)),
        name="merge",
    )(*x[0], o_f, o_b, r_a, yb_lat, yb_ctx, yc_lat, yc_ctx, gate_logits, mods, g_gla, g_ffn,
      w_pa, w_pb, w_pc, w_out, w_router)


def _route_ids(route, axis):
    take = (lambda i: route[:, i:i + 1]) if axis == 1 else (lambda i: route[i:i + 1, :])
    return take(ROUTE_ID).astype(jnp.int32), take(ROUTE_ID + 1).astype(jnp.int32)


def _moe_plan(counts, n_blocks):
    cnt = counts
    pc = (cnt + ROW_CHUNK - 1) // ROW_CHUNK * ROW_CHUNK
    lstart = jnp.cumsum(pc, axis=1) - pc
    tot = jnp.sum(pc, axis=0)
    tot_pad = (tot + FFN_BLOCK - 1) // FFN_BLOCK * FFN_BLOCK
    eend = jnp.cumsum(tot_pad)
    estart = eend - tot_pad
    base = estart[None, :] + jnp.cumsum(pc, axis=0) - pc
    n_used = eend[-1] // FFN_BLOCK
    blk = jnp.minimum(jnp.arange(n_blocks, dtype=jnp.int32), n_used - 1)
    bexp = jnp.sum((blk[:, None] * FFN_BLOCK >= eend[None, :]).astype(jnp.int32), axis=1)
    flat = lambda a: a.reshape(-1).astype(jnp.int32)
    nch = pc // ROW_CHUNK
    per_big = BIG_CHUNK // ROW_CHUNK
    nch = jnp.concatenate([flat(nch), flat(jnp.sum(nch // per_big, axis=1)), flat(jnp.sum(nch % per_big, axis=1))])
    return dict(base=flat(base), lstart=flat(lstart), nch=nch,
                gap0=flat(estart + tot), gapn=flat((tot_pad - tot) // ROW_CHUNK),
                bexp=flat(bexp), nused=flat(n_used))


def _segment_copies(t, base_ref, lstart_ref, nch_ref, make, start):
    per_big = BIG_CHUNK // ROW_CHUNK
    if not start:
        n_tiles = nch_ref.shape[0] // (MOE_EXPERTS + 2)

        def wait(rows):
            def body(i, c):
                make(0, 0, rows).wait()
                return c
            return body

        lax.fori_loop(0, nch_ref[n_tiles * MOE_EXPERTS + t], wait(BIG_CHUNK), 0)
        lax.fori_loop(0, nch_ref[n_tiles * (MOE_EXPERTS + 1) + t], wait(ROW_CHUNK), 0)
        return

    def per_expert(e, carry):
        idx = t * MOE_EXPERTS + e
        loc = lstart_ref[idx]
        glob = base_ref[idx]
        n_big = nch_ref[idx] // per_big
        n_small = nch_ref[idx] - n_big * per_big

        def piece(rows, first):
            def body(i, c):
                off = first + i * rows
                cp = make(pl.multiple_of(loc + off, ROW_CHUNK), pl.multiple_of(glob + off, ROW_CHUNK), rows)
                cp.start() if start else cp.wait()
                return c
            return body

        carry = lax.fori_loop(0, n_big, piece(BIG_CHUNK, 0), carry)
        return lax.fori_loop(0, n_small, piece(ROW_CHUNK, n_big * BIG_CHUNK), carry)

    lax.fori_loop(0, MOE_EXPERTS, per_expert, 0)


def _dispatch_kernel(base_ref, lstart_ref, nch_ref, gap0_ref, gapn_ref, nused_ref, h_ref, route_ref, xs_ref,
                     buf_ref, sem):
    t = pl.program_id(0)
    tile = h_ref.shape[0]
    slots = buf_ref.shape[1]
    rt = route_ref[...].T
    e1, e2 = _route_ids(rt, 0)
    sub = lax.broadcasted_iota(jnp.int32, rt.shape, 0)
    oh1, oh2 = sub == e1, sub == e2
    hit = jnp.where(oh1 | oh2, 1.0, 0.0).astype(BF16)
    before = (lax.broadcasted_iota(jnp.int32, (tile, tile), 0)
              < lax.broadcasted_iota(jnp.int32, (tile, tile), 1))
    rank = _dot(hit, jnp.where(before, 1.0, 0.0).astype(BF16))
    sub1 = lax.broadcasted_iota(jnp.int32, (rt.shape[0], 1), 0)
    seg = jnp.zeros((rt.shape[0], 1), F32)
    for e in range(MOE_EXPERTS):
        seg = jnp.where(sub1 == e, lstart_ref[t * MOE_EXPERTS + e].astype(F32), seg)
    slot_of = rank + seg
    pos1 = jnp.sum(jnp.where(oh1, slot_of, 0.0), axis=0, keepdims=True).astype(jnp.int32)
    pos2 = jnp.sum(jnp.where(oh2, slot_of, 0.0), axis=0, keepdims=True).astype(jnp.int32)
    slot = lax.broadcasted_iota(jnp.int32, (slots, tile), 0).astype(jnp.int16)
    one, zero = jnp.ones((), BF16), jnp.zeros((), BF16)
    perm = jnp.where(slot == pos1.astype(jnp.int16), one, jnp.where(slot == pos2.astype(jnp.int16), one, zero))
    cur = t % 2
    buf_ref[cur] = _dot(perm, h_ref[...]).astype(BF16)

    def maker(b):
        def make(loc, glob, rows):
            return pltpu.make_async_copy(buf_ref.at[b, pl.ds(loc, rows)], xs_ref.at[pl.ds(glob, rows)], sem.at[b])
        return make

    _segment_copies(t, base_ref, lstart_ref, nch_ref, maker(cur), True)

    @pl.when(t > 0)
    def _():
        _segment_copies(t - 1, base_ref, lstart_ref, nch_ref, maker(1 - cur), False)

    @pl.when(t == pl.num_programs(0) - 1)
    def _():
        _segment_copies(t, base_ref, lstart_ref, nch_ref, maker(cur), False)
        buf_ref[cur, 0:FFN_BLOCK, :] = jnp.zeros((FFN_BLOCK, buf_ref.shape[2]), BF16)
        n_blocks = xs_ref.shape[0] // FFN_BLOCK

        def fill(start):
            def per_expert(e, carry):
                def chunk(i, c):
                    row = pl.multiple_of(gap0_ref[e] + i * ROW_CHUNK, ROW_CHUNK)
                    cp = pltpu.make_async_copy(buf_ref.at[cur, 0:ROW_CHUNK], xs_ref.at[pl.ds(row, ROW_CHUNK)],
                                               sem.at[cur])
                    cp.start() if start else cp.wait()
                    return c
                return lax.fori_loop(0, gapn_ref[e], chunk, carry)

            def tail(b, c):
                row = pl.multiple_of(b * FFN_BLOCK, FFN_BLOCK)
                cp = pltpu.make_async_copy(buf_ref.at[cur, 0:FFN_BLOCK], xs_ref.at[pl.ds(row, FFN_BLOCK)],
                                           sem.at[cur])
                cp.start() if start else cp.wait()
                return c

            lax.fori_loop(0, MOE_EXPERTS, per_expert, 0)
            lax.fori_loop(nused_ref[0], n_blocks, tail, 0)

        fill(True)
        fill(False)


def _dispatch(plan, h2, route, n_rows):
    tile, d = DISPATCH_TILE, D_MODEL
    n_tiles = h2.shape[0] // tile
    return pl.pallas_call(
        _dispatch_kernel,
        out_shape=jax.ShapeDtypeStruct((n_rows, d), BF16),
        grid_spec=pltpu.PrefetchScalarGridSpec(
            num_scalar_prefetch=6,
            grid=(n_tiles,),
            in_specs=[pl.BlockSpec((tile, d), lambda t, *_: (t, 0)),
                      pl.BlockSpec((tile, ROUTER_PAD), lambda t, *_: (t, 0))],
            out_specs=pl.BlockSpec(memory_space=pl.ANY),
            scratch_shapes=[pltpu.VMEM((2, SLOT_ROWS, d), BF16), pltpu.SemaphoreType.DMA((2,))],
        ),
        compiler_params=_params(("arbitrary",)),
        name="moe_dispatch",
    )(plan["base"], plan["lstart"], plan["nch"], plan["gap0"], plan["gapn"], plan["nused"], h2, route)


def _ffn_kernel(bexp_ref, nused_ref, x_ref, w1_ref, w3_ref, w2_ref, y_ref, w13_s, w2_s):
    b = pl.program_id(0)
    used = b < nused_ref[0]
    de = D_EXPERT

    @pl.when(used & ((b == 0) | (bexp_ref[b] != bexp_ref[jnp.maximum(b - 1, 0)])))
    def _():
        w13_s[:, :de] = w1_ref[...].astype(BF16)
        w13_s[:, de:] = w3_ref[...].astype(BF16)
        w2_s[...] = w2_ref[...].astype(BF16)

    @pl.when(used)
    def _():
        ab = _dot(x_ref[...], w13_s[...])
        a = ab[:, :de]
        hid = (a * _sigmoid(a)) * ab[:, de:]
        y_ref[...] = _dot(hid.astype(BF16), w2_s[...]).astype(BF16)

    @pl.when(jnp.logical_not(used))
    def _():
        y_ref[...] = jnp.zeros_like(y_ref)


def _expert_ffn(plan, xs, layer, w1, w3, w2):
    d = D_MODEL
    n_blocks = xs.shape[0] // FFN_BLOCK
    row = lambda b, bexp, nused: (jnp.minimum(b, nused[0] - 1), 0)
    wsel = lambda b, bexp, nused: (layer, bexp[b], 0, 0)
    return pl.pallas_call(
        _ffn_kernel,
        out_shape=jax.ShapeDtypeStruct(xs.shape, BF16),
        grid_spec=pltpu.PrefetchScalarGridSpec(
            num_scalar_prefetch=2,
            grid=(n_blocks,),
            in_specs=[pl.BlockSpec((FFN_BLOCK, d), row),
                      pl.BlockSpec((None, None, d, D_EXPERT), wsel),
                      pl.BlockSpec((None, None, d, D_EXPERT), wsel),
                      pl.BlockSpec((None, None, D_EXPERT, d), wsel)],
            out_specs=pl.BlockSpec((FFN_BLOCK, d), lambda b, bexp, nused: (b, 0)),
            scratch_shapes=[pltpu.VMEM((d, 2 * D_EXPERT), BF16), pltpu.VMEM((D_EXPERT, d), BF16)],
        ),
        compiler_params=_params(("arbitrary",)),
        name="moe_ffn",
    )(plan["bexp"], plan["nused"], xs, w1, w3, w2)


def _combine_kernel(base_ref, lstart_ref, nch_ref, route_ref, x1_ref, mod_ref, ys_ref, *rest, n_out_tiles):
    g_ref = rest[0] if len(rest) == 4 else None
    o_ref, buf_ref, sem = rest[-3:]
    d = D_MODEL
    t = pl.program_id(0)
    tile = route_ref.shape[0]
    slots = buf_ref.shape[1]
    cur = t % 2

    def maker(b):
        def make(loc, glob, rows):
            return pltpu.make_async_copy(ys_ref.at[pl.ds(glob, rows)], buf_ref.at[b, pl.ds(loc, rows)], sem.at[b])
        return make

    @pl.when(t == 0)
    def _():
        buf_ref[...] = jnp.zeros_like(buf_ref)
        _segment_copies(t, base_ref, lstart_ref, nch_ref, maker(cur), True)

    @pl.when(t + 1 < pl.num_programs(0))
    def _():
        _segment_copies(t + 1, base_ref, lstart_ref, nch_ref, maker(1 - cur), True)

    route = route_ref[...]
    e1, e2 = _route_ids(route, 1)
    lane = lax.broadcasted_iota(jnp.int32, route.shape, 1)
    oh1, oh2 = lane == e1, lane == e2
    hit = jnp.where(oh1 | oh2, 1.0, 0.0).astype(BF16)
    before = (lax.broadcasted_iota(jnp.int32, (tile, tile), 1)
              < lax.broadcasted_iota(jnp.int32, (tile, tile), 0))
    rank = _dot(jnp.where(before, 1.0, 0.0).astype(BF16), hit)
    lane1 = lax.broadcasted_iota(jnp.int32, (1, route.shape[1]), 1)
    seg = jnp.zeros((1, route.shape[1]), F32)
    for e in range(MOE_EXPERTS):
        seg = jnp.where(lane1 == e, lstart_ref[t * MOE_EXPERTS + e].astype(F32), seg)
    slot_of = rank + seg
    pos1 = jnp.sum(jnp.where(oh1, slot_of, 0.0), axis=1, keepdims=True).astype(jnp.int32)
    pos2 = jnp.sum(jnp.where(oh2, slot_of, 0.0), axis=1, keepdims=True).astype(jnp.int32)
    slot = lax.broadcasted_iota(jnp.int32, (tile, slots), 1).astype(jnp.int16)
    w1 = route[:, ROUTE_W:ROUTE_W + 1].astype(BF16)
    w2 = route[:, ROUTE_W + 1:ROUTE_W + 2].astype(BF16)
    comb = jnp.where(slot == pos1.astype(jnp.int16), w1,
                     jnp.where(slot == pos2.astype(jnp.int16), w2, jnp.zeros((), BF16)))
    _segment_copies(t, base_ref, lstart_ref, nch_ref, maker(cur), False)
    moe = _dot(comb, buf_ref[cur])
    x2 = x1_ref[...] + mod_ref[:, 5 * d:6 * d] * moe
    if g_ref is None:
        o_ref[...] = x2
    else:
        @pl.when(t < n_out_tiles)
        def _():
            o_ref[...] = x2 * lax.rsqrt(jnp.mean(x2 * x2, axis=-1, keepdims=True) + EPS) * g_ref[...]


def _combine(lay, plan, route, x1, mods, layer, ys, g_final):
    tile, d = DISPATCH_TILE, D_MODEL
    assert lay.seq % tile == 0 and (lay.batch * lay.ctx) % tile == 0
    lat_tiles = lay.seq // tile
    n_lat = lay.batch * lat_tiles
    n_tiles = lay.n_tok // tile
    row = lambda t: jnp.where(t < n_lat, t // lat_tiles, lay.batch)
    final = g_final is not None
    n_out_tiles = n_lat if final else n_tiles
    extra_specs = [pl.BlockSpec((1, d), lambda t, *_: (0, 0))] if final else []
    extra_args = [g_final] if final else []
    return pl.pallas_call(
        functools.partial(_combine_kernel, n_out_tiles=n_out_tiles),
        out_shape=jax.ShapeDtypeStruct((n_out_tiles * tile, d), F32),
        grid_spec=pltpu.PrefetchScalarGridSpec(
            num_scalar_prefetch=3,
            grid=(n_tiles,),
            in_specs=[pl.BlockSpec((tile, ROUTER_PAD), lambda t, *_: (t, 0)),
                      pl.BlockSpec((tile, d), lambda t, *_: (t, 0)),
                      pl.BlockSpec((None, None, 1, N_MOD * d), lambda t, *_: (layer, row(t), 0, 0)),
                      pl.BlockSpec(memory_space=pl.ANY)] + extra_specs,
            out_specs=pl.BlockSpec((tile, d), lambda t, *_: (jnp.minimum(t, n_out_tiles - 1), 0)),
            scratch_shapes=[pltpu.VMEM((2, SLOT_ROWS, d), BF16), pltpu.SemaphoreType.DMA((2,))],
        ),
        compiler_params=_params(("arbitrary",)),
        name="moe_combine",
    )(plan["base"], plan["lstart"], plan["nch"], route, x1, mods, ys, *extra_args)


def _moe(lay, h2, x1, route, tile_counts, mods, layer, w1, w3, w2, g_final=None):
    n_tiles = lay.n_tok // DISPATCH_TILE
    max_rows = (MOE_TOPK * lay.n_tok + n_tiles * MOE_EXPERTS * (ROW_CHUNK - 1)
                + MOE_EXPERTS * (FFN_BLOCK - 1))
    n_blocks = -(-max_rows // FFN_BLOCK)
    counts = jnp.sum(tile_counts[:, 0, :MOE_EXPERTS].reshape(n_tiles, DISPATCH_TILE // lay.tile, MOE_EXPERTS), axis=1)
    plan = _moe_plan(counts, n_blocks)
    xs = _dispatch(plan, h2, route, n_blocks * FFN_BLOCK)
    ys = _expert_ffn(plan, xs, layer, w1, w3, w2)
    return _combine(lay, plan, route, x1, mods, layer, ys, g_final)


def _rope_tables(seq, ctx):
    pos = jnp.arange(seq, dtype=jnp.int32)
    inv_freq = ROPE_BASE ** (-jnp.arange(0, AXIS_DIM, 2, dtype=F32) / AXIS_DIM)
    ang_row = (pos // GRID_W).astype(F32)[:, None] * inv_freq
    ang_col = (pos % GRID_W).astype(F32)[:, None] * inv_freq
    cos_h = jnp.concatenate([jnp.cos(ang_row)] * 2 + [jnp.cos(ang_col)] * 2, axis=1)
    sin_h = jnp.concatenate([-jnp.sin(ang_row), jnp.sin(ang_row), -jnp.sin(ang_col), jnp.sin(ang_col)], axis=1)
    reps = LANES // SWA_HEAD_DIM
    cos_t = jnp.concatenate([jnp.tile(cos_h, (1, reps)), jnp.ones((ctx, LANES), F32)], axis=0)
    sin_t = jnp.concatenate([jnp.tile(sin_h, (1, reps)), jnp.zeros((ctx, LANES), F32)], axis=0)
    return cos_t, sin_t


def _channel_dft():
    i = jnp.arange(FNET_GROUP_W, dtype=jnp.int32)
    ang = ((i[:, None] * i[None, :]) % FNET_GROUP_W).astype(F32) * (2.0 * math.pi / FNET_GROUP_W)
    return jnp.concatenate([jnp.cos(ang), jnp.sin(ang)], axis=1).astype(BF16)


def kernel(x, c, ctx, c_ctx, w_ada, b_ada, g_mix, g_ffn, w_in, w_decay_down, w_decay_up, b_decay, g_gla, sink,
           w_pa, w_pb, w_pc, w_out, w_router_group, w_router_expert, w1, w3, w2, g_final):
    batch, seq, d = x.shape
    n_ctx = ctx.shape[1]
    depth = w_ada.shape[0]
    assert d == D_MODEL and seq % GRID_W == 0 and seq % n_ctx == 0
    lay = _Layout(batch, seq, n_ctx, TOKEN_TILE)

    rows = -(-(batch + 1) // 8) * 8
    c_all = jnp.zeros((rows, d), F32).at[:batch].set(c).at[batch].set(c_ctx)
    mods_all = _adaln(c_all, w_ada, b_ada).reshape(depth, rows, 1, N_MOD * d)

    gla_masks = _gla_chunk_masks(TOKEN_TILE)
    cos_t, sin_t = _rope_tables(seq, n_ctx)
    cs = _channel_dft()
    fourier_tile = min(seq // 2, FOURIER_TILE)
    assert fourier_tile % TOKEN_TILE == 0 and seq % (2 * fourier_tile) == 0
    lat_mats = _dft_half_mats(seq, fourier_tile)
    c_ctx_m, s_ctx_m = _dft_mats(n_ctx)

    stream = ((x.reshape(batch * seq, d), ctx.reshape(batch * n_ctx, d)), 0)

    rank = w_decay_down.shape[-1]
    w_in_bf = w_in.astype(BF16)
    down = jnp.concatenate([w_decay_down[:, 0], w_decay_down[:, 1]], axis=2)
    down = jnp.pad(down, ((0, 0), (0, 0), (0, DECAY_PAD - 2 * rank))).astype(BF16)
    w_up = jnp.zeros((depth, DECAY_PAD, 2 * GLA_QK_W), F32)
    w_up = w_up.at[:, :rank, :GLA_QK_W].set(w_decay_up[:, 0]).at[:, rank:2 * rank, GLA_QK_W:].set(w_decay_up[:, 1])
    w_up = w_up.astype(BF16)
    b_dec = b_decay.reshape(depth, 1, 2 * GLA_QK_W)
    sink_b = jnp.broadcast_to(sink[:, :, None], (depth, SWA_HEADS, LANES))
    w_router = jnp.zeros((depth, d, ROUTER_PAD), F32)
    w_router = w_router.at[:, :, :MOE_EXPERTS].set(w_router_expert)
    w_router = w_router.at[:, :, MOE_EXPERTS:MOE_EXPERTS + MOE_GROUPS].set(w_router_group)
    w_router_hi = w_router.astype(BF16)
    w_router = jnp.concatenate([w_router_hi, (w_router - w_router_hi.astype(F32)).astype(BF16)], axis=2)
    g_mix3, g_ffn3, g_gla3 = g_mix[:, None, :], g_ffn[:, None, :], g_gla[:, None, :]
    w_pa_bf, w_pb_bf, w_pc_bf, w_out_bf = (w.astype(BF16) for w in (w_pa, w_pb, w_pc, w_out))

    for l in range(depth):
        last = l == depth - 1
        (k_a, v_a, k_c, v_c, q_a, r_a, u_cos, u_sin, q_c, gate_logits, la_f, la_b) = _inproj(
            lay, stream, g_mix3, mods_all, l, w_in_bf, down, w_up, b_dec, cs, cos_t, sin_t)

        o_f, o_b = _gla(lay, q_a, k_a, v_a, la_f, la_b, gla_masks)
        yb_lat = _fourier_half(u_cos, u_sin, lat_mats, seq, fourier_tile)
        yc_lat = _swa(lay, q_c, k_c, v_c, sink_b, l)
        if last:
            yb_ctx = jnp.zeros((n_ctx, batch * FNET_W), BF16)
            yc_ctx = jnp.zeros((batch * n_ctx, SWA_Q_W), BF16)
        else:
            yb_ctx = _fourier(u_cos, u_sin, c_ctx_m, s_ctx_m, seq, n_ctx, n_ctx, n_ctx)
            yc_ctx = _ctx_attn(lay, q_c, k_c, v_c, sink_b, l)

        x1, h2, gates, tile_counts = _merge(lay, stream, o_f, o_b, r_a, yb_lat, fourier_tile, yb_ctx, yc_lat, yc_ctx,
                                            gate_logits, mods_all, l, g_gla3, g_ffn3,
                                            w_pa_bf, w_pb_bf, w_pc_bf, w_out_bf, w_router)
        xs = _moe(lay, h2, x1, gates, tile_counts, mods_all, l, w1, w3, w2,
                  g_final.reshape(1, d) if last else None)
        stream = ((xs,), None)

    return xs.reshape(batch, seq, d)
```

```python
import functools
import math

import jax
import jax.numpy as jnp
from jax import lax
from jax.experimental import pallas as pl
from jax.experimental.pallas import tpu as pltpu

F32 = jnp.float32
BF16 = jnp.bfloat16
LOG2E = math.log2(math.e)

D_MODEL = 1024
GRID_W = 64
EPS = 1e-6
N_MOD = 6
GLA_HEADS = 4
GLA_DK = 64
GLA_DV = 128
GLA_TAU = 16.0
GLA_CHUNK = 64
GLA_SCALE = GLA_DK ** -0.5
FNET_GROUPS = 4
FNET_GROUP_W = 128
SWA_HEADS = 8
SWA_KV_HEADS = 2
SWA_HEAD_DIM = 64
WINDOW = 128
ROPE_BASE = 10000.0
AXIS_DIM = SWA_HEAD_DIM // 2
MOE_GROUPS = 4
MOE_EXPERTS_PER_GROUP = 4
MOE_EXPERTS = MOE_GROUPS * MOE_EXPERTS_PER_GROUP
MOE_TOPK = 2
D_EXPERT = 512

GLA_QK_W = GLA_HEADS * GLA_DK
GLA_V_W = GLA_HEADS * GLA_DV
FNET_W = FNET_GROUPS * FNET_GROUP_W
SWA_Q_W = SWA_HEADS * SWA_HEAD_DIM
SWA_KV_W = SWA_KV_HEADS * SWA_HEAD_DIM
IN_SIZES = (GLA_QK_W, GLA_V_W, SWA_KV_W, SWA_KV_W, GLA_QK_W, GLA_V_W, FNET_W, SWA_Q_W, 3 * D_MODEL)
IN_OFFS = tuple(int(sum(IN_SIZES[:i])) for i in range(len(IN_SIZES) + 1))
IN_COLS = IN_OFFS[-1]

LANES = 128
TOKEN_TILE = 256
DECAY_PAD = LANES
ROUTER_PAD = LANES
ROUTE_ID = 0
ROUTE_W = 2
DISPATCH_TILE = 512
ROW_CHUNK = 16
BIG_CHUNK = 64
FOURIER_TILE = 512
FOURIER_COL_SPLIT = 2
FFN_BLOCK = 512
SLOT_ROWS = -(-(MOE_TOPK * DISPATCH_TILE + MOE_EXPERTS * (ROW_CHUNK - 1)) // LANES) * LANES
VMEM_LIMIT = 56 * 1024 * 1024


def _layer_block(a, layer):
    return pl.BlockSpec((None,) + a.shape[1:], lambda *_: (layer,) + (0,) * (a.ndim - 1))


def _params(sem, vmem=VMEM_LIMIT):
    return pltpu.CompilerParams(dimension_semantics=sem, vmem_limit_bytes=vmem)


def _sigmoid(x):
    return 0.5 * jnp.tanh(0.5 * x) + 0.5


def _dot(a, b):
    return jnp.dot(a, b, preferred_element_type=F32)


def _dot_nt(a, b):
    return lax.dot_general(a, b, (((1,), (1,)), ((), ())), preferred_element_type=F32)


def _ada_kernel(c_ref, w_ref, b_ref, o_ref):
    c = c_ref[...]
    a = c * _sigmoid(c)
    a_hi, a_lo = _split2(a)
    w_hi, w_lo = _split2(w_ref[...])
    o_ref[...] = _dot(a_hi, w_hi) + _dot(a_lo, w_hi) + _dot(a_hi, w_lo) + b_ref[...]


def _adaln(c_all, w_ada, b_ada):
    depth, d, n = w_ada.shape
    rows = c_all.shape[0]
    tn = 1536
    return pl.pallas_call(
        _ada_kernel,
        out_shape=jax.ShapeDtypeStruct((depth, rows, n), F32),
        grid=(depth, n // tn),
        in_specs=[pl.BlockSpec((rows, d), lambda l, j: (0, 0)),
                  pl.BlockSpec((None, d, tn), lambda l, j: (l, 0, j)),
                  pl.BlockSpec((None, 1, tn), lambda l, j: (l, 0, j))],
        out_specs=pl.BlockSpec((None, rows, tn), lambda l, j: (l, 0, j)),
        compiler_params=_params(("parallel", "parallel")),
        name="adaln",
    )(c_all, w_ada, b_ada.reshape(depth, 1, n))


class _Layout:
    def __init__(self, batch, seq, ctx, tile):
        assert seq % tile == 0 and ctx % tile == 0
        self.batch, self.seq, self.ctx, self.tile = batch, seq, ctx, tile
        self.lat_tiles = seq // tile
        self.ctx_tiles = ctx // tile
        self.n_lat = batch * self.lat_tiles
        self.n_tiles = self.n_lat + batch * self.ctx_tiles
        self.n_tok = self.n_tiles * tile

    def batch_of(self, t):
        return jnp.where(t < self.n_lat, t // self.lat_tiles, (t - self.n_lat) // self.ctx_tiles)

    def mod_row(self, t):
        return jnp.where(t < self.n_lat, t // self.lat_tiles, self.batch)

    def stream_specs(self, width, ctx_first):
        if ctx_first is None:
            return [pl.BlockSpec((self.tile, width), lambda t: (t, 0))]
        lat = pl.BlockSpec((self.tile, width), lambda t: (jnp.minimum(t, self.n_lat - 1), 0))
        ctx = pl.BlockSpec((self.tile, width), lambda t: (ctx_first + jnp.maximum(t - self.n_lat, 0), 0))
        return [lat, ctx]

    def seq_tile(self, t):
        return jnp.where(t < self.n_lat, t % self.lat_tiles,
                         self.lat_tiles + (t - self.n_lat) % self.ctx_tiles)


def _rope(x, cos, sin_signed):
    n = x.shape[-1]
    lane = lax.broadcasted_iota(jnp.int32, x.shape, 1)
    half = AXIS_DIM // 2
    partner = jnp.where((lane & half) == 0, pltpu.roll(x, n - half, 1), pltpu.roll(x, half, 1))
    return x * cos + partner * sin_signed


def _read_stream(x_refs, n_lat):
    if len(x_refs) == 1:
        return x_refs[0][...]
    return jnp.where(pl.program_id(0) >= n_lat, x_refs[1][...], x_refs[0][...])


def _inproj_kernel(*refs, n_lat, n_x):
    _inproj_body(_read_stream(refs[:n_x], n_lat), *refs[n_x:])


def _inproj_body(x, g_ref, mod_ref, w_ref, wd_ref, wu_ref, bdec_ref, cs_ref, cos_ref, sin_ref,
                 ka_ref, va_ref, kc_ref, vc_ref, qa_ref, ra_ref, ua_ref, us_ref, qc_ref, gl_ref,
                 laf_ref, lab_ref):
    d = D_MODEL
    shift = mod_ref[:, 0:d]
    scale = mod_ref[:, d:2 * d]
    h = x * lax.rsqrt(jnp.mean(x * x, axis=-1, keepdims=True) + EPS) * g_ref[...]
    hb = (h * (1.0 + scale) + shift).astype(BF16)

    def proj(i):
        return _dot(hb, w_ref[:, IN_OFFS[i]:IN_OFFS[i + 1]])

    ka_ref[...] = proj(0).astype(BF16)
    va_ref[...] = proj(1).astype(BF16)
    cos = cos_ref[...]
    sin = sin_ref[...]
    kc_ref[...] = _rope(proj(2), cos, sin).astype(BF16)
    vc_ref[...] = proj(3).astype(BF16)
    qa_ref[...] = proj(4).astype(BF16)
    r = proj(5)
    ra_ref[...] = (r * _sigmoid(r)).astype(BF16)
    u = proj(6).astype(BF16)
    for g in range(FNET_GROUPS):
        sl = slice(g * FNET_GROUP_W, (g + 1) * FNET_GROUP_W)
        ab = _dot(u[:, sl], cs_ref[...])
        ua_ref[:, sl] = ab[:, :FNET_GROUP_W].astype(BF16)
        us_ref[:, sl] = ab[:, FNET_GROUP_W:].astype(BF16)
    reps = SWA_Q_W // LANES
    qc_ref[...] = (_rope(proj(7), jnp.concatenate([cos] * reps, axis=1), jnp.concatenate([sin] * reps, axis=1))
                   * (SWA_HEAD_DIM ** -0.5 * LOG2E)).astype(BF16)
    gl_ref[...] = _sigmoid(proj(8)).astype(BF16)
    low = _dot(hb, wd_ref[...]).astype(BF16)
    z = _dot(low, wu_ref[...]) + bdec_ref[...]
    la = (jnp.minimum(z, 0.0) - jnp.log(1.0 + jnp.exp(-jnp.abs(z)))) * (1.0 / GLA_TAU)
    laf_ref[...] = la[:, :GLA_QK_W]
    lab_ref[...] = la[:, GLA_QK_W:]


def _inproj(lay, x, g_mix, mods, layer, w_in, w_down, w_up, b_dec, cs, cos_t, sin_t):
    tm, d = lay.tile, D_MODEL
    n = lay.n_tok
    bf = lambda w: jax.ShapeDtypeStruct((n, w), BF16)
    tok = lambda w: pl.BlockSpec((tm, w), lambda t: (t, 0))
    const = lambda a: pl.BlockSpec(a.shape, lambda t: (0,) * a.ndim)
    seq_rows = lay.seq + lay.ctx
    fnet_shape = jax.ShapeDtypeStruct((seq_rows, lay.batch * FNET_W), BF16)
    fnet_spec = pl.BlockSpec((tm, FNET_W), lambda t: (lay.seq_tile(t), lay.batch_of(t)))
    pos_spec = pl.BlockSpec((tm, LANES), lambda t: (lay.seq_tile(t), 0))
    return pl.pallas_call(
        functools.partial(_inproj_kernel, n_lat=lay.n_lat, n_x=len(x[0])),
        out_shape=(bf(GLA_QK_W), bf(GLA_V_W), bf(SWA_KV_W), bf(SWA_KV_W), bf(GLA_QK_W), bf(GLA_V_W),
                   fnet_shape, fnet_shape, bf(SWA_Q_W), bf(3 * d),
                   jax.ShapeDtypeStruct((n, GLA_QK_W), F32), jax.ShapeDtypeStruct((n, GLA_QK_W), F32)),
        grid=(lay.n_tiles,),
        in_specs=lay.stream_specs(d, x[1]) + [_layer_block(g_mix, layer),
                  pl.BlockSpec((None, None, 1, N_MOD * d), lambda t: (layer, lay.mod_row(t), 0, 0)),
                  pl.BlockSpec((None,) + w_in.shape[1:], lambda t: (layer, 0, 0)),
                  _layer_block(w_down, layer), _layer_block(w_up, layer), _layer_block(b_dec, layer), const(cs),
                  pos_spec, pos_spec],
        out_specs=(tok(GLA_QK_W), tok(GLA_V_W), tok(SWA_KV_W), tok(SWA_KV_W), tok(GLA_QK_W), tok(GLA_V_W),
                   fnet_spec, fnet_spec, tok(SWA_Q_W), tok(3 * d), tok(GLA_QK_W), tok(GLA_QK_W)),
        compiler_params=_params(("parallel",)),
        name="inproj",
    )(*x[0], g_mix, mods, w_in, w_down, w_up, b_dec, cs, cos_t, sin_t)


def _split2(x):
    hi = x.astype(BF16)
    return hi, (x - hi.astype(F32)).astype(BF16)


def _gla_direction(q_ref, k_ref, v_ref, la_ref, tri_ref, o_ref, s_ref, rev):
    t_rows = q_ref.shape[0]
    c = GLA_CHUNK
    n_sub = t_rows // c
    hk = GLA_QK_W
    la_hi, la_lo = _split2(la_ref[...])
    tri = tri_ref[...]
    lc = _dot(tri, la_hi) + _dot(tri, la_lo)
    q = q_ref[...].astype(F32)
    k = k_ref[...].astype(F32)
    qd = (q * jnp.exp(lc) * GLA_SCALE).astype(BF16)
    kd = (k * jnp.exp(-lc)).astype(BF16)
    hrow = lax.broadcasted_iota(jnp.int32, (GLA_HEADS * c, hk), 0) // c
    hcol = lax.broadcasted_iota(jnp.int32, (GLA_HEADS * c, hk), 1) // GLA_DK
    head_mask = hrow == hcol
    trow = lax.broadcasted_iota(jnp.int32, (GLA_HEADS * c, c), 0) % c
    scol = lax.broadcasted_iota(jnp.int32, (GLA_HEADS * c, c), 1)
    causal = (scol >= trow) if rev else (scol <= trow)
    edge = 0 if rev else c - 1
    lasts = [lc[i * c + edge:i * c + edge + 1] for i in range(n_sub)]
    to_end = jnp.concatenate([jnp.broadcast_to(l, (c, hk)) for l in lasts], axis=0) - lc
    kh_t = (k * jnp.exp(to_end)).T
    pad = jnp.zeros((LANES - n_sub, hk), F32)
    decay_t = jnp.exp(jnp.concatenate(lasts + [pad], axis=0).T)
    yield
    chunk_of = lax.broadcasted_iota(jnp.int32, (1, t_rows), 1) // c
    kv = []
    for h in range(GLA_HEADS):
        kh_h = kh_t[h * GLA_DK:(h + 1) * GLA_DK]
        stack = jnp.concatenate([jnp.where(chunk_of == i, kh_h, 0.0) for i in range(n_sub)], axis=0)
        kv.append(_dot(stack.astype(BF16), v_ref[:, h * GLA_DV:(h + 1) * GLA_DV]))
    yield
    order = range(n_sub - 1, -1, -1) if rev else range(n_sub)
    state = s_ref[...]
    state_at = {}
    for i in order:
        state_at[i] = state
        inc = jnp.concatenate([kv[h][i * GLA_DK:(i + 1) * GLA_DK] for h in range(GLA_HEADS)], axis=0)
        state = decay_t[:, i:i + 1] * state + inc
    s_ref[...] = state
    yield
    inters, scoress = {}, {}
    for i in order:
        rows = slice(i * c, (i + 1) * c)
        q_stack = jnp.where(head_mask, jnp.concatenate([qd[rows]] * GLA_HEADS, axis=0), 0.0).astype(BF16)
        inters[i] = _dot(q_stack, state_at[i].astype(BF16))
        scoress[i] = jnp.where(causal, _dot_nt(q_stack, kd[rows]), 0.0).astype(BF16)
    for i in order:
        yield
        rows = slice(i * c, (i + 1) * c)
        v_i = v_ref[rows, :]
        inter, scores = inters[i], scoress[i]
        outs = []
        for h in range(GLA_HEADS):
            hr = slice(h * c, (h + 1) * c)
            hv = slice(h * GLA_DV, (h + 1) * GLA_DV)
            outs.append(inter[hr] + _dot(scores[hr], v_i[:, hv]))
        o_ref[rows, :] = jnp.concatenate(outs, axis=1).astype(BF16)


def _gla_kernel(qf_ref, kf_ref, vf_ref, laf_ref, qb_ref, kb_ref, vb_ref, lab_ref, trif_ref, trib_ref,
                of_ref, ob_ref, sf_ref, sb_ref):
    @pl.when(pl.program_id(1) == 0)
    def _():
        sf_ref[...] = jnp.zeros_like(sf_ref)
        sb_ref[...] = jnp.zeros_like(sb_ref)

    live = [_gla_direction(qf_ref, kf_ref, vf_ref, laf_ref, trif_ref, of_ref, sf_ref, False),
            _gla_direction(qb_ref, kb_ref, vb_ref, lab_ref, trib_ref, ob_ref, sb_ref, True)]
    while live:
        live = [g for g in live if next(g, "done") != "done"]


def _gla_chunk_masks(tile):
    row = lax.broadcasted_iota(jnp.int32, (tile, tile), 0)
    col = lax.broadcasted_iota(jnp.int32, (tile, tile), 1)
    same = (row // GLA_CHUNK) == (col // GLA_CHUNK)
    return jnp.stack([same & (col <= row), same & (col >= row)]).astype(BF16)


def _gla(lay, q_a, k_a, v_a, la_f, la_b, masks):
    tm = lay.tile
    nc, nl = lay.ctx_tiles, lay.lat_tiles

    def fwd(b, j):
        return jnp.where(j < nc, lay.n_lat + b * nc + j, b * nl + (j - nc))

    def bwd(b, j):
        return jnp.where(j < nc, lay.n_lat + b * nc + (nc - 1 - j), b * nl + (nl - 1 - (j - nc)))

    spec = lambda w, f: pl.BlockSpec((tm, w), lambda b, j: (f(b, j), 0))
    out = jax.ShapeDtypeStruct((lay.n_tok, GLA_V_W), BF16)
    return pl.pallas_call(
        _gla_kernel,
        out_shape=(out, out),
        grid=(lay.batch, nc + nl),
        in_specs=[spec(GLA_QK_W, fwd), spec(GLA_QK_W, fwd), spec(GLA_V_W, fwd), spec(GLA_QK_W, fwd),
                  spec(GLA_QK_W, bwd), spec(GLA_QK_W, bwd), spec(GLA_V_W, bwd), spec(GLA_QK_W, bwd),
                  pl.BlockSpec((None, tm, tm), lambda b, j: (0, 0, 0)),
                  pl.BlockSpec((None, tm, tm), lambda b, j: (1, 0, 0))],
        out_specs=(spec(GLA_V_W, fwd), spec(GLA_V_W, bwd)),
        scratch_shapes=[pltpu.VMEM((GLA_QK_W, GLA_DV), F32), pltpu.VMEM((GLA_QK_W, GLA_DV), F32)],
        compiler_params=_params(("parallel", "arbitrary")),
        name="gla",
    )(q_a, k_a, v_a, la_f, q_a, k_a, v_a, la_b, masks, masks)


def _dft_kernel(c_ref, s_ref, a_ref, b_ref, o_ref, acc_ref, *, scale):
    k = pl.program_id(1)

    @pl.when(k == 0)
    def _():
        acc_ref[...] = jnp.zeros_like(acc_ref)

    acc_ref[...] += _dot(c_ref[...], a_ref[...]) + _dot(s_ref[...], b_ref[...])

    @pl.when(k == pl.num_programs(1) - 1)
    def _():
        o_ref[...] = (acc_ref[...] * scale).astype(BF16)


def _dft_mats(n):
    f = GRID_W
    assert n % f == 0
    k = jnp.arange(n, dtype=jnp.int32)[None, :]

    def table(rows, period):
        ang = ((jnp.arange(rows, dtype=jnp.int32)[:, None] * k) % period).astype(F32) * (2.0 * math.pi / period)
        return jnp.cos(ang), jnp.sin(ang)

    ca, sa = table(n // f, n // f)
    cb, sb = table(f, n)
    cos = ca[:, None, :] * cb[None, :, :] - sa[:, None, :] * sb[None, :, :]
    sin = sa[:, None, :] * cb[None, :, :] + ca[:, None, :] * sb[None, :, :]
    return cos.reshape(n, n).astype(BF16), (-sin).reshape(n, n).astype(BF16)


def _dft_half_kernel(c_ref, s_ref, cx_ref, sx_ref, a_ref, b_ref, j_ref, o_ref, af_ref, bf_ref, *, scale):
    it = pl.program_id(1)
    tm = o_ref.shape[1]
    length = a_ref.shape[0]
    half = length // 2
    sub = 16
    jm = j_ref[...]
    row = lax.broadcasted_iota(jnp.int32, (tm, 1), 0)

    @pl.when(it == 0)
    def _():
        for m in range(half // tm):
            lo = slice(tm * m, tm * (m + 1))
            hi = slice(length - tm * (m + 1), length - tm * m)
            for src, dst, sign in ((a_ref, af_ref, 1.0), (b_ref, bf_ref, -1.0)):
                mirrored = _dot(jm, src[hi, :])
                if m > 0:
                    edge = src[length - tm * m:length - tm * m + sub, :][0:1, :].astype(F32)
                    mirrored = jnp.where(row == 0, edge, mirrored)
                dst[lo, :] = (src[lo, :].astype(F32) + sign * mirrored).astype(BF16)

    af = af_ref[...]
    bf = bf_ref[...]
    nyquist = a_ref[half:half + sub, :][0:1, :].astype(F32)
    p = _dot(c_ref[...], af) + jnp.where((row & 1) == 0, nyquist, -nyquist)
    q = _dot(s_ref[...], bf)
    o_ref[0] = ((p + q) * scale).astype(BF16)
    mirror = ((p - q) * scale).astype(BF16)
    flipped = _dot(jm, mirror)
    first = ((_dot(cx_ref[...], af) - _dot(sx_ref[...], bf))[0:1, :] + nyquist) * scale
    o_ref[1] = jnp.where(row == 0, first, flipped).astype(BF16)


def _dft_half_mats(n, tm):
    f = GRID_W
    half = n // 2
    assert half % f == 0 and half % tm == 0
    k = jnp.arange(half, dtype=jnp.int32)[None, :]

    def table(rows, period):
        ang = ((rows[:, None] * k) % period).astype(F32) * (2.0 * math.pi / period)
        return jnp.cos(ang), jnp.sin(ang)

    ca, sa = table(jnp.arange(half // f, dtype=jnp.int32), n // f)
    cb, sb = table(jnp.arange(f, dtype=jnp.int32), n)
    cos = ca[:, None, :] * cb[None, :, :] - sa[:, None, :] * sb[None, :, :]
    sin = sa[:, None, :] * cb[None, :, :] + ca[:, None, :] * sb[None, :, :]
    n_it = half // tm
    cx, sx = table(jnp.arange(1, n_it + 1, dtype=jnp.int32) * tm, n)
    spread = lambda m: jnp.zeros((n_it, 8, half), F32).at[:, 0, :].set(m).reshape(n_it * 8, half).astype(BF16)
    return (cos.reshape(half, half).astype(BF16), (-sin).reshape(half, half).astype(BF16), spread(cx), spread(-sx))


def _fourier_half(ua, us, mats, length, tm):
    cmat, smat, cx, sx = mats
    width = ua.shape[1]
    half = length // 2
    scale = 1.0 / math.sqrt(length * FNET_GROUP_W)
    r = lax.broadcasted_iota(jnp.int32, (tm, tm), 0)
    c = lax.broadcasted_iota(jnp.int32, (tm, tm), 1)
    jmat = jnp.where(c == tm - r, 1.0, 0.0).astype(BF16)
    once = pl.Buffered(1)
    cols = width // FOURIER_COL_SPLIT
    return pl.pallas_call(
        functools.partial(_dft_half_kernel, scale=scale),
        out_shape=jax.ShapeDtypeStruct((2, half, width), BF16),
        grid=(FOURIER_COL_SPLIT, half // tm),
        in_specs=[pl.BlockSpec((tm, half), lambda j, i: (i, 0)),
                  pl.BlockSpec((tm, half), lambda j, i: (i, 0)),
                  pl.BlockSpec((8, half), lambda j, i: (i, 0)),
                  pl.BlockSpec((8, half), lambda j, i: (i, 0)),
                  pl.BlockSpec((length, cols), lambda j, i: (0, j), pipeline_mode=once),
                  pl.BlockSpec((length, cols), lambda j, i: (0, j), pipeline_mode=once),
                  pl.BlockSpec((tm, tm), lambda j, i: (0, 0), pipeline_mode=once)],
        out_specs=pl.BlockSpec((2, tm, cols), lambda j, i: (0, i, j)),
        scratch_shapes=[pltpu.VMEM((half, cols), BF16), pltpu.VMEM((half, cols), BF16)],
        compiler_params=_params(("arbitrary", "arbitrary")),
        name="fourier_half",
    )(cmat, smat, cx, sx, ua, us, jmat)


def _fourier(ua, us, cmat, smat, row0, length, tm, tk):
    width = ua.shape[1]
    off = row0 // tk
    scale = 1.0 / math.sqrt(length * FNET_GROUP_W)
    return pl.pallas_call(
        functools.partial(_dft_kernel, scale=scale),
        out_shape=jax.ShapeDtypeStruct((length, width), BF16),
        grid=(length // tm, length // tk),
        in_specs=[pl.BlockSpec((tm, tk), lambda i, k: (i, k)),
                  pl.BlockSpec((tm, tk), lambda i, k: (i, k)),
                  pl.BlockSpec((tk, width), lambda i, k: (off + k, 0)),
                  pl.BlockSpec((tk, width), lambda i, k: (off + k, 0))],
        out_specs=pl.BlockSpec((tm, width), lambda i, k: (i, 0)),
        scratch_shapes=[pltpu.VMEM((tm, width), F32)],
        compiler_params=_params(("parallel", "arbitrary")),
        name="fourier",
    )(cmat, smat, ua, us)


def _attend(q_ref, k_all, v_all, valid, sink_ref, o_ref):
    hd = SWA_HEAD_DIM
    heads_per_kv = SWA_HEADS // SWA_KV_HEADS
    k_sw = jnp.concatenate([k_all[:, hd:], k_all[:, :hd]], axis=1)
    v_t = v_all.astype(F32).T
    v_t_sw = jnp.concatenate([v_t[hd:], v_t[:hd]], axis=0)
    lane = lax.broadcasted_iota(jnp.int32, (1, LANES), 1)
    lane_half = (lane < hd, lane >= hd)
    row = lax.broadcasted_iota(jnp.int32, (LANES, 1), 0)
    row_half = (row < hd, row >= hd)
    zero = jnp.zeros((), BF16)
    for p in range(SWA_HEADS // 2):
        acc = None
        for par in range(2):
            head = 2 * p + par
            aligned = head // heads_per_kv == par
            qh = jnp.where(lane_half[par], q_ref[:, p * LANES:(p + 1) * LANES], zero)
            s = _dot_nt(k_all if aligned else k_sw, qh)
            if valid is not None:
                kw = valid.shape[0]
                b = kw // 3
                s = jnp.concatenate([jnp.where(valid[:b], s[:b], -jnp.inf), s[b:2 * b],
                                     jnp.where(valid[2 * b:], s[2 * b:kw], -jnp.inf), s[kw:]], axis=0)
            sink = sink_ref[head:head + 1, 0:1] * LOG2E
            m = jnp.maximum(jnp.max(s, axis=0, keepdims=True), sink)
            e = jnp.exp2(s - m)
            denom = jnp.sum(e, axis=0, keepdims=True) + jnp.exp2(sink - m)
            v_use = jnp.where(row_half[par], v_t if aligned else v_t_sw, 0.0).astype(BF16)
            part = _dot(v_use, e.astype(BF16)) * (1.0 / denom)
            acc = part if acc is None else acc + part
        o_ref[:, p * LANES:(p + 1) * LANES] = acc.T.astype(BF16)


def _swa_kernel(q_ref, kp_ref, kc_ref, kn_ref, kx_ref, vp_ref, vc_ref, vn_ref, vx_ref, sink_ref, o_ref,
                *, seq):
    n = pl.program_id(1)
    w = WINDOW
    k_blocks = [kp_ref[...], kc_ref[0:w, :], kc_ref[w:2 * w, :], kn_ref[...]]
    v_blocks = [vp_ref[...], vc_ref[0:w, :], vc_ref[w:2 * w, :], vn_ref[...]]
    j = lax.broadcasted_iota(jnp.int32, (3 * w, w), 0)
    a = lax.broadcasted_iota(jnp.int32, (3 * w, w), 1)
    band = (j >= a) & (j - a <= 2 * w)
    for half in range(2):
        key_pos = (2 * n + half - 1) * w + j
        valid = band & (key_pos >= 0) & (key_pos < seq)
        k_all = jnp.concatenate(k_blocks[half:half + 3] + [kx_ref[...]], axis=0)
        v_all = jnp.concatenate(v_blocks[half:half + 3] + [vx_ref[...]], axis=0)
        rows = pl.ds(half * w, w)
        _attend(q_ref.at[rows], k_all, v_all, valid, sink_ref, o_ref.at[rows])


def _swa(lay, q_c, k_c, v_c, sink_b, layer):
    w = WINDOW
    assert lay.seq % (2 * w) == 0
    nq = lay.seq // w
    steps = nq // 2
    ctx_blk = (lay.batch * lay.seq) // lay.ctx

    def edge(f):
        return pl.BlockSpec((w, SWA_KV_W), lambda b, n: (b * nq + f(n), 0))

    prev = edge(lambda n: jnp.maximum(2 * n - 1, 0))
    nxt = edge(lambda n: jnp.minimum(2 * n + 2, nq - 1))
    cur = pl.BlockSpec((2 * w, SWA_KV_W), lambda b, n: (b * steps + n, 0))
    ctx_spec = pl.BlockSpec((lay.ctx, SWA_KV_W), lambda b, n: (ctx_blk + b, 0))
    q_spec = pl.BlockSpec((2 * w, SWA_Q_W), lambda b, n: (b * steps + n, 0))
    return pl.pallas_call(
        functools.partial(_swa_kernel, seq=lay.seq),
        out_shape=jax.ShapeDtypeStruct((lay.batch * lay.seq, SWA_Q_W), BF16),
        grid=(lay.batch, steps),
        in_specs=[q_spec, prev, cur, nxt, ctx_spec, prev, cur, nxt, ctx_spec,
                  _layer_block(sink_b, layer)],
        out_specs=q_spec,
        compiler_params=_params(("parallel", "parallel")),
        name="swa",
    )(q_c, k_c, k_c, k_c, k_c, v_c, v_c, v_c, v_c, sink_b)


def _ctx_attn_kernel(q_ref, kx_ref, vx_ref, sink_ref, o_ref):
    _attend(q_ref, kx_ref[...], vx_ref[...], None, sink_ref, o_ref)


def _ctx_attn(lay, q_c, k_c, v_c, sink_b, layer):
    ctx_blk = (lay.batch * lay.seq) // lay.ctx
    spec = lambda wd: pl.BlockSpec((lay.ctx, wd), lambda b: (ctx_blk + b, 0))
    return pl.pallas_call(
        _ctx_attn_kernel,
        out_shape=jax.ShapeDtypeStruct((lay.batch * lay.ctx, SWA_Q_W), BF16),
        grid=(lay.batch,),
        in_specs=[spec(SWA_Q_W), spec(SWA_KV_W), spec(SWA_KV_W),
                  _layer_block(sink_b, layer)],
        out_specs=pl.BlockSpec((lay.ctx, SWA_Q_W), lambda b: (b, 0)),
        compiler_params=_params(("parallel",)),
        name="ctx_attn",
    )(q_c, k_c, v_c, sink_b)


def _merge_kernel(*refs, n_lat, n_x):
    _merge_body(_read_stream(refs[:n_x], n_lat), *refs[n_x:], n_lat=n_lat)


def _merge_body(x, of_ref, ob_ref, ra_ref, ybl_ref, ybc_ref, ycl_ref, ycc_ref, gl_ref, mod_ref,
                ggla_ref, gffn_ref, wpa_ref, wpb_ref, wpc_ref, wout_ref, wr_ref,
                x1_ref, h2_ref, gates_ref, count_ref, *, n_lat):
    d = D_MODEL
    is_ctx = pl.program_id(0) >= n_lat
    o = of_ref[...].astype(F32) + ob_ref[...].astype(F32)
    r = ra_ref[...].astype(F32)
    parts = []
    for h in range(GLA_HEADS):
        sl = slice(h * GLA_DV, (h + 1) * GLA_DV)
        oh = o[:, sl]
        parts.append(oh * lax.rsqrt(jnp.mean(oh * oh, axis=-1, keepdims=True) + EPS) * ggla_ref[...])
    y_a = (jnp.concatenate(parts, axis=1) * r).astype(BF16)
    y_b = jnp.where(is_ctx, ybc_ref[...], ybl_ref[...])
    y_c = jnp.where(is_ctx, ycc_ref[...], ycl_ref[...])
    gl = gl_ref[...].astype(F32)
    mix = (gl[:, 0:d] * _dot(y_a, wpa_ref[...])
           + gl[:, d:2 * d] * _dot(y_b, wpb_ref[...])
           + gl[:, 2 * d:3 * d] * _dot(y_c, wpc_ref[...]))
    y = _dot(mix.astype(BF16), wout_ref[...])
    x1 = x + mod_ref[:, 2 * d:3 * d] * y
    x1_ref[...] = x1
    h2 = x1 * lax.rsqrt(jnp.mean(x1 * x1, axis=-1, keepdims=True) + EPS) * gffn_ref[...]
    h2 = h2 * (1.0 + mod_ref[:, 4 * d:5 * d]) + mod_ref[:, 3 * d:4 * d]
    h2_hi = h2.astype(BF16)
    h2_ref[...] = h2_hi
    h2_lo = (h2 - h2_hi.astype(F32)).astype(BF16)
    both = _dot(h2_hi, wr_ref[...])
    logits = both[:, :ROUTER_PAD] + both[:, ROUTER_PAD:] + _dot(h2_lo, wr_ref[:, :ROUTER_PAD])
    lt = logits.T
    sub = lax.broadcasted_iota(jnp.int32, lt.shape, 0)
    sub_f = sub.astype(F32)
    neg = -jnp.inf
    big = float(ROUTER_PAD)
    is_group = (sub >= MOE_EXPERTS) & (sub < MOE_EXPERTS + MOE_GROUPS)
    gl_m = jnp.where(is_group, lt, neg)
    g_max = jnp.max(gl_m, axis=0, keepdims=True)
    g_sel = jnp.min(jnp.where(gl_m == g_max, sub_f, big), axis=0, keepdims=True) - MOE_EXPERTS
    g_gate = 1.0 / jnp.sum(jnp.where(is_group, jnp.exp(lt - g_max), 0.0), axis=0, keepdims=True)
    lo = g_sel * MOE_EXPERTS_PER_GROUP
    in_group = (sub_f >= lo) & (sub_f < lo + MOE_EXPERTS_PER_GROUP)
    e1 = jnp.where(in_group, lt, neg)
    v1 = jnp.max(e1, axis=0, keepdims=True)
    i1 = jnp.min(jnp.where(e1 == v1, sub_f, big), axis=0, keepdims=True)
    e2 = jnp.where(sub_f == i1, neg, e1)
    v2 = jnp.max(e2, axis=0, keepdims=True)
    i2 = jnp.min(jnp.where(e2 == v2, sub_f, big), axis=0, keepdims=True)
    t = jnp.exp(v2 - v1)
    w1 = g_gate / (1.0 + t)
    w2 = g_gate * t / (1.0 + t)
    route_t = jnp.where(sub == ROUTE_ID, i1, 0.0) + jnp.where(sub == ROUTE_ID + 1, i2, 0.0)
    route_t = route_t + jnp.where(sub == ROUTE_W, w1, 0.0) + jnp.where(sub == ROUTE_W + 1, w2, 0.0)
    route = route_t.T
    gates_ref[...] = route
    lane_f = lax.broadcasted_iota(jnp.int32, route.shape, 1).astype(F32)
    hit = jnp.where((lane_f == route[:, ROUTE_ID:ROUTE_ID + 1]) | (lane_f == route[:, ROUTE_ID + 1:ROUTE_ID + 2]),
                    1.0, 0.0)
    count_ref[...] = jnp.sum(hit, axis=0, keepdims=True).astype(jnp.int32)


def _merge(lay, x, o_f, o_b, r_a, yb_lat, fourier_tile, yb_ctx, yc_lat, yc_ctx, gate_logits, mods, layer, g_gla, g_ffn,
           w_pa, w_pb, w_pc, w_out, w_router):
    tm, d = lay.tile, D_MODEL
    n = lay.n_tok
    tok = lambda w: pl.BlockSpec((tm, w), lambda t: (t, 0))
    const = lambda a: pl.BlockSpec(a.shape, lambda t: (0,) * a.ndim)
    lt, ct = lay.lat_tiles, lay.ctx_tiles
    sub = fourier_tile // tm

    def yb_lat_index(t):
        tl = jnp.minimum(t, lay.n_lat - 1)
        s = tl % lt
        u = lt - 1 - s
        upper = s >= lt // 2
        blk = jnp.where(upper, (u // sub) * sub + sub - 1 - u % sub, s)
        return (upper.astype(jnp.int32), blk, tl // lt)

    yb_lat_spec = pl.BlockSpec((None, tm, FNET_W), yb_lat_index)
    yb_ctx_spec = pl.BlockSpec((tm, FNET_W), lambda t: (jnp.maximum(t - lay.n_lat, 0) % ct,
                                                         jnp.maximum(t - lay.n_lat, 0) // ct))
    yc_lat_spec = pl.BlockSpec((tm, SWA_Q_W), lambda t: (jnp.minimum(t, lay.n_lat - 1), 0))
    yc_ctx_spec = pl.BlockSpec((tm, SWA_Q_W), lambda t: (jnp.maximum(t - lay.n_lat, 0), 0))
    return pl.pallas_call(
        functools.partial(_merge_kernel, n_lat=lay.n_lat, n_x=len(x[0])),
        out_shape=(jax.ShapeDtypeStruct((n, d), F32), jax.ShapeDtypeStruct((n, d), BF16),
                   jax.ShapeDtypeStruct((n, ROUTER_PAD), F32),
                   jax.ShapeDtypeStruct((lay.n_tiles, 1, ROUTER_PAD), jnp.int32)),
        grid=(lay.n_tiles,),
        in_specs=lay.stream_specs(d, x[1]) + [tok(GLA_V_W), tok(GLA_V_W), tok(GLA_V_W), yb_lat_spec, yb_ctx_spec,
                  yc_lat_spec, yc_ctx_spec, tok(3 * d),
                  pl.BlockSpec((None, None, 1, N_MOD * d), lambda t: (layer, lay.mod_row(t), 0, 0)),
                  _layer_block(g_gla, layer), _layer_block(g_ffn, layer), _layer_block(w_pa, layer),
                  _layer_block(w_pb, layer), _layer_block(w_pc, layer), _layer_block(w_out, layer),
                  _layer_block(w_router, layer)],
        out_specs=(tok(d), tok(d), tok(ROUTER_PAD),
                   pl.BlockSpec((None, 1, ROUTER_PAD), lambda t: (t, 0, 0))),
        compiler_params=_params(("parallel",)),
        name="merge",
    )(*x[0], o_f, o_b, r_a, yb_lat, yb_ctx, yc_lat, yc_ctx, gate_logits, mods, g_gla, g_ffn,
      w_pa, w_pb, w_pc, w_out, w_router)


def _route_ids(route, axis):
    take = (lambda i: route[:, i:i + 1]) if axis == 1 else (lambda i: route[i:i + 1, :])
    return take(ROUTE_ID).astype(jnp.int32), take(ROUTE_ID + 1).astype(jnp.int32)


def _moe_plan(counts, n_blocks):
    cnt = counts
    pc = (cnt + ROW_CHUNK - 1) // ROW_CHUNK * ROW_CHUNK
    lstart = jnp.cumsum(pc, axis=1) - pc
    tot = jnp.sum(pc, axis=0)
    tot_pad = (tot + FFN_BLOCK - 1) // FFN_BLOCK * FFN_BLOCK
    eend = jnp.cumsum(tot_pad)
    estart = eend - tot_pad
    base = estart[None, :] + jnp.cumsum(pc, axis=0) - pc
    n_used = eend[-1] // FFN_BLOCK
    blk = jnp.minimum(jnp.arange(n_blocks, dtype=jnp.int32), n_used - 1)
    bexp = jnp.sum((blk[:, None] * FFN_BLOCK >= eend[None, :]).astype(jnp.int32), axis=1)
    flat = lambda a: a.reshape(-1).astype(jnp.int32)
    nch = pc // ROW_CHUNK
    per_big = BIG_CHUNK // ROW_CHUNK
    nch = jnp.concatenate([flat(nch), flat(jnp.sum(nch // per_big, axis=1)), flat(jnp.sum(nch % per_big, axis=1))])
    return dict(base=flat(base), lstart=flat(lstart), nch=nch,
                gap0=flat(estart + tot), gapn=flat((tot_pad - tot) // ROW_CHUNK),
                bexp=flat(bexp), nused=flat(n_used))


def _segment_copies(t, base_ref, lstart_ref, nch_ref, make, start):
    per_big = BIG_CHUNK // ROW_CHUNK
    if not start:
        n_tiles = nch_ref.shape[0] // (MOE_EXPERTS + 2)

        def wait(rows):
            def body(i, c):
                make(0, 0, rows).wait()
                return c
            return body

        lax.fori_loop(0, nch_ref[n_tiles * MOE_EXPERTS + t], wait(BIG_CHUNK), 0)
        lax.fori_loop(0, nch_ref[n_tiles * (MOE_EXPERTS + 1) + t], wait(ROW_CHUNK), 0)
        return

    def per_expert(e, carry):
        idx = t * MOE_EXPERTS + e
        loc = lstart_ref[idx]
        glob = base_ref[idx]
        n_big = nch_ref[idx] // per_big
        n_small = nch_ref[idx] - n_big * per_big

        def piece(rows, first):
            def body(i, c):
                off = first + i * rows
                cp = make(pl.multiple_of(loc + off, ROW_CHUNK), pl.multiple_of(glob + off, ROW_CHUNK), rows)
                cp.start() if start else cp.wait()
                return c
            return body

        carry = lax.fori_loop(0, n_big, piece(BIG_CHUNK, 0), carry)
        return lax.fori_loop(0, n_small, piece(ROW_CHUNK, n_big * BIG_CHUNK), carry)

    lax.fori_loop(0, MOE_EXPERTS, per_expert, 0)


def _dispatch_kernel(base_ref, lstart_ref, nch_ref, gap0_ref, gapn_ref, nused_ref, h_ref, route_ref, xs_ref,
                     buf_ref, sem):
    t = pl.program_id(0)
    tile = h_ref.shape[0]
    slots = buf_ref.shape[1]
    rt = route_ref[...].T
    e1, e2 = _route_ids(rt, 0)
    sub = lax.broadcasted_iota(jnp.int32, rt.shape, 0)
    oh1, oh2 = sub == e1, sub == e2
    hit = jnp.where(oh1 | oh2, 1.0, 0.0).astype(BF16)
    before = (lax.broadcasted_iota(jnp.int32, (tile, tile), 0)
              < lax.broadcasted_iota(jnp.int32, (tile, tile), 1))
    rank = _dot(hit, jnp.where(before, 1.0, 0.0).astype(BF16))
    sub1 = lax.broadcasted_iota(jnp.int32, (rt.shape[0], 1), 0)
    seg = jnp.zeros((rt.shape[0], 1), F32)
    for e in range(MOE_EXPERTS):
        seg = jnp.where(sub1 == e, lstart_ref[t * MOE_EXPERTS + e].astype(F32), seg)
    slot_of = rank + seg
    pos1 = jnp.sum(jnp.where(oh1, slot_of, 0.0), axis=0, keepdims=True).astype(jnp.int32)
    pos2 = jnp.sum(jnp.where(oh2, slot_of, 0.0), axis=0, keepdims=True).astype(jnp.int32)
    slot = lax.broadcasted_iota(jnp.int32, (slots, tile), 0).astype(jnp.int16)
    one, zero = jnp.ones((), BF16), jnp.zeros((), BF16)
    perm = jnp.where(slot == pos1.astype(jnp.int16), one, jnp.where(slot == pos2.astype(jnp.int16), one, zero))
    cur = t % 2
    buf_ref[cur] = _dot(perm, h_ref[...]).astype(BF16)

    def maker(b):
        def make(loc, glob, rows):
            return pltpu.make_async_copy(buf_ref.at[b, pl.ds(loc, rows)], xs_ref.at[pl.ds(glob, rows)], sem.at[b])
        return make

    _segment_copies(t, base_ref, lstart_ref, nch_ref, maker(cur), True)

    @pl.when(t > 0)
    def _():
        _segment_copies(t - 1, base_ref, lstart_ref, nch_ref, maker(1 - cur), False)

    @pl.when(t == pl.num_programs(0) - 1)
    def _():
        _segment_copies(t, base_ref, lstart_ref, nch_ref, maker(cur), False)
        buf_ref[cur, 0:FFN_BLOCK, :] = jnp.zeros((FFN_BLOCK, buf_ref.shape[2]), BF16)
        n_blocks = xs_ref.shape[0] // FFN_BLOCK

        def fill(start):
            def per_expert(e, carry):
                def chunk(i, c):
                    row = pl.multiple_of(gap0_ref[e] + i * ROW_CHUNK, ROW_CHUNK)
                    cp = pltpu.make_async_copy(buf_ref.at[cur, 0:ROW_CHUNK], xs_ref.at[pl.ds(row, ROW_CHUNK)],
                                               sem.at[cur])
                    cp.start() if start else cp.wait()
                    return c
                return lax.fori_loop(0, gapn_ref[e], chunk, carry)

            def tail(b, c):
                row = pl.multiple_of(b * FFN_BLOCK, FFN_BLOCK)
                cp = pltpu.make_async_copy(buf_ref.at[cur, 0:FFN_BLOCK], xs_ref.at[pl.ds(row, FFN_BLOCK)],
                                           sem.at[cur])
                cp.start() if start else cp.wait()
                return c

            lax.fori_loop(0, MOE_EXPERTS, per_expert, 0)
            lax.fori_loop(nused_ref[0], n_blocks, tail, 0)

        fill(True)
        fill(False)


def _dispatch(plan, h2, route, n_rows):
    tile, d = DISPATCH_TILE, D_MODEL
    n_tiles = h2.shape[0] // tile
    return pl.pallas_call(
        _dispatch_kernel,
        out_shape=jax.ShapeDtypeStruct((n_rows, d), BF16),
        grid_spec=pltpu.PrefetchScalarGridSpec(
            num_scalar_prefetch=6,
            grid=(n_tiles,),
            in_specs=[pl.BlockSpec((tile, d), lambda t, *_: (t, 0)),
                      pl.BlockSpec((tile, ROUTER_PAD), lambda t, *_: (t, 0))],
            out_specs=pl.BlockSpec(memory_space=pl.ANY),
            scratch_shapes=[pltpu.VMEM((2, SLOT_ROWS, d), BF16), pltpu.SemaphoreType.DMA((2,))],
        ),
        compiler_params=_params(("arbitrary",)),
        name="moe_dispatch",
    )(plan["base"], plan["lstart"], plan["nch"], plan["gap0"], plan["gapn"], plan["nused"], h2, route)


def _ffn_kernel(bexp_ref, nused_ref, x_ref, w1_ref, w3_ref, w2_ref, y_ref, w13_s, w2_s):
    b = pl.program_id(0)
    used = b < nused_ref[0]
    de = D_EXPERT

    @pl.when(used & ((b == 0) | (bexp_ref[b] != bexp_ref[jnp.maximum(b - 1, 0)])))
    def _():
        w13_s[:, :de] = w1_ref[...].astype(BF16)
        w13_s[:, de:] = w3_ref[...].astype(BF16)
        w2_s[...] = w2_ref[...].astype(BF16)

    @pl.when(used)
    def _():
        ab = _dot(x_ref[...], w13_s[...])
        a = ab[:, :de]
        hid = (a * _sigmoid(a)) * ab[:, de:]
        y_ref[...] = _dot(hid.astype(BF16), w2_s[...]).astype(BF16)

    @pl.when(jnp.logical_not(used))
    def _():
        y_ref[...] = jnp.zeros_like(y_ref)


def _expert_ffn(plan, xs, layer, w1, w3, w2):
    d = D_MODEL
    n_blocks = xs.shape[0] // FFN_BLOCK
    row = lambda b, bexp, nused: (jnp.minimum(b, nused[0] - 1), 0)
    wsel = lambda b, bexp, nused: (layer, bexp[b], 0, 0)
    return pl.pallas_call(
        _ffn_kernel,
        out_shape=jax.ShapeDtypeStruct(xs.shape, BF16),
        grid_spec=pltpu.PrefetchScalarGridSpec(
            num_scalar_prefetch=2,
            grid=(n_blocks,),
            in_specs=[pl.BlockSpec((FFN_BLOCK, d), row),
                      pl.BlockSpec((None, None, d, D_EXPERT), wsel),
                      pl.BlockSpec((None, None, d, D_EXPERT), wsel),
                      pl.BlockSpec((None, None, D_EXPERT, d), wsel)],
            out_specs=pl.BlockSpec((FFN_BLOCK, d), lambda b, bexp, nused: (b, 0)),
            scratch_shapes=[pltpu.VMEM((d, 2 * D_EXPERT), BF16), pltpu.VMEM((D_EXPERT, d), BF16)],
        ),
        compiler_params=_params(("arbitrary",)),
        name="moe_ffn",
    )(plan["bexp"], plan["nused"], xs, w1, w3, w2)


def _combine_kernel(base_ref, lstart_ref, nch_ref, route_ref, x1_ref, mod_ref, ys_ref, *rest, n_out_tiles):
    g_ref = rest[0] if len(rest) == 4 else None
    o_ref, buf_ref, sem = rest[-3:]
    d = D_MODEL
    t = pl.program_id(0)
    tile = route_ref.shape[0]
    slots = buf_ref.shape[1]
    cur = t % 2

    def maker(b):
        def make(loc, glob, rows):
            return pltpu.make_async_copy(ys_ref.at[pl.ds(glob, rows)], buf_ref.at[b, pl.ds(loc, rows)], sem.at[b])
        return make

    @pl.when(t == 0)
    def _():
        buf_ref[...] = jnp.zeros_like(buf_ref)
        _segment_copies(t, base_ref, lstart_ref, nch_ref, maker(cur), True)

    @pl.when(t + 1 < pl.num_programs(0))
    def _():
        _segment_copies(t + 1, base_ref, lstart_ref, nch_ref, maker(1 - cur), True)

    route = route_ref[...]
    e1, e2 = _route_ids(route, 1)
    lane = lax.broadcasted_iota(jnp.int32, route.shape, 1)
    oh1, oh2 = lane == e1, lane == e2
    hit = jnp.where(oh1 | oh2, 1.0, 0.0).astype(BF16)
    before = (lax.broadcasted_iota(jnp.int32, (tile, tile), 1)
              < lax.broadcasted_iota(jnp.int32, (tile, tile), 0))
    rank = _dot(jnp.where(before, 1.0, 0.0).astype(BF16), hit)
    lane1 = lax.broadcasted_iota(jnp.int32, (1, route.shape[1]), 1)
    seg = jnp.zeros((1, route.shape[1]), F32)
    for e in range(MOE_EXPERTS):
        seg = jnp.where(lane1 == e, lstart_ref[t * MOE_EXPERTS + e].astype(F32), seg)
    slot_of = rank + seg
    pos1 = jnp.sum(jnp.where(oh1, slot_of, 0.0), axis=1, keepdims=True).astype(jnp.int32)
    pos2 = jnp.sum(jnp.where(oh2, slot_of, 0.0), axis=1, keepdims=True).astype(jnp.int32)
    slot = lax.broadcasted_iota(jnp.int32, (tile, slots), 1).astype(jnp.int16)
    w1 = route[:, ROUTE_W:ROUTE_W + 1].astype(BF16)
    w2 = route[:, ROUTE_W + 1:ROUTE_W + 2].astype(BF16)
    comb = jnp.where(slot == pos1.astype(jnp.int16), w1,
                     jnp.where(slot == pos2.astype(jnp.int16), w2, jnp.zeros((), BF16)))
    _segment_copies(t, base_ref, lstart_ref, nch_ref, maker(cur), False)
    moe = _dot(comb, buf_ref[cur])
    x2 = x1_ref[...] + mod_ref[:, 5 * d:6 * d] * moe
    if g_ref is None:
        o_ref[...] = x2
    else:
        @pl.when(t < n_out_tiles)
        def _():
            o_ref[...] = x2 * lax.rsqrt(jnp.mean(x2 * x2, axis=-1, keepdims=True) + EPS) * g_ref[...]


def _combine(lay, plan, route, x1, mods, layer, ys, g_final):
    tile, d = DISPATCH_TILE, D_MODEL
    assert lay.seq % tile == 0 and (lay.batch * lay.ctx) % tile == 0
    lat_tiles = lay.seq // tile
    n_lat = lay.batch * lat_tiles
    n_tiles = lay.n_tok // tile
    row = lambda t: jnp.where(t < n_lat, t // lat_tiles, lay.batch)
    final = g_final is not None
    n_out_tiles = n_lat if final else n_tiles
    extra_specs = [pl.BlockSpec((1, d), lambda t, *_: (0, 0))] if final else []
    extra_args = [g_final] if final else []
    return pl.pallas_call(
        functools.partial(_combine_kernel, n_out_tiles=n_out_tiles),
        out_shape=jax.ShapeDtypeStruct((n_out_tiles * tile, d), F32),
        grid_spec=pltpu.PrefetchScalarGridSpec(
            num_scalar_prefetch=3,
            grid=(n_tiles,),
            in_specs=[pl.BlockSpec((tile, ROUTER_PAD), lambda t, *_: (t, 0)),
                      pl.BlockSpec((tile, d), lambda t, *_: (t, 0)),
                      pl.BlockSpec((None, None, 1, N_MOD * d), lambda t, *_: (layer, row(t), 0, 0)),
                      pl.BlockSpec(memory_space=pl.ANY)] + extra_specs,
            out_specs=pl.BlockSpec((tile, d), lambda t, *_: (jnp.minimum(t, n_out_tiles - 1), 0)),
            scratch_shapes=[pltpu.VMEM((2, SLOT_ROWS, d), BF16), pltpu.SemaphoreType.DMA((2,))],
        ),
        compiler_params=_params(("arbitrary",)),
        name="moe_combine",
    )(plan["base"], plan["lstart"], plan["nch"], route, x1, mods, ys, *extra_args)


def _moe(lay, h2, x1, route, tile_counts, mods, layer, w1, w3, w2, g_final=None):
    n_tiles = lay.n_tok // DISPATCH_TILE
    max_rows = (MOE_TOPK * lay.n_tok + n_tiles * MOE_EXPERTS * (ROW_CHUNK - 1)
                + MOE_EXPERTS * (FFN_BLOCK - 1))
    n_blocks = -(-max_rows // FFN_BLOCK)
    counts = jnp.sum(tile_counts[:, 0, :MOE_EXPERTS].reshape(n_tiles, DISPATCH_TILE // lay.tile, MOE_EXPERTS), axis=1)
    plan = _moe_plan(counts, n_blocks)
    xs = _dispatch(plan, h2, route, n_blocks * FFN_BLOCK)
    ys = _expert_ffn(plan, xs, layer, w1, w3, w2)
    return _combine(lay, plan, route, x1, mods, layer, ys, g_final)


def _rope_tables(seq, ctx):
    pos = jnp.arange(seq, dtype=jnp.int32)
    inv_freq = ROPE_BASE ** (-jnp.arange(0, AXIS_DIM, 2, dtype=F32) / AXIS_DIM)
    ang_row = (pos // GRID_W).astype(F32)[:, None] * inv_freq
    ang_col = (pos % GRID_W).astype(F32)[:, None] * inv_freq
    cos_h = jnp.concatenate([jnp.cos(ang_row)] * 2 + [jnp.cos(ang_col)] * 2, axis=1)
    sin_h = jnp.concatenate([-jnp.sin(ang_row), jnp.sin(ang_row), -jnp.sin(ang_col), jnp.sin(ang_col)], axis=1)
    reps = LANES // SWA_HEAD_DIM
    cos_t = jnp.concatenate([jnp.tile(cos_h, (1, reps)), jnp.ones((ctx, LANES), F32)], axis=0)
    sin_t = jnp.concatenate([jnp.tile(sin_h, (1, reps)), jnp.zeros((ctx, LANES), F32)], axis=0)
    return cos_t, sin_t


def _channel_dft():
    i = jnp.arange(FNET_GROUP_W, dtype=jnp.int32)
    ang = ((i[:, None] * i[None, :]) % FNET_GROUP_W).astype(F32) * (2.0 * math.pi / FNET_GROUP_W)
    return jnp.concatenate([jnp.cos(ang), jnp.sin(ang)], axis=1).astype(BF16)


def kernel(x, c, ctx, c_ctx, w_ada, b_ada, g_mix, g_ffn, w_in, w_decay_down, w_decay_up, b_decay, g_gla, sink,
           w_pa, w_pb, w_pc, w_out, w_router_group, w_router_expert, w1, w3, w2, g_final):
    batch, seq, d = x.shape
    n_ctx = ctx.shape[1]
    depth = w_ada.shape[0]
    assert d == D_MODEL and seq % GRID_W == 0 and seq % n_ctx == 0
    lay = _Layout(batch, seq, n_ctx, TOKEN_TILE)

    rows = -(-(batch + 1) // 8) * 8
    c_all = jnp.zeros((rows, d), F32).at[:batch].set(c).at[batch].set(c_ctx)
    mods_all = _adaln(c_all, w_ada, b_ada).reshape(depth, rows, 1, N_MOD * d)

    gla_masks = _gla_chunk_masks(TOKEN_TILE)
    cos_t, sin_t = _rope_tables(seq, n_ctx)
    cs = _channel_dft()
    fourier_tile = min(seq // 2, FOURIER_TILE)
    assert fourier_tile % TOKEN_TILE == 0 and seq % (2 * fourier_tile) == 0
    lat_mats = _dft_half_mats(seq, fourier_tile)
    c_ctx_m, s_ctx_m = _dft_mats(n_ctx)

    stream = ((x.reshape(batch * seq, d), ctx.reshape(batch * n_ctx, d)), 0)

    rank = w_decay_down.shape[-1]
    w_in_bf = w_in.astype(BF16)
    down = jnp.concatenate([w_decay_down[:, 0], w_decay_down[:, 1]], axis=2)
    down = jnp.pad(down, ((0, 0), (0, 0), (0, DECAY_PAD - 2 * rank))).astype(BF16)
    w_up = jnp.zeros((depth, DECAY_PAD, 2 * GLA_QK_W), F32)
    w_up = w_up.at[:, :rank, :GLA_QK_W].set(w_decay_up[:, 0]).at[:, rank:2 * rank, GLA_QK_W:].set(w_decay_up[:, 1])
    w_up = w_up.astype(BF16)
    b_dec = b_decay.reshape(depth, 1, 2 * GLA_QK_W)
    sink_b = jnp.broadcast_to(sink[:, :, None], (depth, SWA_HEADS, LANES))
    w_router = jnp.zeros((depth, d, ROUTER_PAD), F32)
    w_router = w_router.at[:, :, :MOE_EXPERTS].set(w_router_expert)
    w_router = w_router.at[:, :, MOE_EXPERTS:MOE_EXPERTS + MOE_GROUPS].set(w_router_group)
    w_router_hi = w_router.astype(BF16)
    w_router = jnp.concatenate([w_router_hi, (w_router - w_router_hi.astype(F32)).astype(BF16)], axis=2)
    g_mix3, g_ffn3, g_gla3 = g_mix[:, None, :], g_ffn[:, None, :], g_gla[:, None, :]
    w_pa_bf, w_pb_bf, w_pc_bf, w_out_bf = (w.astype(BF16) for w in (w_pa, w_pb, w_pc, w_out))

    for l in range(depth):
        last = l == depth - 1
        (k_a, v_a, k_c, v_c, q_a, r_a, u_cos, u_sin, q_c, gate_logits, la_f, la_b) = _inproj(
            lay, stream, g_mix3, mods_all, l, w_in_bf, down, w_up, b_dec, cs, cos_t, sin_t)

        o_f, o_b = _gla(lay, q_a, k_a, v_a, la_f, la_b, gla_masks)
        yb_lat = _fourier_half(u_cos, u_sin, lat_mats, seq, fourier_tile)
        yc_lat = _swa(lay, q_c, k_c, v_c, sink_b, l)
        if last:
            yb_ctx = jnp.zeros((n_ctx, batch * FNET_W), BF16)
            yc_ctx = jnp.zeros((batch * n_ctx, SWA_Q_W), BF16)
        else:
            yb_ctx = _fourier(u_cos, u_sin, c_ctx_m, s_ctx_m, seq, n_ctx, n_ctx, n_ctx)
            yc_ctx = _ctx_attn(lay, q_c, k_c, v_c, sink_b, l)

        x1, h2, gates, tile_counts = _merge(lay, stream, o_f, o_b, r_a, yb_lat, fourier_tile, yb_ctx, yc_lat, yc_ctx,
                                            gate_logits, mods_all, l, g_gla3, g_ffn3,
                                            w_pa_bf, w_pb_bf, w_pc_bf, w_out_bf, w_router)
        xs = _moe(lay, h2, x1, gates, tile_counts, mods_all, l, w1, w3, w2,
                  g_final.reshape(1, d) if last else None)
        stream = ((xs,), None)

    return xs.reshape(batch, seq, d)
```
